```python
import math
import jax, jax.numpy as jnp
from jax import lax
import numpy as np

D_MODEL = 1024
BATCH = 16
SEQ = 2048
DEPTH = 2
DEC_BATCH = 1
DEC_SEQ = 16384
PAST_LEN = 128

ATT_HEADS = 8
KV_HEADS = 2
HEAD_DIM = 64
GQA_GROUP = ATT_HEADS // KV_HEADS
ATT_W = ATT_HEADS * HEAD_DIM
WINDOW = 128
BLOCK = 128
N_BUCKETS = 32
MAX_DIST = 128
HY_CH = D_MODEL - ATT_W
SHORT_K = 3
FILTER_HIDDEN = 64
N_BANDS = 16
POS_DIM = 1 + 2 * N_BANDS
FAST_DECAY_PCT = 0.3
SLOW_DECAY_PCT = 1.5
DECAY_TARGET = 1e-2
IN_W = 3 * HY_CH + (ATT_HEADS + 2 * KV_HEADS) * HEAD_DIM
D_FF = -(-8 * D_MODEL // (3 * 256)) * 256
EPS = 1e-6

kernel_name = "hymba_style_hyena_window_gqa_encoder"


def rms_norm(x, g):
    xf = x.astype(jnp.float32)
    y = xf * lax.rsqrt(jnp.mean(xf * xf, axis=-1, keepdims=True) + EPS)
    return (y * g.astype(jnp.float32)).astype(x.dtype)


def short_conv(u, w, b):
    up = jnp.pad(u, ((0, 0), (1, 1), (0, 0)))
    return up[:, :-2] * w[0] + up[:, 1:-1] * w[1] + up[:, 2:] * w[2] + b


def implicit_filter(L, f_w1, f_b1, f_freq, f_w2, f_b2, f_w3):
    f32 = jnp.float32
    t_idx = jnp.arange(L, dtype=f32)[:, None]
    t_norm = t_idx / max(L - 1, 1)
    bands = jnp.linspace(1e-4, N_BANDS - 1, N_BANDS, dtype=f32)
    w = (2.0 * math.pi) * t_idx * bands[None, :] / L
    z = jnp.concatenate([t_norm, jnp.cos(w), -jnp.sin(w)], axis=-1)
    freq = f_freq.astype(f32)
    h = jnp.sin(freq * (z @ f_w1.astype(f32) + f_b1.astype(f32)))
    h = jnp.sin(freq * (h @ f_w2.astype(f32) + f_b2.astype(f32)))
    h = h @ f_w3.astype(f32)
    min_decay = math.log(DECAY_TARGET) / FAST_DECAY_PCT
    max_decay = math.log(DECAY_TARGET) / SLOW_DECAY_PCT
    deltas = jnp.linspace(min_decay, max_decay, HY_CH, dtype=f32)
    decay = jnp.exp(-t_norm * jnp.abs(deltas)[None, :])
    h = h.reshape(L, 2, HY_CH) * decay[:, None, :]
    return h[:, 0], h[:, 1]


def bidir_long_conv(u, h_f, h_b, d_skip):
    B, L, C = u.shape
    uf = u.astype(jnp.float32)
    k = jnp.concatenate([h_f, jnp.zeros((1, C), jnp.float32), h_b[:0:-1]], axis=0)
    U = jnp.fft.rfft(uf, n=2 * L, axis=1)
    K = jnp.fft.rfft(k, axis=0)
    y = jnp.fft.irfft(U * K[None], n=2 * L, axis=1)[:, :L]
    return y + uf * d_skip.astype(jnp.float32)


def hyena_mixer(u, conv_w, conv_b, f_w1, f_b1, f_freq, f_w2, f_b2, f_w3, hy_skip):
    L = u.shape[1]
    uc = short_conv(u, conv_w, conv_b)
    x0, x1, v = jnp.split(uc, 3, axis=-1)
    h_f, h_b = implicit_filter(L, f_w1, f_b1, f_freq, f_w2, f_b2, f_w3)
    y = bidir_long_conv(x1 * v, h_f, h_b, hy_skip)
    return (x0.astype(jnp.float32) * y).astype(u.dtype)


def t5_bucket(rel):
    nb2 = N_BUCKETS // 2
    max_exact = nb2 // 2
    ret = jnp.where(rel > 0, nb2, 0)
    n = jnp.abs(rel)
    large = max_exact + (jnp.log(jnp.maximum(n, 1).astype(jnp.float32) / max_exact)
                         / math.log(MAX_DIST / max_exact) * (nb2 - max_exact)).astype(jnp.int32)
    large = jnp.minimum(large, nb2 - 1)
    return ret + jnp.where(n < max_exact, n, large)


def band_bias_and_mask(L, rel_bias):
    nb = L // BLOCK
    i = jnp.arange(BLOCK)[:, None]
    j = jnp.arange(3 * BLOCK)[None, :]
    rel = j - BLOCK - i
    bias = rel_bias.astype(jnp.float32)[t5_bucket(rel)]
    bias = bias.transpose(2, 0, 1).reshape(KV_HEADS, GQA_GROUP, BLOCK, 3 * BLOCK)
    kpos = jnp.arange(nb)[:, None, None] * BLOCK + j[None] - BLOCK
    mask = (jnp.abs(rel) <= WINDOW)[None] & (kpos >= 0) & (kpos < L)
    return bias, mask


def to_band(t, L):
    B, _, H, D = t.shape
    nb = L // BLOCK
    tp = jnp.pad(t, ((0, 0), (BLOCK, BLOCK), (0, 0), (0, 0)))
    return jnp.concatenate(
        [tp[:, s * BLOCK: s * BLOCK + L].reshape(B, nb, BLOCK, H, D) for s in range(3)], axis=2)


def window_gqa(q, k, v, q_gain, k_gain, sink, rel_bias):
    B, L = q.shape[0], q.shape[1]
    nb = L // BLOCK
    q = rms_norm(q.reshape(B, L, ATT_HEADS, HEAD_DIM), q_gain).astype(jnp.float32)
    k = rms_norm(k.reshape(B, L, KV_HEADS, HEAD_DIM), k_gain).astype(jnp.float32)
    v = v.reshape(B, L, KV_HEADS, HEAD_DIM).astype(jnp.float32)
    qb = q.reshape(B, nb, BLOCK, KV_HEADS, GQA_GROUP, HEAD_DIM)
    kb = to_band(k, L)
    vb = to_band(v, L)
    bias, mask = band_bias_and_mask(L, rel_bias)
    s = jnp.einsum('bnqhgd,bnkhd->bnhgqk', qb, kb) * (HEAD_DIM ** -0.5) + bias
    s = jnp.where(mask[None, :, None, None], s, -jnp.inf)
    sk = sink.astype(jnp.float32).reshape(KV_HEADS, GQA_GROUP)[None, None, :, :, None]
    m = jnp.maximum(jnp.max(s, axis=-1), sk)
    p = jnp.exp(s - m[..., None])
    denom = jnp.sum(p, axis=-1) + jnp.exp(sk - m)
    o = jnp.einsum('bnhgqk,bnkhd->bnqhgd', p, vb) / jnp.moveaxis(denom, -1, 2)[..., None]
    return o.reshape(B, L, ATT_W)


def encoder_layer(x, norm1, w_in, conv_w, conv_b, f_w1, f_b1, f_freq, f_w2, f_b2, f_w3,
                  hy_skip, q_gain, k_gain, sink, rel_bias, hy_gain, at_gain, w_out,
                  norm2, w_gate_up, w_down):
    xn = rms_norm(x, norm1)
    proj = xn @ w_in
    hy_in = proj[..., :3 * HY_CH]
    o = 3 * HY_CH
    q = proj[..., o:o + ATT_W]
    k = proj[..., o + ATT_W:o + ATT_W + KV_HEADS * HEAD_DIM]
    v = proj[..., o + ATT_W + KV_HEADS * HEAD_DIM:]
    y_hy = hyena_mixer(hy_in, conv_w, conv_b, f_w1, f_b1, f_freq, f_w2, f_b2, f_w3, hy_skip)
    y_at = window_gqa(q, k, v, q_gain, k_gain, sink, rel_bias).astype(x.dtype)
    mixed = jnp.concatenate([rms_norm(y_hy, hy_gain), rms_norm(y_at, at_gain)], axis=-1)
    h = x + mixed @ w_out
    gu = rms_norm(h, norm2) @ w_gate_up
    gate, up = jnp.split(gu, 2, axis=-1)
    return h + (jax.nn.silu(gate) * up) @ w_down


def setup_inputs(seed: int = 0) -> dict:
    key = jax.random.key(seed)
    ks = jax.random.split(key, 24)
    nrm = lambda k, shape, s: jax.random.normal(k, shape, jnp.float32) * s
    return {
        "x_prompt": nrm(ks[0], (BATCH, SEQ, D_MODEL), 1.0),
        "x_sample": nrm(ks[1], (DEC_BATCH, DEC_SEQ, D_MODEL), 1.0),
        "norm1": 1.0 + nrm(ks[2], (DEPTH, D_MODEL), 0.02),
        "w_in": nrm(ks[3], (DEPTH, D_MODEL, IN_W), D_MODEL ** -0.5),
        "conv_w": nrm(ks[4], (DEPTH, SHORT_K, 3 * HY_CH), SHORT_K ** -0.5),
        "conv_b": nrm(ks[5], (DEPTH, 3 * HY_CH), 0.02),
        "f_w1": nrm(ks[6], (DEPTH, POS_DIM, FILTER_HIDDEN), POS_DIM ** -0.5),
        "f_b1": nrm(ks[7], (DEPTH, FILTER_HIDDEN), 0.02),
        "f_freq": 1.0 + nrm(ks[8], (DEPTH, FILTER_HIDDEN), 0.02),
        "f_w2": nrm(ks[9], (DEPTH, FILTER_HIDDEN, FILTER_HIDDEN), FILTER_HIDDEN ** -0.5),
        "f_b2": nrm(ks[10], (DEPTH, FILTER_HIDDEN), 0.02),
        "f_w3": nrm(ks[11], (DEPTH, FILTER_HIDDEN, 2 * HY_CH), FILTER_HIDDEN ** -0.5),
        "hy_skip": nrm(ks[12], (DEPTH, HY_CH), 1.0),
        "q_gain": 1.0 + nrm(ks[13], (DEPTH, HEAD_DIM), 0.02),
        "k_gain": 1.0 + nrm(ks[14], (DEPTH, HEAD_DIM), 0.02),
        "sink": nrm(ks[15], (DEPTH, ATT_HEADS), 0.5),
        "rel_bias": nrm(ks[16], (N_BUCKETS, ATT_HEADS), 0.5),
        "hy_gain": 1.0 + nrm(ks[17], (DEPTH, HY_CH), 0.02),
        "at_gain": 1.0 + nrm(ks[18], (DEPTH, ATT_W), 0.02),
        "w_out": nrm(ks[19], (DEPTH, D_MODEL, D_MODEL), D_MODEL ** -0.5),
        "norm2": 1.0 + nrm(ks[20], (DEPTH, D_MODEL), 0.02),
        "w_gate_up": nrm(ks[21], (DEPTH, D_MODEL, 2 * D_FF), D_MODEL ** -0.5),
        "w_down": nrm(ks[22], (DEPTH, D_FF, D_MODEL), D_FF ** -0.5),
    }


def reference(x_prompt, x_sample, norm1, w_in, conv_w, conv_b, f_w1, f_b1, f_freq, f_w2,
              f_b2, f_w3, hy_skip, q_gain, k_gain, sink, rel_bias, hy_gain, at_gain, w_out,
              norm2, w_gate_up, w_down):
    y_prompt = x_prompt
    y_sample = x_sample
    for l in range(DEPTH):
        layer_args = (norm1[l], w_in[l], conv_w[l], conv_b[l], f_w1[l], f_b1[l], f_freq[l],
                      f_w2[l], f_b2[l], f_w3[l], hy_skip[l], q_gain[l], k_gain[l], sink[l],
                      rel_bias, hy_gain[l], at_gain[l], w_out[l], norm2[l], w_gate_up[l],
                      w_down[l])
        y_prompt = encoder_layer(y_prompt, *layer_args)
        y_sample = encoder_layer(y_sample, *layer_args)
    return (y_prompt, y_sample)
```

```python
import functools
import math

import numpy as np
import jax
import jax.numpy as jnp
from jax import lax
from jax.experimental import pallas as pl
from jax.experimental.pallas import tpu as pltpu

F32 = jnp.float32
BF16 = jnp.bfloat16

D_MODEL = 1024
ATT_HEADS = 8
KV_HEADS = 2
HEAD_DIM = 64
GQA_GROUP = ATT_HEADS // KV_HEADS
ATT_W = ATT_HEADS * HEAD_DIM
KV_W = KV_HEADS * HEAD_DIM
WINDOW = 128
BLOCK = 128
N_BUCKETS = 32
MAX_DIST = 128
HY_CH = D_MODEL - ATT_W
FILTER_HIDDEN = 64
N_BANDS = 16
POS_DIM = 1 + 2 * N_BANDS
FAST_DECAY_PCT = 0.3
SLOW_DECAY_PCT = 1.5
DECAY_TARGET = 1e-2
IN_W = 3 * HY_CH + (ATT_HEADS + 2 * KV_HEADS) * HEAD_DIM
D_FF = -(-8 * D_MODEL // (3 * 256)) * 256
EPS = 1e-6

FFT_N2 = 64
LANES = 128
VMEM_LIMIT = 56 * 1024 * 1024

_NT = (((1,), (1,)), ((), ()))


def _cparams(*sem):
    return pltpu.CompilerParams(dimension_semantics=sem, vmem_limit_bytes=VMEM_LIMIT)


def _const_spec(shape):
    nd = len(shape)
    return pl.BlockSpec(shape, lambda *_: (0,) * nd, pipeline_mode=pl.Buffered(1))


def _inproj_kernel(x_ref, g1_ref, w_ref, bq_ref, bk_ref, qg_ref, kg_ref,
                   x0_ref, x1_ref, vh_ref, q_ref, k_ref, v_ref):
    x = x_ref[...]
    xn = x * lax.rsqrt(jnp.mean(x * x, axis=-1, keepdims=True) + EPS) * g1_ref[...]
    xb = xn.astype(BF16)

    def seg(lo, hi):
        return jnp.dot(xb, w_ref[:, lo:hi], preferred_element_type=F32)

    x0_ref[...] = seg(0, HY_CH).astype(BF16)
    x1_ref[...] = seg(HY_CH, 2 * HY_CH).astype(BF16)
    vh_ref[...] = seg(2 * HY_CH, 3 * HY_CH).astype(BF16)

    o = 3 * HY_CH
    q = seg(o, o + ATT_W)
    ms = jnp.dot((q * q).astype(BF16), bq_ref[...], preferred_element_type=F32)
    q_ref[...] = (q * lax.rsqrt(ms + EPS) * qg_ref[...]).astype(BF16)

    k = seg(o + ATT_W, o + ATT_W + KV_W)
    ms = jnp.dot((k * k).astype(BF16), bk_ref[...], preferred_element_type=F32)
    kn = k * lax.rsqrt(ms + EPS) * kg_ref[...]
    v = seg(o + ATT_W + KV_W, IN_W)

    lane = lax.broadcasted_iota(jnp.int32, kn.shape, 1)
    lo_half = lane < HEAD_DIM

    def dup(t):
        tr = pltpu.roll(t, HEAD_DIM, axis=1)
        return jnp.concatenate([jnp.where(lo_half, t, tr), jnp.where(lo_half, tr, t)], axis=1)

    k_ref[...] = dup(kn).astype(BF16)
    v_ref[...] = dup(v).astype(BF16)


def _inproj(x2d, g1, w_in_b, bq, bk, qg, kg, tm):
    t = x2d.shape[0]
    row = lambda w: pl.BlockSpec((tm, w), lambda i: (i, 0))
    outs = [HY_CH, HY_CH, HY_CH, ATT_W, 2 * KV_W, 2 * KV_W]
    return pl.pallas_call(
        _inproj_kernel,
        grid=(t // tm,),
        in_specs=[row(D_MODEL), _const_spec((1, D_MODEL)), _const_spec((D_MODEL, IN_W)),
                  _const_spec((ATT_W, ATT_W)), _const_spec((KV_W, KV_W)),
                  _const_spec((1, ATT_W)), _const_spec((1, KV_W))],
        out_specs=[row(w) for w in outs],
        out_shape=[jax.ShapeDtypeStruct((t, w), BF16) for w in outs],
        compiler_params=_cparams("parallel"),
        name="inproj",
    )(x2d, g1, w_in_b, bq, bk, qg, kg)


def _hypre_kernel(x0_ref, x1_ref, v_ref, x0p_ref, x1p_ref, vp_ref, x0n_ref, x1n_ref, vn_ref,
                  cw_ref, cb_ref, u_ref, x0c_ref, *, halo):
    i = pl.program_id(1)
    first = i == 0
    last = i == pl.num_programs(1) - 1
    tl = x0_ref.shape[1]
    row = lax.broadcasted_iota(jnp.int32, (tl, HY_CH), 0)

    def conv(m_ref, p_ref, n_ref, j):
        m = m_ref[0].astype(F32)
        prev_row = jnp.where(first, 0.0, p_ref[0, halo - 1:halo, :].astype(F32))
        next_row = jnp.where(last, 0.0, n_ref[0, 0:1, :].astype(F32))
        up = jnp.where(row == 0, prev_row, pltpu.roll(m, 1, axis=0))
        dn = jnp.where(row == tl - 1, next_row, pltpu.roll(m, tl - 1, axis=0))
        c = slice(j * HY_CH, (j + 1) * HY_CH)
        return up * cw_ref[0:1, c] + m * cw_ref[1:2, c] + dn * cw_ref[2:3, c] + cb_ref[0:1, c]

    x0c_ref[0] = conv(x0_ref, x0p_ref, x0n_ref, 0).astype(BF16)
    u_ref[0] = (conv(x1_ref, x1p_ref, x1n_ref, 1) * conv(v_ref, vp_ref, vn_ref, 2)).astype(BF16)


def _hypre(x0, x1, vh, conv_w, conv_b, tl):
    b, l, _ = x0.shape
    halo = 16
    nh = tl // halo
    main = pl.BlockSpec((1, tl, HY_CH), lambda bi, i: (bi, i, 0))
    prev = pl.BlockSpec((1, halo, HY_CH), lambda bi, i: (bi, jnp.maximum(i * nh - 1, 0), 0))
    nxt = pl.BlockSpec((1, halo, HY_CH), lambda bi, i: (bi, jnp.minimum((i + 1) * nh, l // halo - 1), 0))
    return pl.pallas_call(
        functools.partial(_hypre_kernel, halo=halo),
        grid=(b, l // tl),
        in_specs=[main, main, main, prev, prev, prev, nxt, nxt, nxt,
                  _const_spec((3, 3 * HY_CH)), _const_spec((1, 3 * HY_CH))],
        out_specs=[main, main],
        out_shape=[jax.ShapeDtypeStruct((b, l, HY_CH), BF16)] * 2,
        compiler_params=_cparams("parallel", "parallel"),
        name="hypre",
    )(x0, x1, vh, x0, x1, vh, x0, x1, vh, conv_w, conv_b)


def _stage_a_kernel(x_ref, f_ref, o_ref):
    nz = x_ref.shape[0]
    x = jnp.concatenate([x_ref[z, 0].astype(BF16) for z in range(nz)], axis=0)
    o_ref[0] = jnp.dot(f_ref[...], x, preferred_element_type=F32).astype(o_ref.dtype)


def _stage_a(x4, fmat, lcc, out_dtype=BF16):
    nz, p, a, lc = x4.shape
    m = fmat.shape[0]
    return pl.pallas_call(
        _stage_a_kernel,
        grid=(p, lc // lcc),
        in_specs=[pl.BlockSpec((nz, 1, a, lcc), lambda pi, j: (0, pi, 0, j)), _const_spec(fmat.shape)],
        out_specs=pl.BlockSpec((1, m, lcc), lambda pi, j: (pi, 0, j)),
        out_shape=jax.ShapeDtypeStruct((p, m, lc), out_dtype),
        compiler_params=_cparams("parallel", "parallel"),
        name="hy_stage_a",
    )(x4, fmat)


def _stage_b_kernel(a_ref, kf_ref, g_ref, gt_ref, o_ref):
    kb = a_ref.shape[2]
    n2 = a_ref.shape[3]
    for j in range(kb):
        x = jnp.concatenate([a_ref[0, 0, j], a_ref[0, 1, j]], axis=0)
        u = jnp.dot(g_ref[j], x, preferred_element_type=F32)
        ur, ui = u[:n2], u[n2:]
        kr, ki = kf_ref[0, j], kf_ref[1, j]
        p = jnp.concatenate([ur * kr - ui * ki, ur * ki + ui * kr], axis=0).astype(BF16)
        v = jnp.dot(gt_ref[j], p, preferred_element_type=F32)
        o_ref[0, 0, j] = v[:n2].astype(o_ref.dtype)
        o_ref[0, 1, j] = v[n2:].astype(o_ref.dtype)


def _stage_b(a5, kf, g, gt, kb):
    p, _, n1, n2, c = a5.shape
    blk = pl.BlockSpec((1, 2, kb, n2, c), lambda i, pi: (pi, 0, i, 0, 0))
    gspec = pl.BlockSpec((kb, 2 * n2, 2 * n2), lambda i, pi: (i, 0, 0))
    return pl.pallas_call(
        _stage_b_kernel,
        grid=(n1 // kb, p),
        in_specs=[blk, pl.BlockSpec((2, kb, n2, c), lambda i, pi: (0, i, 0, 0)), gspec, gspec],
        out_specs=blk,
        out_shape=jax.ShapeDtypeStruct(a5.shape, BF16),
        compiler_params=_cparams("parallel", "parallel"),
        name="hy_stage_b",
    )(a5, kf, g, gt)


def _stage_bf_kernel(a_ref, g_ref, o_ref, *, scale):
    kb = a_ref.shape[2]
    n2 = a_ref.shape[3]
    for j in range(kb):
        x = jnp.concatenate([a_ref[0, 0, j], a_ref[0, 1, j]], axis=0)
        u = jnp.dot(g_ref[j], x, preferred_element_type=F32) * scale
        o_ref[0, j] = u[:n2]
        o_ref[1, j] = u[n2:]


def _stage_bf(a5, g, kb, scale):
    _, _, n1, n2, c = a5.shape
    return pl.pallas_call(
        functools.partial(_stage_bf_kernel, scale=scale),
        grid=(n1 // kb,),
        in_specs=[pl.BlockSpec((1, 2, kb, n2, c), lambda i: (0, 0, i, 0, 0)),
                  pl.BlockSpec((kb, 2 * n2, 2 * n2), lambda i: (i, 0, 0))],
        out_specs=pl.BlockSpec((2, kb, n2, c), lambda i: (0, i, 0, 0)),
        out_shape=jax.ShapeDtypeStruct((2, n1, n2, c), F32),
        compiler_params=_cparams("parallel"),
        name="hy_filter_spectrum",
    )(a5, g)


def _stage_c_kernel(v_ref, f_ref, x0c_ref, u_ref, skip_ref, o_ref):
    nz, _, a, _ = o_ref.shape
    y = jnp.dot(f_ref[...], v_ref[0], preferred_element_type=F32)
    for z in range(nz):
        yz = y[z * a:(z + 1) * a] + skip_ref[...] * u_ref[z, 0].astype(F32)
        o_ref[z, 0] = (x0c_ref[z, 0].astype(F32) * yz).astype(o_ref.dtype)


def _stage_c(v3, fmat, x0c4, u4, skip_t, lcc):
    nz, p, a, lc = u4.shape
    m2 = v3.shape[1]
    blk = pl.BlockSpec((nz, 1, a, lcc), lambda pi, j: (0, pi, 0, j))
    return pl.pallas_call(
        _stage_c_kernel,
        grid=(p, lc // lcc),
        in_specs=[pl.BlockSpec((1, m2, lcc), lambda pi, j: (pi, 0, j)), _const_spec(fmat.shape),
                  blk, blk, pl.BlockSpec((1, lcc), lambda pi, j: (0, 0))],
        out_specs=blk,
        out_shape=jax.ShapeDtypeStruct(u4.shape, BF16),
        compiler_params=_cparams("parallel", "parallel"),
        name="hy_stage_c",
    )(v3, fmat, x0c4, u4, skip_t)


def _filter_kernel(z_ref, w1_ref, b1_ref, fr_ref, w2_ref, b2_ref, w3_ref, dl_ref, o_ref, *, seq_len):
    hi = lax.Precision.HIGHEST
    z = z_ref[...]
    fr = fr_ref[...]
    h = jnp.sin(fr * (jnp.dot(z, w1_ref[...], precision=hi, preferred_element_type=F32) + b1_ref[...]))
    h = jnp.sin(fr * (jnp.dot(h, w2_ref[...], precision=hi, preferred_element_type=F32) + b2_ref[...]))
    h = jnp.dot(h, w3_ref[...], precision=hi, preferred_element_type=F32)
    tr = z.shape[0]
    n = pl.program_id(0) * tr + lax.broadcasted_iota(jnp.int32, (tr, HY_CH), 0)
    sel = jnp.where(n < seq_len, h[:, :HY_CH], jnp.where(n > seq_len, h[:, HY_CH:], 0.0))
    decay = jnp.exp(-z[:, 0:1] * dl_ref[...])
    o_ref[...] = sel * decay


def _filter_taps(zfeat, f_w1, f_b1, f_freq, f_w2, f_b2, f_w3, absdelta, seq_len, tr):
    n = zfeat.shape[0]
    cs = lambda a: _const_spec(a.shape)
    return pl.pallas_call(
        functools.partial(_filter_kernel, seq_len=seq_len),
        grid=(n // tr,),
        in_specs=[pl.BlockSpec((tr, POS_DIM), lambda i: (i, 0)), cs(f_w1), cs(f_b1), cs(f_freq),
                  cs(f_w2), cs(f_b2), cs(f_w3), cs(absdelta)],
        out_specs=pl.BlockSpec((tr, HY_CH), lambda i: (i, 0)),
        out_shape=jax.ShapeDtypeStruct((n, HY_CH), F32),
        compiler_params=_cparams("parallel"),
        name="hy_filter_taps",
    )(zfeat, f_w1, f_b1, f_freq, f_w2, f_b2, f_w3, absdelta)


def _bias_kernel(rb_ref, oh_ref, o_ref):
    o_ref[...] = jnp.dot(rb_ref[...], oh_ref[...], precision=lax.Precision.HIGHEST,
                         preferred_element_type=F32)


def _bias_table(rel_bias):
    i = jnp.arange(BLOCK)[:, None]
    j = jnp.arange(3 * BLOCK)[None, :]
    rel = j - BLOCK - i
    nb2 = N_BUCKETS // 2
    max_exact = nb2 // 2
    n = jnp.abs(rel)
    large = max_exact + (jnp.log(jnp.maximum(n, 1).astype(F32) / max_exact)
                         / math.log(MAX_DIST / max_exact) * (nb2 - max_exact)).astype(jnp.int32)
    large = jnp.minimum(large, nb2 - 1)
    bucket = jnp.where(rel > 0, nb2, 0) + jnp.where(n < max_exact, n, large)
    onehot = (bucket.reshape(1, -1) == jnp.arange(N_BUCKETS)[:, None]).astype(F32)
    cols = onehot.shape[1]
    tc = cols // 4
    table = pl.pallas_call(
        _bias_kernel,
        grid=(4,),
        in_specs=[_const_spec((ATT_HEADS, N_BUCKETS)), pl.BlockSpec((N_BUCKETS, tc), lambda c: (0, c))],
        out_specs=pl.BlockSpec((ATT_HEADS, tc), lambda c: (0, c)),
        out_shape=jax.ShapeDtypeStruct((ATT_HEADS, cols), F32),
        compiler_params=_cparams("parallel"),
        name="att_bias_table",
    )(rel_bias.T, onehot)
    table = table.reshape(ATT_HEADS, BLOCK, 3 * BLOCK)
    table = jnp.where((n <= WINDOW)[None], table, -jnp.inf)
    return table.reshape(KV_HEADS, GQA_GROUP * BLOCK, 3 * BLOCK)


def _attn_kernel(q_ref, km_ref, kp_ref, kn_ref, vm_ref, vp_ref, vn_ref, bias_ref, sink_ref, o_ref, *, nsub):
    i = pl.program_id(1)
    first_tile = i == 0
    last_tile = i == pl.num_programs(1) - 1
    kwin = jnp.concatenate([kp_ref[0], km_ref[0], kn_ref[0]], axis=0)
    vwin = jnp.concatenate([vp_ref[0], vm_ref[0], vn_ref[0]], axis=0)
    col = lax.broadcasted_iota(jnp.int32, (1, 3 * BLOCK), 1)
    lane = lax.broadcasted_iota(jnp.int32, (BLOCK, LANES), 1)
    lo_half = lane < HEAD_DIM
    neg = -jnp.inf
    for s in range(nsub):
        qs = q_ref[0, s * BLOCK:(s + 1) * BLOCK, :]
        pen = jnp.zeros((1, 3 * BLOCK), F32)
        if s == 0:
            pen = pen + jnp.where(jnp.logical_and(first_tile, col < BLOCK), neg, 0.0)
        if s == nsub - 1:
            pen = pen + jnp.where(jnp.logical_and(last_tile, col >= 2 * BLOCK), neg, 0.0)
        for g in range(KV_HEADS):
            rows = []
            for pr in range(2):
                qp = qs[:, (2 * g + pr) * LANES:(2 * g + pr + 1) * LANES]
                rows.append(jnp.where(lo_half, qp, jnp.zeros_like(qp)))
                rows.append(jnp.where(lo_half, jnp.zeros_like(qp), qp))
            qg = jnp.concatenate(rows, axis=0)
            kg = kwin[s * BLOCK:(s + 3) * BLOCK, g * LANES:(g + 1) * LANES]
            vg = vwin[s * BLOCK:(s + 3) * BLOCK, g * LANES:(g + 1) * LANES]
            sc = lax.dot_general(qg, kg, _NT, preferred_element_type=F32)
            sc = sc + (bias_ref[g] + pen)
            sk = sink_ref[g]
            m = jnp.maximum(jnp.max(sc, axis=-1, keepdims=True), sk)
            p = jnp.exp(sc - m)
            den = jnp.sum(p, axis=-1, keepdims=True) + jnp.exp(sk - m)
            o = jnp.dot(p.astype(BF16), vg, preferred_element_type=F32) / den
            for pr in range(2):
                oe = o[(2 * pr) * BLOCK:(2 * pr + 1) * BLOCK]
                oo = o[(2 * pr + 1) * BLOCK:(2 * pr + 2) * BLOCK]
                o_ref[0, s * BLOCK:(s + 1) * BLOCK, (2 * g + pr) * LANES:(2 * g + pr + 1) * LANES] = (
                    jnp.where(lo_half, oe, oo).astype(o_ref.dtype))


def _attention(q, kd, vd, bias, sinkcol, tq):
    b, l, _ = q.shape
    nsub = tq // BLOCK
    nblk = l // BLOCK
    kw = 2 * KV_W
    main = lambda w: pl.BlockSpec((1, tq, w), lambda bi, i: (bi, i, 0))
    prev = pl.BlockSpec((1, BLOCK, kw), lambda bi, i: (bi, jnp.maximum(i * nsub - 1, 0), 0))
    nxt = pl.BlockSpec((1, BLOCK, kw), lambda bi, i: (bi, jnp.minimum((i + 1) * nsub, nblk - 1), 0))
    return pl.pallas_call(
        functools.partial(_attn_kernel, nsub=nsub),
        grid=(b, l // tq),
        in_specs=[main(ATT_W), main(kw), prev, nxt, main(kw), prev, nxt,
                  _const_spec(bias.shape), _const_spec(sinkcol.shape)],
        out_specs=main(ATT_W),
        out_shape=jax.ShapeDtypeStruct((b, l, ATT_W), BF16),
        compiler_params=_cparams("parallel", "parallel"),
        name="window_attn",
    )(q, kd, kd, kd, vd, vd, vd, bias, sinkcol)


FF_CHUNK = 256


def _ffn_kernel(x_ref, yh_ref, ya_ref, hg_ref, ag_ref, wo_ref, g2_ref, wgu_ref, wd_ref, o_ref, act_ref):
    def rms(t, g):
        return (t * lax.rsqrt(jnp.mean(t * t, axis=-1, keepdims=True) + EPS) * g).astype(BF16)

    mixed = jnp.concatenate([rms(yh_ref[...].astype(F32), hg_ref[...]),
                             rms(ya_ref[...].astype(F32), ag_ref[...])], axis=1)
    h = x_ref[...] + jnp.dot(mixed, wo_ref[...], preferred_element_type=F32)
    hn = rms(h, g2_ref[...])
    for c in range(D_FF // FF_CHUNK):
        lo = c * FF_CHUNK
        gate = jnp.dot(hn, wgu_ref[:, lo:lo + FF_CHUNK], preferred_element_type=F32)
        up = jnp.dot(hn, wgu_ref[:, D_FF + lo:D_FF + lo + FF_CHUNK], preferred_element_type=F32)
        act_ref[:, lo:lo + FF_CHUNK] = (gate / (1.0 + jnp.exp(-gate)) * up).astype(BF16)
    o_ref[...] = h + jnp.dot(act_ref[...], wd_ref[...], preferred_element_type=F32)


def _ffn(x2d, yh, ya, hg, ag, wo_b, g2, wgu_b, wd_b, tm):
    t = x2d.shape[0]
    row = lambda w: pl.BlockSpec((tm, w), lambda i: (i, 0))
    return pl.pallas_call(
        _ffn_kernel,
        grid=(t // tm,),
        in_specs=[row(D_MODEL), row(HY_CH), row(ATT_W), _const_spec((1, HY_CH)), _const_spec((1, ATT_W)),
                  _const_spec((D_MODEL, D_MODEL)), _const_spec((1, D_MODEL)),
                  _const_spec((D_MODEL, 2 * D_FF)), _const_spec((D_FF, D_MODEL))],
        out_specs=row(D_MODEL),
        out_shape=jax.ShapeDtypeStruct((t, D_MODEL), F32),
        scratch_shapes=[pltpu.VMEM((tm, D_FF), BF16)],
        compiler_params=_cparams("parallel"),
        name="outproj_swiglu",
    )(x2d, yh, ya, hg, ag, wo_b, g2, wgu_b, wd_b)


def _dft_tables(seq_len, paired):
    n = 2 * seq_len
    n1 = n // FFT_N2
    a = n1 // 2
    kk = np.arange(n1)[:, None]
    th_half = 2.0 * np.pi * ((kk * np.arange(a)[None, :]) % n1) / n1
    c, s = np.cos(th_half), np.sin(th_half)
    if paired:
        fa = np.block([[c, s], [-s, c]])
        fc = np.block([[c.T, -s.T], [s.T, c.T]])
    else:
        fa = np.concatenate([c, -s], axis=0)
        fc = np.concatenate([c.T, -s.T], axis=1)
    th_full = 2.0 * np.pi * ((kk * np.arange(n1)[None, :]) % n1) / n1
    faf = np.concatenate([np.cos(th_full), -np.sin(th_full)], axis=0)
    to = lambda m: jnp.asarray(m.astype(np.float32)).astype(BF16)
    return to(fa), to(fc), to(faf)


def _twiddle_tables(seq_len):
    n = 2 * seq_len
    n1 = n // FFT_N2
    k1 = jnp.arange(n1, dtype=jnp.int32)[:, None, None]
    k2 = jnp.arange(FFT_N2, dtype=jnp.int32)[None, :, None]
    n2 = jnp.arange(FFT_N2, dtype=jnp.int32)[None, None, :]
    m = (n2 * (k2 * n1 + k1)) % n
    ang = m.astype(F32) * (-2.0 * math.pi / n)
    gr, gi = jnp.cos(ang), jnp.sin(ang)
    g = jnp.concatenate([jnp.concatenate([gr, -gi], axis=2), jnp.concatenate([gi, gr], axis=2)], axis=1)
    return g.astype(BF16), jnp.swapaxes(g, 1, 2).astype(BF16)


def _filter_features(seq_len):
    n = np.arange(2 * seq_len)
    t = np.where(n < seq_len, n, 2 * seq_len - n).astype(np.float32)[:, None]
    t_idx = jnp.asarray(t)
    t_norm = t_idx / max(seq_len - 1, 1)
    bands = jnp.linspace(1e-4, N_BANDS - 1, N_BANDS, dtype=F32)
    w = (2.0 * math.pi) * t_idx * bands[None, :] / seq_len
    return jnp.concatenate([t_norm, jnp.cos(w), -jnp.sin(w)], axis=-1)


def _abs_deltas():
    min_decay = math.log(DECAY_TARGET) / FAST_DECAY_PCT
    max_decay = math.log(DECAY_TARGET) / SLOW_DECAY_PCT
    return jnp.abs(jnp.linspace(min_decay, max_decay, HY_CH, dtype=F32))[None, :]


def _pick(n, target):
    t = min(n, target)
    while n % t:
        t //= 2
    return t


def _layer(x, lw, shared, st):
    b, l, _ = x.shape
    t = b * l
    x2d = x.reshape(t, D_MODEL)
    tm = _pick(t, 512)
    x0, x1, vh, q, kd, vd = _inproj(x2d, lw["norm1"], lw["w_in"], shared["bq"], shared["bk"],
                                    lw["qg"], lw["kg"], tm)
    r3 = lambda a: a.reshape(b, l, a.shape[-1])

    u, x0c = _hypre(r3(x0), r3(x1), r3(vh), lw["conv_w"], lw["conv_b"], _pick(l, 2048))
    n1 = 2 * l // FFT_N2
    a = n1 // 2
    lc = FFT_N2 * HY_CH
    nz = st["nz"]
    p = b // nz
    u4 = u.reshape(nz, p, a, lc)
    x0c4 = x0c.reshape(nz, p, a, lc)
    lcc = _pick(lc, max(LANES, (1 << 20) // (2 * n1)))
    ah = _stage_a(u4, st["fa"], lcc)
    kb = _pick(n1, 8)
    vv = _stage_b(ah.reshape(p, 2, n1, FFT_N2, HY_CH), lw["kf"][st["key"]], st["g"], st["gt"], kb)
    yh = _stage_c(vv.reshape(p, 2 * n1, lc), st["fc"], x0c4, u4, lw["skip_t"], lcc)
    yh = yh.reshape(t, HY_CH)

    ya = _attention(r3(q), r3(kd), r3(vd), shared["bias"], lw["sinkcol"], _pick(l, 512))
    ya = ya.reshape(t, ATT_W)

    out = _ffn(x2d, yh, ya, lw["hy_gain"], lw["at_gain"], lw["w_out"], lw["norm2"],
               lw["w_gate_up"], lw["w_down"], tm)
    return out.reshape(b, l, D_MODEL)


def _filter_spectrum(lw_raw, st, seq_len):
    n = 2 * seq_len
    n1 = n // FFT_N2
    lc = FFT_N2 * HY_CH
    taps = _filter_taps(st["zfeat"], lw_raw["f_w1"], lw_raw["f_b1"], lw_raw["f_freq"], lw_raw["f_w2"],
                        lw_raw["f_b2"], lw_raw["f_w3"], st["absdelta"], seq_len, _pick(n, 512))
    lcc = _pick(lc, max(LANES, (1 << 20) // (2 * n1)))
    ah = _stage_a(taps.reshape(1, 1, n1, lc), st["faf"], lcc)
    return _stage_bf(ah.reshape(1, 2, n1, FFT_N2, HY_CH), st["g"], _pick(n1, 8), 1.0 / n)


def kernel(x_prompt, x_sample, norm1, w_in, conv_w, conv_b, f_w1, f_b1, f_freq, f_w2, f_b2, f_w3,
           hy_skip, q_gain, k_gain, sink, rel_bias, hy_gain, at_gain, w_out, norm2, w_gate_up, w_down):
    depth = norm1.shape[0]
    streams = {}
    for key, x in (("prompt", x_prompt), ("sample", x_sample)):
        b, l, _ = x.shape
        paired = b % 2 == 0
        fa, fc, faf = _dft_tables(l, paired)
        g, gt = _twiddle_tables(l)
        streams[key] = dict(key=key, nz=2 if paired else 1, fa=fa, fc=fc, faf=faf, g=g, gt=gt,
                            zfeat=_filter_features(l), absdelta=_abs_deltas())

    head_id = np.arange(ATT_W) // HEAD_DIM
    shared = dict(
        bq=jnp.asarray((head_id[:, None] == head_id[None, :]).astype(np.float32) / HEAD_DIM).astype(BF16),
        bk=jnp.asarray((head_id[:KV_W, None] == head_id[None, :KV_W]).astype(np.float32) / HEAD_DIM).astype(BF16),
        bias=_bias_table(rel_bias),
    )

    y_prompt, y_sample = x_prompt, x_sample
    for li in range(depth):
        raw = dict(f_w1=f_w1[li], f_b1=f_b1[li][None, :], f_freq=f_freq[li][None, :], f_w2=f_w2[li],
                   f_b2=f_b2[li][None, :], f_w3=f_w3[li])
        lw = dict(
            norm1=norm1[li][None, :], w_in=w_in[li].astype(BF16),
            qg=jnp.tile(q_gain[li], ATT_HEADS)[None, :] * (HEAD_DIM ** -0.5),
            kg=jnp.tile(k_gain[li], KV_HEADS)[None, :],
            conv_w=conv_w[li], conv_b=conv_b[li][None, :],
            skip_t=jnp.tile(hy_skip[li], FFT_N2)[None, :],
            sinkcol=jnp.repeat(sink[li], BLOCK).reshape(KV_HEADS, GQA_GROUP * BLOCK, 1),
            hy_gain=hy_gain[li][None, :], at_gain=at_gain[li][None, :],
            w_out=w_out[li].astype(BF16), norm2=norm2[li][None, :],
            w_gate_up=w_gate_up[li].astype(BF16), w_down=w_down[li].astype(BF16),
        )
        lw["kf"] = {key: _filter_spectrum(raw, st, {"prompt": x_prompt, "sample": x_sample}[key].shape[1])
                    for key, st in streams.items()}
        y_prompt = _layer(y_prompt, lw, shared, streams["prompt"])
        y_sample = _layer(y_sample, lw, shared, streams["sample"])
    return (y_prompt, y_sample)
```

```python
import functools
import math

import numpy as np
import jax
import jax.numpy as jnp
from jax import lax
from jax.experimental import pallas as pl
from jax.experimental.pallas import tpu as pltpu

F32 = jnp.float32
BF16 = jnp.bfloat16

D_MODEL = 1024
ATT_HEADS = 8
KV_HEADS = 2
HEAD_DIM = 64
GQA_GROUP = ATT_HEADS // KV_HEADS
ATT_W = ATT_HEADS * HEAD_DIM
KV_W = KV_HEADS * HEAD_DIM
WINDOW = 128
BLOCK = 128
N_BUCKETS = 32
MAX_DIST = 128
HY_CH = D_MODEL - ATT_W
FILTER_HIDDEN = 64
N_BANDS = 16
POS_DIM = 1 + 2 * N_BANDS
FAST_DECAY_PCT = 0.3
SLOW_DECAY_PCT = 1.5
DECAY_TARGET = 1e-2
IN_W = 3 * HY_CH + (ATT_HEADS + 2 * KV_HEADS) * HEAD_DIM
D_FF = -(-8 * D_MODEL // (3 * 256)) * 256
EPS = 1e-6

FFT_N2 = 64
LANES = 128
VMEM_LIMIT = 56 * 1024 * 1024

_NT = (((1,), (1,)), ((), ()))


def _cparams(*sem):
    return pltpu.CompilerParams(dimension_semantics=sem, vmem_limit_bytes=VMEM_LIMIT)


def _const_spec(shape):
    nd = len(shape)
    return pl.BlockSpec(shape, lambda *_: (0,) * nd, pipeline_mode=pl.Buffered(1))


def _inproj_kernel(x_ref, g1_ref, w_ref, bq_ref, bk_ref, qg_ref, kg_ref,
                   x0_ref, x1_ref, vh_ref, q_ref, k_ref, v_ref):
    x = x_ref[...]
    xn = x * lax.rsqrt(jnp.mean(x * x, axis=-1, keepdims=True) + EPS) * g1_ref[...]
    xb = xn.astype(BF16)

    def seg(lo, hi):
        return jnp.dot(xb, w_ref[:, lo:hi], preferred_element_type=F32)

    x0_ref[...] = seg(0, HY_CH).astype(BF16)
    x1_ref[...] = seg(HY_CH, 2 * HY_CH).astype(BF16)
    vh_ref[...] = seg(2 * HY_CH, 3 * HY_CH).astype(BF16)

    o = 3 * HY_CH
    q = seg(o, o + ATT_W)
    ms = jnp.dot((q * q).astype(BF16), bq_ref[...], preferred_element_type=F32)
    q_ref[...] = (q * lax.rsqrt(ms + EPS) * qg_ref[...]).astype(BF16)

    k = seg(o + ATT_W, o + ATT_W + KV_W)
    ms = jnp.dot((k * k).astype(BF16), bk_ref[...], preferred_element_type=F32)
    kn = k * lax.rsqrt(ms + EPS) * kg_ref[...]
    v = seg(o + ATT_W + KV_W, IN_W)

    lane = lax.broadcasted_iota(jnp.int32, kn.shape, 1)
    lo_half = lane < HEAD_DIM

    def dup(t):
        tr = pltpu.roll(t, HEAD_DIM, axis=1)
        return jnp.concatenate([jnp.where(lo_half, t, tr), jnp.where(lo_half, tr, t)], axis=1)

    k_ref[...] = dup(kn).astype(BF16)
    v_ref[0] = v.T.astype(BF16)


def _inproj(x2d, g1, w_in_b, bq, bk, qg, kg, tm, seq_len):
    t = x2d.shape[0]
    nt = seq_len // tm
    row = lambda w: pl.BlockSpec((tm, w), lambda i: (i, 0))
    outs = [HY_CH, HY_CH, HY_CH, ATT_W, 2 * KV_W]
    return pl.pallas_call(
        _inproj_kernel,
        grid=(t // tm,),
        in_specs=[row(D_MODEL), _const_spec((1, D_MODEL)), _const_spec((D_MODEL, IN_W)),
                  _const_spec((ATT_W, ATT_W)), _const_spec((KV_W, KV_W)),
                  _const_spec((1, ATT_W)), _const_spec((1, KV_W))],
        out_specs=[row(w) for w in outs] + [pl.BlockSpec((1, KV_W, tm), lambda i: (i // nt, 0, i % nt))],
        out_shape=[jax.ShapeDtypeStruct((t, w), BF16) for w in outs]
        + [jax.ShapeDtypeStruct((t // seq_len, KV_W, seq_len), BF16)],
        compiler_params=_cparams("parallel"),
        name="inproj",
    )(x2d, g1, w_in_b, bq, bk, qg, kg)


def _hypre_kernel(x0_ref, x1_ref, v_ref, x0p_ref, x1p_ref, vp_ref, x0n_ref, x1n_ref, vn_ref,
                  cw_ref, cb_ref, u_ref, x0c_ref, *, halo):
    i = pl.program_id(1)
    first = i == 0
    last = i == pl.num_programs(1) - 1
    tl = x0_ref.shape[1]
    row = lax.broadcasted_iota(jnp.int32, (tl, HY_CH), 0)

    def conv(m_ref, p_ref, n_ref, j):
        m = m_ref[0].astype(F32)
        prev_row = jnp.where(first, 0.0, p_ref[0, halo - 1:halo, :].astype(F32))
        next_row = jnp.where(last, 0.0, n_ref[0, 0:1, :].astype(F32))
        up = jnp.where(row == 0, prev_row, pltpu.roll(m, 1, axis=0))
        dn = jnp.where(row == tl - 1, next_row, pltpu.roll(m, tl - 1, axis=0))
        c = slice(j * HY_CH, (j + 1) * HY_CH)
        return up * cw_ref[0:1, c] + m * cw_ref[1:2, c] + dn * cw_ref[2:3, c] + cb_ref[0:1, c]

    x0c_ref[0] = conv(x0_ref, x0p_ref, x0n_ref, 0).astype(BF16)
    u_ref[0] = (conv(x1_ref, x1p_ref, x1n_ref, 1) * conv(v_ref, vp_ref, vn_ref, 2)).astype(BF16)


def _hypre(x0, x1, vh, conv_w, conv_b, tl):
    b, l, _ = x0.shape
    halo = 16
    nh = tl // halo
    main = pl.BlockSpec((1, tl, HY_CH), lambda bi, i: (bi, i, 0))
    prev = pl.BlockSpec((1, halo, HY_CH), lambda bi, i: (bi, jnp.maximum(i * nh - 1, 0), 0))
    nxt = pl.BlockSpec((1, halo, HY_CH), lambda bi, i: (bi, jnp.minimum((i + 1) * nh, l // halo - 1), 0))
    return pl.pallas_call(
        functools.partial(_hypre_kernel, halo=halo),
        grid=(b, l // tl),
        in_specs=[main, main, main, prev, prev, prev, nxt, nxt, nxt,
                  _const_spec((3, 3 * HY_CH)), _const_spec((1, 3 * HY_CH))],
        out_specs=[main, main],
        out_shape=[jax.ShapeDtypeStruct((b, l, HY_CH), BF16)] * 2,
        compiler_params=_cparams("parallel", "parallel"),
        name="hypre",
    )(x0, x1, vh, x0, x1, vh, x0, x1, vh, conv_w, conv_b)


def _stage_a_kernel(x_ref, f_ref, o_ref):
    nz = x_ref.shape[0]
    x = jnp.concatenate([x_ref[z, 0].astype(BF16) for z in range(nz)], axis=0)
    o_ref[0] = jnp.dot(f_ref[...], x, preferred_element_type=F32).astype(o_ref.dtype)


def _stage_a(x4, fmat, lcc, out_dtype=BF16):
    nz, p, a, lc = x4.shape
    m = fmat.shape[0]
    return pl.pallas_call(
        _stage_a_kernel,
        grid=(p, lc // lcc),
        in_specs=[pl.BlockSpec((nz, 1, a, lcc), lambda pi, j: (0, pi, 0, j)), _const_spec(fmat.shape)],
        out_specs=pl.BlockSpec((1, m, lcc), lambda pi, j: (pi, 0, j)),
        out_shape=jax.ShapeDtypeStruct((p, m, lc), out_dtype),
        compiler_params=_cparams("parallel", "parallel"),
        name="hy_stage_a",
    )(x4, fmat)


def _stage_b_kernel(a_ref, kf_ref, g_ref, gt_ref, o_ref):
    kb = a_ref.shape[2]
    n2 = a_ref.shape[3]
    for j in range(kb):
        x = jnp.concatenate([a_ref[0, 0, j], a_ref[0, 1, j]], axis=0)
        u = jnp.dot(g_ref[j], x, preferred_element_type=F32)
        ur, ui = u[:n2], u[n2:]
        kr, ki = kf_ref[0, j], kf_ref[1, j]
        p = jnp.concatenate([ur * kr - ui * ki, ur * ki + ui * kr], axis=0).astype(BF16)
        v = jnp.dot(gt_ref[j], p, preferred_element_type=F32)
        o_ref[0, 0, j] = v[:n2].astype(o_ref.dtype)
        o_ref[0, 1, j] = v[n2:].astype(o_ref.dtype)


def _stage_b(a5, kf, g, gt, kb):
    p, _, n1, n2, c = a5.shape
    blk = pl.BlockSpec((1, 2, kb, n2, c), lambda i, pi: (pi, 0, i, 0, 0))
    gspec = pl.BlockSpec((kb, 2 * n2, 2 * n2), lambda i, pi: (i, 0, 0))
    return pl.pallas_call(
        _stage_b_kernel,
        grid=(n1 // kb, p),
        in_specs=[blk, pl.BlockSpec((2, kb, n2, c), lambda i, pi: (0, i, 0, 0)), gspec, gspec],
        out_specs=blk,
        out_shape=jax.ShapeDtypeStruct(a5.shape, BF16),
        compiler_params=_cparams("parallel", "parallel"),
        name="hy_stage_b",
    )(a5, kf, g, gt)


def _stage_bf_kernel(a_ref, g_ref, o_ref, *, scale):
    kb = a_ref.shape[2]
    n2 = a_ref.shape[3]
    for j in range(kb):
        x = jnp.concatenate([a_ref[0, 0, j], a_ref[0, 1, j]], axis=0)
        u = jnp.dot(g_ref[j], x, preferred_element_type=F32) * scale
        o_ref[0, j] = u[:n2]
        o_ref[1, j] = u[n2:]


def _stage_bf(a5, g, kb, scale):
    _, _, n1, n2, c = a5.shape
    return pl.pallas_call(
        functools.partial(_stage_bf_kernel, scale=scale),
        grid=(n1 // kb,),
        in_specs=[pl.BlockSpec((1, 2, kb, n2, c), lambda i: (0, 0, i, 0, 0)),
                  pl.BlockSpec((kb, 2 * n2, 2 * n2), lambda i: (i, 0, 0))],
        out_specs=pl.BlockSpec((2, kb, n2, c), lambda i: (0, i, 0, 0)),
        out_shape=jax.ShapeDtypeStruct((2, n1, n2, c), F32),
        compiler_params=_cparams("parallel"),
        name="hy_filter_spectrum",
    )(a5, g)


def _stage_c_kernel(v_ref, f_ref, x0c_ref, u_ref, skip_ref, o_ref):
    nz, _, a, _ = o_ref.shape
    y = jnp.dot(f_ref[...], v_ref[0], preferred_element_type=F32)
    for z in range(nz):
        yz = y[z * a:(z + 1) * a] + skip_ref[...] * u_ref[z, 0].astype(F32)
        o_ref[z, 0] = (x0c_ref[z, 0].astype(F32) * yz).astype(o_ref.dtype)


def _stage_c(v3, fmat, x0c4, u4, skip_t, lcc):
    nz, p, a, lc = u4.shape
    m2 = v3.shape[1]
    blk = pl.BlockSpec((nz, 1, a, lcc), lambda pi, j: (0, pi, 0, j))
    return pl.pallas_call(
        _stage_c_kernel,
        grid=(p, lc // lcc),
        in_specs=[pl.BlockSpec((1, m2, lcc), lambda pi, j: (pi, 0, j)), _const_spec(fmat.shape),
                  blk, blk, pl.BlockSpec((1, lcc), lambda pi, j: (0, 0))],
        out_specs=blk,
        out_shape=jax.ShapeDtypeStruct(u4.shape, BF16),
        compiler_params=_cparams("parallel", "parallel"),
        name="hy_stage_c",
    )(v3, fmat, x0c4, u4, skip_t)


def _filter_kernel(zt_ref, tn_ref, w1t_ref, b1_ref, fr_ref, w2t_ref, b2_ref, w3_ref, dl_ref, o_ref, *, seq_len):
    hi = lax.Precision.HIGHEST
    fr = fr_ref[...]
    h = jnp.sin(fr * (jnp.dot(w1t_ref[...], zt_ref[...], precision=hi, preferred_element_type=F32) + b1_ref[...]))
    h = jnp.sin(fr * (jnp.dot(w2t_ref[...], h, precision=hi, preferred_element_type=F32) + b2_ref[...]))
    taps = jnp.dot(h.T.astype(BF16), w3_ref[...], preferred_element_type=F32)
    tr = taps.shape[0]
    n = pl.program_id(0) * tr + lax.broadcasted_iota(jnp.int32, (tr, HY_CH), 0)
    sel = jnp.where(n < seq_len, taps[:, :HY_CH], jnp.where(n > seq_len, taps[:, HY_CH:], 0.0))
    o_ref[...] = sel * jnp.exp(-tn_ref[...] * dl_ref[...])


def _filter_taps(zfeat, f_w1, f_b1, f_freq, f_w2, f_b2, f_w3, absdelta, seq_len, tr):
    n = zfeat.shape[0]
    cs = lambda a: _const_spec(a.shape)
    args = (zfeat.T, zfeat[:, 0:1], f_w1.T, f_b1.T, f_freq.T, f_w2.T, f_b2.T, f_w3.astype(BF16), absdelta)
    return pl.pallas_call(
        functools.partial(_filter_kernel, seq_len=seq_len),
        grid=(n // tr,),
        in_specs=[pl.BlockSpec((POS_DIM, tr), lambda i: (0, i)), pl.BlockSpec((tr, 1), lambda i: (i, 0))]
        + [cs(a) for a in args[2:]],
        out_specs=pl.BlockSpec((tr, HY_CH), lambda i: (i, 0)),
        out_shape=jax.ShapeDtypeStruct((n, HY_CH), F32),
        compiler_params=_cparams("parallel"),
        name="hy_filter_taps",
    )(*args)


def _bias_kernel(rb_ref, oh_ref, o_ref):
    o_ref[...] = jnp.dot(rb_ref[...], oh_ref[...], precision=lax.Precision.HIGHEST,
                         preferred_element_type=F32)


def _bias_table(rel_bias):
    i = jnp.arange(BLOCK)[:, None]
    j = jnp.arange(3 * BLOCK)[None, :]
    rel = j - BLOCK - i
    nb2 = N_BUCKETS // 2
    max_exact = nb2 // 2
    n = jnp.abs(rel)
    large = max_exact + (jnp.log(jnp.maximum(n, 1).astype(F32) / max_exact)
                         / math.log(MAX_DIST / max_exact) * (nb2 - max_exact)).astype(jnp.int32)
    large = jnp.minimum(large, nb2 - 1)
    bucket = jnp.where(rel > 0, nb2, 0) + jnp.where(n < max_exact, n, large)
    onehot = (bucket.reshape(1, -1) == jnp.arange(N_BUCKETS)[:, None]).astype(F32)
    cols = onehot.shape[1]
    tc = cols // 4
    table = pl.pallas_call(
        _bias_kernel,
        grid=(4,),
        in_specs=[_const_spec((ATT_HEADS, N_BUCKETS)), pl.BlockSpec((N_BUCKETS, tc), lambda c: (0, c))],
        out_specs=pl.BlockSpec((ATT_HEADS, tc), lambda c: (0, c)),
        out_shape=jax.ShapeDtypeStruct((ATT_HEADS, cols), F32),
        compiler_params=_cparams("parallel"),
        name="att_bias_table",
    )(rel_bias.T, onehot)
    table = table.reshape(ATT_HEADS, BLOCK, 3 * BLOCK)
    table = jnp.where((n <= WINDOW)[None], table, -jnp.inf)
    table = table.reshape(KV_HEADS, GQA_GROUP, BLOCK, 3 * BLOCK)
    return table.transpose(0, 3, 1, 2).reshape(KV_HEADS, 3 * BLOCK, GQA_GROUP * BLOCK)


def _attn_kernel(q_ref, km_ref, kp_ref, kn_ref, vm_ref, vp_ref, vn_ref, bias_ref, sink_ref, o_ref, *, nsub):
    i = pl.program_id(1)
    neg = -jnp.inf
    pen_first = jnp.where(i == 0, neg, 0.0)
    pen_last = jnp.where(i == pl.num_programs(1) - 1, neg, 0.0)
    kwin = jnp.concatenate([kp_ref[0], km_ref[0], kn_ref[0]], axis=0)
    vwin = jnp.concatenate([vp_ref[0], vm_ref[0], vn_ref[0]], axis=1)
    lane = lax.broadcasted_iota(jnp.int32, (BLOCK, LANES), 1)
    lo_half = lane < HEAD_DIM
    ones = jnp.ones((SUM_ROWS, 3 * BLOCK), BF16)
    for s in range(nsub):
        qs = q_ref[0, s * BLOCK:(s + 1) * BLOCK, :]
        for g in range(KV_HEADS):
            rows = []
            for pr in range(2):
                qp = qs[:, (2 * g + pr) * LANES:(2 * g + pr + 1) * LANES]
                rows.append(jnp.where(lo_half, qp, jnp.zeros_like(qp)))
                rows.append(jnp.where(lo_half, jnp.zeros_like(qp), qp))
            qg = jnp.concatenate(rows, axis=0)
            kg = kwin[s * BLOCK:(s + 3) * BLOCK, g * LANES:(g + 1) * LANES]
            sc = lax.dot_general(kg, qg, _NT, preferred_element_type=F32)
            sc = sc + bias_ref[g]
            if s == 0:
                sc = jnp.concatenate([sc[:BLOCK] + pen_first, sc[BLOCK:]], axis=0)
            if s == nsub - 1:
                sc = jnp.concatenate([sc[:2 * BLOCK], sc[2 * BLOCK:] + pen_last], axis=0)
            sk = sink_ref[g]
            m = jnp.maximum(jnp.max(sc, axis=0, keepdims=True), sk)
            p = jnp.exp(sc - m).astype(BF16)
            vg = jnp.concatenate([vwin[g * HEAD_DIM:(g + 1) * HEAD_DIM, s * BLOCK:(s + 3) * BLOCK], ones],
                                 axis=0)
            o = jnp.dot(vg, p, preferred_element_type=F32)
            den = o[HEAD_DIM:HEAD_DIM + 1] + jnp.exp(sk - m)
            on = o[:HEAD_DIM] / den
            for pr in range(2):
                pair = jnp.concatenate([on[:, (2 * pr) * BLOCK:(2 * pr + 1) * BLOCK],
                                        on[:, (2 * pr + 1) * BLOCK:(2 * pr + 2) * BLOCK]], axis=0)
                o_ref[0, s * BLOCK:(s + 1) * BLOCK, (2 * g + pr) * LANES:(2 * g + pr + 1) * LANES] = (
                    pair.T.astype(o_ref.dtype))


SUM_ROWS = 64


def _attention(q, kd, vt, bias_t, sink_t, tq):
    b, l, _ = q.shape
    nsub = tq // BLOCK
    nblk = l // BLOCK
    kw = 2 * KV_W
    main = lambda w: pl.BlockSpec((1, tq, w), lambda bi, i: (bi, i, 0))
    prev = pl.BlockSpec((1, BLOCK, kw), lambda bi, i: (bi, jnp.maximum(i * nsub - 1, 0), 0))
    nxt = pl.BlockSpec((1, BLOCK, kw), lambda bi, i: (bi, jnp.minimum((i + 1) * nsub, nblk - 1), 0))
    vmain = pl.BlockSpec((1, KV_W, tq), lambda bi, i: (bi, 0, i))
    vprev = pl.BlockSpec((1, KV_W, BLOCK), lambda bi, i: (bi, 0, jnp.maximum(i * nsub - 1, 0)))
    vnxt = pl.BlockSpec((1, KV_W, BLOCK), lambda bi, i: (bi, 0, jnp.minimum((i + 1) * nsub, nblk - 1)))
    return pl.pallas_call(
        functools.partial(_attn_kernel, nsub=nsub),
        grid=(b, l // tq),
        in_specs=[main(ATT_W), main(kw), prev, nxt, vmain, vprev, vnxt,
                  _const_spec(bias_t.shape), _const_spec(sink_t.shape)],
        out_specs=main(ATT_W),
        out_shape=jax.ShapeDtypeStruct((b, l, ATT_W), BF16),
        compiler_params=_cparams("parallel", "parallel"),
        name="window_attn",
    )(q, kd, kd, kd, vt, vt, vt, bias_t, sink_t)


FF_CHUNK = 256


def _ffn_kernel(x_ref, yh_ref, ya_ref, hg_ref, ag_ref, wo_ref, g2_ref, wgu_ref, wd_ref, o_ref, act_ref):
    def rms(t, g):
        return (t * lax.rsqrt(jnp.mean(t * t, axis=-1, keepdims=True) + EPS) * g).astype(BF16)

    mixed = jnp.concatenate([rms(yh_ref[...].astype(F32), hg_ref[...]),
                             rms(ya_ref[...].astype(F32), ag_ref[...])], axis=1)
    h = x_ref[...] + jnp.dot(mixed, wo_ref[...], preferred_element_type=F32)
    hn = rms(h, g2_ref[...])
    for c in range(D_FF // FF_CHUNK):
        lo = c * FF_CHUNK
        gate = jnp.dot(hn, wgu_ref[:, lo:lo + FF_CHUNK], preferred_element_type=F32)
        up = jnp.dot(hn, wgu_ref[:, D_FF + lo:D_FF + lo + FF_CHUNK], preferred_element_type=F32)
        act_ref[:, lo:lo + FF_CHUNK] = (gate / (1.0 + jnp.exp(-gate)) * up).astype(BF16)
    o_ref[...] = h + jnp.dot(act_ref[...], wd_ref[...], preferred_element_type=F32)


def _ffn(x2d, yh, ya, hg, ag, wo_b, g2, wgu_b, wd_b, tm):
    t = x2d.shape[0]
    row = lambda w: pl.BlockSpec((tm, w), lambda i: (i, 0))
    return pl.pallas_call(
        _ffn_kernel,
        grid=(t // tm,),
        in_specs=[row(D_MODEL), row(HY_CH), row(ATT_W), _const_spec((1, HY_CH)), _const_spec((1, ATT_W)),
                  _const_spec((D_MODEL, D_MODEL)), _const_spec((1, D_MODEL)),
                  _const_spec((D_MODEL, 2 * D_FF)), _const_spec((D_FF, D_MODEL))],
        out_specs=row(D_MODEL),
        out_shape=jax.ShapeDtypeStruct((t, D_MODEL), F32),
        scratch_shapes=[pltpu.VMEM((tm, D_FF), BF16)],
        compiler_params=_cparams("parallel"),
        name="outproj_swiglu",
    )(x2d, yh, ya, hg, ag, wo_b, g2, wgu_b, wd_b)


def _dft_tables(seq_len, paired):
    n = 2 * seq_len
    n1 = n // FFT_N2
    a = n1 // 2
    kk = np.arange(n1)[:, None]
    th_half = 2.0 * np.pi * ((kk * np.arange(a)[None, :]) % n1) / n1
    c, s = np.cos(th_half), np.sin(th_half)
    if paired:
        fa = np.block([[c, s], [-s, c]])
        fc = np.block([[c.T, -s.T], [s.T, c.T]])
    else:
        fa = np.concatenate([c, -s], axis=0)
        fc = np.concatenate([c.T, -s.T], axis=1)
    th_full = 2.0 * np.pi * ((kk * np.arange(n1)[None, :]) % n1) / n1
    faf = np.concatenate([np.cos(th_full), -np.sin(th_full)], axis=0)
    to = lambda m: jnp.asarray(m.astype(np.float32)).astype(BF16)
    return to(fa), to(fc), to(faf)


def _twiddle_tables(seq_len):
    n = 2 * seq_len
    n1 = n // FFT_N2
    k1 = jnp.arange(n1, dtype=jnp.int32)[:, None, None]
    k2 = jnp.arange(FFT_N2, dtype=jnp.int32)[None, :, None]
    n2 = jnp.arange(FFT_N2, dtype=jnp.int32)[None, None, :]
    m = (n2 * (k2 * n1 + k1)) % n
    ang = m.astype(F32) * (-2.0 * math.pi / n)
    gr, gi = jnp.cos(ang), jnp.sin(ang)
    g = jnp.concatenate([jnp.concatenate([gr, -gi], axis=2), jnp.concatenate([gi, gr], axis=2)], axis=1)
    return g.astype(BF16), jnp.swapaxes(g, 1, 2).astype(BF16)


def _filter_features(seq_len):
    n = np.arange(2 * seq_len)
    t = np.where(n < seq_len, n, 2 * seq_len - n).astype(np.float32)[:, None]
    t_idx = jnp.asarray(t)
    t_norm = t_idx / max(seq_len - 1, 1)
    bands = jnp.linspace(1e-4, N_BANDS - 1, N_BANDS, dtype=F32)
    w = (2.0 * math.pi) * t_idx * bands[None, :] / seq_len
    return jnp.concatenate([t_norm, jnp.cos(w), -jnp.sin(w)], axis=-1)


def _abs_deltas():
    min_decay = math.log(DECAY_TARGET) / FAST_DECAY_PCT
    max_decay = math.log(DECAY_TARGET) / SLOW_DECAY_PCT
    return jnp.abs(jnp.linspace(min_decay, max_decay, HY_CH, dtype=F32))[None, :]


def _pick(n, target):
    t = min(n, target)
    while n % t:
        t //= 2
    return t


def _layer(x, lw, shared, st):
    b, l, _ = x.shape
    t = b * l
    x2d = x.reshape(t, D_MODEL)
    tm = _pick(t, 512)
    x0, x1, vh, q, kd, vt = _inproj(x2d, lw["norm1"], lw["w_in"], shared["bq"], shared["bk"],
                                    lw["qg"], lw["kg"], tm, l)
    r3 = lambda a: a.reshape(b, l, a.shape[-1])

    u, x0c = _hypre(r3(x0), r3(x1), r3(vh), lw["conv_w"], lw["conv_b"], _pick(l, 2048))
    n1 = 2 * l // FFT_N2
    a = n1 // 2
    lc = FFT_N2 * HY_CH
    nz = st["nz"]
    p = b // nz
    u4 = u.reshape(nz, p, a, lc)
    x0c4 = x0c.reshape(nz, p, a, lc)
    lcc = _pick(lc, max(LANES, (1 << 20) // (2 * n1)))
    ah = _stage_a(u4, st["fa"], lcc)
    kb = _pick(n1, 8)
    vv = _stage_b(ah.reshape(p, 2, n1, FFT_N2, HY_CH), lw["kf"][st["key"]], st["g"], st["gt"], kb)
    yh = _stage_c(vv.reshape(p, 2 * n1, lc), st["fc"], x0c4, u4, lw["skip_t"], lcc)
    yh = yh.reshape(t, HY_CH)

    ya = _attention(r3(q), r3(kd), vt, shared["bias"], lw["sink_t"], _pick(l, 512))
    ya = ya.reshape(t, ATT_W)

    out = _ffn(x2d, yh, ya, lw["hy_gain"], lw["at_gain"], lw["w_out"], lw["norm2"],
               lw["w_gate_up"], lw["w_down"], tm)
    return out.reshape(b, l, D_MODEL)


def _filter_spectrum(lw_raw, st, seq_len):
    n = 2 * seq_len
    n1 = n // FFT_N2
    lc = FFT_N2 * HY_CH
    taps = _filter_taps(st["zfeat"], lw_raw["f_w1"], lw_raw["f_b1"], lw_raw["f_freq"], lw_raw["f_w2"],
                        lw_raw["f_b2"], lw_raw["f_w3"], st["absdelta"], seq_len, _pick(n, 512))
    lcc = _pick(lc, max(LANES, (1 << 20) // (2 * n1)))
    ah = _stage_a(taps.reshape(1, 1, n1, lc), st["faf"], lcc)
    return _stage_bf(ah.reshape(1, 2, n1, FFT_N2, HY_CH), st["g"], _pick(n1, 8), 1.0 / n)


def kernel(x_prompt, x_sample, norm1, w_in, conv_w, conv_b, f_w1, f_b1, f_freq, f_w2, f_b2, f_w3,
           hy_skip, q_gain, k_gain, sink, rel_bias, hy_gain, at_gain, w_out, norm2, w_gate_up, w_down):
    depth = norm1.shape[0]
    streams = {}
    for key, x in (("prompt", x_prompt), ("sample", x_sample)):
        b, l, _ = x.shape
        paired = b % 2 == 0
        fa, fc, faf = _dft_tables(l, paired)
        g, gt = _twiddle_tables(l)
        streams[key] = dict(key=key, nz=2 if paired else 1, fa=fa, fc=fc, faf=faf, g=g, gt=gt,
                            zfeat=_filter_features(l), absdelta=_abs_deltas())

    head_id = np.arange(ATT_W) // HEAD_DIM
    shared = dict(
        bq=jnp.asarray((head_id[:, None] == head_id[None, :]).astype(np.float32) / HEAD_DIM).astype(BF16),
        bk=jnp.asarray((head_id[:KV_W, None] == head_id[None, :KV_W]).astype(np.float32) / HEAD_DIM).astype(BF16),
        bias=_bias_table(rel_bias),
    )

    y_prompt, y_sample = x_prompt, x_sample
    for li in range(depth):
        raw = dict(f_w1=f_w1[li], f_b1=f_b1[li][None, :], f_freq=f_freq[li][None, :], f_w2=f_w2[li],
                   f_b2=f_b2[li][None, :], f_w3=f_w3[li])
        lw = dict(
            norm1=norm1[li][None, :], w_in=w_in[li].astype(BF16),
            qg=jnp.tile(q_gain[li], ATT_HEADS)[None, :] * (HEAD_DIM ** -0.5),
            kg=jnp.tile(k_gain[li], KV_HEADS)[None, :],
            conv_w=conv_w[li], conv_b=conv_b[li][None, :],
            skip_t=jnp.tile(hy_skip[li], FFT_N2)[None, :],
            sink_t=jnp.repeat(sink[li], BLOCK).reshape(KV_HEADS, 1, GQA_GROUP * BLOCK),
            hy_gain=hy_gain[li][None, :], at_gain=at_gain[li][None, :],
            w_out=w_out[li].astype(BF16), norm2=norm2[li][None, :],
            w_gate_up=w_gate_up[li].astype(BF16), w_down=w_down[li].astype(BF16),
        )
        lw["kf"] = {key: _filter_spectrum(raw, st, {"prompt": x_prompt, "sample": x_sample}[key].shape[1])
                    for key, st in streams.items()}
        y_prompt = _layer(y_prompt, lw, shared, streams["prompt"])
        y_sample = _layer(y_sample, lw, shared, streams["sample"])
    return (y_prompt, y_sample)
```

```python
import functools
import math

import numpy as np
import jax
import jax.numpy as jnp
from jax import lax
from jax.experimental import pallas as pl
from jax.experimental.pallas import tpu as pltpu

F32 = jnp.float32
BF16 = jnp.bfloat16

D_MODEL = 1024
ATT_HEADS = 8
KV_HEADS = 2
HEAD_DIM = 64
GQA_GROUP = ATT_HEADS // KV_HEADS
ATT_W = ATT_HEADS * HEAD_DIM
KV_W = KV_HEADS * HEAD_DIM
WINDOW = 128
BLOCK = 128
N_BUCKETS = 32
MAX_DIST = 128
HY_CH = D_MODEL - ATT_W
FILTER_HIDDEN = 64
N_BANDS = 16
POS_DIM = 1 + 2 * N_BANDS
FAST_DECAY_PCT = 0.3
SLOW_DECAY_PCT = 1.5
DECAY_TARGET = 1e-2
IN_W = 3 * HY_CH + (ATT_HEADS + 2 * KV_HEADS) * HEAD_DIM
D_FF = -(-8 * D_MODEL // (3 * 256)) * 256
EPS = 1e-6

FFT_N2 = 64
LANES = 128
N2_TILE = 8
SLABS = HY_CH // LANES
STAGE_MB = 512
VMEM_LIMIT = 56 * 1024 * 1024

_NT = (((1,), (1,)), ((), ()))


def _cparams(*sem):
    return pltpu.CompilerParams(dimension_semantics=sem, vmem_limit_bytes=VMEM_LIMIT)


def _const_spec(shape):
    nd = len(shape)
    return pl.BlockSpec(shape, lambda *_: (0,) * nd, pipeline_mode=pl.Buffered(1))


def _inproj_kernel(x_ref, g1_ref, w_ref, bq_ref, bk_ref, qg_ref, kg_ref,
                   x0_ref, x1_ref, vh_ref, q_ref, k_ref, v_ref):
    x = x_ref[...]
    xn = x * lax.rsqrt(jnp.mean(x * x, axis=-1, keepdims=True) + EPS) * g1_ref[...]
    xb = xn.astype(BF16)

    def seg(lo, hi):
        return jnp.dot(xb, w_ref[:, lo:hi], preferred_element_type=F32)

    x0_ref[...] = seg(0, HY_CH).astype(BF16)
    x1_ref[...] = seg(HY_CH, 2 * HY_CH).astype(BF16)
    vh_ref[...] = seg(2 * HY_CH, 3 * HY_CH).astype(BF16)

    o = 3 * HY_CH
    q = seg(o, o + ATT_W)
    ms = jnp.dot((q * q).astype(BF16), bq_ref[...], preferred_element_type=F32)
    q_ref[...] = (q * lax.rsqrt(ms + EPS) * qg_ref[...]).astype(BF16)

    k = seg(o + ATT_W, o + ATT_W + KV_W)
    ms = jnp.dot((k * k).astype(BF16), bk_ref[...], preferred_element_type=F32)
    kn = k * lax.rsqrt(ms + EPS) * kg_ref[...]
    v = seg(o + ATT_W + KV_W, IN_W)

    lane = lax.broadcasted_iota(jnp.int32, kn.shape, 1)
    lo_half = lane < HEAD_DIM

    def dup(t):
        tr = pltpu.roll(t, HEAD_DIM, axis=1)
        return jnp.concatenate([jnp.where(lo_half, t, tr), jnp.where(lo_half, tr, t)], axis=1)

    k_ref[...] = dup(kn).astype(BF16)
    v_ref[0] = v.T.astype(BF16)


def _inproj(x2d, g1, w_in_b, bq, bk, qg, kg, tm, seq_len):
    t = x2d.shape[0]
    nt = seq_len // tm
    row = lambda w: pl.BlockSpec((tm, w), lambda i: (i, 0))
    outs = [HY_CH, HY_CH, HY_CH, ATT_W, 2 * KV_W]
    return pl.pallas_call(
        _inproj_kernel,
        grid=(t // tm,),
        in_specs=[row(D_MODEL), _const_spec((1, D_MODEL)), _const_spec((D_MODEL, IN_W)),
                  _const_spec((ATT_W, ATT_W)), _const_spec((KV_W, KV_W)),
                  _const_spec((1, ATT_W)), _const_spec((1, KV_W))],
        out_specs=[row(w) for w in outs] + [pl.BlockSpec((1, KV_W, tm), lambda i: (i // nt, 0, i % nt))],
        out_shape=[jax.ShapeDtypeStruct((t, w), BF16) for w in outs]
        + [jax.ShapeDtypeStruct((t // seq_len, KV_W, seq_len), BF16)],
        compiler_params=_cparams("parallel"),
        name="inproj",
    )(x2d, g1, w_in_b, bq, bk, qg, kg)


def _hypre_kernel(x0_ref, x1_ref, v_ref, x0p_ref, x1p_ref, vp_ref, x0n_ref, x1n_ref, vn_ref,
                  cw_ref, cb_ref, u_ref, x0c_ref, *, halo):
    i = pl.program_id(1)
    first = i == 0
    last = i == pl.num_programs(1) - 1
    tl = x0_ref.shape[1]
    row = lax.broadcasted_iota(jnp.int32, (tl, HY_CH), 0)

    def conv(m_ref, p_ref, n_ref, j):
        m = m_ref[0].astype(F32)
        prev_row = jnp.where(first, 0.0, p_ref[0, halo - 1:halo, :].astype(F32))
        next_row = jnp.where(last, 0.0, n_ref[0, 0:1, :].astype(F32))
        up = jnp.where(row == 0, prev_row, pltpu.roll(m, 1, axis=0))
        dn = jnp.where(row == tl - 1, next_row, pltpu.roll(m, tl - 1, axis=0))
        c = slice(j * HY_CH, (j + 1) * HY_CH)
        return up * cw_ref[0:1, c] + m * cw_ref[1:2, c] + dn * cw_ref[2:3, c] + cb_ref[0:1, c]

    x0c_ref[0] = conv(x0_ref, x0p_ref, x0n_ref, 0)
    u_ref[0] = conv(x1_ref, x1p_ref, x1n_ref, 1) * conv(v_ref, vp_ref, vn_ref, 2)


def _hypre(x0, x1, vh, conv_w, conv_b, tl):
    b, l, _ = x0.shape
    halo = 16
    nh = tl // halo
    main = pl.BlockSpec((1, tl, HY_CH), lambda bi, i: (bi, i, 0))
    prev = pl.BlockSpec((1, halo, HY_CH), lambda bi, i: (bi, jnp.maximum(i * nh - 1, 0), 0))
    nxt = pl.BlockSpec((1, halo, HY_CH), lambda bi, i: (bi, jnp.minimum((i + 1) * nh, l // halo - 1), 0))
    return pl.pallas_call(
        functools.partial(_hypre_kernel, halo=halo),
        grid=(b, l // tl),
        in_specs=[main, main, main, prev, prev, prev, nxt, nxt, nxt,
                  _const_spec((3, 3 * HY_CH)), _const_spec((1, 3 * HY_CH))],
        out_specs=[main, main],
        out_shape=[jax.ShapeDtypeStruct((b, l, HY_CH), F32)] * 2,
        compiler_params=_cparams("parallel", "parallel"),
        name="hypre",
    )(x0, x1, vh, x0, x1, vh, x0, x1, vh, conv_w, conv_b)


def _stage_a_kernel(*refs, nz, a, mb):
    x_refs, f_ref, o_ref = refs[:SLABS], refs[SLABS], refs[SLABS + 1]
    xs = [r.reshape(nz * a * N2_TILE, LANES) for r in x_refs]
    o2 = o_ref.reshape(SLABS * mb * N2_TILE, LANES)
    f = f_ref[...]
    for s in range(N2_TILE):
        x = jnp.concatenate(
            [jnp.concatenate([xs[c][pl.ds(z * a * N2_TILE + s, a, stride=N2_TILE), :] for z in range(nz)], axis=0)
             for c in range(SLABS)], axis=1).astype(BF16)
        r = jnp.dot(f, x, preferred_element_type=F32)
        for c in range(SLABS):
            o2[pl.ds(c * mb * N2_TILE + s, mb, stride=N2_TILE), :] = r[:, c * LANES:(c + 1) * LANES]


def _stage_a(x5, fmat, mb):
    nz, p, a, n2, _ = x5.shape
    m = fmat.shape[0]
    xspec = lambda c: pl.BlockSpec((nz, 1, a, N2_TILE, LANES), lambda pi, j, mi: (0, pi, 0, j, c))
    return pl.pallas_call(
        functools.partial(_stage_a_kernel, nz=nz, a=a, mb=mb),
        grid=(p, n2 // N2_TILE, m // mb),
        in_specs=[xspec(c) for c in range(SLABS)] + [pl.BlockSpec((mb, nz * a), lambda pi, j, mi: (mi, 0))],
        out_specs=pl.BlockSpec((1, SLABS, mb, N2_TILE, LANES), lambda pi, j, mi: (pi, 0, mi, j, 0)),
        out_shape=jax.ShapeDtypeStruct((p, SLABS, m, n2, LANES), F32),
        compiler_params=_cparams("parallel", "parallel", "parallel"),
        name="hy_stage_a",
    )(*([x5] * SLABS), fmat)


def _slab_rows(a_ref, j):
    return jnp.concatenate([jnp.concatenate([a_ref[0, c, 0, j], a_ref[0, c, 1, j]], axis=0)
                            for c in range(SLABS)], axis=1).astype(BF16)


def _stage_b_kernel(a_ref, kf_ref, g_ref, gt_ref, o_ref):
    kb, n2 = a_ref.shape[3], a_ref.shape[4]
    for j in range(kb):
        u = jnp.dot(g_ref[j], _slab_rows(a_ref, j), preferred_element_type=F32)
        ur, ui = u[:n2], u[n2:]
        kr, ki = kf_ref[0, j], kf_ref[1, j]
        p = jnp.concatenate([ur * kr - ui * ki, ur * ki + ui * kr], axis=0).astype(BF16)
        v = jnp.dot(gt_ref[j], p, preferred_element_type=F32)
        for c in range(SLABS):
            o_ref[0, c, 0, j] = v[:n2, c * LANES:(c + 1) * LANES]
            o_ref[0, c, 1, j] = v[n2:, c * LANES:(c + 1) * LANES]


def _stage_b(a6, kf, g, gt, kb):
    p, _, _, n1, n2, _ = a6.shape
    blk = pl.BlockSpec((1, SLABS, 2, kb, n2, LANES), lambda i, pi: (pi, 0, 0, i, 0, 0))
    gspec = pl.BlockSpec((kb, 2 * n2, 2 * n2), lambda i, pi: (i, 0, 0))
    return pl.pallas_call(
        _stage_b_kernel,
        grid=(n1 // kb, p),
        in_specs=[blk, pl.BlockSpec((2, kb, n2, HY_CH), lambda i, pi: (0, i, 0, 0)), gspec, gspec],
        out_specs=blk,
        out_shape=jax.ShapeDtypeStruct(a6.shape, F32),
        compiler_params=_cparams("parallel", "parallel"),
        name="hy_stage_b",
    )(a6, kf, g, gt)


def _stage_bf_kernel(a_ref, g_ref, o_ref, *, scale):
    kb, n2 = a_ref.shape[3], a_ref.shape[4]
    for j in range(kb):
        u = jnp.dot(g_ref[j], _slab_rows(a_ref, j), preferred_element_type=F32) * scale
        o_ref[0, j] = u[:n2]
        o_ref[1, j] = u[n2:]


def _stage_bf(a6, g, kb, scale):
    _, _, _, n1, n2, _ = a6.shape
    return pl.pallas_call(
        functools.partial(_stage_bf_kernel, scale=scale),
        grid=(n1 // kb,),
        in_specs=[pl.BlockSpec((1, SLABS, 2, kb, n2, LANES), lambda i: (0, 0, 0, i, 0, 0)),
                  pl.BlockSpec((kb, 2 * n2, 2 * n2), lambda i: (i, 0, 0))],
        out_specs=pl.BlockSpec((2, kb, n2, HY_CH), lambda i: (0, i, 0, 0)),
        out_shape=jax.ShapeDtypeStruct((2, n1, n2, HY_CH), F32),
        compiler_params=_cparams("parallel"),
        name="hy_filter_spectrum",
    )(a6, g)


def _stage_c_kernel(*refs, nz, a, mb):
    v_ref, f_ref = refs[0], refs[1]
    x0c_refs, u_refs = refs[2:2 + SLABS], refs[2 + SLABS:2 + 2 * SLABS]
    skip_ref, o_ref, acc_ref = refs[2 + 2 * SLABS:]
    mk = pl.program_id(2)
    v2 = v_ref.reshape(SLABS * mb * N2_TILE, LANES)

    @pl.when(mk == 0)
    def _():
        acc_ref[...] = jnp.zeros_like(acc_ref)

    for s in range(N2_TILE):
        vs = jnp.concatenate([v2[pl.ds(c * mb * N2_TILE + s, mb, stride=N2_TILE), :] for c in range(SLABS)],
                             axis=1).astype(BF16)
        acc_ref[s] += jnp.dot(f_ref[...], vs, preferred_element_type=F32)

    @pl.when(mk == pl.num_programs(2) - 1)
    def _():
        o2 = o_ref.reshape(nz * SLABS * a * N2_TILE, LANES)
        x2 = [r.reshape(nz * a * N2_TILE, LANES) for r in x0c_refs]
        u2 = [r.reshape(nz * a * N2_TILE, LANES) for r in u_refs]
        for s in range(N2_TILE):
            y = acc_ref[s]
            for z in range(nz):
                for c in range(SLABS):
                    rows = pl.ds(z * a * N2_TILE + s, a, stride=N2_TILE)
                    yc = y[z * a:(z + 1) * a, c * LANES:(c + 1) * LANES]
                    val = x2[c][rows, :] * (yc + skip_ref[0:1, c * LANES:(c + 1) * LANES] * u2[c][rows, :])
                    o2[pl.ds((z * SLABS + c) * a * N2_TILE + s, a, stride=N2_TILE), :] = val


def _stage_c(v5, fmat, x0c5, u5, skip, mb):
    nz, p, a, n2, _ = u5.shape
    m = v5.shape[2]
    xspec = lambda c: pl.BlockSpec((nz, 1, a, N2_TILE, LANES), lambda pi, j, mk: (0, pi, 0, j, c))
    return pl.pallas_call(
        functools.partial(_stage_c_kernel, nz=nz, a=a, mb=mb),
        grid=(p, n2 // N2_TILE, m // mb),
        in_specs=[pl.BlockSpec((1, SLABS, mb, N2_TILE, LANES), lambda pi, j, mk: (pi, 0, mk, j, 0)),
                  pl.BlockSpec((nz * a, mb), lambda pi, j, mk: (0, mk))]
        + [xspec(c) for c in range(SLABS)] * 2 + [_const_spec((1, HY_CH))],
        out_specs=pl.BlockSpec((nz, 1, SLABS, a, N2_TILE, LANES), lambda pi, j, mk: (0, pi, 0, 0, j, 0)),
        out_shape=jax.ShapeDtypeStruct((nz, p, SLABS, a, n2, LANES), F32),
        scratch_shapes=[pltpu.VMEM((N2_TILE, nz * a, HY_CH), F32)],
        compiler_params=_cparams("parallel", "parallel", "arbitrary"),
        name="hy_stage_c",
    )(v5, fmat, *([x0c5] * SLABS), *([u5] * SLABS), skip)


def _filter_kernel(zt_ref, tn_ref, w1t_ref, b1_ref, fr_ref, w2t_ref, b2_ref, w3_ref, dl_ref, o_ref, *, seq_len):
    hi = lax.Precision.HIGHEST
    fr = fr_ref[...]
    h = jnp.sin(fr * (jnp.dot(w1t_ref[...], zt_ref[...], precision=hi, preferred_element_type=F32) + b1_ref[...]))
    h = jnp.sin(fr * (jnp.dot(w2t_ref[...], h, precision=hi, preferred_element_type=F32) + b2_ref[...]))
    taps = jnp.dot(h.T.astype(BF16), w3_ref[...], preferred_element_type=F32)
    tr = taps.shape[0]
    n = pl.program_id(0) * tr + lax.broadcasted_iota(jnp.int32, (tr, HY_CH), 0)
    sel = jnp.where(n < seq_len, taps[:, :HY_CH], jnp.where(n > seq_len, taps[:, HY_CH:], 0.0))
    o_ref[...] = sel * jnp.exp(-tn_ref[...] * dl_ref[...])


def _filter_taps(zfeat, f_w1, f_b1, f_freq, f_w2, f_b2, f_w3, absdelta, seq_len, tr):
    n = zfeat.shape[0]
    cs = lambda a: _const_spec(a.shape)
    args = (zfeat.T, zfeat[:, 0:1], f_w1.T, f_b1.T, f_freq.T, f_w2.T, f_b2.T, f_w3.astype(BF16), absdelta)
    return pl.pallas_call(
        functools.partial(_filter_kernel, seq_len=seq_len),
        grid=(n // tr,),
        in_specs=[pl.BlockSpec((POS_DIM, tr), lambda i: (0, i)), pl.BlockSpec((tr, 1), lambda i: (i, 0))]
        + [cs(a) for a in args[2:]],
        out_specs=pl.BlockSpec((tr, HY_CH), lambda i: (i, 0)),
        out_shape=jax.ShapeDtypeStruct((n, HY_CH), F32),
        compiler_params=_cparams("parallel"),
        name="hy_filter_taps",
    )(*args)


def _bias_kernel(rb_ref, oh_ref, o_ref):
    o_ref[...] = jnp.dot(rb_ref[...], oh_ref[...], precision=lax.Precision.HIGHEST,
                         preferred_element_type=F32)


def _bias_table(rel_bias):
    i = jnp.arange(BLOCK)[:, None]
    j = jnp.arange(3 * BLOCK)[None, :]
    rel = j - BLOCK - i
    nb2 = N_BUCKETS // 2
    max_exact = nb2 // 2
    n = jnp.abs(rel)
    large = max_exact + (jnp.log(jnp.maximum(n, 1).astype(F32) / max_exact)
                         / math.log(MAX_DIST / max_exact) * (nb2 - max_exact)).astype(jnp.int32)
    large = jnp.minimum(large, nb2 - 1)
    bucket = jnp.where(rel > 0, nb2, 0) + jnp.where(n < max_exact, n, large)
    onehot = (bucket.reshape(1, -1) == jnp.arange(N_BUCKETS)[:, None]).astype(F32)
    cols = onehot.shape[1]
    tc = cols // 4
    table = pl.pallas_call(
        _bias_kernel,
        grid=(4,),
        in_specs=[_const_spec((ATT_HEADS, N_BUCKETS)), pl.BlockSpec((N_BUCKETS, tc), lambda c: (0, c))],
        out_specs=pl.BlockSpec((ATT_HEADS, tc), lambda c: (0, c)),
        out_shape=jax.ShapeDtypeStruct((ATT_HEADS, cols), F32),
        compiler_params=_cparams("parallel"),
        name="att_bias_table",
    )(rel_bias.T, onehot)
    table = table.reshape(ATT_HEADS, BLOCK, 3 * BLOCK)
    table = jnp.where((n <= WINDOW)[None], table, -jnp.inf)
    table = table.reshape(KV_HEADS, GQA_GROUP, BLOCK, 3 * BLOCK)
    return table.transpose(0, 3, 1, 2).reshape(KV_HEADS, 3 * BLOCK, GQA_GROUP * BLOCK)


def _attn_kernel(q_ref, km_ref, kp_ref, kn_ref, vm_ref, vp_ref, vn_ref, bias_ref, sink_ref, o_ref, *, nsub):
    i = pl.program_id(1)
    neg = -jnp.inf
    pen_first = jnp.where(i == 0, neg, 0.0)
    pen_last = jnp.where(i == pl.num_programs(1) - 1, neg, 0.0)
    kwin = jnp.concatenate([kp_ref[0], km_ref[0], kn_ref[0]], axis=0)
    vwin = jnp.concatenate([vp_ref[0], vm_ref[0], vn_ref[0]], axis=1)
    lane = lax.broadcasted_iota(jnp.int32, (BLOCK, LANES), 1)
    lo_half = lane < HEAD_DIM
    ones = jnp.ones((SUM_ROWS, 3 * BLOCK), BF16)
    for s in range(nsub):
        qs = q_ref[0, s * BLOCK:(s + 1) * BLOCK, :]
        for g in range(KV_HEADS):
            rows = []
            for pr in range(2):
                qp = qs[:, (2 * g + pr) * LANES:(2 * g + pr + 1) * LANES]
                rows.append(jnp.where(lo_half, qp, jnp.zeros_like(qp)))
                rows.append(jnp.where(lo_half, jnp.zeros_like(qp), qp))
            qg = jnp.concatenate(rows, axis=0)
            kg = kwin[s * BLOCK:(s + 3) * BLOCK, g * LANES:(g + 1) * LANES]
            sc = lax.dot_general(kg, qg, _NT, preferred_element_type=F32)
            sc = sc + bias_ref[g]
            if s == 0:
                sc = jnp.concatenate([sc[:BLOCK] + pen_first, sc[BLOCK:]], axis=0)
            if s == nsub - 1:
                sc = jnp.concatenate([sc[:2 * BLOCK], sc[2 * BLOCK:] + pen_last], axis=0)
            sk = sink_ref[g]
            m = jnp.maximum(jnp.max(sc, axis=0, keepdims=True), sk)
            p = jnp.exp(sc - m).astype(BF16)
            vg = jnp.concatenate([vwin[g * HEAD_DIM:(g + 1) * HEAD_DIM, s * BLOCK:(s + 3) * BLOCK], ones],
                                 axis=0)
            o = jnp.dot(vg, p, preferred_element_type=F32)
            den = o[HEAD_DIM:HEAD_DIM + 1] + jnp.exp(sk - m)
            on = o[:HEAD_DIM] / den
            for pr in range(2):
                pair = jnp.concatenate([on[:, (2 * pr) * BLOCK:(2 * pr + 1) * BLOCK],
                                        on[:, (2 * pr + 1) * BLOCK:(2 * pr + 2) * BLOCK]], axis=0)
                o_ref[0, s * BLOCK:(s + 1) * BLOCK, (2 * g + pr) * LANES:(2 * g + pr + 1) * LANES] = (
                    pair.T.astype(o_ref.dtype))


SUM_ROWS = 64


def _attention(q, kd, vt, bias_t, sink_t, tq):
    b, l, _ = q.shape
    nsub = tq // BLOCK
    nblk = l // BLOCK
    kw = 2 * KV_W
    main = lambda w: pl.BlockSpec((1, tq, w), lambda bi, i: (bi, i, 0))
    prev = pl.BlockSpec((1, BLOCK, kw), lambda bi, i: (bi, jnp.maximum(i * nsub - 1, 0), 0))
    nxt = pl.BlockSpec((1, BLOCK, kw), lambda bi, i: (bi, jnp.minimum((i + 1) * nsub, nblk - 1), 0))
    vmain = pl.BlockSpec((1, KV_W, tq), lambda bi, i: (bi, 0, i))
    vprev = pl.BlockSpec((1, KV_W, BLOCK), lambda bi, i: (bi, 0, jnp.maximum(i * nsub - 1, 0)))
    vnxt = pl.BlockSpec((1, KV_W, BLOCK), lambda bi, i: (bi, 0, jnp.minimum((i + 1) * nsub, nblk - 1)))
    return pl.pallas_call(
        functools.partial(_attn_kernel, nsub=nsub),
        grid=(b, l // tq),
        in_specs=[main(ATT_W), main(kw), prev, nxt, vmain, vprev, vnxt,
                  _const_spec(bias_t.shape), _const_spec(sink_t.shape)],
        out_specs=main(ATT_W),
        out_shape=jax.ShapeDtypeStruct((b, l, ATT_W), BF16),
        compiler_params=_cparams("parallel", "parallel"),
        name="window_attn",
    )(q, kd, kd, kd, vt, vt, vt, bias_t, sink_t)


FF_CHUNK = 256


def _ffn_kernel(x_ref, yh_ref, ya_ref, hg_ref, ag_ref, wo_ref, g2_ref, wgu_ref, wd_ref, o_ref, act_ref):
    def rms(t, g):
        return (t * lax.rsqrt(jnp.mean(t * t, axis=-1, keepdims=True) + EPS) * g).astype(BF16)

    yh = jnp.concatenate([yh_ref[0, c] for c in range(SLABS)], axis=1)
    mixed = jnp.concatenate([rms(yh, hg_ref[...]), rms(ya_ref[...].astype(F32), ag_ref[...])], axis=1)
    h = x_ref[...] + jnp.dot(mixed, wo_ref[...], preferred_element_type=F32)
    hn = rms(h, g2_ref[...])
    for c in range(D_FF // FF_CHUNK):
        lo = c * FF_CHUNK
        gate = jnp.dot(hn, wgu_ref[:, lo:lo + FF_CHUNK], preferred_element_type=F32)
        up = jnp.dot(hn, wgu_ref[:, D_FF + lo:D_FF + lo + FF_CHUNK], preferred_element_type=F32)
        act_ref[:, lo:lo + FF_CHUNK] = (gate / (1.0 + jnp.exp(-gate)) * up).astype(BF16)
    o_ref[...] = h + jnp.dot(act_ref[...], wd_ref[...], preferred_element_type=F32)


def _ffn(x2d, yh, ya, hg, ag, wo_b, g2, wgu_b, wd_b, tm):
    t = x2d.shape[0]
    nt = yh.shape[2] // tm
    row = lambda w: pl.BlockSpec((tm, w), lambda i: (i, 0))
    yspec = pl.BlockSpec((1, SLABS, tm, LANES), lambda i: (i // nt, 0, i % nt, 0))
    return pl.pallas_call(
        _ffn_kernel,
        grid=(t // tm,),
        in_specs=[row(D_MODEL), yspec, row(ATT_W), _const_spec((1, HY_CH)), _const_spec((1, ATT_W)),
                  _const_spec((D_MODEL, D_MODEL)), _const_spec((1, D_MODEL)),
                  _const_spec((D_MODEL, 2 * D_FF)), _const_spec((D_FF, D_MODEL))],
        out_specs=row(D_MODEL),
        out_shape=jax.ShapeDtypeStruct((t, D_MODEL), F32),
        scratch_shapes=[pltpu.VMEM((tm, D_FF), BF16)],
        compiler_params=_cparams("parallel"),
        name="outproj_swiglu",
    )(x2d, yh, ya, hg, ag, wo_b, g2, wgu_b, wd_b)


def _dft_tables(seq_len, paired):
    n = 2 * seq_len
    n1 = n // FFT_N2
    a = n1 // 2
    kk = np.arange(n1)[:, None]
    th_half = 2.0 * np.pi * ((kk * np.arange(a)[None, :]) % n1) / n1
    c, s = np.cos(th_half), np.sin(th_half)
    if paired:
        fa = np.block([[c, s], [-s, c]])
        fc = np.block([[c.T, -s.T], [s.T, c.T]])
    else:
        fa = np.concatenate([c, -s], axis=0)
        fc = np.concatenate([c.T, -s.T], axis=1)
    th_full = 2.0 * np.pi * ((kk * np.arange(n1)[None, :]) % n1) / n1
    faf = np.concatenate([np.cos(th_full), -np.sin(th_full)], axis=0)
    to = lambda m: jnp.asarray(m.astype(np.float32)).astype(BF16)
    return to(fa), to(fc), to(faf)


def _twiddle_tables(seq_len):
    n = 2 * seq_len
    n1 = n // FFT_N2
    k1 = jnp.arange(n1, dtype=jnp.int32)[:, None, None]
    k2 = jnp.arange(FFT_N2, dtype=jnp.int32)[None, :, None]
    n2 = jnp.arange(FFT_N2, dtype=jnp.int32)[None, None, :]
    m = (n2 * (k2 * n1 + k1)) % n
    ang = m.astype(F32) * (-2.0 * math.pi / n)
    gr, gi = jnp.cos(ang), jnp.sin(ang)
    g = jnp.concatenate([jnp.concatenate([gr, -gi], axis=2), jnp.concatenate([gi, gr], axis=2)], axis=1)
    return g.astype(BF16), jnp.swapaxes(g, 1, 2).astype(BF16)


def _filter_features(seq_len):
    n = np.arange(2 * seq_len)
    t = np.where(n < seq_len, n, 2 * seq_len - n).astype(np.float32)[:, None]
    t_idx = jnp.asarray(t)
    t_norm = t_idx / max(seq_len - 1, 1)
    bands = jnp.linspace(1e-4, N_BANDS - 1, N_BANDS, dtype=F32)
    w = (2.0 * math.pi) * t_idx * bands[None, :] / seq_len
    return jnp.concatenate([t_norm, jnp.cos(w), -jnp.sin(w)], axis=-1)


def _abs_deltas():
    min_decay = math.log(DECAY_TARGET) / FAST_DECAY_PCT
    max_decay = math.log(DECAY_TARGET) / SLOW_DECAY_PCT
    return jnp.abs(jnp.linspace(min_decay, max_decay, HY_CH, dtype=F32))[None, :]


def _pick(n, target):
    t = min(n, target)
    while n % t:
        t //= 2
    return t


def _layer(x, lw, shared, st):
    b, l, _ = x.shape
    t = b * l
    x2d = x.reshape(t, D_MODEL)
    tm = _pick(t, 512)
    x0, x1, vh, q, kd, vt = _inproj(x2d, lw["norm1"], lw["w_in"], shared["bq"], shared["bk"],
                                    lw["qg"], lw["kg"], tm, l)
    r3 = lambda a: a.reshape(b, l, a.shape[-1])

    u, x0c = _hypre(r3(x0), r3(x1), r3(vh), lw["conv_w"], lw["conv_b"], _pick(l, 2048))
    n1 = 2 * l // FFT_N2
    a = n1 // 2
    nz = st["nz"]
    p = b // nz
    u5 = u.reshape(nz, p, a, FFT_N2, HY_CH)
    x0c5 = x0c.reshape(nz, p, a, FFT_N2, HY_CH)
    mb = _pick(2 * n1, STAGE_MB)
    ah = _stage_a(u5, st["fa"], mb)
    kb = _pick(n1, 8)
    vv = _stage_b(ah.reshape(p, SLABS, 2, n1, FFT_N2, LANES), lw["kf"][st["key"]], st["g"], st["gt"], kb)
    yh = _stage_c(vv.reshape(p, SLABS, 2 * n1, FFT_N2, LANES), st["fc"], x0c5, u5, lw["skip"], mb)
    yh = yh.reshape(b, SLABS, l, LANES)

    ya = _attention(r3(q), r3(kd), vt, shared["bias"], lw["sink_t"], _pick(l, 512))
    ya = ya.reshape(t, ATT_W)

    out = _ffn(x2d, yh, ya, lw["hy_gain"], lw["at_gain"], lw["w_out"], lw["norm2"],
               lw["w_gate_up"], lw["w_down"], tm)
    return out.reshape(b, l, D_MODEL)


def _filter_spectrum(lw_raw, st, seq_len):
    n = 2 * seq_len
    n1 = n // FFT_N2
    taps = _filter_taps(st["zfeat"], lw_raw["f_w1"], lw_raw["f_b1"], lw_raw["f_freq"], lw_raw["f_w2"],
                        lw_raw["f_b2"], lw_raw["f_w3"], st["absdelta"], seq_len, _pick(n, 512))
    ah = _stage_a(taps.reshape(1, 1, n1, FFT_N2, HY_CH), st["faf"], _pick(2 * n1, STAGE_MB))
    return _stage_bf(ah.reshape(1, SLABS, 2, n1, FFT_N2, LANES), st["g"], _pick(n1, 8), 1.0 / n)


def kernel(x_prompt, x_sample, norm1, w_in, conv_w, conv_b, f_w1, f_b1, f_freq, f_w2, f_b2, f_w3,
           hy_skip, q_gain, k_gain, sink, rel_bias, hy_gain, at_gain, w_out, norm2, w_gate_up, w_down):
    depth = norm1.shape[0]
    streams = {}
    for key, x in (("prompt", x_prompt), ("sample", x_sample)):
        b, l, _ = x.shape
        paired = b % 2 == 0
        fa, fc, faf = _dft_tables(l, paired)
        g, gt = _twiddle_tables(l)
        streams[key] = dict(key=key, nz=2 if paired else 1, fa=fa, fc=fc, faf=faf, g=g, gt=gt,
                            zfeat=_filter_features(l), absdelta=_abs_deltas())

    head_id = np.arange(ATT_W) // HEAD_DIM
    shared = dict(
        bq=jnp.asarray((head_id[:, None] == head_id[None, :]).astype(np.float32) / HEAD_DIM).astype(BF16),
        bk=jnp.asarray((head_id[:KV_W, None] == head_id[None, :KV_W]).astype(np.float32) / HEAD_DIM).astype(BF16),
        bias=_bias_table(rel_bias),
    )

    y_prompt, y_sample = x_prompt, x_sample
    for li in range(depth):
        raw = dict(f_w1=f_w1[li], f_b1=f_b1[li][None, :], f_freq=f_freq[li][None, :], f_w2=f_w2[li],
                   f_b2=f_b2[li][None, :], f_w3=f_w3[li])
        lw = dict(
            norm1=norm1[li][None, :], w_in=w_in[li].astype(BF16),
            qg=jnp.tile(q_gain[li], ATT_HEADS)[None, :] * (HEAD_DIM ** -0.5),
            kg=jnp.tile(k_gain[li], KV_HEADS)[None, :],
            conv_w=conv_w[li], conv_b=conv_b[li][None, :],
            skip=hy_skip[li][None, :],
            sink_t=jnp.repeat(sink[li], BLOCK).reshape(KV_HEADS, 1, GQA_GROUP * BLOCK),
            hy_gain=hy_gain[li][None, :], at_gain=at_gain[li][None, :],
            w_out=w_out[li].astype(BF16), norm2=norm2[li][None, :],
            w_gate_up=w_gate_up[li].astype(BF16), w_down=w_down[li].astype(BF16),
        )
        lw["kf"] = {key: _filter_spectrum(raw, st, {"prompt": x_prompt, "sample": x_sample}[key].shape[1])
                    for key, st in streams.items()}
        y_prompt = _layer(y_prompt, lw, shared, streams["prompt"])
        y_sample = _layer(y_sample, lw, shared, streams["sample"])
    return (y_prompt, y_sample)
```

```python
import functools
import math

import numpy as np
import jax
import jax.numpy as jnp
from jax import lax
from jax.experimental import pallas as pl
from jax.experimental.pallas import tpu as pltpu

F32 = jnp.float32
BF16 = jnp.bfloat16

D_MODEL = 1024
ATT_HEADS = 8
KV_HEADS = 2
HEAD_DIM = 64
GQA_GROUP = ATT_HEADS // KV_HEADS
ATT_W = ATT_HEADS * HEAD_DIM
KV_W = KV_HEADS * HEAD_DIM
WINDOW = 128
BLOCK = 128
N_BUCKETS = 32
MAX_DIST = 128
HY_CH = D_MODEL - ATT_W
FILTER_HIDDEN = 64
N_BANDS = 16
POS_DIM = 1 + 2 * N_BANDS
FAST_DECAY_PCT = 0.3
SLOW_DECAY_PCT = 1.5
DECAY_TARGET = 1e-2
IN_W = 3 * HY_CH + (ATT_HEADS + 2 * KV_HEADS) * HEAD_DIM
D_FF = -(-8 * D_MODEL // (3 * 256)) * 256
EPS = 1e-6

FFT_N2 = 64
LANES = 128
N2_TILE = 8
SLABS = HY_CH // LANES
STAGE_MB = 512
VMEM_LIMIT = 56 * 1024 * 1024

_NT = (((1,), (1,)), ((), ()))


def _cparams(*sem):
    return pltpu.CompilerParams(dimension_semantics=sem, vmem_limit_bytes=VMEM_LIMIT)


def _const_spec(shape):
    nd = len(shape)
    return pl.BlockSpec(shape, lambda *_: (0,) * nd, pipeline_mode=pl.Buffered(1))


HALO = 16


def _inproj_kernel(x_ref, xp_ref, xn_ref, g1_ref, w_ref, cw_ref, cb_ref, bq_ref, bk_ref, qg_ref, kg_ref,
                   u_ref, x0c_ref, q_ref, k_ref, v_ref, *, nt):
    i = pl.program_id(0)
    tm = x_ref.shape[0]
    xp = jnp.where(i % nt == 0, 0.0, xp_ref[...])
    xn = jnp.where(i % nt == nt - 1, 0.0, xn_ref[...])
    xe = jnp.concatenate([xp, x_ref[...], xn], axis=0)
    xe = (xe * lax.rsqrt(jnp.mean(xe * xe, axis=-1, keepdims=True) + EPS) * g1_ref[...]).astype(BF16)
    xb = xe[HALO:HALO + tm]

    he = jnp.dot(xe, w_ref[:, 0:3 * HY_CH], preferred_element_type=F32)
    rows = slice(HALO, HALO + tm)
    conv = (pltpu.roll(he, 1, axis=0)[rows] * cw_ref[0:1, :] + he[rows] * cw_ref[1:2, :]
            + pltpu.roll(he, tm + 2 * HALO - 1, axis=0)[rows] * cw_ref[2:3, :] + cb_ref[...])
    x0c_ref[...] = conv[:, 0:HY_CH]
    u_ref[...] = conv[:, HY_CH:2 * HY_CH] * conv[:, 2 * HY_CH:3 * HY_CH]

    def seg(lo, hi):
        return jnp.dot(xb, w_ref[:, lo:hi], preferred_element_type=F32)

    o = 3 * HY_CH
    q = seg(o, o + ATT_W)
    ms = jnp.dot((q * q).astype(BF16), bq_ref[...], preferred_element_type=F32)
    q_ref[...] = (q * lax.rsqrt(ms + EPS) * qg_ref[...]).astype(BF16)

    k = seg(o + ATT_W, o + ATT_W + KV_W)
    ms = jnp.dot((k * k).astype(BF16), bk_ref[...], preferred_element_type=F32)
    kn = k * lax.rsqrt(ms + EPS) * kg_ref[...]
    v = seg(o + ATT_W + KV_W, IN_W)

    lane = lax.broadcasted_iota(jnp.int32, kn.shape, 1)
    lo_half = lane < HEAD_DIM

    def dup(t):
        tr = pltpu.roll(t, HEAD_DIM, axis=1)
        return jnp.concatenate([jnp.where(lo_half, t, tr), jnp.where(lo_half, tr, t)], axis=1)

    k_ref[...] = dup(kn).astype(BF16)
    v_ref[0] = v.T.astype(BF16)


def _inproj(x2d, g1, w_in_b, conv_w, conv_b, bq, bk, qg, kg, tm, seq_len):
    t = x2d.shape[0]
    nt = seq_len // tm
    nh = tm // HALO
    row = lambda w: pl.BlockSpec((tm, w), lambda i: (i, 0))
    prev = pl.BlockSpec((HALO, D_MODEL), lambda i: (jnp.maximum(i * nh - 1, 0), 0))
    nxt = pl.BlockSpec((HALO, D_MODEL), lambda i: (jnp.minimum((i + 1) * nh, t // HALO - 1), 0))
    return pl.pallas_call(
        functools.partial(_inproj_kernel, nt=nt),
        grid=(t // tm,),
        in_specs=[row(D_MODEL), prev, nxt, _const_spec((1, D_MODEL)), _const_spec((D_MODEL, IN_W)),
                  _const_spec((3, 3 * HY_CH)), _const_spec((1, 3 * HY_CH)),
                  _const_spec((ATT_W, ATT_W)), _const_spec((KV_W, KV_W)),
                  _const_spec((1, ATT_W)), _const_spec((1, KV_W))],
        out_specs=[row(HY_CH), row(HY_CH), row(ATT_W), row(2 * KV_W),
                   pl.BlockSpec((1, KV_W, tm), lambda i: (i // nt, 0, i % nt))],
        out_shape=[jax.ShapeDtypeStruct((t, HY_CH), F32), jax.ShapeDtypeStruct((t, HY_CH), F32),
                   jax.ShapeDtypeStruct((t, ATT_W), BF16), jax.ShapeDtypeStruct((t, 2 * KV_W), BF16),
                   jax.ShapeDtypeStruct((t // seq_len, KV_W, seq_len), BF16)],
        compiler_params=_cparams("parallel"),
        name="inproj",
    )(x2d, x2d, x2d, g1, w_in_b, conv_w, conv_b, bq, bk, qg, kg)


def _stage_a_kernel(*refs, nz, a, mb):
    x_refs, f_ref, o_ref = refs[:SLABS], refs[SLABS], refs[SLABS + 1]
    kblk = mb // 2
    xs = [r.reshape(nz * a * N2_TILE, LANES) for r in x_refs]
    o2 = o_ref.reshape(SLABS * kblk * N2_TILE, LANES)
    f = f_ref[...]
    for s in range(N2_TILE):
        x = jnp.concatenate(
            [jnp.concatenate([xs[c][pl.ds(z * a * N2_TILE + s, a, stride=N2_TILE), :] for z in range(nz)], axis=0)
             for c in range(SLABS)], axis=1).astype(BF16)
        r = jnp.dot(f, x, preferred_element_type=F32)
        packed = _pack_pair(r[:kblk], r[kblk:])
        for c in range(SLABS):
            o2[pl.ds(c * kblk * N2_TILE + s, kblk, stride=N2_TILE), :] = packed[:, c * LANES:(c + 1) * LANES]


def _pack_pair(re, im):
    rb = lax.bitcast_convert_type(re.astype(BF16).astype(F32), jnp.uint32)
    ib = lax.bitcast_convert_type(im.astype(BF16).astype(F32), jnp.uint32)
    return (rb >> 16) | ib


def _unpack_pair(p):
    re = lax.bitcast_convert_type(p << 16, F32)
    im = lax.bitcast_convert_type(p & jnp.uint32(0xFFFF0000), F32)
    return re, im


def _stage_a(x5, fmat, mb):
    nz, p, a, n2, _ = x5.shape
    m = fmat.shape[0]
    kblk = mb // 2
    xspec = lambda c: pl.BlockSpec((nz, 1, a, N2_TILE, LANES), lambda pi, j, mi: (0, pi, 0, j, c))
    return pl.pallas_call(
        functools.partial(_stage_a_kernel, nz=nz, a=a, mb=mb),
        grid=(p, n2 // N2_TILE, m // mb),
        in_specs=[xspec(c) for c in range(SLABS)] + [pl.BlockSpec((mb, nz * a), lambda pi, j, mi: (mi, 0))],
        out_specs=pl.BlockSpec((1, SLABS, kblk, N2_TILE, LANES), lambda pi, j, mi: (pi, 0, mi, j, 0)),
        out_shape=jax.ShapeDtypeStruct((p, SLABS, m // 2, n2, LANES), jnp.uint32),
        compiler_params=_cparams("parallel", "parallel", "parallel"),
        name="hy_stage_a",
    )(*([x5] * SLABS), fmat)


def _slab_rows(a_ref, j):
    parts = [_unpack_pair(a_ref[0, c, j]) for c in range(SLABS)]
    return jnp.concatenate([jnp.concatenate([re, im], axis=0) for re, im in parts], axis=1).astype(BF16)


def _stage_b_kernel(a_ref, kf_ref, g_ref, gt_ref, o_ref):
    kb, n2 = a_ref.shape[2], a_ref.shape[3]
    for j in range(kb):
        u = jnp.dot(g_ref[j], _slab_rows(a_ref, j), preferred_element_type=F32)
        ur, ui = u[:n2], u[n2:]
        kr, ki = kf_ref[0, j], kf_ref[1, j]
        p = jnp.concatenate([ur * kr - ui * ki, ur * ki + ui * kr], axis=0).astype(BF16)
        v = jnp.dot(gt_ref[j], p, preferred_element_type=F32)
        packed = _pack_pair(v[:n2], v[n2:])
        for c in range(SLABS):
            o_ref[0, c, j] = packed[:, c * LANES:(c + 1) * LANES]


def _stage_b(a5, kf, g, gt, kb):
    p, _, n1, n2, _ = a5.shape
    blk = pl.BlockSpec((1, SLABS, kb, n2, LANES), lambda i, pi: (pi, 0, i, 0, 0))
    gspec = pl.BlockSpec((kb, 2 * n2, 2 * n2), lambda i, pi: (i, 0, 0))
    return pl.pallas_call(
        _stage_b_kernel,
        grid=(n1 // kb, p),
        in_specs=[blk, pl.BlockSpec((2, kb, n2, HY_CH), lambda i, pi: (0, i, 0, 0)), gspec, gspec],
        out_specs=blk,
        out_shape=jax.ShapeDtypeStruct(a5.shape, jnp.uint32),
        compiler_params=_cparams("parallel", "parallel"),
        name="hy_stage_b",
    )(a5, kf, g, gt)


def _stage_bf_kernel(a_ref, g_ref, o_ref, *, scale):
    kb, n2 = a_ref.shape[2], a_ref.shape[3]
    for j in range(kb):
        u = jnp.dot(g_ref[j], _slab_rows(a_ref, j), preferred_element_type=F32) * scale
        o_ref[0, j] = u[:n2]
        o_ref[1, j] = u[n2:]


def _stage_bf(a5, g, kb, scale):
    _, _, n1, n2, _ = a5.shape
    return pl.pallas_call(
        functools.partial(_stage_bf_kernel, scale=scale),
        grid=(n1 // kb,),
        in_specs=[pl.BlockSpec((1, SLABS, kb, n2, LANES), lambda i: (0, 0, i, 0, 0)),
                  pl.BlockSpec((kb, 2 * n2, 2 * n2), lambda i: (i, 0, 0))],
        out_specs=pl.BlockSpec((2, kb, n2, HY_CH), lambda i: (0, i, 0, 0)),
        out_shape=jax.ShapeDtypeStruct((2, n1, n2, HY_CH), F32),
        compiler_params=_cparams("parallel"),
        name="hy_filter_spectrum",
    )(a5, g)


def _stage_c_kernel(*refs, nz, a, mb):
    v_ref, f_ref = refs[0], refs[1]
    x0c_refs, u_refs = refs[2:2 + SLABS], refs[2 + SLABS:2 + 2 * SLABS]
    skip_ref, o_ref, acc_ref = refs[2 + 2 * SLABS:]
    mk = pl.program_id(2)
    kblk = mb // 2
    v2 = v_ref.reshape(SLABS * kblk * N2_TILE, LANES)

    @pl.when(mk == 0)
    def _():
        acc_ref[...] = jnp.zeros_like(acc_ref)

    for s in range(N2_TILE):
        packed = jnp.concatenate([v2[pl.ds(c * kblk * N2_TILE + s, kblk, stride=N2_TILE), :] for c in range(SLABS)],
                                 axis=1)
        vs = jnp.concatenate(_unpack_pair(packed), axis=0).astype(BF16)
        acc_ref[s] += jnp.dot(f_ref[...], vs, preferred_element_type=F32)

    @pl.when(mk == pl.num_programs(2) - 1)
    def _():
        o2 = o_ref.reshape(nz * SLABS * a * N2_TILE, LANES)
        x2 = [r.reshape(nz * a * N2_TILE, LANES) for r in x0c_refs]
        u2 = [r.reshape(nz * a * N2_TILE, LANES) for r in u_refs]
        for s in range(N2_TILE):
            y = acc_ref[s]
            for z in range(nz):
                for c in range(SLABS):
                    rows = pl.ds(z * a * N2_TILE + s, a, stride=N2_TILE)
                    yc = y[z * a:(z + 1) * a, c * LANES:(c + 1) * LANES]
                    val = x2[c][rows, :] * (yc + skip_ref[0:1, c * LANES:(c + 1) * LANES] * u2[c][rows, :])
                    o2[pl.ds((z * SLABS + c) * a * N2_TILE + s, a, stride=N2_TILE), :] = val


def _stage_c(v5, fmat, x0c5, u5, skip, mb):
    nz, p, a, n2, _ = u5.shape
    m = 2 * v5.shape[2]
    xspec = lambda c: pl.BlockSpec((nz, 1, a, N2_TILE, LANES), lambda pi, j, mk: (0, pi, 0, j, c))
    return pl.pallas_call(
        functools.partial(_stage_c_kernel, nz=nz, a=a, mb=mb),
        grid=(p, n2 // N2_TILE, m // mb),
        in_specs=[pl.BlockSpec((1, SLABS, mb // 2, N2_TILE, LANES), lambda pi, j, mk: (pi, 0, mk, j, 0)),
                  pl.BlockSpec((nz * a, mb), lambda pi, j, mk: (0, mk))]
        + [xspec(c) for c in range(SLABS)] * 2 + [_const_spec((1, HY_CH))],
        out_specs=pl.BlockSpec((nz, 1, SLABS, a, N2_TILE, LANES), lambda pi, j, mk: (0, pi, 0, 0, j, 0)),
        out_shape=jax.ShapeDtypeStruct((nz, p, SLABS, a, n2, LANES), F32),
        scratch_shapes=[pltpu.VMEM((N2_TILE, nz * a, HY_CH), F32)],
        compiler_params=_cparams("parallel", "parallel", "arbitrary"),
        name="hy_stage_c",
    )(v5, fmat, *([x0c5] * SLABS), *([u5] * SLABS), skip)


def _filter_kernel(zt_ref, tn_ref, w1t_ref, b1_ref, fr_ref, w2t_ref, b2_ref, w3_ref, dl_ref, o_ref, *, seq_len):
    hi = lax.Precision.HIGHEST
    fr = fr_ref[...]
    h = jnp.sin(fr * (jnp.dot(w1t_ref[...], zt_ref[...], precision=hi, preferred_element_type=F32) + b1_ref[...]))
    h = jnp.sin(fr * (jnp.dot(w2t_ref[...], h, precision=hi, preferred_element_type=F32) + b2_ref[...]))
    taps = jnp.dot(h.T.astype(BF16), w3_ref[...], preferred_element_type=F32)
    tr = taps.shape[0]
    n = pl.program_id(0) * tr + lax.broadcasted_iota(jnp.int32, (tr, HY_CH), 0)
    sel = jnp.where(n < seq_len, taps[:, :HY_CH], jnp.where(n > seq_len, taps[:, HY_CH:], 0.0))
    o_ref[...] = sel * jnp.exp(-tn_ref[...] * dl_ref[...])


def _filter_taps(zfeat, f_w1, f_b1, f_freq, f_w2, f_b2, f_w3, absdelta, seq_len, tr):
    n = zfeat.shape[0]
    cs = lambda a: _const_spec(a.shape)
    args = (zfeat.T, zfeat[:, 0:1], f_w1.T, f_b1.T, f_freq.T, f_w2.T, f_b2.T, f_w3.astype(BF16), absdelta)
    return pl.pallas_call(
        functools.partial(_filter_kernel, seq_len=seq_len),
        grid=(n // tr,),
        in_specs=[pl.BlockSpec((POS_DIM, tr), lambda i: (0, i)), pl.BlockSpec((tr, 1), lambda i: (i, 0))]
        + [cs(a) for a in args[2:]],
        out_specs=pl.BlockSpec((tr, HY_CH), lambda i: (i, 0)),
        out_shape=jax.ShapeDtypeStruct((n, HY_CH), F32),
        compiler_params=_cparams("parallel"),
        name="hy_filter_taps",
    )(*args)


def _bias_kernel(rb_ref, oh_ref, o_ref):
    o_ref[...] = jnp.dot(rb_ref[...], oh_ref[...], precision=lax.Precision.HIGHEST,
                         preferred_element_type=F32)


def _bias_table(rel_bias):
    i = jnp.arange(BLOCK)[:, None]
    j = jnp.arange(3 * BLOCK)[None, :]
    rel = j - BLOCK - i
    nb2 = N_BUCKETS // 2
    max_exact = nb2 // 2
    n = jnp.abs(rel)
    large = max_exact + (jnp.log(jnp.maximum(n, 1).astype(F32) / max_exact)
                         / math.log(MAX_DIST / max_exact) * (nb2 - max_exact)).astype(jnp.int32)
    large = jnp.minimum(large, nb2 - 1)
    bucket = jnp.where(rel > 0, nb2, 0) + jnp.where(n < max_exact, n, large)
    onehot = (bucket.reshape(1, -1) == jnp.arange(N_BUCKETS)[:, None]).astype(F32)
    cols = onehot.shape[1]
    tc = cols // 4
    table = pl.pallas_call(
        _bias_kernel,
        grid=(4,),
        in_specs=[_const_spec((ATT_HEADS, N_BUCKETS)), pl.BlockSpec((N_BUCKETS, tc), lambda c: (0, c))],
        out_specs=pl.BlockSpec((ATT_HEADS, tc), lambda c: (0, c)),
        out_shape=jax.ShapeDtypeStruct((ATT_HEADS, cols), F32),
        compiler_params=_cparams("parallel"),
        name="att_bias_table",
    )(rel_bias.T, onehot)
    table = table.reshape(ATT_HEADS, BLOCK, 3 * BLOCK)
    table = jnp.where((n <= WINDOW)[None], table, -jnp.inf)
    table = table.reshape(KV_HEADS, GQA_GROUP, BLOCK, 3 * BLOCK)
    return table.transpose(0, 3, 1, 2).reshape(KV_HEADS, 3 * BLOCK, GQA_GROUP * BLOCK)


def _attn_kernel(q_ref, km_ref, kp_ref, kn_ref, vm_ref, vp_ref, vn_ref, bias_ref, sink_ref, o_ref, *, nsub):
    i = pl.program_id(1)
    neg = -jnp.inf
    pen_first = jnp.where(i == 0, neg, 0.0)
    pen_last = jnp.where(i == pl.num_programs(1) - 1, neg, 0.0)
    kwin = jnp.concatenate([kp_ref[0], km_ref[0], kn_ref[0]], axis=0)
    vwin = jnp.concatenate([vp_ref[0], vm_ref[0], vn_ref[0]], axis=1)
    lane = lax.broadcasted_iota(jnp.int32, (BLOCK, LANES), 1)
    lo_half = lane < HEAD_DIM
    ones = jnp.ones((SUM_ROWS, 3 * BLOCK), BF16)
    for s in range(nsub):
        qs = q_ref[0, s * BLOCK:(s + 1) * BLOCK, :]
        for g in range(KV_HEADS):
            rows = []
            for pr in range(2):
                qp = qs[:, (2 * g + pr) * LANES:(2 * g + pr + 1) * LANES]
                rows.append(jnp.where(lo_half, qp, jnp.zeros_like(qp)))
                rows.append(jnp.where(lo_half, jnp.zeros_like(qp), qp))
            qg = jnp.concatenate(rows, axis=0)
            kg = kwin[s * BLOCK:(s + 3) * BLOCK, g * LANES:(g + 1) * LANES]
            sc = lax.dot_general(kg, qg, _NT, preferred_element_type=F32)
            sc = sc + bias_ref[g]
            if s == 0:
                sc = jnp.concatenate([sc[:BLOCK] + pen_first, sc[BLOCK:]], axis=0)
            if s == nsub - 1:
                sc = jnp.concatenate([sc[:2 * BLOCK], sc[2 * BLOCK:] + pen_last], axis=0)
            sk = sink_ref[g]
            m = jnp.maximum(jnp.max(sc, axis=0, keepdims=True), sk)
            p = jnp.exp(sc - m).astype(BF16)
            vg = jnp.concatenate([vwin[g * HEAD_DIM:(g + 1) * HEAD_DIM, s * BLOCK:(s + 3) * BLOCK], ones],
                                 axis=0)
            o = jnp.dot(vg, p, preferred_element_type=F32)
            den = o[HEAD_DIM:HEAD_DIM + 1] + jnp.exp(sk - m)
            on = o[:HEAD_DIM] / den
            for pr in range(2):
                pair = jnp.concatenate([on[:, (2 * pr) * BLOCK:(2 * pr + 1) * BLOCK],
                                        on[:, (2 * pr + 1) * BLOCK:(2 * pr + 2) * BLOCK]], axis=0)
                o_ref[0, s * BLOCK:(s + 1) * BLOCK, (2 * g + pr) * LANES:(2 * g + pr + 1) * LANES] = (
                    pair.T.astype(o_ref.dtype))


SUM_ROWS = 64


def _attention(q, kd, vt, bias_t, sink_t, tq):
    b, l, _ = q.shape
    nsub = tq // BLOCK
    nblk = l // BLOCK
    kw = 2 * KV_W
    main = lambda w: pl.BlockSpec((1, tq, w), lambda bi, i: (bi, i, 0))
    prev = pl.BlockSpec((1, BLOCK, kw), lambda bi, i: (bi, jnp.maximum(i * nsub - 1, 0), 0))
    nxt = pl.BlockSpec((1, BLOCK, kw), lambda bi, i: (bi, jnp.minimum((i + 1) * nsub, nblk - 1), 0))
    vmain = pl.BlockSpec((1, KV_W, tq), lambda bi, i: (bi, 0, i))
    vprev = pl.BlockSpec((1, KV_W, BLOCK), lambda bi, i: (bi, 0, jnp.maximum(i * nsub - 1, 0)))
    vnxt = pl.BlockSpec((1, KV_W, BLOCK), lambda bi, i: (bi, 0, jnp.minimum((i + 1) * nsub, nblk - 1)))
    return pl.pallas_call(
        functools.partial(_attn_kernel, nsub=nsub),
        grid=(b, l // tq),
        in_specs=[main(ATT_W), main(kw), prev, nxt, vmain, vprev, vnxt,
                  _const_spec(bias_t.shape), _const_spec(sink_t.shape)],
        out_specs=main(ATT_W),
        out_shape=jax.ShapeDtypeStruct((b, l, ATT_W), BF16),
        compiler_params=_cparams("parallel", "parallel"),
        name="window_attn",
    )(q, kd, kd, kd, vt, vt, vt, bias_t, sink_t)


FF_CHUNK = 256


def _ffn_kernel(x_ref, yh_ref, ya_ref, hg_ref, ag_ref, wo_ref, g2_ref, wgu_ref, wd_ref, o_ref, act_ref):
    def rms(t, g):
        return (t * lax.rsqrt(jnp.mean(t * t, axis=-1, keepdims=True) + EPS) * g).astype(BF16)

    yh = jnp.concatenate([yh_ref[0, c] for c in range(SLABS)], axis=1)
    mixed = jnp.concatenate([rms(yh, hg_ref[...]), rms(ya_ref[...].astype(F32), ag_ref[...])], axis=1)
    h = x_ref[...] + jnp.dot(mixed, wo_ref[...], preferred_element_type=F32)
    hn = rms(h, g2_ref[...])
    for c in range(D_FF // FF_CHUNK):
        lo = c * FF_CHUNK
        gate = jnp.dot(hn, wgu_ref[:, lo:lo + FF_CHUNK], preferred_element_type=F32)
        up = jnp.dot(hn, wgu_ref[:, D_FF + lo:D_FF + lo + FF_CHUNK], preferred_element_type=F32)
        act_ref[:, lo:lo + FF_CHUNK] = (gate / (1.0 + jnp.exp(-gate)) * up).astype(BF16)
    o_ref[...] = h + jnp.dot(act_ref[...], wd_ref[...], preferred_element_type=F32)


def _ffn(x2d, yh, ya, hg, ag, wo_b, g2, wgu_b, wd_b, tm):
    t = x2d.shape[0]
    nt = yh.shape[2] // tm
    row = lambda w: pl.BlockSpec((tm, w), lambda i: (i, 0))
    yspec = pl.BlockSpec((1, SLABS, tm, LANES), lambda i: (i // nt, 0, i % nt, 0))
    return pl.pallas_call(
        _ffn_kernel,
        grid=(t // tm,),
        in_specs=[row(D_MODEL), yspec, row(ATT_W), _const_spec((1, HY_CH)), _const_spec((1, ATT_W)),
                  _const_spec((D_MODEL, D_MODEL)), _const_spec((1, D_MODEL)),
                  _const_spec((D_MODEL, 2 * D_FF)), _const_spec((D_FF, D_MODEL))],
        out_specs=row(D_MODEL),
        out_shape=jax.ShapeDtypeStruct((t, D_MODEL), F32),
        scratch_shapes=[pltpu.VMEM((tm, D_FF), BF16)],
        compiler_params=_cparams("parallel"),
        name="outproj_swiglu",
    )(x2d, yh, ya, hg, ag, wo_b, g2, wgu_b, wd_b)


def _dft_tables(seq_len, paired, kblk):
    n = 2 * seq_len
    n1 = n // FFT_N2
    a = n1 // 2
    kk = np.arange(n1)[:, None]
    th_half = 2.0 * np.pi * ((kk * np.arange(a)[None, :]) % n1) / n1
    c, s = np.cos(th_half), np.sin(th_half)

    def blocked(re, im):
        return np.concatenate([np.concatenate([re[i:i + kblk], im[i:i + kblk]], axis=0)
                               for i in range(0, n1, kblk)], axis=0)

    if paired:
        fa = blocked(np.concatenate([c, s], axis=1), np.concatenate([-s, c], axis=1))
        fc = blocked(np.concatenate([c, s], axis=1), np.concatenate([-s, c], axis=1)).T
    else:
        fa = blocked(c, -s)
        fc = blocked(c, -s).T
    th_full = 2.0 * np.pi * ((kk * np.arange(n1)[None, :]) % n1) / n1
    faf = blocked(np.cos(th_full), -np.sin(th_full))
    to = lambda m: jnp.asarray(m.astype(np.float32)).astype(BF16)
    return to(fa), to(fc), to(faf)


def _twiddle_tables(seq_len):
    n = 2 * seq_len
    n1 = n // FFT_N2
    k1 = jnp.arange(n1, dtype=jnp.int32)[:, None, None]
    k2 = jnp.arange(FFT_N2, dtype=jnp.int32)[None, :, None]
    n2 = jnp.arange(FFT_N2, dtype=jnp.int32)[None, None, :]
    m = (n2 * (k2 * n1 + k1)) % n
    ang = m.astype(F32) * (-2.0 * math.pi / n)
    gr, gi = jnp.cos(ang), jnp.sin(ang)
    g = jnp.concatenate([jnp.concatenate([gr, -gi], axis=2), jnp.concatenate([gi, gr], axis=2)], axis=1)
    return g.astype(BF16), jnp.swapaxes(g, 1, 2).astype(BF16)


def _filter_features(seq_len):
    n = np.arange(2 * seq_len)
    t = np.where(n < seq_len, n, 2 * seq_len - n).astype(np.float32)[:, None]
    t_idx = jnp.asarray(t)
    t_norm = t_idx / max(seq_len - 1, 1)
    bands = jnp.linspace(1e-4, N_BANDS - 1, N_BANDS, dtype=F32)
    w = (2.0 * math.pi) * t_idx * bands[None, :] / seq_len
    return jnp.concatenate([t_norm, jnp.cos(w), -jnp.sin(w)], axis=-1)


def _abs_deltas():
    min_decay = math.log(DECAY_TARGET) / FAST_DECAY_PCT
    max_decay = math.log(DECAY_TARGET) / SLOW_DECAY_PCT
    return jnp.abs(jnp.linspace(min_decay, max_decay, HY_CH, dtype=F32))[None, :]


def _pick(n, target):
    t = min(n, target)
    while n % t:
        t //= 2
    return t


def _layer(x, lw, shared, st):
    b, l, _ = x.shape
    t = b * l
    x2d = x.reshape(t, D_MODEL)
    tm = _pick(t, 512)
    u, x0c, q, kd, vt = _inproj(x2d, lw["norm1"], lw["w_in"], lw["conv_w"], lw["conv_b"],
                                shared["bq"], shared["bk"], lw["qg"], lw["kg"], tm, l)
    r3 = lambda a: a.reshape(b, l, a.shape[-1])

    n1 = 2 * l // FFT_N2
    a = n1 // 2
    nz = st["nz"]
    p = b // nz
    u5 = u.reshape(nz, p, a, FFT_N2, HY_CH)
    x0c5 = x0c.reshape(nz, p, a, FFT_N2, HY_CH)
    mb = _pick(2 * n1, STAGE_MB)
    ah = _stage_a(u5, st["fa"], mb)
    vv = _stage_b(ah, lw["kf"][st["key"]], st["g"], st["gt"], _pick(n1, 8))
    yh = _stage_c(vv, st["fc"], x0c5, u5, lw["skip"], mb)
    yh = yh.reshape(b, SLABS, l, LANES)

    ya = _attention(r3(q), r3(kd), vt, shared["bias"], lw["sink_t"], _pick(l, 512))
    ya = ya.reshape(t, ATT_W)

    out = _ffn(x2d, yh, ya, lw["hy_gain"], lw["at_gain"], lw["w_out"], lw["norm2"],
               lw["w_gate_up"], lw["w_down"], tm)
    return out.reshape(b, l, D_MODEL)


def _filter_spectrum(lw_raw, st, seq_len):
    n = 2 * seq_len
    n1 = n // FFT_N2
    taps = _filter_taps(st["zfeat"], lw_raw["f_w1"], lw_raw["f_b1"], lw_raw["f_freq"], lw_raw["f_w2"],
                        lw_raw["f_b2"], lw_raw["f_w3"], st["absdelta"], seq_len, _pick(n, 512))
    ah = _stage_a(taps.reshape(1, 1, n1, FFT_N2, HY_CH), st["faf"], _pick(2 * n1, STAGE_MB))
    return _stage_bf(ah, st["g"], _pick(n1, 8), 1.0 / n)


def kernel(x_prompt, x_sample, norm1, w_in, conv_w, conv_b, f_w1, f_b1, f_freq, f_w2, f_b2, f_w3,
           hy_skip, q_gain, k_gain, sink, rel_bias, hy_gain, at_gain, w_out, norm2, w_gate_up, w_down):
    depth = norm1.shape[0]
    streams = {}
    for key, x in (("prompt", x_prompt), ("sample", x_sample)):
        b, l, _ = x.shape
        paired = b % 2 == 0
        fa, fc, faf = _dft_tables(l, paired, _pick(4 * l // FFT_N2, STAGE_MB) // 2)
        g, gt = _twiddle_tables(l)
        streams[key] = dict(key=key, nz=2 if paired else 1, fa=fa, fc=fc, faf=faf, g=g, gt=gt,
                            zfeat=_filter_features(l), absdelta=_abs_deltas())

    head_id = np.arange(ATT_W) // HEAD_DIM
    shared = dict(
        bq=jnp.asarray((head_id[:, None] == head_id[None, :]).astype(np.float32) / HEAD_DIM).astype(BF16),
        bk=jnp.asarray((head_id[:KV_W, None] == head_id[None, :KV_W]).astype(np.float32) / HEAD_DIM).astype(BF16),
        bias=_bias_table(rel_bias),
    )

    y_prompt, y_sample = x_prompt, x_sample
    for li in range(depth):
        raw = dict(f_w1=f_w1[li], f_b1=f_b1[li][None, :], f_freq=f_freq[li][None, :], f_w2=f_w2[li],
                   f_b2=f_b2[li][None, :], f_w3=f_w3[li])
        lw = dict(
            norm1=norm1[li][None, :], w_in=w_in[li].astype(BF16),
            qg=jnp.tile(q_gain[li], ATT_HEADS)[None, :] * (HEAD_DIM ** -0.5),
            kg=jnp.tile(k_gain[li], KV_HEADS)[None, :],
            conv_w=conv_w[li], conv_b=conv_b[li][None, :],
            skip=hy_skip[li][None, :],
            sink_t=jnp.repeat(sink[li], BLOCK).reshape(KV_HEADS, 1, GQA_GROUP * BLOCK),
            hy_gain=hy_gain[li][None, :], at_gain=at_gain[li][None, :],
            w_out=w_out[li].astype(BF16), norm2=norm2[li][None, :],
            w_gate_up=w_gate_up[li].astype(BF16), w_down=w_down[li].astype(BF16),
        )
        lw["kf"] = {key: _filter_spectrum(raw, st, {"prompt": x_prompt, "sample": x_sample}[key].shape[1])
                    for key, st in streams.items()}
        y_prompt = _layer(y_prompt, lw, shared, streams["prompt"])
        y_sample = _layer(y_sample, lw, shared, streams["sample"])
    return (y_prompt, y_sample)
```

```python
import functools
import math

import numpy as np
import jax
import jax.numpy as jnp
from jax import lax
from jax.experimental import pallas as pl
from jax.experimental.pallas import tpu as pltpu

F32 = jnp.float32
BF16 = jnp.bfloat16

D_MODEL = 1024
ATT_HEADS = 8
KV_HEADS = 2
HEAD_DIM = 64
GQA_GROUP = ATT_HEADS // KV_HEADS
ATT_W = ATT_HEADS * HEAD_DIM
KV_W = KV_HEADS * HEAD_DIM
WINDOW = 128
BLOCK = 128
N_BUCKETS = 32
MAX_DIST = 128
HY_CH = D_MODEL - ATT_W
FILTER_HIDDEN = 64
N_BANDS = 16
POS_DIM = 1 + 2 * N_BANDS
FAST_DECAY_PCT = 0.3
SLOW_DECAY_PCT = 1.5
DECAY_TARGET = 1e-2
IN_W = 3 * HY_CH + (ATT_HEADS + 2 * KV_HEADS) * HEAD_DIM
D_FF = -(-8 * D_MODEL // (3 * 256)) * 256
EPS = 1e-6
LOG2E = math.log2(math.e)

FFT_N2 = 64
LANES = 128
N2_TILE = 8
SLABS = HY_CH // LANES
STAGE_MB = 512
STAGE_KB = 16
VMEM_LIMIT = 56 * 1024 * 1024

_NT = (((1,), (1,)), ((), ()))


def _cparams(*sem):
    return pltpu.CompilerParams(dimension_semantics=sem, vmem_limit_bytes=VMEM_LIMIT)


def _const_spec(shape):
    nd = len(shape)
    return pl.BlockSpec(shape, lambda *_: (0,) * nd, pipeline_mode=pl.Buffered(1))


HALO = 16


def _inproj_kernel(x_ref, xp_ref, xn_ref, g1_ref, w_ref, cw_ref, cb_ref, bq_ref, bk_ref, qg_ref, kg_ref,
                   u_ref, x0c_ref, q_ref, k_ref, v_ref, *, nt):
    i = pl.program_id(0)
    tm = x_ref.shape[0]
    xp = jnp.where(i % nt == 0, 0.0, xp_ref[...])
    xn = jnp.where(i % nt == nt - 1, 0.0, xn_ref[...])
    xe = jnp.concatenate([xp, x_ref[...], xn], axis=0)
    xe = (xe * lax.rsqrt(jnp.mean(xe * xe, axis=-1, keepdims=True) + EPS) * g1_ref[...]).astype(BF16)
    xb = xe[HALO:HALO + tm]

    def seg(lo, hi):
        return jnp.dot(xb, w_ref[:, lo:hi], preferred_element_type=F32)

    o = 3 * HY_CH
    v = seg(o + ATT_W + KV_W, IN_W)
    v_ref[0] = v.T.astype(BF16)

    k = seg(o + ATT_W, o + ATT_W + KV_W)
    ms = jnp.dot((k * k).astype(BF16), bk_ref[...], preferred_element_type=F32)
    kn = k * lax.rsqrt(ms + EPS) * kg_ref[...]
    lo_half = lax.broadcasted_iota(jnp.int32, kn.shape, 1) < HEAD_DIM
    kr = pltpu.roll(kn, HEAD_DIM, axis=1)
    k_ref[...] = jnp.concatenate([jnp.where(lo_half, kn, kr), jnp.where(lo_half, kr, kn)], axis=1).astype(BF16)

    q = seg(o, o + ATT_W)
    ms = jnp.dot((q * q).astype(BF16), bq_ref[...], preferred_element_type=F32)
    q_ref[...] = (q * lax.rsqrt(ms + EPS) * qg_ref[...]).astype(BF16)

    rows = slice(HALO, HALO + tm)

    def conv(j):
        c = slice(j * HY_CH, (j + 1) * HY_CH)
        he = jnp.dot(xe, w_ref[:, c], preferred_element_type=F32)
        return (pltpu.roll(he, 1, axis=0)[rows] * cw_ref[0:1, c] + he[rows] * cw_ref[1:2, c]
                + pltpu.roll(he, tm + 2 * HALO - 1, axis=0)[rows] * cw_ref[2:3, c] + cb_ref[0:1, c])

    x0c_ref[...] = conv(0)
    u_ref[...] = conv(1) * conv(2)


def _inproj(x2d, g1, w_in_b, conv_w, conv_b, bq, bk, qg, kg, tm, seq_len):
    t = x2d.shape[0]
    nt = seq_len // tm
    nh = tm // HALO
    row = lambda w: pl.BlockSpec((tm, w), lambda i: (i, 0))
    prev = pl.BlockSpec((HALO, D_MODEL), lambda i: (jnp.maximum(i * nh - 1, 0), 0))
    nxt = pl.BlockSpec((HALO, D_MODEL), lambda i: (jnp.minimum((i + 1) * nh, t // HALO - 1), 0))
    return pl.pallas_call(
        functools.partial(_inproj_kernel, nt=nt),
        grid=(t // tm,),
        in_specs=[row(D_MODEL), prev, nxt, _const_spec((1, D_MODEL)), _const_spec((D_MODEL, IN_W)),
                  _const_spec((3, 3 * HY_CH)), _const_spec((1, 3 * HY_CH)),
                  _const_spec((ATT_W, ATT_W)), _const_spec((KV_W, KV_W)),
                  _const_spec((1, ATT_W)), _const_spec((1, KV_W))],
        out_specs=[row(HY_CH), row(HY_CH), row(ATT_W), row(2 * KV_W),
                   pl.BlockSpec((1, KV_W, tm), lambda i: (i // nt, 0, i % nt))],
        out_shape=[jax.ShapeDtypeStruct((t, HY_CH), F32), jax.ShapeDtypeStruct((t, HY_CH), F32),
                   jax.ShapeDtypeStruct((t, ATT_W), BF16), jax.ShapeDtypeStruct((t, 2 * KV_W), BF16),
                   jax.ShapeDtypeStruct((t // seq_len, KV_W, seq_len), BF16)],
        compiler_params=_cparams("parallel"),
        name="inproj",
    )(x2d, x2d, x2d, g1, w_in_b, conv_w, conv_b, bq, bk, qg, kg)


def _stage_a_kernel(*refs, nz, a, mb):
    x_refs, f_ref, o_ref = refs[:SLABS], refs[SLABS], refs[SLABS + 1]
    kblk = mb // 2
    xs = [r.reshape(nz * a * N2_TILE, LANES) for r in x_refs]
    o2 = o_ref.reshape(SLABS * kblk * N2_TILE, LANES)
    f = f_ref[...]
    for s in range(N2_TILE):
        x = jnp.concatenate(
            [jnp.concatenate([xs[c][pl.ds(z * a * N2_TILE + s, a, stride=N2_TILE), :] for z in range(nz)], axis=0)
             for c in range(SLABS)], axis=1).astype(BF16)
        r = jnp.dot(f, x, preferred_element_type=F32)
        packed = _pack_pair(r[:kblk], r[kblk:])
        for c in range(SLABS):
            o2[pl.ds(c * kblk * N2_TILE + s, kblk, stride=N2_TILE), :] = packed[:, c * LANES:(c + 1) * LANES]


def _pack_pair(re, im):
    rb = lax.bitcast_convert_type(re.astype(BF16).astype(F32), jnp.uint32)
    ib = lax.bitcast_convert_type(im.astype(BF16).astype(F32), jnp.uint32)
    return (rb >> 16) | ib


def _unpack_pair(p):
    re = lax.bitcast_convert_type(p << 16, F32)
    im = lax.bitcast_convert_type(p & jnp.uint32(0xFFFF0000), F32)
    return re, im


def _stage_a(x5, fmat, mb):
    nz, p, a, n2, _ = x5.shape
    m = fmat.shape[0]
    kblk = mb // 2
    xspec = lambda c: pl.BlockSpec((nz, 1, a, N2_TILE, LANES), lambda pi, j, mi: (0, pi, 0, j, c))
    return pl.pallas_call(
        functools.partial(_stage_a_kernel, nz=nz, a=a, mb=mb),
        grid=(p, n2 // N2_TILE, m // mb),
        in_specs=[xspec(c) for c in range(SLABS)] + [pl.BlockSpec((mb, nz * a), lambda pi, j, mi: (mi, 0))],
        out_specs=pl.BlockSpec((1, SLABS, kblk, N2_TILE, LANES), lambda pi, j, mi: (pi, 0, mi, j, 0)),
        out_shape=jax.ShapeDtypeStruct((p, SLABS, m // 2, n2, LANES), jnp.uint32),
        compiler_params=_cparams("parallel", "parallel", "parallel"),
        name="hy_stage_a",
    )(*([x5] * SLABS), fmat)


MXU_COLS = 256
SLABS_PER_DOT = MXU_COLS // LANES


def _slab_rows(a_ref, j, h):
    parts = [_unpack_pair(a_ref[0, h * SLABS_PER_DOT + c, j]) for c in range(SLABS_PER_DOT)]
    return jnp.concatenate([jnp.concatenate([re, im], axis=0) for re, im in parts], axis=1).astype(BF16)


def _stage_b_kernel(a_ref, kf_ref, g_ref, gt_ref, o_ref):
    kb, n2 = a_ref.shape[2], a_ref.shape[3]
    for j in range(kb):
        for h in range(SLABS // SLABS_PER_DOT):
            lanes = slice(h * MXU_COLS, (h + 1) * MXU_COLS)
            u = jnp.dot(g_ref[j], _slab_rows(a_ref, j, h), preferred_element_type=F32)
            ur, ui = u[:n2], u[n2:]
            kr, ki = kf_ref[0, j, :, lanes], kf_ref[1, j, :, lanes]
            p = jnp.concatenate([ur * kr - ui * ki, ur * ki + ui * kr], axis=0).astype(BF16)
            v = jnp.dot(gt_ref[j], p, preferred_element_type=F32)
            packed = _pack_pair(v[:n2], v[n2:])
            for c in range(SLABS_PER_DOT):
                o_ref[0, h * SLABS_PER_DOT + c, j] = packed[:, c * LANES:(c + 1) * LANES]


def _stage_b(a5, kf, g, gt, kb):
    p, _, n1, n2, _ = a5.shape
    blk = pl.BlockSpec((1, SLABS, kb, n2, LANES), lambda i, pi: (pi, 0, i, 0, 0))
    gspec = pl.BlockSpec((kb, 2 * n2, 2 * n2), lambda i, pi: (i, 0, 0))
    return pl.pallas_call(
        _stage_b_kernel,
        grid=(n1 // kb, p),
        in_specs=[blk, pl.BlockSpec((2, kb, n2, HY_CH), lambda i, pi: (0, i, 0, 0)), gspec, gspec],
        out_specs=blk,
        out_shape=jax.ShapeDtypeStruct(a5.shape, jnp.uint32),
        compiler_params=_cparams("parallel", "parallel"),
        name="hy_stage_b",
    )(a5, kf, g, gt)


def _stage_bf_kernel(a_ref, g_ref, o_ref, *, scale):
    kb, n2 = a_ref.shape[2], a_ref.shape[3]
    for j in range(kb):
        for h in range(SLABS // SLABS_PER_DOT):
            lanes = slice(h * MXU_COLS, (h + 1) * MXU_COLS)
            u = jnp.dot(g_ref[j], _slab_rows(a_ref, j, h), preferred_element_type=F32) * scale
            o_ref[0, j, :, lanes] = u[:n2]
            o_ref[1, j, :, lanes] = u[n2:]


def _stage_bf(a5, g, kb, scale):
    _, _, n1, n2, _ = a5.shape
    return pl.pallas_call(
        functools.partial(_stage_bf_kernel, scale=scale),
        grid=(n1 // kb,),
        in_specs=[pl.BlockSpec((1, SLABS, kb, n2, LANES), lambda i: (0, 0, i, 0, 0)),
                  pl.BlockSpec((kb, 2 * n2, 2 * n2), lambda i: (i, 0, 0))],
        out_specs=pl.BlockSpec((2, kb, n2, HY_CH), lambda i: (0, i, 0, 0)),
        out_shape=jax.ShapeDtypeStruct((2, n1, n2, HY_CH), F32),
        compiler_params=_cparams("parallel"),
        name="hy_filter_spectrum",
    )(a5, g)


def _stage_c_kernel(*refs, nz, a, mb):
    v_ref, f_ref = refs[0], refs[1]
    x0c_refs, u_refs = refs[2:2 + SLABS], refs[2 + SLABS:2 + 2 * SLABS]
    skip_ref, o_ref, acc_ref = refs[2 + 2 * SLABS:]
    mk = pl.program_id(2)
    kblk = mb // 2
    v2 = v_ref.reshape(SLABS * kblk * N2_TILE, LANES)

    @pl.when(mk == 0)
    def _():
        acc_ref[...] = jnp.zeros_like(acc_ref)

    for s in range(N2_TILE):
        packed = jnp.concatenate([v2[pl.ds(c * kblk * N2_TILE + s, kblk, stride=N2_TILE), :] for c in range(SLABS)],
                                 axis=1)
        vs = jnp.concatenate(_unpack_pair(packed), axis=0).astype(BF16)
        acc_ref[s] += jnp.dot(f_ref[...], vs, preferred_element_type=F32)

    @pl.when(mk == pl.num_programs(2) - 1)
    def _():
        o2 = o_ref.reshape(nz * SLABS * a * N2_TILE, LANES)
        x2 = [r.reshape(nz * a * N2_TILE, LANES) for r in x0c_refs]
        u2 = [r.reshape(nz * a * N2_TILE, LANES) for r in u_refs]
        for s in range(N2_TILE):
            y = acc_ref[s]
            for z in range(nz):
                for c in range(SLABS):
                    rows = pl.ds(z * a * N2_TILE + s, a, stride=N2_TILE)
                    yc = y[z * a:(z + 1) * a, c * LANES:(c + 1) * LANES]
                    val = x2[c][rows, :] * (yc + skip_ref[0:1, c * LANES:(c + 1) * LANES] * u2[c][rows, :])
                    o2[pl.ds((z * SLABS + c) * a * N2_TILE + s, a, stride=N2_TILE), :] = val


def _stage_c(v5, fmat, x0c5, u5, skip, mb):
    nz, p, a, n2, _ = u5.shape
    m = 2 * v5.shape[2]
    xspec = lambda c: pl.BlockSpec((nz, 1, a, N2_TILE, LANES), lambda pi, j, mk: (0, pi, 0, j, c))
    return pl.pallas_call(
        functools.partial(_stage_c_kernel, nz=nz, a=a, mb=mb),
        grid=(p, n2 // N2_TILE, m // mb),
        in_specs=[pl.BlockSpec((1, SLABS, mb // 2, N2_TILE, LANES), lambda pi, j, mk: (pi, 0, mk, j, 0)),
                  pl.BlockSpec((nz * a, mb), lambda pi, j, mk: (0, mk))]
        + [xspec(c) for c in range(SLABS)] * 2 + [_const_spec((1, HY_CH))],
        out_specs=pl.BlockSpec((nz, 1, SLABS, a, N2_TILE, LANES), lambda pi, j, mk: (0, pi, 0, 0, j, 0)),
        out_shape=jax.ShapeDtypeStruct((nz, p, SLABS, a, n2, LANES), F32),
        scratch_shapes=[pltpu.VMEM((N2_TILE, nz * a, HY_CH), F32)],
        compiler_params=_cparams("parallel", "parallel", "arbitrary"),
        name="hy_stage_c",
    )(v5, fmat, *([x0c5] * SLABS), *([u5] * SLABS), skip)


def _filter_kernel(zt_ref, tn_ref, w1t_ref, b1_ref, fr_ref, w2t_ref, b2_ref, w3_ref, dl_ref, o_ref, *, seq_len):
    hi = lax.Precision.HIGHEST
    fr = fr_ref[...]
    h = jnp.sin(fr * (jnp.dot(w1t_ref[...], zt_ref[...], precision=hi, preferred_element_type=F32) + b1_ref[...]))
    h = jnp.sin(fr * (jnp.dot(w2t_ref[...], h, precision=hi, preferred_element_type=F32) + b2_ref[...]))
    taps = jnp.dot(h.T.astype(BF16), w3_ref[...], preferred_element_type=F32)
    tr = taps.shape[0]
    n = pl.program_id(0) * tr + lax.broadcasted_iota(jnp.int32, (tr, HY_CH), 0)
    sel = jnp.where(n < seq_len, taps[:, :HY_CH], jnp.where(n > seq_len, taps[:, HY_CH:], 0.0))
    o_ref[...] = sel * jnp.exp(-tn_ref[...] * dl_ref[...])


def _filter_taps(zfeat, f_w1, f_b1, f_freq, f_w2, f_b2, f_w3, absdelta, seq_len, tr):
    n = zfeat.shape[0]
    cs = lambda a: _const_spec(a.shape)
    args = (zfeat.T, zfeat[:, 0:1], f_w1.T, f_b1.T, f_freq.T, f_w2.T, f_b2.T, f_w3.astype(BF16), absdelta)
    return pl.pallas_call(
        functools.partial(_filter_kernel, seq_len=seq_len),
        grid=(n // tr,),
        in_specs=[pl.BlockSpec((POS_DIM, tr), lambda i: (0, i)), pl.BlockSpec((tr, 1), lambda i: (i, 0))]
        + [cs(a) for a in args[2:]],
        out_specs=pl.BlockSpec((tr, HY_CH), lambda i: (i, 0)),
        out_shape=jax.ShapeDtypeStruct((n, HY_CH), F32),
        compiler_params=_cparams("parallel"),
        name="hy_filter_taps",
    )(*args)


def _bias_kernel(rb_ref, oh_ref, o_ref):
    o_ref[...] = jnp.dot(rb_ref[...], oh_ref[...], precision=lax.Precision.HIGHEST,
                         preferred_element_type=F32)


def _bias_table(rel_bias):
    i = jnp.arange(BLOCK)[:, None]
    j = jnp.arange(3 * BLOCK)[None, :]
    rel = j - BLOCK - i
    nb2 = N_BUCKETS // 2
    max_exact = nb2 // 2
    n = jnp.abs(rel)
    large = max_exact + (jnp.log(jnp.maximum(n, 1).astype(F32) / max_exact)
                         / math.log(MAX_DIST / max_exact) * (nb2 - max_exact)).astype(jnp.int32)
    large = jnp.minimum(large, nb2 - 1)
    bucket = jnp.where(rel > 0, nb2, 0) + jnp.where(n < max_exact, n, large)
    onehot = (bucket.reshape(1, -1) == jnp.arange(N_BUCKETS)[:, None]).astype(F32)
    cols = onehot.shape[1]
    tc = cols // 4
    table = pl.pallas_call(
        _bias_kernel,
        grid=(4,),
        in_specs=[_const_spec((ATT_HEADS, N_BUCKETS)), pl.BlockSpec((N_BUCKETS, tc), lambda c: (0, c))],
        out_specs=pl.BlockSpec((ATT_HEADS, tc), lambda c: (0, c)),
        out_shape=jax.ShapeDtypeStruct((ATT_HEADS, cols), F32),
        compiler_params=_cparams("parallel"),
        name="att_bias_table",
    )(rel_bias.T, onehot)
    table = table.reshape(ATT_HEADS, BLOCK, 3 * BLOCK)
    table = jnp.where((n <= WINDOW)[None], table, -jnp.inf)
    table = table.reshape(KV_HEADS, GQA_GROUP, BLOCK, 3 * BLOCK)
    return table.transpose(0, 3, 1, 2).reshape(KV_HEADS, 3 * BLOCK, GQA_GROUP * BLOCK)


def _attn_kernel(q_ref, km_ref, kp_ref, kn_ref, vm_ref, vp_ref, vn_ref, bias_ref, sink_ref, o_ref, *, nsub):
    i = pl.program_id(1)
    neg = -jnp.inf
    pen_first = jnp.where(i == 0, neg, 0.0)
    pen_last = jnp.where(i == pl.num_programs(1) - 1, neg, 0.0)
    kwin = jnp.concatenate([kp_ref[0], km_ref[0], kn_ref[0]], axis=0)
    vwin = jnp.concatenate([vp_ref[0], vm_ref[0], vn_ref[0]], axis=1)
    lane = lax.broadcasted_iota(jnp.int32, (BLOCK, LANES), 1)
    lo_half = lane < HEAD_DIM
    ones = jnp.ones((SUM_ROWS, 3 * BLOCK), BF16)
    units = [(s, g) for s in range(nsub) for g in range(KV_HEADS)]

    def scores(s, g):
        qs = q_ref[0, s * BLOCK:(s + 1) * BLOCK, :]
        rows = []
        for pr in range(2):
            qp = qs[:, (2 * g + pr) * LANES:(2 * g + pr + 1) * LANES]
            rows.append(jnp.where(lo_half, qp, jnp.zeros_like(qp)))
            rows.append(jnp.where(lo_half, jnp.zeros_like(qp), qp))
        qg = jnp.concatenate(rows, axis=0)
        kg = kwin[s * BLOCK:(s + 3) * BLOCK, g * LANES:(g + 1) * LANES]
        return lax.dot_general(kg, qg, _NT, preferred_element_type=F32)

    sc_next = scores(*units[0])
    for idx, (s, g) in enumerate(units):
        sc = sc_next + bias_ref[g]
        if idx + 1 < len(units):
            sc_next = scores(*units[idx + 1])
        if s == 0:
            sc = jnp.concatenate([sc[:BLOCK] + pen_first, sc[BLOCK:]], axis=0)
        if s == nsub - 1:
            sc = jnp.concatenate([sc[:2 * BLOCK], sc[2 * BLOCK:] + pen_last], axis=0)
        sk = sink_ref[g]
        m = jnp.maximum(jnp.max(sc, axis=0, keepdims=True), sk)
        p = jnp.exp2(sc - m).astype(BF16)
        vg = jnp.concatenate([vwin[g * HEAD_DIM:(g + 1) * HEAD_DIM, s * BLOCK:(s + 3) * BLOCK], ones],
                             axis=0)
        o = jnp.dot(vg, p, preferred_element_type=F32)
        den = o[HEAD_DIM:HEAD_DIM + 1] + jnp.exp2(sk - m)
        on = o[:HEAD_DIM] / den
        for pr in range(2):
            pair = jnp.concatenate([on[:, (2 * pr) * BLOCK:(2 * pr + 1) * BLOCK],
                                    on[:, (2 * pr + 1) * BLOCK:(2 * pr + 2) * BLOCK]], axis=0)
            o_ref[0, s * BLOCK:(s + 1) * BLOCK, (2 * g + pr) * LANES:(2 * g + pr + 1) * LANES] = (
                pair.T.astype(o_ref.dtype))


SUM_ROWS = 64


def _attention(q, kd, vt, bias_t, sink_t, tq):
    b, l, _ = q.shape
    nsub = tq // BLOCK
    nblk = l // BLOCK
    kw = 2 * KV_W
    main = lambda w: pl.BlockSpec((1, tq, w), lambda bi, i: (bi, i, 0))
    prev = pl.BlockSpec((1, BLOCK, kw), lambda bi, i: (bi, jnp.maximum(i * nsub - 1, 0), 0))
    nxt = pl.BlockSpec((1, BLOCK, kw), lambda bi, i: (bi, jnp.minimum((i + 1) * nsub, nblk - 1), 0))
    vmain = pl.BlockSpec((1, KV_W, tq), lambda bi, i: (bi, 0, i))
    vprev = pl.BlockSpec((1, KV_W, BLOCK), lambda bi, i: (bi, 0, jnp.maximum(i * nsub - 1, 0)))
    vnxt = pl.BlockSpec((1, KV_W, BLOCK), lambda bi, i: (bi, 0, jnp.minimum((i + 1) * nsub, nblk - 1)))
    return pl.pallas_call(
        functools.partial(_attn_kernel, nsub=nsub),
        grid=(b, l // tq),
        in_specs=[main(ATT_W), main(kw), prev, nxt, vmain, vprev, vnxt,
                  _const_spec(bias_t.shape), _const_spec(sink_t.shape)],
        out_specs=main(ATT_W),
        out_shape=jax.ShapeDtypeStruct((b, l, ATT_W), BF16),
        compiler_params=_cparams("parallel", "parallel"),
        name="window_attn",
    )(q, kd, kd, kd, vt, vt, vt, bias_t, sink_t)


FF_CHUNK = 256


def _ffn_kernel(x_ref, yh_ref, ya_ref, hg_ref, ag_ref, wo_ref, g2_ref, wgu_ref, wd_ref, o_ref, act_ref):
    def rms(t, g):
        return (t * lax.rsqrt(jnp.mean(t * t, axis=-1, keepdims=True) + EPS) * g).astype(BF16)

    yh = jnp.concatenate([yh_ref[0, c] for c in range(SLABS)], axis=1)
    mixed = jnp.concatenate([rms(yh, hg_ref[...]), rms(ya_ref[...].astype(F32), ag_ref[...])], axis=1)
    h = x_ref[...] + jnp.dot(mixed, wo_ref[...], preferred_element_type=F32)
    hn = rms(h, g2_ref[...])
    for c in range(D_FF // FF_CHUNK):
        lo = c * FF_CHUNK
        gate = jnp.dot(hn, wgu_ref[:, lo:lo + FF_CHUNK], preferred_element_type=F32)
        up = jnp.dot(hn, wgu_ref[:, D_FF + lo:D_FF + lo + FF_CHUNK], preferred_element_type=F32)
        act_ref[:, lo:lo + FF_CHUNK] = (gate / (1.0 + jnp.exp(-gate)) * up).astype(BF16)
    o_ref[...] = h + jnp.dot(act_ref[...], wd_ref[...], preferred_element_type=F32)


def _ffn(x2d, yh, ya, hg, ag, wo_b, g2, wgu_b, wd_b, tm):
    t = x2d.shape[0]
    nt = yh.shape[2] // tm
    row = lambda w: pl.BlockSpec((tm, w), lambda i: (i, 0))
    yspec = pl.BlockSpec((1, SLABS, tm, LANES), lambda i: (i // nt, 0, i % nt, 0))
    return pl.pallas_call(
        _ffn_kernel,
        grid=(t // tm,),
        in_specs=[row(D_MODEL), yspec, row(ATT_W), _const_spec((1, HY_CH)), _const_spec((1, ATT_W)),
                  _const_spec((D_MODEL, D_MODEL)), _const_spec((1, D_MODEL)),
                  _const_spec((D_MODEL, 2 * D_FF)), _const_spec((D_FF, D_MODEL))],
        out_specs=row(D_MODEL),
        out_shape=jax.ShapeDtypeStruct((t, D_MODEL), F32),
        scratch_shapes=[pltpu.VMEM((tm, D_FF), BF16)],
        compiler_params=_cparams("parallel"),
        name="outproj_swiglu",
    )(x2d, yh, ya, hg, ag, wo_b, g2, wgu_b, wd_b)


def _dft_tables(seq_len, paired, kblk):
    n = 2 * seq_len
    n1 = n // FFT_N2
    a = n1 // 2
    kk = np.arange(n1)[:, None]
    th_half = 2.0 * np.pi * ((kk * np.arange(a)[None, :]) % n1) / n1
    c, s = np.cos(th_half), np.sin(th_half)

    def blocked(re, im):
        return np.concatenate([np.concatenate([re[i:i + kblk], im[i:i + kblk]], axis=0)
                               for i in range(0, n1, kblk)], axis=0)

    if paired:
        fa = blocked(np.concatenate([c, s], axis=1), np.concatenate([-s, c], axis=1))
        fc = blocked(np.concatenate([c, s], axis=1), np.concatenate([-s, c], axis=1)).T
    else:
        fa = blocked(c, -s)
        fc = blocked(c, -s).T
    th_full = 2.0 * np.pi * ((kk * np.arange(n1)[None, :]) % n1) / n1
    faf = blocked(np.cos(th_full), -np.sin(th_full))
    to = lambda m: jnp.asarray(m.astype(np.float32)).astype(BF16)
    return to(fa), to(fc), to(faf)


def _twiddle_tables(seq_len):
    n = 2 * seq_len
    n1 = n // FFT_N2
    k1 = jnp.arange(n1, dtype=jnp.int32)[:, None, None]
    k2 = jnp.arange(FFT_N2, dtype=jnp.int32)[None, :, None]
    n2 = jnp.arange(FFT_N2, dtype=jnp.int32)[None, None, :]
    m = (n2 * (k2 * n1 + k1)) % n
    ang = m.astype(F32) * (-2.0 * math.pi / n)
    gr, gi = jnp.cos(ang), jnp.sin(ang)
    g = jnp.concatenate([jnp.concatenate([gr, -gi], axis=2), jnp.concatenate([gi, gr], axis=2)], axis=1)
    return g.astype(BF16), jnp.swapaxes(g, 1, 2).astype(BF16)


def _filter_features(seq_len):
    n = np.arange(2 * seq_len)
    t = np.where(n < seq_len, n, 2 * seq_len - n).astype(np.float32)[:, None]
    t_idx = jnp.asarray(t)
    t_norm = t_idx / max(seq_len - 1, 1)
    bands = jnp.linspace(1e-4, N_BANDS - 1, N_BANDS, dtype=F32)
    w = (2.0 * math.pi) * t_idx * bands[None, :] / seq_len
    return jnp.concatenate([t_norm, jnp.cos(w), -jnp.sin(w)], axis=-1)


def _abs_deltas():
    min_decay = math.log(DECAY_TARGET) / FAST_DECAY_PCT
    max_decay = math.log(DECAY_TARGET) / SLOW_DECAY_PCT
    return jnp.abs(jnp.linspace(min_decay, max_decay, HY_CH, dtype=F32))[None, :]


def _pick(n, target):
    t = min(n, target)
    while n % t:
        t //= 2
    return t


def _layer(x, lw, shared, st):
    b, l, _ = x.shape
    t = b * l
    x2d = x.reshape(t, D_MODEL)
    tm = _pick(t, 512)
    u, x0c, q, kd, vt = _inproj(x2d, lw["norm1"], lw["w_in"], lw["conv_w"], lw["conv_b"],
                                shared["bq"], shared["bk"], lw["qg"], lw["kg"], tm, l)
    r3 = lambda a: a.reshape(b, l, a.shape[-1])

    n1 = 2 * l // FFT_N2
    a = n1 // 2
    nz = st["nz"]
    p = b // nz
    u5 = u.reshape(nz, p, a, FFT_N2, HY_CH)
    x0c5 = x0c.reshape(nz, p, a, FFT_N2, HY_CH)
    mb = _pick(2 * n1, STAGE_MB)
    ah = _stage_a(u5, st["fa"], mb)
    vv = _stage_b(ah, lw["kf"][st["key"]], st["g"], st["gt"], _pick(n1, STAGE_KB))
    yh = _stage_c(vv, st["fc"], x0c5, u5, lw["skip"], mb)
    yh = yh.reshape(b, SLABS, l, LANES)

    ya = _attention(r3(q), r3(kd), vt, shared["bias"], lw["sink_t"], _pick(l, 512))
    ya = ya.reshape(t, ATT_W)

    out = _ffn(x2d, yh, ya, lw["hy_gain"], lw["at_gain"], lw["w_out"], lw["norm2"],
               lw["w_gate_up"], lw["w_down"], tm)
    return out.reshape(b, l, D_MODEL)


def _filter_spectrum(lw_raw, st, seq_len):
    n = 2 * seq_len
    n1 = n // FFT_N2
    taps = _filter_taps(st["zfeat"], lw_raw["f_w1"], lw_raw["f_b1"], lw_raw["f_freq"], lw_raw["f_w2"],
                        lw_raw["f_b2"], lw_raw["f_w3"], st["absdelta"], seq_len, _pick(n, 512))
    ah = _stage_a(taps.reshape(1, 1, n1, FFT_N2, HY_CH), st["faf"], _pick(2 * n1, STAGE_MB))
    return _stage_bf(ah, st["g"], _pick(n1, STAGE_KB), 1.0 / n)


def kernel(x_prompt, x_sample, norm1, w_in, conv_w, conv_b, f_w1, f_b1, f_freq, f_w2, f_b2, f_w3,
           hy_skip, q_gain, k_gain, sink, rel_bias, hy_gain, at_gain, w_out, norm2, w_gate_up, w_down):
    depth = norm1.shape[0]
    streams = {}
    for key, x in (("prompt", x_prompt), ("sample", x_sample)):
        b, l, _ = x.shape
        paired = b % 2 == 0
        fa, fc, faf = _dft_tables(l, paired, _pick(4 * l // FFT_N2, STAGE_MB) // 2)
        g, gt = _twiddle_tables(l)
        streams[key] = dict(key=key, nz=2 if paired else 1, fa=fa, fc=fc, faf=faf, g=g, gt=gt,
                            zfeat=_filter_features(l), absdelta=_abs_deltas())

    head_id = np.arange(ATT_W) // HEAD_DIM
    shared = dict(
        bq=jnp.asarray((head_id[:, None] == head_id[None, :]).astype(np.float32) / HEAD_DIM).astype(BF16),
        bk=jnp.asarray((head_id[:KV_W, None] == head_id[None, :KV_W]).astype(np.float32) / HEAD_DIM).astype(BF16),
        bias=_bias_table(rel_bias) * LOG2E,
    )

    y_prompt, y_sample = x_prompt, x_sample
    for li in range(depth):
        raw = dict(f_w1=f_w1[li], f_b1=f_b1[li][None, :], f_freq=f_freq[li][None, :], f_w2=f_w2[li],
                   f_b2=f_b2[li][None, :], f_w3=f_w3[li])
        lw = dict(
            norm1=norm1[li][None, :], w_in=w_in[li].astype(BF16),
            qg=jnp.tile(q_gain[li], ATT_HEADS)[None, :] * (HEAD_DIM ** -0.5 * LOG2E),
            kg=jnp.tile(k_gain[li], KV_HEADS)[None, :],
            conv_w=conv_w[li], conv_b=conv_b[li][None, :],
            skip=hy_skip[li][None, :],
            sink_t=jnp.repeat(sink[li] * LOG2E, BLOCK).reshape(KV_HEADS, 1, GQA_GROUP * BLOCK),
            hy_gain=hy_gain[li][None, :], at_gain=at_gain[li][None, :],
            w_out=w_out[li].astype(BF16), norm2=norm2[li][None, :],
            w_gate_up=w_gate_up[li].astype(BF16), w_down=w_down[li].astype(BF16),
        )
        lw["kf"] = {key: _filter_spectrum(raw, st, {"prompt": x_prompt, "sample": x_sample}[key].shape[1])
                    for key, st in streams.items()}
        y_prompt = _layer(y_prompt, lw, shared, streams["prompt"])
        y_sample = _layer(y_sample, lw, shared, streams["sample"])
    return (y_prompt, y_sample)
```

```python
import functools
import math

import numpy as np
import jax
import jax.numpy as jnp
from jax import lax
from jax.experimental import pallas as pl
from jax.experimental.pallas import tpu as pltpu

F32 = jnp.float32
BF16 = jnp.bfloat16

D_MODEL = 1024
ATT_HEADS = 8
KV_HEADS = 2
HEAD_DIM = 64
GQA_GROUP = ATT_HEADS // KV_HEADS
ATT_W = ATT_HEADS * HEAD_DIM
KV_W = KV_HEADS * HEAD_DIM
WINDOW = 128
BLOCK = 128
N_BUCKETS = 32
MAX_DIST = 128
HY_CH = D_MODEL - ATT_W
FILTER_HIDDEN = 64
N_BANDS = 16
POS_DIM = 1 + 2 * N_BANDS
FAST_DECAY_PCT = 0.3
SLOW_DECAY_PCT = 1.5
DECAY_TARGET = 1e-2
IN_W = 3 * HY_CH + (ATT_HEADS + 2 * KV_HEADS) * HEAD_DIM
D_FF = -(-8 * D_MODEL // (3 * 256)) * 256
EPS = 1e-6
LOG2E = math.log2(math.e)

FFT_N2 = 64
LANES = 128
N2_TILE = 8
SLABS = HY_CH // LANES
STAGE_MB = 512
STAGE_KB = 16
INPROJ_TM = 512
FFN_TM = 512
VMEM_LIMIT = 56 * 1024 * 1024

_NT = (((1,), (1,)), ((), ()))


def _cparams(*sem):
    return pltpu.CompilerParams(dimension_semantics=sem, vmem_limit_bytes=VMEM_LIMIT)


def _const_spec(shape):
    nd = len(shape)
    return pl.BlockSpec(shape, lambda *_: (0,) * nd, pipeline_mode=pl.Buffered(1))


HALO = 16


def _inproj_kernel(x_ref, xp_ref, xn_ref, g1_ref, w_ref, cw_ref, cb_ref, bq_ref, bk_ref, qg_ref, kg_ref,
                   u_ref, x0c_ref, q_ref, k_ref, v_ref, *, nt):
    i = pl.program_id(0)
    tm = x_ref.shape[0]
    xp = jnp.where(i % nt == 0, 0.0, xp_ref[...])
    xn = jnp.where(i % nt == nt - 1, 0.0, xn_ref[...])
    xe = jnp.concatenate([xp, x_ref[...], xn], axis=0)
    xe = (xe * lax.rsqrt(jnp.mean(xe * xe, axis=-1, keepdims=True) + EPS) * g1_ref[...]).astype(BF16)
    xb = xe[HALO:HALO + tm]

    def seg(lo, hi):
        return jnp.dot(xb, w_ref[:, lo:hi], preferred_element_type=F32)

    o = 3 * HY_CH
    v = seg(o + ATT_W + KV_W, IN_W)
    v_ref[0] = v.T.astype(BF16)

    k = seg(o + ATT_W, o + ATT_W + KV_W)
    ms = jnp.dot((k * k).astype(BF16), bk_ref[...], preferred_element_type=F32)
    kn = k * lax.rsqrt(ms + EPS) * kg_ref[...]
    lo_half = lax.broadcasted_iota(jnp.int32, kn.shape, 1) < HEAD_DIM
    kr = pltpu.roll(kn, HEAD_DIM, axis=1)
    k_ref[...] = jnp.concatenate([jnp.where(lo_half, kn, kr), jnp.where(lo_half, kr, kn)], axis=1).astype(BF16)

    q = seg(o, o + ATT_W)
    ms = jnp.dot((q * q).astype(BF16), bq_ref[...], preferred_element_type=F32)
    q_ref[...] = (q * lax.rsqrt(ms + EPS) * qg_ref[...]).astype(BF16)

    rows = slice(HALO, HALO + tm)

    def conv(j):
        c = slice(j * HY_CH, (j + 1) * HY_CH)
        he = jnp.dot(xe, w_ref[:, c], preferred_element_type=F32)
        return (pltpu.roll(he, 1, axis=0)[rows] * cw_ref[0:1, c] + he[rows] * cw_ref[1:2, c]
                + pltpu.roll(he, tm + 2 * HALO - 1, axis=0)[rows] * cw_ref[2:3, c] + cb_ref[0:1, c])

    x0c_ref[...] = conv(0)
    u_ref[...] = conv(1) * conv(2)


def _inproj(x2d, g1, w_in_b, conv_w, conv_b, bq, bk, qg, kg, tm, seq_len):
    t = x2d.shape[0]
    nt = seq_len // tm
    nh = tm // HALO
    row = lambda w: pl.BlockSpec((tm, w), lambda i: (i, 0))
    prev = pl.BlockSpec((HALO, D_MODEL), lambda i: (jnp.maximum(i * nh - 1, 0), 0))
    nxt = pl.BlockSpec((HALO, D_MODEL), lambda i: (jnp.minimum((i + 1) * nh, t // HALO - 1), 0))
    return pl.pallas_call(
        functools.partial(_inproj_kernel, nt=nt),
        grid=(t // tm,),
        in_specs=[row(D_MODEL), prev, nxt, _const_spec((1, D_MODEL)), _const_spec((D_MODEL, IN_W)),
                  _const_spec((3, 3 * HY_CH)), _const_spec((1, 3 * HY_CH)),
                  _const_spec((ATT_W, ATT_W)), _const_spec((KV_W, KV_W)),
                  _const_spec((1, ATT_W)), _const_spec((1, KV_W))],
        out_specs=[row(HY_CH), row(HY_CH), row(ATT_W), row(2 * KV_W),
                   pl.BlockSpec((1, KV_W, tm), lambda i: (i // nt, 0, i % nt))],
        out_shape=[jax.ShapeDtypeStruct((t, HY_CH), F32), jax.ShapeDtypeStruct((t, HY_CH), F32),
                   jax.ShapeDtypeStruct((t, ATT_W), BF16), jax.ShapeDtypeStruct((t, 2 * KV_W), BF16),
                   jax.ShapeDtypeStruct((t // seq_len, KV_W, seq_len), BF16)],
        compiler_params=_cparams("parallel"),
        name="inproj",
    )(x2d, x2d, x2d, g1, w_in_b, conv_w, conv_b, bq, bk, qg, kg)


def _stage_a_kernel(*refs, nz, a, mb):
    x_refs, f_ref, o_ref = refs[:SLABS], refs[SLABS], refs[SLABS + 1]
    kblk = mb // 2
    xs = [r.reshape(nz * a * N2_TILE, LANES) for r in x_refs]
    o2 = o_ref.reshape(SLABS * kblk * N2_TILE, LANES)
    f = f_ref[...]
    for s in range(N2_TILE):
        x = jnp.concatenate(
            [jnp.concatenate([xs[c][pl.ds(z * a * N2_TILE + s, a, stride=N2_TILE), :] for z in range(nz)], axis=0)
             for c in range(SLABS)], axis=1).astype(BF16)
        r = jnp.dot(f, x, preferred_element_type=F32)
        packed = _pack_pair(r[:kblk], r[kblk:])
        for c in range(SLABS):
            o2[pl.ds(c * kblk * N2_TILE + s, kblk, stride=N2_TILE), :] = packed[:, c * LANES:(c + 1) * LANES]


def _pack_pair(re, im):
    rb = lax.bitcast_convert_type(re.astype(BF16).astype(F32), jnp.uint32)
    ib = lax.bitcast_convert_type(im.astype(BF16).astype(F32), jnp.uint32)
    return (rb >> 16) | ib


def _unpack_pair(p):
    re = lax.bitcast_convert_type(p << 16, F32)
    im = lax.bitcast_convert_type(p & jnp.uint32(0xFFFF0000), F32)
    return re, im


def _stage_a(x5, fmat, mb):
    nz, p, a, n2, _ = x5.shape
    m = fmat.shape[0]
    kblk = mb // 2
    xspec = lambda c: pl.BlockSpec((nz, 1, a, N2_TILE, LANES), lambda pi, j, mi: (0, pi, 0, j, c))
    return pl.pallas_call(
        functools.partial(_stage_a_kernel, nz=nz, a=a, mb=mb),
        grid=(p, n2 // N2_TILE, m // mb),
        in_specs=[xspec(c) for c in range(SLABS)] + [pl.BlockSpec((mb, nz * a), lambda pi, j, mi: (mi, 0))],
        out_specs=pl.BlockSpec((1, SLABS, kblk, N2_TILE, LANES), lambda pi, j, mi: (pi, 0, mi, j, 0)),
        out_shape=jax.ShapeDtypeStruct((p, SLABS, m // 2, n2, LANES), jnp.uint32),
        compiler_params=_cparams("parallel", "parallel", "parallel"),
        name="hy_stage_a",
    )(*([x5] * SLABS), fmat)


MXU_COLS = 256
SLABS_PER_DOT = MXU_COLS // LANES


def _slab_rows(a_ref, j, h):
    parts = [_unpack_pair(a_ref[0, h * SLABS_PER_DOT + c, j]) for c in range(SLABS_PER_DOT)]
    return jnp.concatenate([jnp.concatenate([re, im], axis=0) for re, im in parts], axis=1).astype(BF16)


def _spectral_mul(u, kr, ki, n2):
    ur, ui = u[:n2], u[n2:]
    return jnp.concatenate([ur * kr - ui * ki, ur * ki + ui * kr], axis=0).astype(BF16)


def _stage_b_kernel(a_ref, kf_ref, g_ref, gt_ref, *rest, herm):
    if herm:
        edge_ref, kfh_ref, o_ref = rest
    else:
        (o_ref,) = rest
    kb, n2 = a_ref.shape[2], a_ref.shape[3]

    def store(j, h, v):
        packed = _pack_pair(v[:n2], v[n2:])
        for c in range(SLABS_PER_DOT):
            o_ref[0, h * SLABS_PER_DOT + c, j] = packed[:, c * LANES:(c + 1) * LANES]

    def regular(j, h):
        lanes = slice(h * MXU_COLS, (h + 1) * MXU_COLS)
        u = jnp.dot(g_ref[j], _slab_rows(a_ref, j, h), preferred_element_type=F32)
        p = _spectral_mul(u, kf_ref[0, j, :, lanes], kf_ref[1, j, :, lanes], n2)
        store(j, h, jnp.dot(gt_ref[j], p, preferred_element_type=F32))

    def edge(h):
        lanes = slice(h * MXU_COLS, (h + 1) * MXU_COLS)
        x = _slab_rows(a_ref, 0, h)
        p0 = _spectral_mul(jnp.dot(edge_ref[0, 0], x, preferred_element_type=F32),
                           kf_ref[0, 0, :, lanes], kf_ref[1, 0, :, lanes], n2)
        ph = _spectral_mul(jnp.dot(edge_ref[0, 1], x, preferred_element_type=F32),
                           kfh_ref[0, 0, :, lanes], kfh_ref[0, 1, :, lanes], n2)
        store(0, h, jnp.dot(edge_ref[0, 2], p0, preferred_element_type=F32)
              + jnp.dot(edge_ref[0, 3], ph, preferred_element_type=F32))

    for j in range(kb):
        for h in range(SLABS // SLABS_PER_DOT):
            if herm and j == 0:
                edge(h)
            else:
                regular(j, h)


def _stage_b(a5, kf, g, gt, kb, edge=None, kfh=None):
    p, _, k1n, n2, _ = a5.shape
    herm = edge is not None
    blk = pl.BlockSpec((1, SLABS, kb, n2, LANES), lambda i, pi: (pi, 0, i, 0, 0))
    gspec = pl.BlockSpec((kb, 2 * n2, 2 * n2), lambda i, pi: (i, 0, 0))
    extra = [pl.BlockSpec((1,) + edge.shape[1:], lambda i, pi: (i, 0, 0, 0)),
             pl.BlockSpec((1,) + kfh.shape[1:], lambda i, pi: (i, 0, 0, 0))] if herm else []
    return pl.pallas_call(
        functools.partial(_stage_b_kernel, herm=herm),
        grid=(k1n // kb, p),
        in_specs=[blk, pl.BlockSpec((2, kb, n2, HY_CH), lambda i, pi: (0, i, 0, 0)), gspec, gspec] + extra,
        out_specs=blk,
        out_shape=jax.ShapeDtypeStruct(a5.shape, jnp.uint32),
        compiler_params=_cparams("parallel", "parallel"),
        name="hy_stage_b",
    )(a5, kf, g, gt, *([edge, kfh] if herm else []))


def _stage_bf_kernel(a_ref, g_ref, *rest, scale, herm):
    if herm:
        edge_ref, o_ref, oh_ref = rest
    else:
        (o_ref,) = rest
    kb, n2 = a_ref.shape[2], a_ref.shape[3]

    def regular(j, h):
        lanes = slice(h * MXU_COLS, (h + 1) * MXU_COLS)
        u = jnp.dot(g_ref[j], _slab_rows(a_ref, j, h), preferred_element_type=F32) * scale
        o_ref[0, j, :, lanes] = u[:n2]
        o_ref[1, j, :, lanes] = u[n2:]

    def edge(h):
        lanes = slice(h * MXU_COLS, (h + 1) * MXU_COLS)
        x = _slab_rows(a_ref, 0, h)
        u0 = jnp.dot(edge_ref[0, 0], x, preferred_element_type=F32) * scale
        uh = jnp.dot(edge_ref[0, 1], x, preferred_element_type=F32) * scale
        o_ref[0, 0, :, lanes] = u0[:n2]
        o_ref[1, 0, :, lanes] = u0[n2:]
        oh_ref[0, 0, :, lanes] = uh[:n2]
        oh_ref[0, 1, :, lanes] = uh[n2:]

    for j in range(kb):
        for h in range(SLABS // SLABS_PER_DOT):
            if herm and j == 0:
                edge(h)
            else:
                regular(j, h)


def _stage_bf(a5, g, kb, scale, edge=None):
    _, _, k1n, n2, _ = a5.shape
    herm = edge is not None
    out_specs = [pl.BlockSpec((2, kb, n2, HY_CH), lambda i: (0, i, 0, 0))]
    out_shape = [jax.ShapeDtypeStruct((2, k1n, n2, HY_CH), F32)]
    if herm:
        out_specs.append(pl.BlockSpec((1, 2, n2, HY_CH), lambda i: (i, 0, 0, 0)))
        out_shape.append(jax.ShapeDtypeStruct((k1n // kb, 2, n2, HY_CH), F32))
    return pl.pallas_call(
        functools.partial(_stage_bf_kernel, scale=scale, herm=herm),
        grid=(k1n // kb,),
        in_specs=[pl.BlockSpec((1, SLABS, kb, n2, LANES), lambda i: (0, 0, i, 0, 0)),
                  pl.BlockSpec((kb, 2 * n2, 2 * n2), lambda i: (i, 0, 0))]
        + ([pl.BlockSpec((1,) + edge.shape[1:], lambda i: (i, 0, 0, 0))] if herm else []),
        out_specs=out_specs,
        out_shape=out_shape,
        compiler_params=_cparams("parallel"),
        name="hy_filter_spectrum",
    )(a5, g, *([edge] if herm else []))


def _stage_c_kernel(*refs, nz, a, mb):
    v_ref, f_ref = refs[0], refs[1]
    x0c_refs, u_refs = refs[2:2 + SLABS], refs[2 + SLABS:2 + 2 * SLABS]
    skip_ref, o_ref, acc_ref = refs[2 + 2 * SLABS:]
    mk = pl.program_id(2)
    kblk = mb // 2
    v2 = v_ref.reshape(SLABS * kblk * N2_TILE, LANES)

    @pl.when(mk == 0)
    def _():
        acc_ref[...] = jnp.zeros_like(acc_ref)

    for s in range(N2_TILE):
        packed = jnp.concatenate([v2[pl.ds(c * kblk * N2_TILE + s, kblk, stride=N2_TILE), :] for c in range(SLABS)],
                                 axis=1)
        vs = jnp.concatenate(_unpack_pair(packed), axis=0).astype(BF16)
        acc_ref[s] += jnp.dot(f_ref[...], vs, preferred_element_type=F32)

    @pl.when(mk == pl.num_programs(2) - 1)
    def _():
        o2 = o_ref.reshape(nz * SLABS * a * N2_TILE, LANES)
        x2 = [r.reshape(nz * a * N2_TILE, LANES) for r in x0c_refs]
        u2 = [r.reshape(nz * a * N2_TILE, LANES) for r in u_refs]
        for s in range(N2_TILE):
            y = acc_ref[s]
            for z in range(nz):
                for c in range(SLABS):
                    rows = pl.ds(z * a * N2_TILE + s, a, stride=N2_TILE)
                    yc = y[z * a:(z + 1) * a, c * LANES:(c + 1) * LANES]
                    val = x2[c][rows, :] * (yc + skip_ref[0:1, c * LANES:(c + 1) * LANES] * u2[c][rows, :])
                    o2[pl.ds((z * SLABS + c) * a * N2_TILE + s, a, stride=N2_TILE), :] = val


def _stage_c(v5, fmat, x0c5, u5, skip, mb):
    nz, p, a, n2, _ = u5.shape
    m = 2 * v5.shape[2]
    xspec = lambda c: pl.BlockSpec((nz, 1, a, N2_TILE, LANES), lambda pi, j, mk: (0, pi, 0, j, c))
    return pl.pallas_call(
        functools.partial(_stage_c_kernel, nz=nz, a=a, mb=mb),
        grid=(p, n2 // N2_TILE, m // mb),
        in_specs=[pl.BlockSpec((1, SLABS, mb // 2, N2_TILE, LANES), lambda pi, j, mk: (pi, 0, mk, j, 0)),
                  pl.BlockSpec((nz * a, mb), lambda pi, j, mk: (0, mk))]
        + [xspec(c) for c in range(SLABS)] * 2 + [_const_spec((1, HY_CH))],
        out_specs=pl.BlockSpec((nz, 1, SLABS, a, N2_TILE, LANES), lambda pi, j, mk: (0, pi, 0, 0, j, 0)),
        out_shape=jax.ShapeDtypeStruct((nz, p, SLABS, a, n2, LANES), F32),
        scratch_shapes=[pltpu.VMEM((N2_TILE, nz * a, HY_CH), F32)],
        compiler_params=_cparams("parallel", "parallel", "arbitrary"),
        name="hy_stage_c",
    )(v5, fmat, *([x0c5] * SLABS), *([u5] * SLABS), skip)


def _filter_kernel(zt_ref, tn_ref, w1t_ref, b1_ref, fr_ref, w2t_ref, b2_ref, w3_ref, dl_ref, o_ref, *, seq_len):
    hi = lax.Precision.HIGHEST
    fr = fr_ref[...]
    h = jnp.sin(fr * (jnp.dot(w1t_ref[...], zt_ref[...], precision=hi, preferred_element_type=F32) + b1_ref[...]))
    h = jnp.sin(fr * (jnp.dot(w2t_ref[...], h, precision=hi, preferred_element_type=F32) + b2_ref[...]))
    taps = jnp.dot(h.T.astype(BF16), w3_ref[...], preferred_element_type=F32)
    tr = taps.shape[0]
    n = pl.program_id(0) * tr + lax.broadcasted_iota(jnp.int32, (tr, HY_CH), 0)
    sel = jnp.where(n < seq_len, taps[:, :HY_CH], jnp.where(n > seq_len, taps[:, HY_CH:], 0.0))
    o_ref[...] = sel * jnp.exp(-tn_ref[...] * dl_ref[...])


def _filter_taps(zfeat, f_w1, f_b1, f_freq, f_w2, f_b2, f_w3, absdelta, seq_len, tr):
    n = zfeat.shape[0]
    cs = lambda a: _const_spec(a.shape)
    args = (zfeat.T, zfeat[:, 0:1], f_w1.T, f_b1.T, f_freq.T, f_w2.T, f_b2.T, f_w3.astype(BF16), absdelta)
    return pl.pallas_call(
        functools.partial(_filter_kernel, seq_len=seq_len),
        grid=(n // tr,),
        in_specs=[pl.BlockSpec((POS_DIM, tr), lambda i: (0, i)), pl.BlockSpec((tr, 1), lambda i: (i, 0))]
        + [cs(a) for a in args[2:]],
        out_specs=pl.BlockSpec((tr, HY_CH), lambda i: (i, 0)),
        out_shape=jax.ShapeDtypeStruct((n, HY_CH), F32),
        compiler_params=_cparams("parallel"),
        name="hy_filter_taps",
    )(*args)


def _bias_kernel(rb_ref, oh_ref, o_ref):
    o_ref[...] = jnp.dot(rb_ref[...], oh_ref[...], precision=lax.Precision.HIGHEST,
                         preferred_element_type=F32)


def _bias_table(rel_bias):
    i = jnp.arange(BLOCK)[:, None]
    j = jnp.arange(3 * BLOCK)[None, :]
    rel = j - BLOCK - i
    nb2 = N_BUCKETS // 2
    max_exact = nb2 // 2
    n = jnp.abs(rel)
    large = max_exact + (jnp.log(jnp.maximum(n, 1).astype(F32) / max_exact)
                         / math.log(MAX_DIST / max_exact) * (nb2 - max_exact)).astype(jnp.int32)
    large = jnp.minimum(large, nb2 - 1)
    bucket = jnp.where(rel > 0, nb2, 0) + jnp.where(n < max_exact, n, large)
    onehot = (bucket.reshape(1, -1) == jnp.arange(N_BUCKETS)[:, None]).astype(F32)
    cols = onehot.shape[1]
    tc = cols // 4
    table = pl.pallas_call(
        _bias_kernel,
        grid=(4,),
        in_specs=[_const_spec((ATT_HEADS, N_BUCKETS)), pl.BlockSpec((N_BUCKETS, tc), lambda c: (0, c))],
        out_specs=pl.BlockSpec((ATT_HEADS, tc), lambda c: (0, c)),
        out_shape=jax.ShapeDtypeStruct((ATT_HEADS, cols), F32),
        compiler_params=_cparams("parallel"),
        name="att_bias_table",
    )(rel_bias.T, onehot)
    table = table.reshape(ATT_HEADS, BLOCK, 3 * BLOCK)
    table = jnp.where((n <= WINDOW)[None], table, -jnp.inf)
    table = table.reshape(KV_HEADS, GQA_GROUP, BLOCK, 3 * BLOCK)
    return table.transpose(0, 3, 1, 2).reshape(KV_HEADS, 3 * BLOCK, GQA_GROUP * BLOCK)


def _attn_kernel(q_ref, km_ref, kp_ref, kn_ref, vm_ref, vp_ref, vn_ref, bias_ref, sink_ref, o_ref, *, nsub):
    i = pl.program_id(1)
    neg = -jnp.inf
    pen_first = jnp.where(i == 0, neg, 0.0)
    pen_last = jnp.where(i == pl.num_programs(1) - 1, neg, 0.0)
    kwin = jnp.concatenate([kp_ref[0], km_ref[0], kn_ref[0]], axis=0)
    vwin = jnp.concatenate([vp_ref[0], vm_ref[0], vn_ref[0]], axis=1)
    lane = lax.broadcasted_iota(jnp.int32, (BLOCK, LANES), 1)
    lo_half = lane < HEAD_DIM
    ones = jnp.ones((SUM_ROWS, 3 * BLOCK), BF16)
    units = [(s, g) for s in range(nsub) for g in range(KV_HEADS)]

    def scores(s, g):
        qs = q_ref[0, s * BLOCK:(s + 1) * BLOCK, :]
        rows = []
        for pr in range(2):
            qp = qs[:, (2 * g + pr) * LANES:(2 * g + pr + 1) * LANES]
            rows.append(jnp.where(lo_half, qp, jnp.zeros_like(qp)))
            rows.append(jnp.where(lo_half, jnp.zeros_like(qp), qp))
        qg = jnp.concatenate(rows, axis=0)
        kg = kwin[s * BLOCK:(s + 3) * BLOCK, g * LANES:(g + 1) * LANES]
        return lax.dot_general(kg, qg, _NT, preferred_element_type=F32)

    sc_next = scores(*units[0])
    for idx, (s, g) in enumerate(units):
        sc = sc_next + bias_ref[g]
        if idx + 1 < len(units):
            sc_next = scores(*units[idx + 1])
        if s == 0:
            sc = jnp.concatenate([sc[:BLOCK] + pen_first, sc[BLOCK:]], axis=0)
        if s == nsub - 1:
            sc = jnp.concatenate([sc[:2 * BLOCK], sc[2 * BLOCK:] + pen_last], axis=0)
        sk = sink_ref[g]
        m = jnp.maximum(jnp.max(sc, axis=0, keepdims=True), sk)
        p = jnp.exp2(sc - m).astype(BF16)
        vg = jnp.concatenate([vwin[g * HEAD_DIM:(g + 1) * HEAD_DIM, s * BLOCK:(s + 3) * BLOCK], ones],
                             axis=0)
        o = jnp.dot(vg, p, preferred_element_type=F32)
        den = o[HEAD_DIM:HEAD_DIM + 1] + jnp.exp2(sk - m)
        on = o[:HEAD_DIM] / den
        for pr in range(2):
            pair = jnp.concatenate([on[:, (2 * pr) * BLOCK:(2 * pr + 1) * BLOCK],
                                    on[:, (2 * pr + 1) * BLOCK:(2 * pr + 2) * BLOCK]], axis=0)
            o_ref[0, s * BLOCK:(s + 1) * BLOCK, (2 * g + pr) * LANES:(2 * g + pr + 1) * LANES] = (
                pair.T.astype(o_ref.dtype))


SUM_ROWS = 64


def _attention(q, kd, vt, bias_t, sink_t, tq):
    b, l, _ = q.shape
    nsub = tq // BLOCK
    nblk = l // BLOCK
    kw = 2 * KV_W
    main = lambda w: pl.BlockSpec((1, tq, w), lambda bi, i: (bi, i, 0))
    prev = pl.BlockSpec((1, BLOCK, kw), lambda bi, i: (bi, jnp.maximum(i * nsub - 1, 0), 0))
    nxt = pl.BlockSpec((1, BLOCK, kw), lambda bi, i: (bi, jnp.minimum((i + 1) * nsub, nblk - 1), 0))
    vmain = pl.BlockSpec((1, KV_W, tq), lambda bi, i: (bi, 0, i))
    vprev = pl.BlockSpec((1, KV_W, BLOCK), lambda bi, i: (bi, 0, jnp.maximum(i * nsub - 1, 0)))
    vnxt = pl.BlockSpec((1, KV_W, BLOCK), lambda bi, i: (bi, 0, jnp.minimum((i + 1) * nsub, nblk - 1)))
    return pl.pallas_call(
        functools.partial(_attn_kernel, nsub=nsub),
        grid=(b, l // tq),
        in_specs=[main(ATT_W), main(kw), prev, nxt, vmain, vprev, vnxt,
                  _const_spec(bias_t.shape), _const_spec(sink_t.shape)],
        out_specs=main(ATT_W),
        out_shape=jax.ShapeDtypeStruct((b, l, ATT_W), BF16),
        compiler_params=_cparams("parallel", "parallel"),
        name="window_attn",
    )(q, kd, kd, kd, vt, vt, vt, bias_t, sink_t)


FF_CHUNK = 256


def _ffn_kernel(x_ref, yh_ref, ya_ref, hg_ref, ag_ref, wo_ref, g2_ref, wgu_ref, wd_ref, o_ref, act_ref):
    def rms(t, g):
        return (t * lax.rsqrt(jnp.mean(t * t, axis=-1, keepdims=True) + EPS) * g).astype(BF16)

    yh = jnp.concatenate([yh_ref[0, c] for c in range(SLABS)], axis=1)
    mixed = jnp.concatenate([rms(yh, hg_ref[...]), rms(ya_ref[...].astype(F32), ag_ref[...])], axis=1)
    h = x_ref[...] + jnp.dot(mixed, wo_ref[...], preferred_element_type=F32)
    hn = rms(h, g2_ref[...])
    for c in range(D_FF // FF_CHUNK):
        lo = c * FF_CHUNK
        gate = jnp.dot(hn, wgu_ref[:, lo:lo + FF_CHUNK], preferred_element_type=F32)
        up = jnp.dot(hn, wgu_ref[:, D_FF + lo:D_FF + lo + FF_CHUNK], preferred_element_type=F32)
        act_ref[:, lo:lo + FF_CHUNK] = (gate / (1.0 + jnp.exp(-gate)) * up).astype(BF16)
    o_ref[...] = h + jnp.dot(act_ref[...], wd_ref[...], preferred_element_type=F32)


def _ffn(x2d, yh, ya, hg, ag, wo_b, g2, wgu_b, wd_b, tm):
    t = x2d.shape[0]
    nt = yh.shape[2] // tm
    row = lambda w: pl.BlockSpec((tm, w), lambda i: (i, 0))
    yspec = pl.BlockSpec((1, SLABS, tm, LANES), lambda i: (i // nt, 0, i % nt, 0))
    return pl.pallas_call(
        _ffn_kernel,
        grid=(t // tm,),
        in_specs=[row(D_MODEL), yspec, row(ATT_W), _const_spec((1, HY_CH)), _const_spec((1, ATT_W)),
                  _const_spec((D_MODEL, D_MODEL)), _const_spec((1, D_MODEL)),
                  _const_spec((D_MODEL, 2 * D_FF)), _const_spec((D_FF, D_MODEL))],
        out_specs=row(D_MODEL),
        out_shape=jax.ShapeDtypeStruct((t, D_MODEL), F32),
        scratch_shapes=[pltpu.VMEM((tm, D_FF), BF16)],
        compiler_params=_cparams("parallel"),
        name="outproj_swiglu",
    )(x2d, yh, ya, hg, ag, wo_b, g2, wgu_b, wd_b)


def _dft_tables(seq_len, paired, kblk):
    n = 2 * seq_len
    n1 = n // FFT_N2
    a = n1 // 2
    kk = np.arange(n1)[:, None]
    th_half = 2.0 * np.pi * ((kk * np.arange(a)[None, :]) % n1) / n1
    c, s = np.cos(th_half), np.sin(th_half)

    def blocked(re, im):
        return np.concatenate([np.concatenate([re[i:i + kblk], im[i:i + kblk]], axis=0)
                               for i in range(0, re.shape[0], kblk)], axis=0)

    def real_input_rows(cc, ss):
        im = -ss[:a].copy()
        im[0] = cc[a]
        return cc[:a], im

    th_full = 2.0 * np.pi * ((kk * np.arange(n1)[None, :]) % n1) / n1
    cf, sf = np.cos(th_full), np.sin(th_full)
    if paired:
        fa = blocked(np.concatenate([c, s], axis=1), np.concatenate([-s, c], axis=1))
        fc = fa.T
        faf = blocked(cf, -sf)
    else:
        fa = blocked(*real_input_rows(c, s))
        re_cols = np.concatenate([c[:1], 2.0 * c[1:a]], axis=0)
        im_cols = np.concatenate([c[a:a + 1], -2.0 * s[1:a]], axis=0)
        fc = blocked(re_cols, im_cols).T
        faf = blocked(*real_input_rows(cf, sf))
    to = lambda m: jnp.asarray(m.astype(np.float32)).astype(BF16)
    return to(fa), to(fc), to(faf)


def _edge_matrices(g, half, kb):
    zero = jnp.zeros_like(g[0])
    z = zero[:, :FFT_N2]
    col = lambda m: m[:, :FFT_N2]
    sa = jnp.concatenate([col(g[0]), z], axis=1)
    sb = jnp.concatenate([z, col(g[half])], axis=1)
    blocks = [jnp.stack([sa, sb, sa.T, sb.T])]
    blocks += [jnp.stack([g[i], zero, g[i].T, zero]) for i in range(kb, half, kb)]
    return jnp.stack(blocks)


def _twiddle_tables(seq_len, k1_count):
    n = 2 * seq_len
    n1 = n // FFT_N2
    k1 = np.arange(k1_count, dtype=np.int64)[:, None, None]
    k2 = np.arange(FFT_N2, dtype=np.int64)[None, :, None]
    n2 = np.arange(FFT_N2, dtype=np.int64)[None, None, :]
    ang = ((n2 * (k2 * n1 + k1)) % n) * (-2.0 * np.pi / n)
    gr, gi = np.cos(ang), np.sin(ang)
    g = np.concatenate([np.concatenate([gr, -gi], axis=2), np.concatenate([gi, gr], axis=2)], axis=1)
    g = g.astype(np.float32)
    return jnp.asarray(g.astype(BF16)), jnp.asarray(np.swapaxes(g, 1, 2).astype(BF16))


def _filter_features(seq_len):
    n = np.arange(2 * seq_len)
    t = np.where(n < seq_len, n, 2 * seq_len - n).astype(np.float32)[:, None]
    t_idx = jnp.asarray(t)
    t_norm = t_idx / max(seq_len - 1, 1)
    bands = jnp.linspace(1e-4, N_BANDS - 1, N_BANDS, dtype=F32)
    w = (2.0 * math.pi) * t_idx * bands[None, :] / seq_len
    return jnp.concatenate([t_norm, jnp.cos(w), -jnp.sin(w)], axis=-1)


def _abs_deltas():
    min_decay = math.log(DECAY_TARGET) / FAST_DECAY_PCT
    max_decay = math.log(DECAY_TARGET) / SLOW_DECAY_PCT
    return jnp.abs(jnp.linspace(min_decay, max_decay, HY_CH, dtype=F32))[None, :]


def _pick(n, target):
    t = min(n, target)
    while n % t:
        t //= 2
    return t


def _layer(x, lw, shared, st):
    b, l, _ = x.shape
    t = b * l
    x2d = x.reshape(t, D_MODEL)
    tm = _pick(l, FFN_TM)
    u, x0c, q, kd, vt = _inproj(x2d, lw["norm1"], lw["w_in"], lw["conv_w"], lw["conv_b"],
                                shared["bq"], shared["bk"], lw["qg"], lw["kg"], _pick(l, INPROJ_TM), l)
    r3 = lambda a: a.reshape(b, l, a.shape[-1])

    n1 = 2 * l // FFT_N2
    a = n1 // 2
    nz = st["nz"]
    p = b // nz
    u5 = u.reshape(nz, p, a, FFT_N2, HY_CH)
    x0c5 = x0c.reshape(nz, p, a, FFT_N2, HY_CH)
    mb = _pick(st["fa"].shape[0], STAGE_MB)
    ah = _stage_a(u5, st["fa"], mb)
    kf, kfh = lw["kf"][st["key"]]
    vv = _stage_b(ah, kf, st["g"], st["gt"], _pick(ah.shape[2], STAGE_KB), st["edge"], kfh)
    yh = _stage_c(vv, st["fc"], x0c5, u5, lw["skip"], mb)
    yh = yh.reshape(b, SLABS, l, LANES)

    ya = _attention(r3(q), r3(kd), vt, shared["bias"], lw["sink_t"], _pick(l, 512))
    ya = ya.reshape(t, ATT_W)

    out = _ffn(x2d, yh, ya, lw["hy_gain"], lw["at_gain"], lw["w_out"], lw["norm2"],
               lw["w_gate_up"], lw["w_down"], tm)
    return out.reshape(b, l, D_MODEL)


def _filter_spectrum(lw_raw, st, seq_len):
    n = 2 * seq_len
    n1 = n // FFT_N2
    taps = _filter_taps(st["zfeat"], lw_raw["f_w1"], lw_raw["f_b1"], lw_raw["f_freq"], lw_raw["f_w2"],
                        lw_raw["f_b2"], lw_raw["f_w3"], st["absdelta"], seq_len, _pick(n, 512))
    ah = _stage_a(taps.reshape(1, 1, n1, FFT_N2, HY_CH), st["faf"], _pick(st["faf"].shape[0], STAGE_MB))
    out = _stage_bf(ah, st["g"], _pick(ah.shape[2], STAGE_KB), 1.0 / n, st["edge"])
    return (out[0], out[1]) if st["edge"] is not None else (out[0], None)


def kernel(x_prompt, x_sample, norm1, w_in, conv_w, conv_b, f_w1, f_b1, f_freq, f_w2, f_b2, f_w3,
           hy_skip, q_gain, k_gain, sink, rel_bias, hy_gain, at_gain, w_out, norm2, w_gate_up, w_down):
    depth = norm1.shape[0]
    streams = {}
    for key, x in (("prompt", x_prompt), ("sample", x_sample)):
        b, l, _ = x.shape
        paired = b % 2 == 0
        n1 = 2 * l // FFT_N2
        fa, fc, faf = _dft_tables(l, paired, _pick(2 * n1 if paired else n1, STAGE_MB) // 2)
        g, gt = _twiddle_tables(l, n1 if paired else n1 // 2 + 1)
        streams[key] = dict(key=key, nz=2 if paired else 1, fa=fa, fc=fc, faf=faf, g=g, gt=gt,
                            edge=None if paired else _edge_matrices(g, n1 // 2, _pick(n1 // 2, STAGE_KB)),
                            zfeat=_filter_features(l), absdelta=_abs_deltas())

    head_id = np.arange(ATT_W) // HEAD_DIM
    shared = dict(
        bq=jnp.asarray((head_id[:, None] == head_id[None, :]).astype(np.float32) / HEAD_DIM).astype(BF16),
        bk=jnp.asarray((head_id[:KV_W, None] == head_id[None, :KV_W]).astype(np.float32) / HEAD_DIM).astype(BF16),
        bias=_bias_table(rel_bias) * LOG2E,
    )

    y_prompt, y_sample = x_prompt, x_sample
    for li in range(depth):
        raw = dict(f_w1=f_w1[li], f_b1=f_b1[li][None, :], f_freq=f_freq[li][None, :], f_w2=f_w2[li],
                   f_b2=f_b2[li][None, :], f_w3=f_w3[li])
        lw = dict(
            norm1=norm1[li][None, :], w_in=w_in[li].astype(BF16),
            qg=jnp.tile(q_gain[li], ATT_HEADS)[None, :] * (HEAD_DIM ** -0.5 * LOG2E),
            kg=jnp.tile(k_gain[li], KV_HEADS)[None, :],
            conv_w=conv_w[li], conv_b=conv_b[li][None, :],
            skip=hy_skip[li][None, :],
            sink_t=jnp.repeat(sink[li] * LOG2E, BLOCK).reshape(KV_HEADS, 1, GQA_GROUP * BLOCK),
            hy_gain=hy_gain[li][None, :], at_gain=at_gain[li][None, :],
            w_out=w_out[li].astype(BF16), norm2=norm2[li][None, :],
            w_gate_up=w_gate_up[li].astype(BF16), w_down=w_down[li].astype(BF16),
        )
        lw["kf"] = {key: _filter_spectrum(raw, st, {"prompt": x_prompt, "sample": x_sample}[key].shape[1])
                    for key, st in streams.items()}
        y_prompt = _layer(y_prompt, lw, shared, streams["prompt"])
        y_sample = _layer(y_sample, lw, shared, streams["sample"])
    return (y_prompt, y_sample)
```

```python
import functools
import math

import numpy as np
import jax
import jax.numpy as jnp
from jax import lax
from jax.experimental import pallas as pl
from jax.experimental.pallas import tpu as pltpu

F32 = jnp.float32
BF16 = jnp.bfloat16

D_MODEL = 1024
ATT_HEADS = 8
KV_HEADS = 2
HEAD_DIM = 64
GQA_GROUP = ATT_HEADS // KV_HEADS
ATT_W = ATT_HEADS * HEAD_DIM
KV_W = KV_HEADS * HEAD_DIM
WINDOW = 128
BLOCK = 128
N_BUCKETS = 32
MAX_DIST = 128
HY_CH = D_MODEL - ATT_W
FILTER_HIDDEN = 64
N_BANDS = 16
POS_DIM = 1 + 2 * N_BANDS
FAST_DECAY_PCT = 0.3
SLOW_DECAY_PCT = 1.5
DECAY_TARGET = 1e-2
IN_W = 3 * HY_CH + (ATT_HEADS + 2 * KV_HEADS) * HEAD_DIM
D_FF = -(-8 * D_MODEL // (3 * 256)) * 256
EPS = 1e-6
LOG2E = math.log2(math.e)

FFT_N2 = 64
LANES = 128
N2_TILE = 8
SLABS = HY_CH // LANES
STAGE_MB = 512
STAGE_KB = 16
STAGE_B_CHAINS = 8
INPROJ_TM = 512
FFN_TM = 512
VMEM_LIMIT = 56 * 1024 * 1024

_NT = (((1,), (1,)), ((), ()))


def _cparams(*sem):
    return pltpu.CompilerParams(dimension_semantics=sem, vmem_limit_bytes=VMEM_LIMIT)


def _const_spec(shape):
    nd = len(shape)
    return pl.BlockSpec(shape, lambda *_: (0,) * nd, pipeline_mode=pl.Buffered(1))


HALO = 16


def _inproj_kernel(x_ref, xp_ref, xn_ref, g1_ref, w_ref, cw_ref, cb_ref, bq_ref, bk_ref, qg_ref, kg_ref,
                   u_ref, x0c_ref, q_ref, k_ref, v_ref, *, nt):
    i = pl.program_id(0)
    tm = x_ref.shape[0]
    xp = jnp.where(i % nt == 0, 0.0, xp_ref[...])
    xn = jnp.where(i % nt == nt - 1, 0.0, xn_ref[...])
    xe = jnp.concatenate([xp, x_ref[...], xn], axis=0)
    xe = (xe * lax.rsqrt(jnp.mean(xe * xe, axis=-1, keepdims=True) + EPS) * g1_ref[...]).astype(BF16)
    xb = xe[HALO:HALO + tm]

    def seg(lo, hi):
        return jnp.dot(xb, w_ref[:, lo:hi], preferred_element_type=F32)

    o = 3 * HY_CH
    v = seg(o + ATT_W + KV_W, IN_W)
    v_ref[0] = v.T.astype(BF16)

    k = seg(o + ATT_W, o + ATT_W + KV_W)
    ms = jnp.dot((k * k).astype(BF16), bk_ref[...], preferred_element_type=F32)
    kn = k * lax.rsqrt(ms + EPS) * kg_ref[...]
    lo_half = lax.broadcasted_iota(jnp.int32, kn.shape, 1) < HEAD_DIM
    kr = pltpu.roll(kn, HEAD_DIM, axis=1)
    k_ref[...] = jnp.concatenate([jnp.where(lo_half, kn, kr), jnp.where(lo_half, kr, kn)], axis=1).astype(BF16)

    q = seg(o, o + ATT_W)
    ms = jnp.dot((q * q).astype(BF16), bq_ref[...], preferred_element_type=F32)
    q_ref[...] = (q * lax.rsqrt(ms + EPS) * qg_ref[...]).astype(BF16)

    rows = slice(HALO, HALO + tm)

    def conv(j):
        c = slice(j * HY_CH, (j + 1) * HY_CH)
        he = jnp.dot(xe, w_ref[:, c], preferred_element_type=F32)
        return (pltpu.roll(he, 1, axis=0)[rows] * cw_ref[0:1, c] + he[rows] * cw_ref[1:2, c]
                + pltpu.roll(he, tm + 2 * HALO - 1, axis=0)[rows] * cw_ref[2:3, c] + cb_ref[0:1, c])

    x0c_ref[...] = conv(0)
    u_ref[...] = conv(1) * conv(2)


def _inproj(x2d, g1, w_in_b, conv_w, conv_b, bq, bk, qg, kg, tm, seq_len):
    t = x2d.shape[0]
    nt = seq_len // tm
    nh = tm // HALO
    row = lambda w: pl.BlockSpec((tm, w), lambda i: (i, 0))
    prev = pl.BlockSpec((HALO, D_MODEL), lambda i: (jnp.maximum(i * nh - 1, 0), 0))
    nxt = pl.BlockSpec((HALO, D_MODEL), lambda i: (jnp.minimum((i + 1) * nh, t // HALO - 1), 0))
    return pl.pallas_call(
        functools.partial(_inproj_kernel, nt=nt),
        grid=(t // tm,),
        in_specs=[row(D_MODEL), prev, nxt, _const_spec((1, D_MODEL)), _const_spec((D_MODEL, IN_W)),
                  _const_spec((3, 3 * HY_CH)), _const_spec((1, 3 * HY_CH)),
                  _const_spec((ATT_W, ATT_W)), _const_spec((KV_W, KV_W)),
                  _const_spec((1, ATT_W)), _const_spec((1, KV_W))],
        out_specs=[row(HY_CH), row(HY_CH), row(ATT_W), row(2 * KV_W),
                   pl.BlockSpec((1, KV_W, tm), lambda i: (i // nt, 0, i % nt))],
        out_shape=[jax.ShapeDtypeStruct((t, HY_CH), F32), jax.ShapeDtypeStruct((t, HY_CH), F32),
                   jax.ShapeDtypeStruct((t, ATT_W), BF16), jax.ShapeDtypeStruct((t, 2 * KV_W), BF16),
                   jax.ShapeDtypeStruct((t // seq_len, KV_W, seq_len), BF16)],
        compiler_params=_cparams("parallel"),
        name="inproj",
    )(x2d, x2d, x2d, g1, w_in_b, conv_w, conv_b, bq, bk, qg, kg)


def _stage_a_kernel(*refs, nz, a, mb):
    x_refs, f_ref, o_ref = refs[:SLABS], refs[SLABS], refs[SLABS + 1]
    kblk = mb // 2
    xs = [r.reshape(nz * a * N2_TILE, LANES) for r in x_refs]
    o2 = o_ref.reshape(SLABS * kblk * N2_TILE, LANES)
    f = f_ref[...]

    def gather(s):
        return jnp.concatenate(
            [jnp.concatenate([xs[c][pl.ds(z * a * N2_TILE + s, a, stride=N2_TILE), :] for z in range(nz)], axis=0)
             for c in range(SLABS)], axis=1).astype(BF16)

    r = [jnp.dot(f, gather(s), preferred_element_type=F32) for s in range(N2_TILE)]
    for s in range(N2_TILE):
        packed = _pack_pair(r[s][:kblk], r[s][kblk:])
        for c in range(SLABS):
            o2[pl.ds(c * kblk * N2_TILE + s, kblk, stride=N2_TILE), :] = packed[:, c * LANES:(c + 1) * LANES]


def _pack_pair(re, im):
    rb = lax.bitcast_convert_type(re.astype(BF16).astype(F32), jnp.uint32)
    ib = lax.bitcast_convert_type(im.astype(BF16).astype(F32), jnp.uint32)
    return (rb >> 16) | ib


def _unpack_pair(p):
    re = lax.bitcast_convert_type(p << 16, F32)
    im = lax.bitcast_convert_type(p & jnp.uint32(0xFFFF0000), F32)
    return re, im


def _stage_a(x5, fmat, mb):
    nz, p, a, n2, _ = x5.shape
    m = fmat.shape[0]
    kblk = mb // 2
    xspec = lambda c: pl.BlockSpec((nz, 1, a, N2_TILE, LANES), lambda pi, j, mi: (0, pi, 0, j, c))
    return pl.pallas_call(
        functools.partial(_stage_a_kernel, nz=nz, a=a, mb=mb),
        grid=(p, n2 // N2_TILE, m // mb),
        in_specs=[xspec(c) for c in range(SLABS)] + [pl.BlockSpec((mb, nz * a), lambda pi, j, mi: (mi, 0))],
        out_specs=pl.BlockSpec((1, SLABS, kblk, N2_TILE, LANES), lambda pi, j, mi: (pi, 0, mi, j, 0)),
        out_shape=jax.ShapeDtypeStruct((p, SLABS, m // 2, n2, LANES), jnp.uint32),
        compiler_params=_cparams("parallel", "parallel", "parallel"),
        name="hy_stage_a",
    )(*([x5] * SLABS), fmat)


MXU_COLS = 256
SLABS_PER_DOT = MXU_COLS // LANES


def _slab_rows(a_ref, j, h):
    parts = [_unpack_pair(a_ref[0, h * SLABS_PER_DOT + c, j]) for c in range(SLABS_PER_DOT)]
    return jnp.concatenate([jnp.concatenate([re, im], axis=0) for re, im in parts], axis=1).astype(BF16)


def _spectral_mul(u, kr, ki, n2):
    ur, ui = u[:n2], u[n2:]
    return jnp.concatenate([ur * kr - ui * ki, ur * ki + ui * kr], axis=0).astype(BF16)


def _stage_b_kernel(a_ref, kf_ref, g_ref, gt_ref, *rest, herm):
    if herm:
        edge_ref, kfh_ref, o_ref = rest
    else:
        (o_ref,) = rest
    kb, n2 = a_ref.shape[2], a_ref.shape[3]

    def store(j, h, v):
        packed = _pack_pair(v[:n2], v[n2:])
        for c in range(SLABS_PER_DOT):
            o_ref[0, h * SLABS_PER_DOT + c, j] = packed[:, c * LANES:(c + 1) * LANES]

    dot = functools.partial(jnp.dot, preferred_element_type=F32)
    chains = [(j, h) for j in range(kb) for h in range(SLABS // SLABS_PER_DOT)]
    for c0 in range(0, len(chains), STAGE_B_CHAINS):
        batch = chains[c0:c0 + STAGE_B_CHAINS]
        fwd = []
        for j, h in batch:
            x = _slab_rows(a_ref, j, h)
            if herm and j == 0:
                fwd.append((dot(edge_ref[0, 0], x), dot(edge_ref[0, 1], x)))
            else:
                fwd.append((dot(g_ref[j], x),))
        prod = []
        for (j, h), us in zip(batch, fwd):
            lanes = slice(h * MXU_COLS, (h + 1) * MXU_COLS)
            ps = [_spectral_mul(us[0], kf_ref[0, j, :, lanes], kf_ref[1, j, :, lanes], n2)]
            if len(us) == 2:
                ps.append(_spectral_mul(us[1], kfh_ref[0, 0, :, lanes], kfh_ref[0, 1, :, lanes], n2))
            prod.append(ps)
        for (j, h), ps in zip(batch, prod):
            if len(ps) == 2:
                store(j, h, dot(edge_ref[0, 2], ps[0]) + dot(edge_ref[0, 3], ps[1]))
            else:
                store(j, h, dot(gt_ref[j], ps[0]))


def _stage_b(a5, kf, g, gt, kb, edge=None, kfh=None):
    p, _, k1n, n2, _ = a5.shape
    herm = edge is not None
    blk = pl.BlockSpec((1, SLABS, kb, n2, LANES), lambda i, pi: (pi, 0, i, 0, 0))
    gspec = pl.BlockSpec((kb, 2 * n2, 2 * n2), lambda i, pi: (i, 0, 0))
    extra = [pl.BlockSpec((1,) + edge.shape[1:], lambda i, pi: (i, 0, 0, 0)),
             pl.BlockSpec((1,) + kfh.shape[1:], lambda i, pi: (i, 0, 0, 0))] if herm else []
    return pl.pallas_call(
        functools.partial(_stage_b_kernel, herm=herm),
        grid=(k1n // kb, p),
        in_specs=[blk, pl.BlockSpec((2, kb, n2, HY_CH), lambda i, pi: (0, i, 0, 0)), gspec, gspec] + extra,
        out_specs=blk,
        out_shape=jax.ShapeDtypeStruct(a5.shape, jnp.uint32),
        compiler_params=_cparams("parallel", "parallel"),
        name="hy_stage_b",
    )(a5, kf, g, gt, *([edge, kfh] if herm else []))


def _stage_bf_kernel(a_ref, g_ref, *rest, scale, herm):
    if herm:
        edge_ref, o_ref, oh_ref = rest
    else:
        (o_ref,) = rest
    kb, n2 = a_ref.shape[2], a_ref.shape[3]

    def regular(j, h):
        lanes = slice(h * MXU_COLS, (h + 1) * MXU_COLS)
        u = jnp.dot(g_ref[j], _slab_rows(a_ref, j, h), preferred_element_type=F32) * scale
        o_ref[0, j, :, lanes] = u[:n2]
        o_ref[1, j, :, lanes] = u[n2:]

    def edge(h):
        lanes = slice(h * MXU_COLS, (h + 1) * MXU_COLS)
        x = _slab_rows(a_ref, 0, h)
        u0 = jnp.dot(edge_ref[0, 0], x, preferred_element_type=F32) * scale
        uh = jnp.dot(edge_ref[0, 1], x, preferred_element_type=F32) * scale
        o_ref[0, 0, :, lanes] = u0[:n2]
        o_ref[1, 0, :, lanes] = u0[n2:]
        oh_ref[0, 0, :, lanes] = uh[:n2]
        oh_ref[0, 1, :, lanes] = uh[n2:]

    for j in range(kb):
        for h in range(SLABS // SLABS_PER_DOT):
            if herm and j == 0:
                edge(h)
            else:
                regular(j, h)


def _stage_bf(a5, g, kb, scale, edge=None):
    _, _, k1n, n2, _ = a5.shape
    herm = edge is not None
    out_specs = [pl.BlockSpec((2, kb, n2, HY_CH), lambda i: (0, i, 0, 0))]
    out_shape = [jax.ShapeDtypeStruct((2, k1n, n2, HY_CH), F32)]
    if herm:
        out_specs.append(pl.BlockSpec((1, 2, n2, HY_CH), lambda i: (i, 0, 0, 0)))
        out_shape.append(jax.ShapeDtypeStruct((k1n // kb, 2, n2, HY_CH), F32))
    return pl.pallas_call(
        functools.partial(_stage_bf_kernel, scale=scale, herm=herm),
        grid=(k1n // kb,),
        in_specs=[pl.BlockSpec((1, SLABS, kb, n2, LANES), lambda i: (0, 0, i, 0, 0)),
                  pl.BlockSpec((kb, 2 * n2, 2 * n2), lambda i: (i, 0, 0))]
        + ([pl.BlockSpec((1,) + edge.shape[1:], lambda i: (i, 0, 0, 0))] if herm else []),
        out_specs=out_specs,
        out_shape=out_shape,
        compiler_params=_cparams("parallel"),
        name="hy_filter_spectrum",
    )(a5, g, *([edge] if herm else []))


def _stage_c_kernel(*refs, nz, a, mb):
    v_ref, f_ref = refs[0], refs[1]
    x0c_refs, u_refs = refs[2:2 + SLABS], refs[2 + SLABS:2 + 2 * SLABS]
    skip_ref, o_ref, acc_ref = refs[2 + 2 * SLABS:]
    mk = pl.program_id(2)
    kblk = mb // 2
    v2 = v_ref.reshape(SLABS * kblk * N2_TILE, LANES)

    @pl.when(mk == 0)
    def _():
        acc_ref[...] = jnp.zeros_like(acc_ref)

    for s in range(N2_TILE):
        packed = jnp.concatenate([v2[pl.ds(c * kblk * N2_TILE + s, kblk, stride=N2_TILE), :] for c in range(SLABS)],
                                 axis=1)
        vs = jnp.concatenate(_unpack_pair(packed), axis=0).astype(BF16)
        acc_ref[s] += jnp.dot(f_ref[...], vs, preferred_element_type=F32)

    @pl.when(mk == pl.num_programs(2) - 1)
    def _():
        o2 = o_ref.reshape(nz * SLABS * a * N2_TILE, LANES)
        x2 = [r.reshape(nz * a * N2_TILE, LANES) for r in x0c_refs]
        u2 = [r.reshape(nz * a * N2_TILE, LANES) for r in u_refs]
        for s in range(N2_TILE):
            y = acc_ref[s]
            for z in range(nz):
                for c in range(SLABS):
                    rows = pl.ds(z * a * N2_TILE + s, a, stride=N2_TILE)
                    yc = y[z * a:(z + 1) * a, c * LANES:(c + 1) * LANES]
                    val = x2[c][rows, :] * (yc + skip_ref[0:1, c * LANES:(c + 1) * LANES] * u2[c][rows, :])
                    o2[pl.ds((z * SLABS + c) * a * N2_TILE + s, a, stride=N2_TILE), :] = val


def _stage_c(v5, fmat, x0c5, u5, skip, mb):
    nz, p, a, n2, _ = u5.shape
    m = 2 * v5.shape[2]
    xspec = lambda c: pl.BlockSpec((nz, 1, a, N2_TILE, LANES), lambda pi, j, mk: (0, pi, 0, j, c))
    return pl.pallas_call(
        functools.partial(_stage_c_kernel, nz=nz, a=a, mb=mb),
        grid=(p, n2 // N2_TILE, m // mb),
        in_specs=[pl.BlockSpec((1, SLABS, mb // 2, N2_TILE, LANES), lambda pi, j, mk: (pi, 0, mk, j, 0)),
                  pl.BlockSpec((nz * a, mb), lambda pi, j, mk: (0, mk))]
        + [xspec(c) for c in range(SLABS)] * 2 + [_const_spec((1, HY_CH))],
        out_specs=pl.BlockSpec((nz, 1, SLABS, a, N2_TILE, LANES), lambda pi, j, mk: (0, pi, 0, 0, j, 0)),
        out_shape=jax.ShapeDtypeStruct((nz, p, SLABS, a, n2, LANES), F32),
        scratch_shapes=[pltpu.VMEM((N2_TILE, nz * a, HY_CH), F32)],
        compiler_params=_cparams("parallel", "parallel", "arbitrary"),
        name="hy_stage_c",
    )(v5, fmat, *([x0c5] * SLABS), *([u5] * SLABS), skip)


def _filter_kernel(zt_ref, tn_ref, w1t_ref, b1_ref, fr_ref, w2t_ref, b2_ref, w3_ref, dl_ref, o_ref, *, seq_len):
    hi = lax.Precision.HIGHEST
    fr = fr_ref[...]
    h = jnp.sin(fr * (jnp.dot(w1t_ref[...], zt_ref[...], precision=hi, preferred_element_type=F32) + b1_ref[...]))
    h = jnp.sin(fr * (jnp.dot(w2t_ref[...], h, precision=hi, preferred_element_type=F32) + b2_ref[...]))
    taps = jnp.dot(h.T.astype(BF16), w3_ref[...], preferred_element_type=F32)
    tr = taps.shape[0]
    n = pl.program_id(0) * tr + lax.broadcasted_iota(jnp.int32, (tr, HY_CH), 0)
    sel = jnp.where(n < seq_len, taps[:, :HY_CH], jnp.where(n > seq_len, taps[:, HY_CH:], 0.0))
    o_ref[...] = sel * jnp.exp(-tn_ref[...] * dl_ref[...])


def _filter_taps(zfeat, f_w1, f_b1, f_freq, f_w2, f_b2, f_w3, absdelta, seq_len, tr):
    n = zfeat.shape[0]
    cs = lambda a: _const_spec(a.shape)
    args = (zfeat.T, zfeat[:, 0:1], f_w1.T, f_b1.T, f_freq.T, f_w2.T, f_b2.T, f_w3.astype(BF16), absdelta)
    return pl.pallas_call(
        functools.partial(_filter_kernel, seq_len=seq_len),
        grid=(n // tr,),
        in_specs=[pl.BlockSpec((POS_DIM, tr), lambda i: (0, i)), pl.BlockSpec((tr, 1), lambda i: (i, 0))]
        + [cs(a) for a in args[2:]],
        out_specs=pl.BlockSpec((tr, HY_CH), lambda i: (i, 0)),
        out_shape=jax.ShapeDtypeStruct((n, HY_CH), F32),
        compiler_params=_cparams("parallel"),
        name="hy_filter_taps",
    )(*args)


def _bias_kernel(rb_ref, oh_ref, o_ref):
    o_ref[...] = jnp.dot(rb_ref[...], oh_ref[...], precision=lax.Precision.HIGHEST,
                         preferred_element_type=F32)


def _bias_table(rel_bias):
    i = jnp.arange(BLOCK)[:, None]
    j = jnp.arange(3 * BLOCK)[None, :]
    rel = j - BLOCK - i
    nb2 = N_BUCKETS // 2
    max_exact = nb2 // 2
    n = jnp.abs(rel)
    large = max_exact + (jnp.log(jnp.maximum(n, 1).astype(F32) / max_exact)
                         / math.log(MAX_DIST / max_exact) * (nb2 - max_exact)).astype(jnp.int32)
    large = jnp.minimum(large, nb2 - 1)
    bucket = jnp.where(rel > 0, nb2, 0) + jnp.where(n < max_exact, n, large)
    onehot = (bucket.reshape(1, -1) == jnp.arange(N_BUCKETS)[:, None]).astype(F32)
    cols = onehot.shape[1]
    tc = cols // 4
    table = pl.pallas_call(
        _bias_kernel,
        grid=(4,),
        in_specs=[_const_spec((ATT_HEADS, N_BUCKETS)), pl.BlockSpec((N_BUCKETS, tc), lambda c: (0, c))],
        out_specs=pl.BlockSpec((ATT_HEADS, tc), lambda c: (0, c)),
        out_shape=jax.ShapeDtypeStruct((ATT_HEADS, cols), F32),
        compiler_params=_cparams("parallel"),
        name="att_bias_table",
    )(rel_bias.T, onehot)
    table = table.reshape(ATT_HEADS, BLOCK, 3 * BLOCK)
    table = jnp.where((n <= WINDOW)[None], table, -jnp.inf)
    table = table.reshape(KV_HEADS, GQA_GROUP, BLOCK, 3 * BLOCK)
    return table.transpose(0, 3, 1, 2).reshape(KV_HEADS, 3 * BLOCK, GQA_GROUP * BLOCK)


def _attn_kernel(q_ref, km_ref, kp_ref, kn_ref, vm_ref, vp_ref, vn_ref, bias_ref, sink_ref, o_ref, *, nsub):
    i = pl.program_id(1)
    neg = -jnp.inf
    pen_first = jnp.where(i == 0, neg, 0.0)
    pen_last = jnp.where(i == pl.num_programs(1) - 1, neg, 0.0)
    kwin = jnp.concatenate([kp_ref[0], km_ref[0], kn_ref[0]], axis=0)
    vwin = jnp.concatenate([vp_ref[0], vm_ref[0], vn_ref[0]], axis=1)
    lane = lax.broadcasted_iota(jnp.int32, (BLOCK, LANES), 1)
    lo_half = lane < HEAD_DIM
    ones = jnp.ones((SUM_ROWS, 3 * BLOCK), BF16)
    units = [(s, g) for s in range(nsub) for g in range(KV_HEADS)]

    def scores(s, g):
        qs = q_ref[0, s * BLOCK:(s + 1) * BLOCK, :]
        rows = []
        for pr in range(2):
            qp = qs[:, (2 * g + pr) * LANES:(2 * g + pr + 1) * LANES]
            rows.append(jnp.where(lo_half, qp, jnp.zeros_like(qp)))
            rows.append(jnp.where(lo_half, jnp.zeros_like(qp), qp))
        qg = jnp.concatenate(rows, axis=0)
        kg = kwin[s * BLOCK:(s + 3) * BLOCK, g * LANES:(g + 1) * LANES]
        t = lax.dot_general(kg, qg, _NT, preferred_element_type=F32) + bias_ref[g]
        if s == 0:
            t = jnp.concatenate([t[:BLOCK] + pen_first, t[BLOCK:]], axis=0)
        if s == nsub - 1:
            t = jnp.concatenate([t[:2 * BLOCK], t[2 * BLOCK:] + pen_last], axis=0)
        return t

    def pv(s, g, p):
        vg = jnp.concatenate([vwin[g * HEAD_DIM:(g + 1) * HEAD_DIM, s * BLOCK:(s + 3) * BLOCK], ones],
                             axis=0)
        return jnp.dot(vg, p, preferred_element_type=F32)

    for u0 in range(0, len(units), ATT_UNITS):
        batch = units[u0:u0 + ATT_UNITS]
        sc = [scores(s, g) for s, g in batch]
        m = [jnp.maximum(jnp.max(t, axis=0, keepdims=True), sink_ref[g]) for t, (_, g) in zip(sc, batch)]
        p = [jnp.exp2(t - mm).astype(BF16) for t, mm in zip(sc, m)]
        o = [pv(s, g, pp) for (s, g), pp in zip(batch, p)]
        for (s, g), oo, mm in zip(batch, o, m):
            on = oo[:HEAD_DIM] / (oo[HEAD_DIM:HEAD_DIM + 1] + jnp.exp2(sink_ref[g] - mm))
            for pr in range(2):
                pair = jnp.concatenate([on[:, (2 * pr) * BLOCK:(2 * pr + 1) * BLOCK],
                                        on[:, (2 * pr + 1) * BLOCK:(2 * pr + 2) * BLOCK]], axis=0)
                o_ref[0, s * BLOCK:(s + 1) * BLOCK, (2 * g + pr) * LANES:(2 * g + pr + 1) * LANES] = (
                    pair.T.astype(o_ref.dtype))


SUM_ROWS = 64
ATT_UNITS = 8
ATT_TQ = 1024


def _attention(q, kd, vt, bias_t, sink_t, tq):
    b, l, _ = q.shape
    nsub = tq // BLOCK
    nblk = l // BLOCK
    kw = 2 * KV_W
    main = lambda w: pl.BlockSpec((1, tq, w), lambda bi, i: (bi, i, 0))
    prev = pl.BlockSpec((1, BLOCK, kw), lambda bi, i: (bi, jnp.maximum(i * nsub - 1, 0), 0))
    nxt = pl.BlockSpec((1, BLOCK, kw), lambda bi, i: (bi, jnp.minimum((i + 1) * nsub, nblk - 1), 0))
    vmain = pl.BlockSpec((1, KV_W, tq), lambda bi, i: (bi, 0, i))
    vprev = pl.BlockSpec((1, KV_W, BLOCK), lambda bi, i: (bi, 0, jnp.maximum(i * nsub - 1, 0)))
    vnxt = pl.BlockSpec((1, KV_W, BLOCK), lambda bi, i: (bi, 0, jnp.minimum((i + 1) * nsub, nblk - 1)))
    return pl.pallas_call(
        functools.partial(_attn_kernel, nsub=nsub),
        grid=(b, l // tq),
        in_specs=[main(ATT_W), main(kw), prev, nxt, vmain, vprev, vnxt,
                  _const_spec(bias_t.shape), _const_spec(sink_t.shape)],
        out_specs=main(ATT_W),
        out_shape=jax.ShapeDtypeStruct((b, l, ATT_W), BF16),
        compiler_params=_cparams("parallel", "parallel"),
        name="window_attn",
    )(q, kd, kd, kd, vt, vt, vt, bias_t, sink_t)


FF_CHUNK = 256


def _ffn_kernel(x_ref, yh_ref, ya_ref, hg_ref, ag_ref, wo_ref, g2_ref, wgu_ref, wd_ref, o_ref, act_ref):
    def rms(t, g):
        return (t * lax.rsqrt(jnp.mean(t * t, axis=-1, keepdims=True) + EPS) * g).astype(BF16)

    yh = jnp.concatenate([yh_ref[0, c] for c in range(SLABS)], axis=1)
    mixed = jnp.concatenate([rms(yh, hg_ref[...]), rms(ya_ref[...].astype(F32), ag_ref[...])], axis=1)
    h = x_ref[...] + jnp.dot(mixed, wo_ref[...], preferred_element_type=F32)
    hn = rms(h, g2_ref[...])
    for c in range(D_FF // FF_CHUNK):
        lo = c * FF_CHUNK
        gate = jnp.dot(hn, wgu_ref[:, lo:lo + FF_CHUNK], preferred_element_type=F32)
        up = jnp.dot(hn, wgu_ref[:, D_FF + lo:D_FF + lo + FF_CHUNK], preferred_element_type=F32)
        act_ref[:, lo:lo + FF_CHUNK] = (gate / (1.0 + jnp.exp(-gate)) * up).astype(BF16)
    o_ref[...] = h + jnp.dot(act_ref[...], wd_ref[...], preferred_element_type=F32)


def _ffn(x2d, yh, ya, hg, ag, wo_b, g2, wgu_b, wd_b, tm):
    t = x2d.shape[0]
    nt = yh.shape[2] // tm
    row = lambda w: pl.BlockSpec((tm, w), lambda i: (i, 0))
    yspec = pl.BlockSpec((1, SLABS, tm, LANES), lambda i: (i // nt, 0, i % nt, 0))
    return pl.pallas_call(
        _ffn_kernel,
        grid=(t // tm,),
        in_specs=[row(D_MODEL), yspec, row(ATT_W), _const_spec((1, HY_CH)), _const_spec((1, ATT_W)),
                  _const_spec((D_MODEL, D_MODEL)), _const_spec((1, D_MODEL)),
                  _const_spec((D_MODEL, 2 * D_FF)), _const_spec((D_FF, D_MODEL))],
        out_specs=row(D_MODEL),
        out_shape=jax.ShapeDtypeStruct((t, D_MODEL), F32),
        scratch_shapes=[pltpu.VMEM((tm, D_FF), BF16)],
        compiler_params=_cparams("parallel"),
        name="outproj_swiglu",
    )(x2d, yh, ya, hg, ag, wo_b, g2, wgu_b, wd_b)


def _dft_tables(seq_len, paired, kblk):
    n = 2 * seq_len
    n1 = n // FFT_N2
    a = n1 // 2
    kk = np.arange(n1)[:, None]
    th_half = 2.0 * np.pi * ((kk * np.arange(a)[None, :]) % n1) / n1
    c, s = np.cos(th_half), np.sin(th_half)

    def blocked(re, im):
        return np.concatenate([np.concatenate([re[i:i + kblk], im[i:i + kblk]], axis=0)
                               for i in range(0, re.shape[0], kblk)], axis=0)

    def real_input_rows(cc, ss):
        im = -ss[:a].copy()
        im[0] = cc[a]
        return cc[:a], im

    th_full = 2.0 * np.pi * ((kk * np.arange(n1)[None, :]) % n1) / n1
    cf, sf = np.cos(th_full), np.sin(th_full)
    if paired:
        fa = blocked(np.concatenate([c, s], axis=1), np.concatenate([-s, c], axis=1))
        fc = fa.T
        faf = blocked(cf, -sf)
    else:
        fa = blocked(*real_input_rows(c, s))
        re_cols = np.concatenate([c[:1], 2.0 * c[1:a]], axis=0)
        im_cols = np.concatenate([c[a:a + 1], -2.0 * s[1:a]], axis=0)
        fc = blocked(re_cols, im_cols).T
        faf = blocked(*real_input_rows(cf, sf))
    to = lambda m: jnp.asarray(m.astype(np.float32)).astype(BF16)
    return to(fa), to(fc), to(faf)


def _edge_matrices(g, half, kb):
    zero = jnp.zeros_like(g[0])
    z = zero[:, :FFT_N2]
    col = lambda m: m[:, :FFT_N2]
    sa = jnp.concatenate([col(g[0]), z], axis=1)
    sb = jnp.concatenate([z, col(g[half])], axis=1)
    blocks = [jnp.stack([sa, sb, sa.T, sb.T])]
    blocks += [jnp.stack([g[i], zero, g[i].T, zero]) for i in range(kb, half, kb)]
    return jnp.stack(blocks)


def _twiddle_tables(seq_len, k1_count):
    n = 2 * seq_len
    n1 = n // FFT_N2
    k1 = np.arange(k1_count, dtype=np.int64)[:, None, None]
    k2 = np.arange(FFT_N2, dtype=np.int64)[None, :, None]
    n2 = np.arange(FFT_N2, dtype=np.int64)[None, None, :]
    ang = ((n2 * (k2 * n1 + k1)) % n) * (-2.0 * np.pi / n)
    gr, gi = np.cos(ang), np.sin(ang)
    g = np.concatenate([np.concatenate([gr, -gi], axis=2), np.concatenate([gi, gr], axis=2)], axis=1)
    g = g.astype(np.float32)
    return jnp.asarray(g.astype(BF16)), jnp.asarray(np.swapaxes(g, 1, 2).astype(BF16))


def _filter_features(seq_len):
    n = np.arange(2 * seq_len)
    t = np.where(n < seq_len, n, 2 * seq_len - n).astype(np.float32)[:, None]
    t_idx = jnp.asarray(t)
    t_norm = t_idx / max(seq_len - 1, 1)
    bands = jnp.linspace(1e-4, N_BANDS - 1, N_BANDS, dtype=F32)
    w = (2.0 * math.pi) * t_idx * bands[None, :] / seq_len
    return jnp.concatenate([t_norm, jnp.cos(w), -jnp.sin(w)], axis=-1)


def _abs_deltas():
    min_decay = math.log(DECAY_TARGET) / FAST_DECAY_PCT
    max_decay = math.log(DECAY_TARGET) / SLOW_DECAY_PCT
    return jnp.abs(jnp.linspace(min_decay, max_decay, HY_CH, dtype=F32))[None, :]


def _pick(n, target):
    t = min(n, target)
    while n % t:
        t //= 2
    return t


def _layer(x, lw, shared, st):
    b, l, _ = x.shape
    t = b * l
    x2d = x.reshape(t, D_MODEL)
    tm = _pick(l, FFN_TM)
    u, x0c, q, kd, vt = _inproj(x2d, lw["norm1"], lw["w_in"], lw["conv_w"], lw["conv_b"],
                                shared["bq"], shared["bk"], lw["qg"], lw["kg"], _pick(l, INPROJ_TM), l)
    r3 = lambda a: a.reshape(b, l, a.shape[-1])

    n1 = 2 * l // FFT_N2
    a = n1 // 2
    nz = st["nz"]
    p = b // nz
    u5 = u.reshape(nz, p, a, FFT_N2, HY_CH)
    x0c5 = x0c.reshape(nz, p, a, FFT_N2, HY_CH)
    mb = _pick(st["fa"].shape[0], STAGE_MB)
    ah = _stage_a(u5, st["fa"], mb)
    kf, kfh = lw["kf"][st["key"]]
    vv = _stage_b(ah, kf, st["g"], st["gt"], _pick(ah.shape[2], STAGE_KB), st["edge"], kfh)
    yh = _stage_c(vv, st["fc"], x0c5, u5, lw["skip"], mb)
    yh = yh.reshape(b, SLABS, l, LANES)

    ya = _attention(r3(q), r3(kd), vt, shared["bias"], lw["sink_t"], _pick(l, ATT_TQ))
    ya = ya.reshape(t, ATT_W)

    out = _ffn(x2d, yh, ya, lw["hy_gain"], lw["at_gain"], lw["w_out"], lw["norm2"],
               lw["w_gate_up"], lw["w_down"], tm)
    return out.reshape(b, l, D_MODEL)


def _filter_spectrum(lw_raw, st, seq_len):
    n = 2 * seq_len
    n1 = n // FFT_N2
    taps = _filter_taps(st["zfeat"], lw_raw["f_w1"], lw_raw["f_b1"], lw_raw["f_freq"], lw_raw["f_w2"],
                        lw_raw["f_b2"], lw_raw["f_w3"], st["absdelta"], seq_len, _pick(n, 512))
    ah = _stage_a(taps.reshape(1, 1, n1, FFT_N2, HY_CH), st["faf"], _pick(st["faf"].shape[0], STAGE_MB))
    out = _stage_bf(ah, st["g"], _pick(ah.shape[2], STAGE_KB), 1.0 / n, st["edge"])
    return (out[0], out[1]) if st["edge"] is not None else (out[0], None)


def kernel(x_prompt, x_sample, norm1, w_in, conv_w, conv_b, f_w1, f_b1, f_freq, f_w2, f_b2, f_w3,
           hy_skip, q_gain, k_gain, sink, rel_bias, hy_gain, at_gain, w_out, norm2, w_gate_up, w_down):
    depth = norm1.shape[0]
    streams = {}
    for key, x in (("prompt", x_prompt), ("sample", x_sample)):
        b, l, _ = x.shape
        paired = b % 2 == 0
        n1 = 2 * l // FFT_N2
        fa, fc, faf = _dft_tables(l, paired, _pick(2 * n1 if paired else n1, STAGE_MB) // 2)
        g, gt = _twiddle_tables(l, n1 if paired else n1 // 2 + 1)
        streams[key] = dict(key=key, nz=2 if paired else 1, fa=fa, fc=fc, faf=faf, g=g, gt=gt,
                            edge=None if paired else _edge_matrices(g, n1 // 2, _pick(n1 // 2, STAGE_KB)),
                            zfeat=_filter_features(l), absdelta=_abs_deltas())

    head_id = np.arange(ATT_W) // HEAD_DIM
    shared = dict(
        bq=jnp.asarray((head_id[:, None] == head_id[None, :]).astype(np.float32) / HEAD_DIM).astype(BF16),
        bk=jnp.asarray((head_id[:KV_W, None] == head_id[None, :KV_W]).astype(np.float32) / HEAD_DIM).astype(BF16),
        bias=_bias_table(rel_bias) * LOG2E,
    )

    y_prompt, y_sample = x_prompt, x_sample
    for li in range(depth):
        raw = dict(f_w1=f_w1[li], f_b1=f_b1[li][None, :], f_freq=f_freq[li][None, :], f_w2=f_w2[li],
                   f_b2=f_b2[li][None, :], f_w3=f_w3[li])
        lw = dict(
            norm1=norm1[li][None, :], w_in=w_in[li].astype(BF16),
            qg=jnp.tile(q_gain[li], ATT_HEADS)[None, :] * (HEAD_DIM ** -0.5 * LOG2E),
            kg=jnp.tile(k_gain[li], KV_HEADS)[None, :],
            conv_w=conv_w[li], conv_b=conv_b[li][None, :],
            skip=hy_skip[li][None, :],
            sink_t=jnp.repeat(sink[li] * LOG2E, BLOCK).reshape(KV_HEADS, 1, GQA_GROUP * BLOCK),
            hy_gain=hy_gain[li][None, :], at_gain=at_gain[li][None, :],
            w_out=w_out[li].astype(BF16), norm2=norm2[li][None, :],
            w_gate_up=w_gate_up[li].astype(BF16), w_down=w_down[li].astype(BF16),
        )
        lw["kf"] = {key: _filter_spectrum(raw, st, {"prompt": x_prompt, "sample": x_sample}[key].shape[1])
                    for key, st in streams.items()}
        y_prompt = _layer(y_prompt, lw, shared, streams["prompt"])
        y_sample = _layer(y_sample, lw, shared, streams["sample"])
    return (y_prompt, y_sample)
```

```python
import functools
import math

import numpy as np
import jax
import jax.numpy as jnp
from jax import lax
from jax.experimental import pallas as pl
from jax.experimental.pallas import tpu as pltpu

F32 = jnp.float32
BF16 = jnp.bfloat16

D_MODEL = 1024
ATT_HEADS = 8
KV_HEADS = 2
HEAD_DIM = 64
GQA_GROUP = ATT_HEADS // KV_HEADS
ATT_W = ATT_HEADS * HEAD_DIM
KV_W = KV_HEADS * HEAD_DIM
WINDOW = 128
BLOCK = 128
N_BUCKETS = 32
MAX_DIST = 128
HY_CH = D_MODEL - ATT_W
FILTER_HIDDEN = 64
N_BANDS = 16
POS_DIM = 1 + 2 * N_BANDS
FAST_DECAY_PCT = 0.3
SLOW_DECAY_PCT = 1.5
DECAY_TARGET = 1e-2
IN_W = 3 * HY_CH + (ATT_HEADS + 2 * KV_HEADS) * HEAD_DIM
D_FF = -(-8 * D_MODEL // (3 * 256)) * 256
EPS = 1e-6
LOG2E = math.log2(math.e)

FFT_N2 = 64
LANES = 128
N2_TILE = 8
SLABS = HY_CH // LANES
STAGE_MB = 512
STAGE_KB = 16
STAGE_B_CHAINS = 8
INPROJ_TM = 512
FFN_TM = 512
VMEM_LIMIT = 56 * 1024 * 1024

_NT = (((1,), (1,)), ((), ()))


def _cparams(*sem):
    return pltpu.CompilerParams(dimension_semantics=sem, vmem_limit_bytes=VMEM_LIMIT)


def _const_spec(shape):
    nd = len(shape)
    return pl.BlockSpec(shape, lambda *_: (0,) * nd, pipeline_mode=pl.Buffered(1))


HALO = 16


def _inproj_kernel(x_ref, xp_ref, xn_ref, g1_ref, w_ref, cw_ref, cb_ref, bq_ref, bk_ref, qg_ref, kg_ref,
                   ux_ref, q_ref, k_ref, v_ref, *, nt):
    i = pl.program_id(0)
    tm = x_ref.shape[0]
    xp = jnp.where(i % nt == 0, 0.0, xp_ref[...])
    xn = jnp.where(i % nt == nt - 1, 0.0, xn_ref[...])
    xe = jnp.concatenate([xp, x_ref[...], xn], axis=0)
    xe = (xe * lax.rsqrt(jnp.mean(xe * xe, axis=-1, keepdims=True) + EPS) * g1_ref[...]).astype(BF16)
    xb = xe[HALO:HALO + tm]

    def seg(lo, hi):
        return jnp.dot(xb, w_ref[:, lo:hi], preferred_element_type=F32)

    o = 3 * HY_CH
    v = seg(o + ATT_W + KV_W, IN_W)
    v_ref[0] = v.T.astype(BF16)

    k = seg(o + ATT_W, o + ATT_W + KV_W)
    ms = jnp.dot((k * k).astype(BF16), bk_ref[...], preferred_element_type=F32)
    kn = k * lax.rsqrt(ms + EPS) * kg_ref[...]
    lo_half = lax.broadcasted_iota(jnp.int32, kn.shape, 1) < HEAD_DIM
    kr = pltpu.roll(kn, HEAD_DIM, axis=1)
    k_ref[...] = jnp.concatenate([jnp.where(lo_half, kn, kr), jnp.where(lo_half, kr, kn)], axis=1).astype(BF16)

    q = seg(o, o + ATT_W)
    ms = jnp.dot((q * q).astype(BF16), bq_ref[...], preferred_element_type=F32)
    q_ref[...] = (q * lax.rsqrt(ms + EPS) * qg_ref[...]).astype(BF16)

    rows = slice(HALO, HALO + tm)

    def conv(j):
        c = slice(j * HY_CH, (j + 1) * HY_CH)
        he = jnp.dot(xe, w_ref[:, c], preferred_element_type=F32)
        return (pltpu.roll(he, 1, axis=0)[rows] * cw_ref[0:1, c] + he[rows] * cw_ref[1:2, c]
                + pltpu.roll(he, tm + 2 * HALO - 1, axis=0)[rows] * cw_ref[2:3, c] + cb_ref[0:1, c])

    ux_ref[...] = _pack_pair(conv(1) * conv(2), conv(0))


def _inproj(x2d, g1, w_in_b, conv_w, conv_b, bq, bk, qg, kg, tm, seq_len):
    t = x2d.shape[0]
    nt = seq_len // tm
    nh = tm // HALO
    row = lambda w: pl.BlockSpec((tm, w), lambda i: (i, 0))
    prev = pl.BlockSpec((HALO, D_MODEL), lambda i: (jnp.maximum(i * nh - 1, 0), 0))
    nxt = pl.BlockSpec((HALO, D_MODEL), lambda i: (jnp.minimum((i + 1) * nh, t // HALO - 1), 0))
    return pl.pallas_call(
        functools.partial(_inproj_kernel, nt=nt),
        grid=(t // tm,),
        in_specs=[row(D_MODEL), prev, nxt, _const_spec((1, D_MODEL)), _const_spec((D_MODEL, IN_W)),
                  _const_spec((3, 3 * HY_CH)), _const_spec((1, 3 * HY_CH)),
                  _const_spec((ATT_W, ATT_W)), _const_spec((KV_W, KV_W)),
                  _const_spec((1, ATT_W)), _const_spec((1, KV_W))],
        out_specs=[row(HY_CH), row(ATT_W), row(2 * KV_W),
                   pl.BlockSpec((1, KV_W, tm), lambda i: (i // nt, 0, i % nt))],
        out_shape=[jax.ShapeDtypeStruct((t, HY_CH), jnp.uint32),
                   jax.ShapeDtypeStruct((t, ATT_W), BF16), jax.ShapeDtypeStruct((t, 2 * KV_W), BF16),
                   jax.ShapeDtypeStruct((t // seq_len, KV_W, seq_len), BF16)],
        compiler_params=_cparams("parallel"),
        name="inproj",
    )(x2d, x2d, x2d, g1, w_in_b, conv_w, conv_b, bq, bk, qg, kg)


def _stage_a_kernel(*refs, nz, a, mb, packed_in):
    x_refs, f_ref, o_ref = refs[:SLABS], refs[SLABS], refs[SLABS + 1]
    kblk = mb // 2
    xs = [r.reshape(nz * a * N2_TILE, LANES) for r in x_refs]
    o2 = o_ref.reshape(SLABS * kblk * N2_TILE, LANES)
    f = f_ref[...]

    def gather(s):
        x = jnp.concatenate(
            [jnp.concatenate([xs[c][pl.ds(z * a * N2_TILE + s, a, stride=N2_TILE), :] for z in range(nz)], axis=0)
             for c in range(SLABS)], axis=1)
        return (_unpack_pair(x)[0] if packed_in else x).astype(BF16)

    r = [jnp.dot(f, gather(s), preferred_element_type=F32) for s in range(N2_TILE)]
    for s in range(N2_TILE):
        packed = _pack_pair(r[s][:kblk], r[s][kblk:])
        for c in range(SLABS):
            o2[pl.ds(c * kblk * N2_TILE + s, kblk, stride=N2_TILE), :] = packed[:, c * LANES:(c + 1) * LANES]


def _pack_pair(re, im):
    rb = lax.bitcast_convert_type(re.astype(BF16).astype(F32), jnp.uint32)
    ib = lax.bitcast_convert_type(im.astype(BF16).astype(F32), jnp.uint32)
    return (rb >> 16) | ib


def _unpack_pair(p):
    re = lax.bitcast_convert_type(p << 16, F32)
    im = lax.bitcast_convert_type(p & jnp.uint32(0xFFFF0000), F32)
    return re, im


def _stage_a(x5, fmat, mb):
    nz, p, a, n2, _ = x5.shape
    m = fmat.shape[0]
    kblk = mb // 2
    xspec = lambda c: pl.BlockSpec((nz, 1, a, N2_TILE, LANES), lambda pi, j, mi: (0, pi, 0, j, c))
    return pl.pallas_call(
        functools.partial(_stage_a_kernel, nz=nz, a=a, mb=mb, packed_in=x5.dtype == jnp.uint32),
        grid=(p, n2 // N2_TILE, m // mb),
        in_specs=[xspec(c) for c in range(SLABS)] + [pl.BlockSpec((mb, nz * a), lambda pi, j, mi: (mi, 0))],
        out_specs=pl.BlockSpec((1, SLABS, kblk, N2_TILE, LANES), lambda pi, j, mi: (pi, 0, mi, j, 0)),
        out_shape=jax.ShapeDtypeStruct((p, SLABS, m // 2, n2, LANES), jnp.uint32),
        compiler_params=_cparams("parallel", "parallel", "parallel"),
        name="hy_stage_a",
    )(*([x5] * SLABS), fmat)


MXU_COLS = 256
SLABS_PER_DOT = MXU_COLS // LANES


def _slab_rows(a_ref, j, h):
    parts = [_unpack_pair(a_ref[0, h * SLABS_PER_DOT + c, j]) for c in range(SLABS_PER_DOT)]
    return jnp.concatenate([jnp.concatenate([re, im], axis=0) for re, im in parts], axis=1).astype(BF16)


def _spectral_mul(u, kr, ki, n2):
    ur, ui = u[:n2], u[n2:]
    return jnp.concatenate([ur * kr - ui * ki, ur * ki + ui * kr], axis=0).astype(BF16)


def _stage_b_kernel(a_ref, kf_ref, g_ref, gt_ref, *rest, herm):
    if herm:
        edge_ref, kfh_ref, o_ref = rest
    else:
        (o_ref,) = rest
    kb, n2 = a_ref.shape[2], a_ref.shape[3]

    def store(j, h, v):
        packed = _pack_pair(v[:n2], v[n2:])
        for c in range(SLABS_PER_DOT):
            o_ref[0, h * SLABS_PER_DOT + c, j] = packed[:, c * LANES:(c + 1) * LANES]

    dot = functools.partial(jnp.dot, preferred_element_type=F32)
    chains = [(j, h) for j in range(kb) for h in range(SLABS // SLABS_PER_DOT)]
    for c0 in range(0, len(chains), STAGE_B_CHAINS):
        batch = chains[c0:c0 + STAGE_B_CHAINS]
        fwd = []
        for j, h in batch:
            x = _slab_rows(a_ref, j, h)
            if herm and j == 0:
                fwd.append((dot(edge_ref[0, 0], x), dot(edge_ref[0, 1], x)))
            else:
                fwd.append((dot(g_ref[j], x),))
        prod = []
        for (j, h), us in zip(batch, fwd):
            lanes = slice(h * MXU_COLS, (h + 1) * MXU_COLS)
            ps = [_spectral_mul(us[0], kf_ref[0, j, :, lanes], kf_ref[1, j, :, lanes], n2)]
            if len(us) == 2:
                ps.append(_spectral_mul(us[1], kfh_ref[0, 0, :, lanes], kfh_ref[0, 1, :, lanes], n2))
            prod.append(ps)
        for (j, h), ps in zip(batch, prod):
            if len(ps) == 2:
                store(j, h, dot(edge_ref[0, 2], ps[0]) + dot(edge_ref[0, 3], ps[1]))
            else:
                store(j, h, dot(gt_ref[j], ps[0]))


def _stage_b(a5, kf, g, gt, kb, edge=None, kfh=None):
    p, _, k1n, n2, _ = a5.shape
    herm = edge is not None
    blk = pl.BlockSpec((1, SLABS, kb, n2, LANES), lambda i, pi: (pi, 0, i, 0, 0))
    gspec = pl.BlockSpec((kb, 2 * n2, 2 * n2), lambda i, pi: (i, 0, 0))
    extra = [pl.BlockSpec((1,) + edge.shape[1:], lambda i, pi: (i, 0, 0, 0)),
             pl.BlockSpec((1,) + kfh.shape[1:], lambda i, pi: (i, 0, 0, 0))] if herm else []
    return pl.pallas_call(
        functools.partial(_stage_b_kernel, herm=herm),
        grid=(k1n // kb, p),
        in_specs=[blk, pl.BlockSpec((2, kb, n2, HY_CH), lambda i, pi: (0, i, 0, 0)), gspec, gspec] + extra,
        out_specs=blk,
        out_shape=jax.ShapeDtypeStruct(a5.shape, jnp.uint32),
        compiler_params=_cparams("parallel", "parallel"),
        name="hy_stage_b",
    )(a5, kf, g, gt, *([edge, kfh] if herm else []))


def _stage_bf_kernel(a_ref, g_ref, *rest, scale, herm):
    if herm:
        edge_ref, o_ref, oh_ref = rest
    else:
        (o_ref,) = rest
    kb, n2 = a_ref.shape[2], a_ref.shape[3]

    def regular(j, h):
        lanes = slice(h * MXU_COLS, (h + 1) * MXU_COLS)
        u = jnp.dot(g_ref[j], _slab_rows(a_ref, j, h), preferred_element_type=F32) * scale
        o_ref[0, j, :, lanes] = u[:n2]
        o_ref[1, j, :, lanes] = u[n2:]

    def edge(h):
        lanes = slice(h * MXU_COLS, (h + 1) * MXU_COLS)
        x = _slab_rows(a_ref, 0, h)
        u0 = jnp.dot(edge_ref[0, 0], x, preferred_element_type=F32) * scale
        uh = jnp.dot(edge_ref[0, 1], x, preferred_element_type=F32) * scale
        o_ref[0, 0, :, lanes] = u0[:n2]
        o_ref[1, 0, :, lanes] = u0[n2:]
        oh_ref[0, 0, :, lanes] = uh[:n2]
        oh_ref[0, 1, :, lanes] = uh[n2:]

    for j in range(kb):
        for h in range(SLABS // SLABS_PER_DOT):
            if herm and j == 0:
                edge(h)
            else:
                regular(j, h)


def _stage_bf(a5, g, kb, scale, edge=None):
    _, _, k1n, n2, _ = a5.shape
    herm = edge is not None
    out_specs = [pl.BlockSpec((2, kb, n2, HY_CH), lambda i: (0, i, 0, 0))]
    out_shape = [jax.ShapeDtypeStruct((2, k1n, n2, HY_CH), F32)]
    if herm:
        out_specs.append(pl.BlockSpec((1, 2, n2, HY_CH), lambda i: (i, 0, 0, 0)))
        out_shape.append(jax.ShapeDtypeStruct((k1n // kb, 2, n2, HY_CH), F32))
    return pl.pallas_call(
        functools.partial(_stage_bf_kernel, scale=scale, herm=herm),
        grid=(k1n // kb,),
        in_specs=[pl.BlockSpec((1, SLABS, kb, n2, LANES), lambda i: (0, 0, i, 0, 0)),
                  pl.BlockSpec((kb, 2 * n2, 2 * n2), lambda i: (i, 0, 0))]
        + ([pl.BlockSpec((1,) + edge.shape[1:], lambda i: (i, 0, 0, 0))] if herm else []),
        out_specs=out_specs,
        out_shape=out_shape,
        compiler_params=_cparams("parallel"),
        name="hy_filter_spectrum",
    )(a5, g, *([edge] if herm else []))


def _stage_c_kernel(*refs, nz, a, mb):
    v_ref, f_ref = refs[0], refs[1]
    ux_refs = refs[2:2 + SLABS]
    skip_ref, o_ref, acc_ref = refs[2 + SLABS:]
    mk = pl.program_id(2)
    kblk = mb // 2
    v2 = v_ref.reshape(SLABS * kblk * N2_TILE, LANES)

    @pl.when(mk == 0)
    def _():
        acc_ref[...] = jnp.zeros_like(acc_ref)

    for s in range(N2_TILE):
        packed = jnp.concatenate([v2[pl.ds(c * kblk * N2_TILE + s, kblk, stride=N2_TILE), :] for c in range(SLABS)],
                                 axis=1)
        vs = jnp.concatenate(_unpack_pair(packed), axis=0).astype(BF16)
        acc_ref[s] += jnp.dot(f_ref[...], vs, preferred_element_type=F32)

    @pl.when(mk == pl.num_programs(2) - 1)
    def _():
        o2 = o_ref.reshape(SLABS * a * N2_TILE, LANES)
        ux2 = [r.reshape(nz * a * N2_TILE, LANES) for r in ux_refs]
        for s in range(N2_TILE):
            y = acc_ref[s]
            for c in range(SLABS):
                vals = []
                for z in range(nz):
                    uu, xx = _unpack_pair(ux2[c][pl.ds(z * a * N2_TILE + s, a, stride=N2_TILE), :])
                    yc = y[z * a:(z + 1) * a, c * LANES:(c + 1) * LANES]
                    vals.append(xx * (yc + skip_ref[0:1, c * LANES:(c + 1) * LANES] * uu))
                o2[pl.ds(c * a * N2_TILE + s, a, stride=N2_TILE), :] = _pack_pair(*vals) if nz == 2 else vals[0]


def _stage_c(v5, fmat, ux5, skip, mb):
    nz, p, a, n2, _ = ux5.shape
    m = 2 * v5.shape[2]
    xspec = lambda c: pl.BlockSpec((nz, 1, a, N2_TILE, LANES), lambda pi, j, mk: (0, pi, 0, j, c))
    return pl.pallas_call(
        functools.partial(_stage_c_kernel, nz=nz, a=a, mb=mb),
        grid=(p, n2 // N2_TILE, m // mb),
        in_specs=[pl.BlockSpec((1, SLABS, mb // 2, N2_TILE, LANES), lambda pi, j, mk: (pi, 0, mk, j, 0)),
                  pl.BlockSpec((nz * a, mb), lambda pi, j, mk: (0, mk))]
        + [xspec(c) for c in range(SLABS)] + [_const_spec((1, HY_CH))],
        out_specs=pl.BlockSpec((1, SLABS, a, N2_TILE, LANES), lambda pi, j, mk: (pi, 0, 0, j, 0)),
        out_shape=jax.ShapeDtypeStruct((p, SLABS, a, n2, LANES), jnp.uint32 if nz == 2 else F32),
        scratch_shapes=[pltpu.VMEM((N2_TILE, nz * a, HY_CH), F32)],
        compiler_params=_cparams("parallel", "parallel", "arbitrary"),
        name="hy_stage_c",
    )(v5, fmat, *([ux5] * SLABS), skip)


def _filter_kernel(zt_ref, tn_ref, w1t_ref, b1_ref, fr_ref, w2t_ref, b2_ref, w3_ref, dl_ref, o_ref, *, seq_len):
    hi = lax.Precision.HIGHEST
    fr = fr_ref[...]
    h = jnp.sin(fr * (jnp.dot(w1t_ref[...], zt_ref[...], precision=hi, preferred_element_type=F32) + b1_ref[...]))
    h = jnp.sin(fr * (jnp.dot(w2t_ref[...], h, precision=hi, preferred_element_type=F32) + b2_ref[...]))
    taps = jnp.dot(h.T.astype(BF16), w3_ref[...], preferred_element_type=F32)
    tr = taps.shape[0]
    n = pl.program_id(0) * tr + lax.broadcasted_iota(jnp.int32, (tr, HY_CH), 0)
    sel = jnp.where(n < seq_len, taps[:, :HY_CH], jnp.where(n > seq_len, taps[:, HY_CH:], 0.0))
    o_ref[...] = sel * jnp.exp(-tn_ref[...] * dl_ref[...])


def _filter_taps(zfeat, f_w1, f_b1, f_freq, f_w2, f_b2, f_w3, absdelta, seq_len, tr):
    n = zfeat.shape[0]
    cs = lambda a: _const_spec(a.shape)
    args = (zfeat.T, zfeat[:, 0:1], f_w1.T, f_b1.T, f_freq.T, f_w2.T, f_b2.T, f_w3.astype(BF16), absdelta)
    return pl.pallas_call(
        functools.partial(_filter_kernel, seq_len=seq_len),
        grid=(n // tr,),
        in_specs=[pl.BlockSpec((POS_DIM, tr), lambda i: (0, i)), pl.BlockSpec((tr, 1), lambda i: (i, 0))]
        + [cs(a) for a in args[2:]],
        out_specs=pl.BlockSpec((tr, HY_CH), lambda i: (i, 0)),
        out_shape=jax.ShapeDtypeStruct((n, HY_CH), F32),
        compiler_params=_cparams("parallel"),
        name="hy_filter_taps",
    )(*args)


def _bias_kernel(rb_ref, oh_ref, o_ref):
    o_ref[...] = jnp.dot(rb_ref[...], oh_ref[...], precision=lax.Precision.HIGHEST,
                         preferred_element_type=F32)


def _bias_table(rel_bias):
    i = jnp.arange(BLOCK)[:, None]
    j = jnp.arange(3 * BLOCK)[None, :]
    rel = j - BLOCK - i
    nb2 = N_BUCKETS // 2
    max_exact = nb2 // 2
    n = jnp.abs(rel)
    large = max_exact + (jnp.log(jnp.maximum(n, 1).astype(F32) / max_exact)
                         / math.log(MAX_DIST / max_exact) * (nb2 - max_exact)).astype(jnp.int32)
    large = jnp.minimum(large, nb2 - 1)
    bucket = jnp.where(rel > 0, nb2, 0) + jnp.where(n < max_exact, n, large)
    onehot = (bucket.reshape(1, -1) == jnp.arange(N_BUCKETS)[:, None]).astype(F32)
    cols = onehot.shape[1]
    tc = cols // 4
    table = pl.pallas_call(
        _bias_kernel,
        grid=(4,),
        in_specs=[_const_spec((ATT_HEADS, N_BUCKETS)), pl.BlockSpec((N_BUCKETS, tc), lambda c: (0, c))],
        out_specs=pl.BlockSpec((ATT_HEADS, tc), lambda c: (0, c)),
        out_shape=jax.ShapeDtypeStruct((ATT_HEADS, cols), F32),
        compiler_params=_cparams("parallel"),
        name="att_bias_table",
    )(rel_bias.T, onehot)
    table = table.reshape(ATT_HEADS, BLOCK, 3 * BLOCK)
    table = jnp.where((n <= WINDOW)[None], table, -jnp.inf)
    table = table.reshape(KV_HEADS, GQA_GROUP, BLOCK, 3 * BLOCK)
    return table.transpose(0, 3, 1, 2).reshape(KV_HEADS, 3 * BLOCK, GQA_GROUP * BLOCK)


def _attn_kernel(q_ref, km_ref, kp_ref, kn_ref, vm_ref, vp_ref, vn_ref, bias_ref, sink_ref, o_ref, *, nsub):
    i = pl.program_id(1)
    neg = -jnp.inf
    pen_first = jnp.where(i == 0, neg, 0.0)
    pen_last = jnp.where(i == pl.num_programs(1) - 1, neg, 0.0)
    kwin = jnp.concatenate([kp_ref[0], km_ref[0], kn_ref[0]], axis=0)
    vwin = jnp.concatenate([vp_ref[0], vm_ref[0], vn_ref[0]], axis=1)
    lane = lax.broadcasted_iota(jnp.int32, (BLOCK, LANES), 1)
    lo_half = lane < HEAD_DIM
    ones = jnp.ones((SUM_ROWS, 3 * BLOCK), BF16)
    units = [(s, g) for s in range(nsub) for g in range(KV_HEADS)]

    def scores(s, g):
        qs = q_ref[0, s * BLOCK:(s + 1) * BLOCK, :]
        rows = []
        for pr in range(2):
            qp = qs[:, (2 * g + pr) * LANES:(2 * g + pr + 1) * LANES]
            rows.append(jnp.where(lo_half, qp, jnp.zeros_like(qp)))
            rows.append(jnp.where(lo_half, jnp.zeros_like(qp), qp))
        qg = jnp.concatenate(rows, axis=0)
        kg = kwin[s * BLOCK:(s + 3) * BLOCK, g * LANES:(g + 1) * LANES]
        t = lax.dot_general(kg, qg, _NT, preferred_element_type=F32) + bias_ref[g]
        if s == 0:
            t = jnp.concatenate([t[:BLOCK] + pen_first, t[BLOCK:]], axis=0)
        if s == nsub - 1:
            t = jnp.concatenate([t[:2 * BLOCK], t[2 * BLOCK:] + pen_last], axis=0)
        return t

    def pv(s, g, p):
        vg = jnp.concatenate([vwin[g * HEAD_DIM:(g + 1) * HEAD_DIM, s * BLOCK:(s + 3) * BLOCK], ones],
                             axis=0)
        return jnp.dot(vg, p, preferred_element_type=F32)

    for u0 in range(0, len(units), ATT_UNITS):
        batch = units[u0:u0 + ATT_UNITS]
        sc = [scores(s, g) for s, g in batch]
        m = [jnp.maximum(jnp.max(t, axis=0, keepdims=True), sink_ref[g]) for t, (_, g) in zip(sc, batch)]
        p = [jnp.exp2(t - mm).astype(BF16) for t, mm in zip(sc, m)]
        o = [pv(s, g, pp) for (s, g), pp in zip(batch, p)]
        for (s, g), oo, mm in zip(batch, o, m):
            on = oo[:HEAD_DIM] / (oo[HEAD_DIM:HEAD_DIM + 1] + jnp.exp2(sink_ref[g] - mm))
            for pr in range(2):
                pair = jnp.concatenate([on[:, (2 * pr) * BLOCK:(2 * pr + 1) * BLOCK],
                                        on[:, (2 * pr + 1) * BLOCK:(2 * pr + 2) * BLOCK]], axis=0)
                o_ref[0, s * BLOCK:(s + 1) * BLOCK, (2 * g + pr) * LANES:(2 * g + pr + 1) * LANES] = (
                    pair.T.astype(o_ref.dtype))


SUM_ROWS = 64
ATT_UNITS = 8
ATT_TQ = 1024


def _attention(q, kd, vt, bias_t, sink_t, tq):
    b, l, _ = q.shape
    nsub = tq // BLOCK
    nblk = l // BLOCK
    kw = 2 * KV_W
    main = lambda w: pl.BlockSpec((1, tq, w), lambda bi, i: (bi, i, 0))
    prev = pl.BlockSpec((1, BLOCK, kw), lambda bi, i: (bi, jnp.maximum(i * nsub - 1, 0), 0))
    nxt = pl.BlockSpec((1, BLOCK, kw), lambda bi, i: (bi, jnp.minimum((i + 1) * nsub, nblk - 1), 0))
    vmain = pl.BlockSpec((1, KV_W, tq), lambda bi, i: (bi, 0, i))
    vprev = pl.BlockSpec((1, KV_W, BLOCK), lambda bi, i: (bi, 0, jnp.maximum(i * nsub - 1, 0)))
    vnxt = pl.BlockSpec((1, KV_W, BLOCK), lambda bi, i: (bi, 0, jnp.minimum((i + 1) * nsub, nblk - 1)))
    return pl.pallas_call(
        functools.partial(_attn_kernel, nsub=nsub),
        grid=(b, l // tq),
        in_specs=[main(ATT_W), main(kw), prev, nxt, vmain, vprev, vnxt,
                  _const_spec(bias_t.shape), _const_spec(sink_t.shape)],
        out_specs=main(ATT_W),
        out_shape=jax.ShapeDtypeStruct((b, l, ATT_W), BF16),
        compiler_params=_cparams("parallel", "parallel"),
        name="window_attn",
    )(q, kd, kd, kd, vt, vt, vt, bias_t, sink_t)


FF_CHUNK = 256


def _ffn_kernel(x_ref, yh_ref, ya_ref, hg_ref, ag_ref, wo_ref, g2_ref, wgu_ref, wd_ref, o_ref, act_ref, *, pairs, nt):
    def rms(t, g):
        return (t * lax.rsqrt(jnp.mean(t * t, axis=-1, keepdims=True) + EPS) * g).astype(BF16)

    yh = jnp.concatenate([yh_ref[0, c] for c in range(SLABS)], axis=1)
    if pairs:
        lo, hi = _unpack_pair(yh)
        yh = jnp.where(pl.program_id(0) // nt < pairs, lo, hi)
    mixed = jnp.concatenate([rms(yh, hg_ref[...]), rms(ya_ref[...].astype(F32), ag_ref[...])], axis=1)
    h = x_ref[...] + jnp.dot(mixed, wo_ref[...], preferred_element_type=F32)
    hn = rms(h, g2_ref[...])
    for c in range(D_FF // FF_CHUNK):
        lo = c * FF_CHUNK
        gate = jnp.dot(hn, wgu_ref[:, lo:lo + FF_CHUNK], preferred_element_type=F32)
        up = jnp.dot(hn, wgu_ref[:, D_FF + lo:D_FF + lo + FF_CHUNK], preferred_element_type=F32)
        act_ref[:, lo:lo + FF_CHUNK] = (gate / (1.0 + jnp.exp(-gate)) * up).astype(BF16)
    o_ref[...] = h + jnp.dot(act_ref[...], wd_ref[...], preferred_element_type=F32)


def _ffn(x2d, yh, ya, hg, ag, wo_b, g2, wgu_b, wd_b, tm):
    t = x2d.shape[0]
    nt = yh.shape[2] // tm
    pairs = yh.shape[0] if yh.dtype == jnp.uint32 else 0
    row = lambda w: pl.BlockSpec((tm, w), lambda i: (i, 0))
    yspec = pl.BlockSpec((1, SLABS, tm, LANES), lambda i: ((i // nt) % yh.shape[0], 0, i % nt, 0))
    return pl.pallas_call(
        functools.partial(_ffn_kernel, pairs=pairs, nt=nt),
        grid=(t // tm,),
        in_specs=[row(D_MODEL), yspec, row(ATT_W), _const_spec((1, HY_CH)), _const_spec((1, ATT_W)),
                  _const_spec((D_MODEL, D_MODEL)), _const_spec((1, D_MODEL)),
                  _const_spec((D_MODEL, 2 * D_FF)), _const_spec((D_FF, D_MODEL))],
        out_specs=row(D_MODEL),
        out_shape=jax.ShapeDtypeStruct((t, D_MODEL), F32),
        scratch_shapes=[pltpu.VMEM((tm, D_FF), BF16)],
        compiler_params=_cparams("parallel"),
        name="outproj_swiglu",
    )(x2d, yh, ya, hg, ag, wo_b, g2, wgu_b, wd_b)


def _dft_tables(seq_len, paired, kblk):
    n = 2 * seq_len
    n1 = n // FFT_N2
    a = n1 // 2
    kk = np.arange(n1)[:, None]
    th_half = 2.0 * np.pi * ((kk * np.arange(a)[None, :]) % n1) / n1
    c, s = np.cos(th_half), np.sin(th_half)

    def blocked(re, im):
        return np.concatenate([np.concatenate([re[i:i + kblk], im[i:i + kblk]], axis=0)
                               for i in range(0, re.shape[0], kblk)], axis=0)

    def real_input_rows(cc, ss):
        im = -ss[:a].copy()
        im[0] = cc[a]
        return cc[:a], im

    th_full = 2.0 * np.pi * ((kk * np.arange(n1)[None, :]) % n1) / n1
    cf, sf = np.cos(th_full), np.sin(th_full)
    if paired:
        fa = blocked(np.concatenate([c, s], axis=1), np.concatenate([-s, c], axis=1))
        fc = fa.T
        faf = blocked(cf, -sf)
    else:
        fa = blocked(*real_input_rows(c, s))
        re_cols = np.concatenate([c[:1], 2.0 * c[1:a]], axis=0)
        im_cols = np.concatenate([c[a:a + 1], -2.0 * s[1:a]], axis=0)
        fc = blocked(re_cols, im_cols).T
        faf = blocked(*real_input_rows(cf, sf))
    to = lambda m: jnp.asarray(m.astype(np.float32)).astype(BF16)
    return to(fa), to(fc), to(faf)


def _edge_matrices(g, half, kb):
    zero = jnp.zeros_like(g[0])
    z = zero[:, :FFT_N2]
    col = lambda m: m[:, :FFT_N2]
    sa = jnp.concatenate([col(g[0]), z], axis=1)
    sb = jnp.concatenate([z, col(g[half])], axis=1)
    blocks = [jnp.stack([sa, sb, sa.T, sb.T])]
    blocks += [jnp.stack([g[i], zero, g[i].T, zero]) for i in range(kb, half, kb)]
    return jnp.stack(blocks)


def _twiddle_tables(seq_len, k1_count):
    n = 2 * seq_len
    n1 = n // FFT_N2
    k1 = np.arange(k1_count, dtype=np.int64)[:, None, None]
    k2 = np.arange(FFT_N2, dtype=np.int64)[None, :, None]
    n2 = np.arange(FFT_N2, dtype=np.int64)[None, None, :]
    ang = ((n2 * (k2 * n1 + k1)) % n) * (-2.0 * np.pi / n)
    gr, gi = np.cos(ang), np.sin(ang)
    g = np.concatenate([np.concatenate([gr, -gi], axis=2), np.concatenate([gi, gr], axis=2)], axis=1)
    g = g.astype(np.float32)
    return jnp.asarray(g.astype(BF16)), jnp.asarray(np.swapaxes(g, 1, 2).astype(BF16))


def _filter_features(seq_len):
    n = np.arange(2 * seq_len)
    t = np.where(n < seq_len, n, 2 * seq_len - n).astype(np.float32)[:, None]
    t_idx = jnp.asarray(t)
    t_norm = t_idx / max(seq_len - 1, 1)
    bands = jnp.linspace(1e-4, N_BANDS - 1, N_BANDS, dtype=F32)
    w = (2.0 * math.pi) * t_idx * bands[None, :] / seq_len
    return jnp.concatenate([t_norm, jnp.cos(w), -jnp.sin(w)], axis=-1)


def _abs_deltas():
    min_decay = math.log(DECAY_TARGET) / FAST_DECAY_PCT
    max_decay = math.log(DECAY_TARGET) / SLOW_DECAY_PCT
    return jnp.abs(jnp.linspace(min_decay, max_decay, HY_CH, dtype=F32))[None, :]


def _pick(n, target):
    t = min(n, target)
    while n % t:
        t //= 2
    return t


def _layer(x, lw, shared, st):
    b, l, _ = x.shape
    t = b * l
    x2d = x.reshape(t, D_MODEL)
    tm = _pick(l, FFN_TM)
    ux, q, kd, vt = _inproj(x2d, lw["norm1"], lw["w_in"], lw["conv_w"], lw["conv_b"],
                            shared["bq"], shared["bk"], lw["qg"], lw["kg"], _pick(l, INPROJ_TM), l)
    r3 = lambda a: a.reshape(b, l, a.shape[-1])

    n1 = 2 * l // FFT_N2
    a = n1 // 2
    nz = st["nz"]
    p = b // nz
    ux5 = ux.reshape(nz, p, a, FFT_N2, HY_CH)
    mb = _pick(st["fa"].shape[0], STAGE_MB)
    ah = _stage_a(ux5, st["fa"], mb)
    kf, kfh = lw["kf"][st["key"]]
    vv = _stage_b(ah, kf, st["g"], st["gt"], _pick(ah.shape[2], STAGE_KB), st["edge"], kfh)
    yh = _stage_c(vv, st["fc"], ux5, lw["skip"], mb)
    yh = yh.reshape(p, SLABS, l, LANES)

    ya = _attention(r3(q), r3(kd), vt, shared["bias"], lw["sink_t"], _pick(l, ATT_TQ))
    ya = ya.reshape(t, ATT_W)

    out = _ffn(x2d, yh, ya, lw["hy_gain"], lw["at_gain"], lw["w_out"], lw["norm2"],
               lw["w_gate_up"], lw["w_down"], tm)
    return out.reshape(b, l, D_MODEL)


def _filter_spectrum(lw_raw, st, seq_len):
    n = 2 * seq_len
    n1 = n // FFT_N2
    taps = _filter_taps(st["zfeat"], lw_raw["f_w1"], lw_raw["f_b1"], lw_raw["f_freq"], lw_raw["f_w2"],
                        lw_raw["f_b2"], lw_raw["f_w3"], st["absdelta"], seq_len, _pick(n, 512))
    ah = _stage_a(taps.reshape(1, 1, n1, FFT_N2, HY_CH), st["faf"], _pick(st["faf"].shape[0], STAGE_MB))
    out = _stage_bf(ah, st["g"], _pick(ah.shape[2], STAGE_KB), 1.0 / n, st["edge"])
    return (out[0], out[1]) if st["edge"] is not None else (out[0], None)


def kernel(x_prompt, x_sample, norm1, w_in, conv_w, conv_b, f_w1, f_b1, f_freq, f_w2, f_b2, f_w3,
           hy_skip, q_gain, k_gain, sink, rel_bias, hy_gain, at_gain, w_out, norm2, w_gate_up, w_down):
    depth = norm1.shape[0]
    streams = {}
    for key, x in (("prompt", x_prompt), ("sample", x_sample)):
        b, l, _ = x.shape
        paired = b % 2 == 0
        n1 = 2 * l // FFT_N2
        fa, fc, faf = _dft_tables(l, paired, _pick(2 * n1 if paired else n1, STAGE_MB) // 2)
        g, gt = _twiddle_tables(l, n1 if paired else n1 // 2 + 1)
        streams[key] = dict(key=key, nz=2 if paired else 1, fa=fa, fc=fc, faf=faf, g=g, gt=gt,
                            edge=None if paired else _edge_matrices(g, n1 // 2, _pick(n1 // 2, STAGE_KB)),
                            zfeat=_filter_features(l), absdelta=_abs_deltas())

    head_id = np.arange(ATT_W) // HEAD_DIM
    shared = dict(
        bq=jnp.asarray((head_id[:, None] == head_id[None, :]).astype(np.float32) / HEAD_DIM).astype(BF16),
        bk=jnp.asarray((head_id[:KV_W, None] == head_id[None, :KV_W]).astype(np.float32) / HEAD_DIM).astype(BF16),
        bias=_bias_table(rel_bias) * LOG2E,
    )

    y_prompt, y_sample = x_prompt, x_sample
    for li in range(depth):
        raw = dict(f_w1=f_w1[li], f_b1=f_b1[li][None, :], f_freq=f_freq[li][None, :], f_w2=f_w2[li],
                   f_b2=f_b2[li][None, :], f_w3=f_w3[li])
        lw = dict(
            norm1=norm1[li][None, :], w_in=w_in[li].astype(BF16),
            qg=jnp.tile(q_gain[li], ATT_HEADS)[None, :] * (HEAD_DIM ** -0.5 * LOG2E),
            kg=jnp.tile(k_gain[li], KV_HEADS)[None, :],
            conv_w=conv_w[li], conv_b=conv_b[li][None, :],
            skip=hy_skip[li][None, :],
            sink_t=jnp.repeat(sink[li] * LOG2E, BLOCK).reshape(KV_HEADS, 1, GQA_GROUP * BLOCK),
            hy_gain=hy_gain[li][None, :], at_gain=at_gain[li][None, :],
            w_out=w_out[li].astype(BF16), norm2=norm2[li][None, :],
            w_gate_up=w_gate_up[li].astype(BF16), w_down=w_down[li].astype(BF16),
        )
        lw["kf"] = {key: _filter_spectrum(raw, st, {"prompt": x_prompt, "sample": x_sample}[key].shape[1])
                    for key, st in streams.items()}
        y_prompt = _layer(y_prompt, lw, shared, streams["prompt"])
        y_sample = _layer(y_sample, lw, shared, streams["sample"])
    return (y_prompt, y_sample)
```

```python
import functools
import math

import numpy as np
import jax
import jax.numpy as jnp
from jax import lax
from jax.experimental import pallas as pl
from jax.experimental.pallas import tpu as pltpu

F32 = jnp.float32
BF16 = jnp.bfloat16

D_MODEL = 1024
ATT_HEADS = 8
KV_HEADS = 2
HEAD_DIM = 64
GQA_GROUP = ATT_HEADS // KV_HEADS
ATT_W = ATT_HEADS * HEAD_DIM
KV_W = KV_HEADS * HEAD_DIM
WINDOW = 128
BLOCK = 128
N_BUCKETS = 32
MAX_DIST = 128
HY_CH = D_MODEL - ATT_W
FILTER_HIDDEN = 64
N_BANDS = 16
POS_DIM = 1 + 2 * N_BANDS
FAST_DECAY_PCT = 0.3
SLOW_DECAY_PCT = 1.5
DECAY_TARGET = 1e-2
IN_W = 3 * HY_CH + (ATT_HEADS + 2 * KV_HEADS) * HEAD_DIM
D_FF = -(-8 * D_MODEL // (3 * 256)) * 256
EPS = 1e-6
LOG2E = math.log2(math.e)

FFT_N2 = 64
LANES = 128
N2_TILE = 8
SLABS = HY_CH // LANES
STAGE_MB = 512
STAGE_KB = 16
STAGE_B_CHAINS = 8
INPROJ_TM = 512
FFN_TM = 512
VMEM_LIMIT = 56 * 1024 * 1024

_NT = (((1,), (1,)), ((), ()))


def _cparams(*sem):
    return pltpu.CompilerParams(dimension_semantics=sem, vmem_limit_bytes=VMEM_LIMIT)


def _const_spec(shape):
    nd = len(shape)
    return pl.BlockSpec(shape, lambda *_: (0,) * nd, pipeline_mode=pl.Buffered(1))


HALO = 16


def _inproj_kernel(x_ref, xp_ref, xn_ref, g1_ref, w_ref, cw_ref, cb_ref, bq_ref, bk_ref, qg_ref, kg_ref,
                   ux_ref, q_ref, k_ref, v_ref, he_ref, qkv_ref, *, nt, ntiles):
    i = pl.program_id(0)
    tm = x_ref.shape[0]

    @pl.when(i == 0)
    def _():
        he_ref[...] = jnp.zeros_like(he_ref)
        qkv_ref[...] = jnp.zeros_like(qkv_ref)

    v = qkv_ref[:, ATT_W + KV_W:]
    v_ref[0] = v.T.astype(BF16)

    k = qkv_ref[:, ATT_W:ATT_W + KV_W]
    ms = jnp.dot((k * k).astype(BF16), bk_ref[...], preferred_element_type=F32)
    kn = k * lax.rsqrt(ms + EPS) * kg_ref[...]
    lo_half = lax.broadcasted_iota(jnp.int32, kn.shape, 1) < HEAD_DIM
    kr = pltpu.roll(kn, HEAD_DIM, axis=1)
    k_ref[...] = jnp.concatenate([jnp.where(lo_half, kn, kr), jnp.where(lo_half, kr, kn)], axis=1).astype(BF16)

    q = qkv_ref[:, 0:ATT_W]
    ms = jnp.dot((q * q).astype(BF16), bq_ref[...], preferred_element_type=F32)
    q_ref[...] = (q * lax.rsqrt(ms + EPS) * qg_ref[...]).astype(BF16)

    rows = slice(HALO, HALO + tm)

    def conv(j):
        c = slice(j * HY_CH, (j + 1) * HY_CH)
        he = he_ref[:, c]
        return (pltpu.roll(he, 1, axis=0)[rows] * cw_ref[0:1, c] + he[rows] * cw_ref[1:2, c]
                + pltpu.roll(he, tm + 2 * HALO - 1, axis=0)[rows] * cw_ref[2:3, c] + cb_ref[0:1, c])

    ux_ref[...] = _pack_pair(conv(1) * conv(2), conv(0))

    ti = jnp.minimum(i, ntiles - 1)
    xp = jnp.where(ti % nt == 0, 0.0, xp_ref[...])
    xn = jnp.where(ti % nt == nt - 1, 0.0, xn_ref[...])
    xe = jnp.concatenate([xp, x_ref[...], xn], axis=0)
    xe = (xe * lax.rsqrt(jnp.mean(xe * xe, axis=-1, keepdims=True) + EPS) * g1_ref[...]).astype(BF16)
    he_ref[...] = jnp.dot(xe, w_ref[:, 0:3 * HY_CH], preferred_element_type=F32)
    qkv_ref[...] = jnp.dot(xe[HALO:HALO + tm], w_ref[:, 3 * HY_CH:], preferred_element_type=F32)


def _inproj(x2d, g1, w_in_b, conv_w, conv_b, bq, bk, qg, kg, tm, seq_len):
    t = x2d.shape[0]
    nt = seq_len // tm
    ntiles = t // tm
    nh = tm // HALO
    cur = lambda i: jnp.minimum(i, ntiles - 1)
    done = lambda i: jnp.maximum(i - 1, 0)
    row_in = pl.BlockSpec((tm, D_MODEL), lambda i: (cur(i), 0))
    prev = pl.BlockSpec((HALO, D_MODEL), lambda i: (jnp.maximum(cur(i) * nh - 1, 0), 0))
    nxt = pl.BlockSpec((HALO, D_MODEL), lambda i: (jnp.minimum((cur(i) + 1) * nh, t // HALO - 1), 0))
    row_out = lambda w: pl.BlockSpec((tm, w), lambda i: (done(i), 0))
    return pl.pallas_call(
        functools.partial(_inproj_kernel, nt=nt, ntiles=ntiles),
        grid=(ntiles + 1,),
        in_specs=[row_in, prev, nxt, _const_spec((1, D_MODEL)), _const_spec((D_MODEL, IN_W)),
                  _const_spec((3, 3 * HY_CH)), _const_spec((1, 3 * HY_CH)),
                  _const_spec((ATT_W, ATT_W)), _const_spec((KV_W, KV_W)),
                  _const_spec((1, ATT_W)), _const_spec((1, KV_W))],
        out_specs=[row_out(HY_CH), row_out(ATT_W), row_out(2 * KV_W),
                   pl.BlockSpec((1, KV_W, tm), lambda i: (done(i) // nt, 0, done(i) % nt))],
        out_shape=[jax.ShapeDtypeStruct((t, HY_CH), jnp.uint32),
                   jax.ShapeDtypeStruct((t, ATT_W), BF16), jax.ShapeDtypeStruct((t, 2 * KV_W), BF16),
                   jax.ShapeDtypeStruct((t // seq_len, KV_W, seq_len), BF16)],
        scratch_shapes=[pltpu.VMEM((tm + 2 * HALO, 3 * HY_CH), F32), pltpu.VMEM((tm, IN_W - 3 * HY_CH), F32)],
        compiler_params=_cparams("arbitrary"),
        name="inproj",
    )(x2d, x2d, x2d, g1, w_in_b, conv_w, conv_b, bq, bk, qg, kg)


def _stage_a_kernel(*refs, nz, a, mb, packed_in):
    x_refs, f_ref, o_ref = refs[:SLABS], refs[SLABS], refs[SLABS + 1]
    kblk = mb // 2
    xs = [r.reshape(nz * a * N2_TILE, LANES) for r in x_refs]
    o2 = o_ref.reshape(SLABS * kblk * N2_TILE, LANES)
    f = f_ref[...]

    def gather(s):
        x = jnp.concatenate(
            [jnp.concatenate([xs[c][pl.ds(z * a * N2_TILE + s, a, stride=N2_TILE), :] for z in range(nz)], axis=0)
             for c in range(SLABS)], axis=1)
        return (_unpack_pair(x)[0] if packed_in else x).astype(BF16)

    r = [jnp.dot(f, gather(s), preferred_element_type=F32) for s in range(N2_TILE)]
    for s in range(N2_TILE):
        packed = _pack_pair(r[s][:kblk], r[s][kblk:])
        for c in range(SLABS):
            o2[pl.ds(c * kblk * N2_TILE + s, kblk, stride=N2_TILE), :] = packed[:, c * LANES:(c + 1) * LANES]


def _pack_pair(re, im):
    rb = lax.bitcast_convert_type(re.astype(BF16).astype(F32), jnp.uint32)
    ib = lax.bitcast_convert_type(im.astype(BF16).astype(F32), jnp.uint32)
    return (rb >> 16) | ib


def _unpack_pair(p):
    re = lax.bitcast_convert_type(p << 16, F32)
    im = lax.bitcast_convert_type(p & jnp.uint32(0xFFFF0000), F32)
    return re, im


def _stage_a(x5, fmat, mb):
    nz, p, a, n2, _ = x5.shape
    m = fmat.shape[0]
    kblk = mb // 2
    xspec = lambda c: pl.BlockSpec((nz, 1, a, N2_TILE, LANES), lambda pi, j, mi: (0, pi, 0, j, c))
    return pl.pallas_call(
        functools.partial(_stage_a_kernel, nz=nz, a=a, mb=mb, packed_in=x5.dtype == jnp.uint32),
        grid=(p, n2 // N2_TILE, m // mb),
        in_specs=[xspec(c) for c in range(SLABS)] + [pl.BlockSpec((mb, nz * a), lambda pi, j, mi: (mi, 0))],
        out_specs=pl.BlockSpec((1, SLABS, kblk, N2_TILE, LANES), lambda pi, j, mi: (pi, 0, mi, j, 0)),
        out_shape=jax.ShapeDtypeStruct((p, SLABS, m // 2, n2, LANES), jnp.uint32),
        compiler_params=_cparams("parallel", "parallel", "parallel"),
        name="hy_stage_a",
    )(*([x5] * SLABS), fmat)


MXU_COLS = 256
SLABS_PER_DOT = MXU_COLS // LANES


def _slab_rows(a_ref, j, h):
    parts = [_unpack_pair(a_ref[0, h * SLABS_PER_DOT + c, j]) for c in range(SLABS_PER_DOT)]
    return jnp.concatenate([jnp.concatenate([re, im], axis=0) for re, im in parts], axis=1).astype(BF16)


def _spectral_mul(u, kr, ki, n2):
    ur, ui = u[:n2], u[n2:]
    return jnp.concatenate([ur * kr - ui * ki, ur * ki + ui * kr], axis=0).astype(BF16)


def _stage_b_kernel(a_ref, kf_ref, g_ref, gt_ref, *rest, herm):
    if herm:
        edge_ref, kfh_ref, o_ref = rest
    else:
        (o_ref,) = rest
    kb, n2 = a_ref.shape[2], a_ref.shape[3]

    def store(j, h, v):
        packed = _pack_pair(v[:n2], v[n2:])
        for c in range(SLABS_PER_DOT):
            o_ref[0, h * SLABS_PER_DOT + c, j] = packed[:, c * LANES:(c + 1) * LANES]

    dot = functools.partial(jnp.dot, preferred_element_type=F32)
    chains = [(j, h) for j in range(kb) for h in range(SLABS // SLABS_PER_DOT)]
    for c0 in range(0, len(chains), STAGE_B_CHAINS):
        batch = chains[c0:c0 + STAGE_B_CHAINS]
        fwd = []
        for j, h in batch:
            x = _slab_rows(a_ref, j, h)
            if herm and j == 0:
                fwd.append((dot(edge_ref[0, 0], x), dot(edge_ref[0, 1], x)))
            else:
                fwd.append((dot(g_ref[j], x),))
        prod = []
        for (j, h), us in zip(batch, fwd):
            lanes = slice(h * MXU_COLS, (h + 1) * MXU_COLS)
            ps = [_spectral_mul(us[0], kf_ref[0, j, :, lanes], kf_ref[1, j, :, lanes], n2)]
            if len(us) == 2:
                ps.append(_spectral_mul(us[1], kfh_ref[0, 0, :, lanes], kfh_ref[0, 1, :, lanes], n2))
            prod.append(ps)
        for (j, h), ps in zip(batch, prod):
            if len(ps) == 2:
                store(j, h, dot(edge_ref[0, 2], ps[0]) + dot(edge_ref[0, 3], ps[1]))
            else:
                store(j, h, dot(gt_ref[j], ps[0]))


def _stage_b(a5, kf, g, gt, kb, edge=None, kfh=None):
    p, _, k1n, n2, _ = a5.shape
    herm = edge is not None
    blk = pl.BlockSpec((1, SLABS, kb, n2, LANES), lambda i, pi: (pi, 0, i, 0, 0))
    gspec = pl.BlockSpec((kb, 2 * n2, 2 * n2), lambda i, pi: (i, 0, 0))
    extra = [pl.BlockSpec((1,) + edge.shape[1:], lambda i, pi: (i, 0, 0, 0)),
             pl.BlockSpec((1,) + kfh.shape[1:], lambda i, pi: (i, 0, 0, 0))] if herm else []
    return pl.pallas_call(
        functools.partial(_stage_b_kernel, herm=herm),
        grid=(k1n // kb, p),
        in_specs=[blk, pl.BlockSpec((2, kb, n2, HY_CH), lambda i, pi: (0, i, 0, 0)), gspec, gspec] + extra,
        out_specs=blk,
        out_shape=jax.ShapeDtypeStruct(a5.shape, jnp.uint32),
        compiler_params=_cparams("parallel", "parallel"),
        name="hy_stage_b",
    )(a5, kf, g, gt, *([edge, kfh] if herm else []))


def _stage_bf_kernel(af_ref, ab_ref, g_ref, *rest, scale, herm):
    if herm:
        edge_ref, o_ref, oh_ref = rest
    else:
        (o_ref,) = rest
    kb, n2 = af_ref.shape[2], af_ref.shape[3]

    def spectrum(mat, j, h):
        conj_mat = jnp.concatenate([mat[:n2], -mat[n2:]], axis=0)
        x = jnp.concatenate([_slab_rows(af_ref, j, h), _slab_rows(ab_ref, j, h)], axis=0)
        return jnp.dot(jnp.concatenate([mat, conj_mat], axis=1), x, preferred_element_type=F32) * scale

    for j in range(kb):
        for h in range(SLABS // SLABS_PER_DOT):
            lanes = slice(h * MXU_COLS, (h + 1) * MXU_COLS)
            if herm and j == 0:
                u = spectrum(edge_ref[0, 0], 0, h)
                uh = spectrum(edge_ref[0, 1], 0, h)
                oh_ref[0, 0, :, lanes] = uh[:n2]
                oh_ref[0, 1, :, lanes] = uh[n2:]
            else:
                u = spectrum(g_ref[j], j, h)
            o_ref[0, j, :, lanes] = u[:n2]
            o_ref[1, j, :, lanes] = u[n2:]


def _stage_bf(a5f, a5b, g, kb, scale, edge=None):
    _, _, k1n, n2, _ = a5f.shape
    herm = edge is not None
    aspec = pl.BlockSpec((1, SLABS, kb, n2, LANES), lambda i: (0, 0, i, 0, 0))
    out_specs = [pl.BlockSpec((2, kb, n2, HY_CH), lambda i: (0, i, 0, 0))]
    out_shape = [jax.ShapeDtypeStruct((2, k1n, n2, HY_CH), F32)]
    if herm:
        out_specs.append(pl.BlockSpec((1, 2, n2, HY_CH), lambda i: (i, 0, 0, 0)))
        out_shape.append(jax.ShapeDtypeStruct((k1n // kb, 2, n2, HY_CH), F32))
    return pl.pallas_call(
        functools.partial(_stage_bf_kernel, scale=scale, herm=herm),
        grid=(k1n // kb,),
        in_specs=[aspec, aspec, pl.BlockSpec((kb, 2 * n2, 2 * n2), lambda i: (i, 0, 0))]
        + ([pl.BlockSpec((1,) + edge.shape[1:], lambda i: (i, 0, 0, 0))] if herm else []),
        out_specs=out_specs,
        out_shape=out_shape,
        compiler_params=_cparams("parallel"),
        name="hy_filter_spectrum",
    )(a5f, a5b, g, *([edge] if herm else []))


def _stage_c_kernel(*refs, nz, a, mb):
    v_ref, f_ref = refs[0], refs[1]
    ux_refs = refs[2:2 + SLABS]
    skip_ref, o_ref, acc_ref = refs[2 + SLABS:]
    mk = pl.program_id(2)
    kblk = mb // 2
    v2 = v_ref.reshape(SLABS * kblk * N2_TILE, LANES)

    @pl.when(mk == 0)
    def _():
        acc_ref[...] = jnp.zeros_like(acc_ref)

    for s in range(N2_TILE):
        packed = jnp.concatenate([v2[pl.ds(c * kblk * N2_TILE + s, kblk, stride=N2_TILE), :] for c in range(SLABS)],
                                 axis=1)
        vs = jnp.concatenate(_unpack_pair(packed), axis=0).astype(BF16)
        acc_ref[s] += jnp.dot(f_ref[...], vs, preferred_element_type=F32)

    @pl.when(mk == pl.num_programs(2) - 1)
    def _():
        o2 = o_ref.reshape(SLABS * a * N2_TILE, LANES)
        ux2 = [r.reshape(nz * a * N2_TILE, LANES) for r in ux_refs]
        for s in range(N2_TILE):
            y = acc_ref[s]
            for c in range(SLABS):
                vals = []
                for z in range(nz):
                    uu, xx = _unpack_pair(ux2[c][pl.ds(z * a * N2_TILE + s, a, stride=N2_TILE), :])
                    yc = y[z * a:(z + 1) * a, c * LANES:(c + 1) * LANES]
                    vals.append(xx * (yc + skip_ref[0:1, c * LANES:(c + 1) * LANES] * uu))
                o2[pl.ds(c * a * N2_TILE + s, a, stride=N2_TILE), :] = _pack_pair(*vals) if nz == 2 else vals[0]


def _stage_c(v5, fmat, ux5, skip, mb):
    nz, p, a, n2, _ = ux5.shape
    m = 2 * v5.shape[2]
    xspec = lambda c: pl.BlockSpec((nz, 1, a, N2_TILE, LANES), lambda pi, j, mk: (0, pi, 0, j, c))
    return pl.pallas_call(
        functools.partial(_stage_c_kernel, nz=nz, a=a, mb=mb),
        grid=(p, n2 // N2_TILE, m // mb),
        in_specs=[pl.BlockSpec((1, SLABS, mb // 2, N2_TILE, LANES), lambda pi, j, mk: (pi, 0, mk, j, 0)),
                  pl.BlockSpec((nz * a, mb), lambda pi, j, mk: (0, mk))]
        + [xspec(c) for c in range(SLABS)] + [_const_spec((1, HY_CH))],
        out_specs=pl.BlockSpec((1, SLABS, a, N2_TILE, LANES), lambda pi, j, mk: (pi, 0, 0, j, 0)),
        out_shape=jax.ShapeDtypeStruct((p, SLABS, a, n2, LANES), jnp.uint32 if nz == 2 else F32),
        scratch_shapes=[pltpu.VMEM((N2_TILE, nz * a, HY_CH), F32)],
        compiler_params=_cparams("parallel", "parallel", "arbitrary"),
        name="hy_stage_c",
    )(v5, fmat, *([ux5] * SLABS), skip)


def _filter_kernel(zt_ref, tn_ref, w1t_ref, b1_ref, fr_ref, w2t_ref, b2_ref, w3_ref, dl_ref, of_ref, ob_ref):
    hi = lax.Precision.HIGHEST
    fr = fr_ref[...]
    h = jnp.sin(fr * (jnp.dot(w1t_ref[...], zt_ref[...], precision=hi, preferred_element_type=F32) + b1_ref[...]))
    h = jnp.sin(fr * (jnp.dot(w2t_ref[...], h, precision=hi, preferred_element_type=F32) + b2_ref[...]))
    taps = jnp.dot(h.T.astype(BF16), w3_ref[...], preferred_element_type=F32)
    tr = taps.shape[0]
    decay = jnp.exp(-tn_ref[...] * dl_ref[...])
    of_ref[...] = taps[:, :HY_CH] * decay
    t = pl.program_id(0) * tr + lax.broadcasted_iota(jnp.int32, (tr, HY_CH), 0)
    ob_ref[...] = jnp.where(t == 0, 0.0, taps[:, HY_CH:] * decay)


def _filter_taps(zfeat, f_w1, f_b1, f_freq, f_w2, f_b2, f_w3, absdelta, tr):
    n = zfeat.shape[0]
    cs = lambda a: _const_spec(a.shape)
    args = (zfeat.T, zfeat[:, 0:1], f_w1.T, f_b1.T, f_freq.T, f_w2.T, f_b2.T, f_w3.astype(BF16), absdelta)
    out = pl.BlockSpec((tr, HY_CH), lambda i: (i, 0))
    return pl.pallas_call(
        _filter_kernel,
        grid=(n // tr,),
        in_specs=[pl.BlockSpec((POS_DIM, tr), lambda i: (0, i)), pl.BlockSpec((tr, 1), lambda i: (i, 0))]
        + [cs(a) for a in args[2:]],
        out_specs=[out, out],
        out_shape=[jax.ShapeDtypeStruct((n, HY_CH), F32)] * 2,
        compiler_params=_cparams("parallel"),
        name="hy_filter_taps",
    )(*args)


def _bias_kernel(rb_ref, oh_ref, o_ref):
    o_ref[...] = jnp.dot(rb_ref[...], oh_ref[...], precision=lax.Precision.HIGHEST,
                         preferred_element_type=F32)


def _bias_table(rel_bias):
    i = jnp.arange(BLOCK)[:, None]
    j = jnp.arange(3 * BLOCK)[None, :]
    rel = j - BLOCK - i
    nb2 = N_BUCKETS // 2
    max_exact = nb2 // 2
    n = jnp.abs(rel)
    large = max_exact + (jnp.log(jnp.maximum(n, 1).astype(F32) / max_exact)
                         / math.log(MAX_DIST / max_exact) * (nb2 - max_exact)).astype(jnp.int32)
    large = jnp.minimum(large, nb2 - 1)
    bucket = jnp.where(rel > 0, nb2, 0) + jnp.where(n < max_exact, n, large)
    onehot = (bucket.reshape(1, -1) == jnp.arange(N_BUCKETS)[:, None]).astype(F32)
    cols = onehot.shape[1]
    tc = cols // 4
    table = pl.pallas_call(
        _bias_kernel,
        grid=(4,),
        in_specs=[_const_spec((ATT_HEADS, N_BUCKETS)), pl.BlockSpec((N_BUCKETS, tc), lambda c: (0, c))],
        out_specs=pl.BlockSpec((ATT_HEADS, tc), lambda c: (0, c)),
        out_shape=jax.ShapeDtypeStruct((ATT_HEADS, cols), F32),
        compiler_params=_cparams("parallel"),
        name="att_bias_table",
    )(rel_bias.T, onehot)
    table = table.reshape(ATT_HEADS, BLOCK, 3 * BLOCK)
    table = jnp.where((n <= WINDOW)[None], table, -jnp.inf)
    table = table.reshape(KV_HEADS, GQA_GROUP, BLOCK, 3 * BLOCK)
    return table.transpose(0, 3, 1, 2).reshape(KV_HEADS, 3 * BLOCK, GQA_GROUP * BLOCK)


def _attn_kernel(q_ref, km_ref, kp_ref, kn_ref, vm_ref, vp_ref, vn_ref, bias_ref, sink_ref, o_ref, *, nsub):
    i = pl.program_id(1)
    neg = -jnp.inf
    pen_first = jnp.where(i == 0, neg, 0.0)
    pen_last = jnp.where(i == pl.num_programs(1) - 1, neg, 0.0)
    kwin = jnp.concatenate([kp_ref[0], km_ref[0], kn_ref[0]], axis=0)
    vwin = jnp.concatenate([vp_ref[0], vm_ref[0], vn_ref[0]], axis=1)
    lane = lax.broadcasted_iota(jnp.int32, (BLOCK, LANES), 1)
    lo_half = lane < HEAD_DIM
    ones = jnp.ones((SUM_ROWS, 3 * BLOCK), BF16)
    units = [(s, g) for s in range(nsub) for g in range(KV_HEADS)]

    def scores(s, g):
        qs = q_ref[0, s * BLOCK:(s + 1) * BLOCK, :]
        rows = []
        for pr in range(2):
            qp = qs[:, (2 * g + pr) * LANES:(2 * g + pr + 1) * LANES]
            rows.append(jnp.where(lo_half, qp, jnp.zeros_like(qp)))
            rows.append(jnp.where(lo_half, jnp.zeros_like(qp), qp))
        qg = jnp.concatenate(rows, axis=0)
        kg = kwin[s * BLOCK:(s + 3) * BLOCK, g * LANES:(g + 1) * LANES]
        t = lax.dot_general(kg, qg, _NT, preferred_element_type=F32) + bias_ref[g]
        if s == 0:
            t = jnp.concatenate([t[:BLOCK] + pen_first, t[BLOCK:]], axis=0)
        if s == nsub - 1:
            t = jnp.concatenate([t[:2 * BLOCK], t[2 * BLOCK:] + pen_last], axis=0)
        return t

    def pv(s, g, p):
        vg = jnp.concatenate([vwin[g * HEAD_DIM:(g + 1) * HEAD_DIM, s * BLOCK:(s + 3) * BLOCK], ones],
                             axis=0)
        return jnp.dot(vg, p, preferred_element_type=F32)

    for u0 in range(0, len(units), ATT_UNITS):
        batch = units[u0:u0 + ATT_UNITS]
        sc = [scores(s, g) for s, g in batch]
        m = [jnp.maximum(jnp.max(t, axis=0, keepdims=True), sink_ref[g]) for t, (_, g) in zip(sc, batch)]
        p = [jnp.exp2(t - mm).astype(BF16) for t, mm in zip(sc, m)]
        o = [pv(s, g, pp) for (s, g), pp in zip(batch, p)]
        for (s, g), oo, mm in zip(batch, o, m):
            on = oo[:HEAD_DIM] / (oo[HEAD_DIM:HEAD_DIM + 1] + jnp.exp2(sink_ref[g] - mm))
            for pr in range(2):
                pair = jnp.concatenate([on[:, (2 * pr) * BLOCK:(2 * pr + 1) * BLOCK],
                                        on[:, (2 * pr + 1) * BLOCK:(2 * pr + 2) * BLOCK]], axis=0)
                o_ref[0, s * BLOCK:(s + 1) * BLOCK, (2 * g + pr) * LANES:(2 * g + pr + 1) * LANES] = (
                    pair.T.astype(o_ref.dtype))


SUM_ROWS = 16
ATT_UNITS = 8
ATT_TQ = 1024


def _attention(q, kd, vt, bias_t, sink_t, tq):
    b, l, _ = q.shape
    nsub = tq // BLOCK
    nblk = l // BLOCK
    kw = 2 * KV_W
    main = lambda w: pl.BlockSpec((1, tq, w), lambda bi, i: (bi, i, 0))
    prev = pl.BlockSpec((1, BLOCK, kw), lambda bi, i: (bi, jnp.maximum(i * nsub - 1, 0), 0))
    nxt = pl.BlockSpec((1, BLOCK, kw), lambda bi, i: (bi, jnp.minimum((i + 1) * nsub, nblk - 1), 0))
    vmain = pl.BlockSpec((1, KV_W, tq), lambda bi, i: (bi, 0, i))
    vprev = pl.BlockSpec((1, KV_W, BLOCK), lambda bi, i: (bi, 0, jnp.maximum(i * nsub - 1, 0)))
    vnxt = pl.BlockSpec((1, KV_W, BLOCK), lambda bi, i: (bi, 0, jnp.minimum((i + 1) * nsub, nblk - 1)))
    return pl.pallas_call(
        functools.partial(_attn_kernel, nsub=nsub),
        grid=(b, l // tq),
        in_specs=[main(ATT_W), main(kw), prev, nxt, vmain, vprev, vnxt,
                  _const_spec(bias_t.shape), _const_spec(sink_t.shape)],
        out_specs=main(ATT_W),
        out_shape=jax.ShapeDtypeStruct((b, l, ATT_W), BF16),
        compiler_params=_cparams("parallel", "parallel"),
        name="window_attn",
    )(q, kd, kd, kd, vt, vt, vt, bias_t, sink_t)


FF_CHUNK = 256


def _ffn_kernel(x_ref, yh_ref, ya_ref, hg_ref, ag_ref, wo_ref, g2_ref, wgu_ref, wd_ref, o_ref, act_ref, *, pairs, nt):
    def rms(t, g):
        return (t * lax.rsqrt(jnp.mean(t * t, axis=-1, keepdims=True) + EPS) * g).astype(BF16)

    yh = jnp.concatenate([yh_ref[0, c] for c in range(SLABS)], axis=1)
    if pairs:
        lo, hi = _unpack_pair(yh)
        yh = jnp.where(pl.program_id(0) // nt < pairs, lo, hi)
    mixed = jnp.concatenate([rms(yh, hg_ref[...]), rms(ya_ref[...].astype(F32), ag_ref[...])], axis=1)
    h = x_ref[...] + jnp.dot(mixed, wo_ref[...], preferred_element_type=F32)
    hn = rms(h, g2_ref[...])
    for c in range(D_FF // FF_CHUNK):
        lo = c * FF_CHUNK
        gate = jnp.dot(hn, wgu_ref[:, lo:lo + FF_CHUNK], preferred_element_type=F32)
        up = jnp.dot(hn, wgu_ref[:, D_FF + lo:D_FF + lo + FF_CHUNK], preferred_element_type=F32)
        act_ref[:, lo:lo + FF_CHUNK] = (gate / (1.0 + jnp.exp(-gate)) * up).astype(BF16)
    o_ref[...] = h + jnp.dot(act_ref[...], wd_ref[...], preferred_element_type=F32)


def _ffn(x2d, yh, ya, hg, ag, wo_b, g2, wgu_b, wd_b, tm):
    t = x2d.shape[0]
    nt = yh.shape[2] // tm
    pairs = yh.shape[0] if yh.dtype == jnp.uint32 else 0
    row = lambda w: pl.BlockSpec((tm, w), lambda i: (i, 0))
    yspec = pl.BlockSpec((1, SLABS, tm, LANES), lambda i: ((i // nt) % yh.shape[0], 0, i % nt, 0))
    return pl.pallas_call(
        functools.partial(_ffn_kernel, pairs=pairs, nt=nt),
        grid=(t // tm,),
        in_specs=[row(D_MODEL), yspec, row(ATT_W), _const_spec((1, HY_CH)), _const_spec((1, ATT_W)),
                  _const_spec((D_MODEL, D_MODEL)), _const_spec((1, D_MODEL)),
                  _const_spec((D_MODEL, 2 * D_FF)), _const_spec((D_FF, D_MODEL))],
        out_specs=row(D_MODEL),
        out_shape=jax.ShapeDtypeStruct((t, D_MODEL), F32),
        scratch_shapes=[pltpu.VMEM((tm, D_FF), BF16)],
        compiler_params=_cparams("parallel"),
        name="outproj_swiglu",
    )(x2d, yh, ya, hg, ag, wo_b, g2, wgu_b, wd_b)


def _dft_tables(seq_len, paired, kblk):
    n = 2 * seq_len
    n1 = n // FFT_N2
    a = n1 // 2
    kk = np.arange(n1)[:, None]
    th_half = 2.0 * np.pi * ((kk * np.arange(a)[None, :]) % n1) / n1
    c, s = np.cos(th_half), np.sin(th_half)

    def blocked(re, im):
        return np.concatenate([np.concatenate([re[i:i + kblk], im[i:i + kblk]], axis=0)
                               for i in range(0, re.shape[0], kblk)], axis=0)

    def real_input_rows(cc, ss):
        im = -ss[:a].copy()
        im[0] = cc[a]
        return cc[:a], im

    if paired:
        fa = blocked(np.concatenate([c, s], axis=1), np.concatenate([-s, c], axis=1))
        fc = fa.T
        faf = blocked(c, -s)
    else:
        fa = blocked(*real_input_rows(c, s))
        re_cols = np.concatenate([c[:1], 2.0 * c[1:a]], axis=0)
        im_cols = np.concatenate([c[a:a + 1], -2.0 * s[1:a]], axis=0)
        fc = blocked(re_cols, im_cols).T
        faf = fa
    to = lambda m: jnp.asarray(m.astype(np.float32)).astype(BF16)
    return to(fa), to(fc), to(faf)


def _edge_matrices(g, half, kb):
    zero = jnp.zeros_like(g[0])
    z = zero[:, :FFT_N2]
    col = lambda m: m[:, :FFT_N2]
    sa = jnp.concatenate([col(g[0]), z], axis=1)
    sb = jnp.concatenate([z, col(g[half])], axis=1)
    blocks = [jnp.stack([sa, sb, sa.T, sb.T])]
    blocks += [jnp.stack([g[i], zero, g[i].T, zero]) for i in range(kb, half, kb)]
    return jnp.stack(blocks)


def _twiddle_tables(seq_len, k1_count):
    n = 2 * seq_len
    n1 = n // FFT_N2
    k1 = np.arange(k1_count, dtype=np.int64)[:, None, None]
    k2 = np.arange(FFT_N2, dtype=np.int64)[None, :, None]
    n2 = np.arange(FFT_N2, dtype=np.int64)[None, None, :]
    ang = ((n2 * (k2 * n1 + k1)) % n) * (-2.0 * np.pi / n)
    gr, gi = np.cos(ang), np.sin(ang)
    g = np.concatenate([np.concatenate([gr, -gi], axis=2), np.concatenate([gi, gr], axis=2)], axis=1)
    g = g.astype(np.float32)
    return jnp.asarray(g.astype(BF16)), jnp.asarray(np.swapaxes(g, 1, 2).astype(BF16))


def _filter_features(seq_len):
    t_idx = jnp.arange(seq_len, dtype=F32)[:, None]
    t_norm = t_idx / max(seq_len - 1, 1)
    bands = jnp.linspace(1e-4, N_BANDS - 1, N_BANDS, dtype=F32)
    w = (2.0 * math.pi) * t_idx * bands[None, :] / seq_len
    return jnp.concatenate([t_norm, jnp.cos(w), -jnp.sin(w)], axis=-1)


def _abs_deltas():
    min_decay = math.log(DECAY_TARGET) / FAST_DECAY_PCT
    max_decay = math.log(DECAY_TARGET) / SLOW_DECAY_PCT
    return jnp.abs(jnp.linspace(min_decay, max_decay, HY_CH, dtype=F32))[None, :]


def _pick(n, target):
    t = min(n, target)
    while n % t:
        t //= 2
    return t


def _layer(x, lw, shared, st):
    b, l, _ = x.shape
    t = b * l
    x2d = x.reshape(t, D_MODEL)
    tm = _pick(l, FFN_TM)
    ux, q, kd, vt = _inproj(x2d, lw["norm1"], lw["w_in"], lw["conv_w"], lw["conv_b"],
                            shared["bq"], shared["bk"], lw["qg"], lw["kg"], _pick(l, INPROJ_TM), l)
    r3 = lambda a: a.reshape(b, l, a.shape[-1])

    n1 = 2 * l // FFT_N2
    a = n1 // 2
    nz = st["nz"]
    p = b // nz
    ux5 = ux.reshape(nz, p, a, FFT_N2, HY_CH)
    mb = _pick(st["fa"].shape[0], STAGE_MB)
    ah = _stage_a(ux5, st["fa"], mb)
    kf, kfh = lw["kf"][st["key"]]
    vv = _stage_b(ah, kf, st["g"], st["gt"], _pick(ah.shape[2], STAGE_KB), st["edge"], kfh)
    yh = _stage_c(vv, st["fc"], ux5, lw["skip"], mb)
    yh = yh.reshape(p, SLABS, l, LANES)

    ya = _attention(r3(q), r3(kd), vt, shared["bias"], lw["sink_t"], _pick(l, ATT_TQ))
    ya = ya.reshape(t, ATT_W)

    out = _ffn(x2d, yh, ya, lw["hy_gain"], lw["at_gain"], lw["w_out"], lw["norm2"],
               lw["w_gate_up"], lw["w_down"], tm)
    return out.reshape(b, l, D_MODEL)


def _filter_spectrum(lw_raw, st, seq_len):
    n = 2 * seq_len
    a = seq_len // FFT_N2
    taps_f, taps_b = _filter_taps(st["zfeat"], lw_raw["f_w1"], lw_raw["f_b1"], lw_raw["f_freq"], lw_raw["f_w2"],
                                  lw_raw["f_b2"], lw_raw["f_w3"], st["absdelta"], _pick(seq_len, 512))
    mb = _pick(st["faf"].shape[0], STAGE_MB)
    ahf = _stage_a(taps_f.reshape(1, 1, a, FFT_N2, HY_CH), st["faf"], mb)
    ahb = _stage_a(taps_b.reshape(1, 1, a, FFT_N2, HY_CH), st["faf"], mb)
    out = _stage_bf(ahf, ahb, st["g"], _pick(ahf.shape[2], STAGE_KB), 1.0 / n, st["edge"])
    return (out[0], out[1]) if st["edge"] is not None else (out[0], None)


def kernel(x_prompt, x_sample, norm1, w_in, conv_w, conv_b, f_w1, f_b1, f_freq, f_w2, f_b2, f_w3,
           hy_skip, q_gain, k_gain, sink, rel_bias, hy_gain, at_gain, w_out, norm2, w_gate_up, w_down):
    depth = norm1.shape[0]
    streams = {}
    for key, x in (("prompt", x_prompt), ("sample", x_sample)):
        b, l, _ = x.shape
        paired = b % 2 == 0
        n1 = 2 * l // FFT_N2
        fa, fc, faf = _dft_tables(l, paired, _pick(2 * n1 if paired else n1, STAGE_MB) // 2)
        g, gt = _twiddle_tables(l, n1 if paired else n1 // 2 + 1)
        streams[key] = dict(key=key, nz=2 if paired else 1, fa=fa, fc=fc, faf=faf, g=g, gt=gt,
                            edge=None if paired else _edge_matrices(g, n1 // 2, _pick(n1 // 2, STAGE_KB)),
                            zfeat=_filter_features(l), absdelta=_abs_deltas())

    head_id = np.arange(ATT_W) // HEAD_DIM
    shared = dict(
        bq=jnp.asarray((head_id[:, None] == head_id[None, :]).astype(np.float32) / HEAD_DIM).astype(BF16),
        bk=jnp.asarray((head_id[:KV_W, None] == head_id[None, :KV_W]).astype(np.float32) / HEAD_DIM).astype(BF16),
        bias=_bias_table(rel_bias) * LOG2E,
    )

    y_prompt, y_sample = x_prompt, x_sample
    for li in range(depth):
        raw = dict(f_w1=f_w1[li], f_b1=f_b1[li][None, :], f_freq=f_freq[li][None, :], f_w2=f_w2[li],
                   f_b2=f_b2[li][None, :], f_w3=f_w3[li])
        lw = dict(
            norm1=norm1[li][None, :], w_in=w_in[li].astype(BF16),
            qg=jnp.tile(q_gain[li], ATT_HEADS)[None, :] * (HEAD_DIM ** -0.5 * LOG2E),
            kg=jnp.tile(k_gain[li], KV_HEADS)[None, :],
            conv_w=conv_w[li], conv_b=conv_b[li][None, :],
            skip=hy_skip[li][None, :],
            sink_t=jnp.repeat(sink[li] * LOG2E, BLOCK).reshape(KV_HEADS, 1, GQA_GROUP * BLOCK),
            hy_gain=hy_gain[li][None, :], at_gain=at_gain[li][None, :],
            w_out=w_out[li].astype(BF16), norm2=norm2[li][None, :],
            w_gate_up=w_gate_up[li].astype(BF16), w_down=w_down[li].astype(BF16),
        )
        lw["kf"] = {key: _filter_spectrum(raw, st, {"prompt": x_prompt, "sample": x_sample}[key].shape[1])
                    for key, st in streams.items()}
        y_prompt = _layer(y_prompt, lw, shared, streams["prompt"])
        y_sample = _layer(y_sample, lw, shared, streams["sample"])
    return (y_prompt, y_sample)
```

```python
import functools
import math

import numpy as np
import jax
import jax.numpy as jnp
from jax import lax
from jax.experimental import pallas as pl
from jax.experimental.pallas import tpu as pltpu

F32 = jnp.float32
BF16 = jnp.bfloat16

D_MODEL = 1024
ATT_HEADS = 8
KV_HEADS = 2
HEAD_DIM = 64
GQA_GROUP = ATT_HEADS // KV_HEADS
ATT_W = ATT_HEADS * HEAD_DIM
KV_W = KV_HEADS * HEAD_DIM
WINDOW = 128
BLOCK = 128
N_BUCKETS = 32
MAX_DIST = 128
HY_CH = D_MODEL - ATT_W
FILTER_HIDDEN = 64
N_BANDS = 16
POS_DIM = 1 + 2 * N_BANDS
FAST_DECAY_PCT = 0.3
SLOW_DECAY_PCT = 1.5
DECAY_TARGET = 1e-2
IN_W = 3 * HY_CH + (ATT_HEADS + 2 * KV_HEADS) * HEAD_DIM
D_FF = -(-8 * D_MODEL // (3 * 256)) * 256
EPS = 1e-6
LOG2E = math.log2(math.e)

FFT_N2 = 64
LANES = 128
N2_TILE = 8
SLABS = HY_CH // LANES
STAGE_MB = 512
STAGE_KB = 16
PAIR_KRON_ROWS = 1024
STAGE_B_CHAINS = 8
INPROJ_TM = 512
FFN_TM = 512
VMEM_LIMIT = 56 * 1024 * 1024

_NT = (((1,), (1,)), ((), ()))


def _cparams(*sem):
    return pltpu.CompilerParams(dimension_semantics=sem, vmem_limit_bytes=VMEM_LIMIT)


def _const_spec(shape):
    nd = len(shape)
    return pl.BlockSpec(shape, lambda *_: (0,) * nd, pipeline_mode=pl.Buffered(1))


HALO = 16


def _inproj_kernel(x_ref, xp_ref, xn_ref, g1_ref, w_ref, cw_ref, cb_ref, bq_ref, bk_ref, qg_ref, kg_ref,
                   ux_ref, q_ref, k_ref, v_ref, he_ref, qkv_ref, *, nt, ntiles):
    i = pl.program_id(0)
    tm = x_ref.shape[0]

    @pl.when(i == 0)
    def _():
        he_ref[...] = jnp.zeros_like(he_ref)
        qkv_ref[...] = jnp.zeros_like(qkv_ref)

    v = qkv_ref[:, ATT_W + KV_W:]
    v_ref[0] = v.T.astype(BF16)

    k = qkv_ref[:, ATT_W:ATT_W + KV_W]
    ms = jnp.dot((k * k).astype(BF16), bk_ref[...], preferred_element_type=F32)
    kn = k * lax.rsqrt(ms + EPS) * kg_ref[...]
    lo_half = lax.broadcasted_iota(jnp.int32, kn.shape, 1) < HEAD_DIM
    kr = pltpu.roll(kn, HEAD_DIM, axis=1)
    k_ref[...] = jnp.concatenate([jnp.where(lo_half, kn, kr), jnp.where(lo_half, kr, kn)], axis=1).astype(BF16)

    q = qkv_ref[:, 0:ATT_W]
    ms = jnp.dot((q * q).astype(BF16), bq_ref[...], preferred_element_type=F32)
    q_ref[...] = (q * lax.rsqrt(ms + EPS) * qg_ref[...]).astype(BF16)

    rows = slice(HALO, HALO + tm)

    def conv(j):
        c = slice(j * HY_CH, (j + 1) * HY_CH)
        he = he_ref[:, c]
        return (pltpu.roll(he, 1, axis=0)[rows] * cw_ref[0:1, c] + he[rows] * cw_ref[1:2, c]
                + pltpu.roll(he, tm + 2 * HALO - 1, axis=0)[rows] * cw_ref[2:3, c] + cb_ref[0:1, c])

    ux_ref[...] = _pack_pair(conv(1) * conv(2), conv(0))

    ti = jnp.minimum(i, ntiles - 1)
    xp = jnp.where(ti % nt == 0, 0.0, xp_ref[...])
    xn = jnp.where(ti % nt == nt - 1, 0.0, xn_ref[...])
    xe = jnp.concatenate([xp, x_ref[...], xn], axis=0)
    xe = (xe * lax.rsqrt(jnp.mean(xe * xe, axis=-1, keepdims=True) + EPS) * g1_ref[...]).astype(BF16)
    he_ref[...] = jnp.dot(xe, w_ref[:, 0:3 * HY_CH], preferred_element_type=F32)
    qkv_ref[...] = jnp.dot(xe[HALO:HALO + tm], w_ref[:, 3 * HY_CH:], preferred_element_type=F32)


def _inproj(x2d, g1, w_in_b, conv_w, conv_b, bq, bk, qg, kg, tm, seq_len):
    t = x2d.shape[0]
    nt = seq_len // tm
    ntiles = t // tm
    nh = tm // HALO
    cur = lambda i: jnp.minimum(i, ntiles - 1)
    done = lambda i: jnp.maximum(i - 1, 0)
    row_in = pl.BlockSpec((tm, D_MODEL), lambda i: (cur(i), 0))
    prev = pl.BlockSpec((HALO, D_MODEL), lambda i: (jnp.maximum(cur(i) * nh - 1, 0), 0))
    nxt = pl.BlockSpec((HALO, D_MODEL), lambda i: (jnp.minimum((cur(i) + 1) * nh, t // HALO - 1), 0))
    row_out = lambda w: pl.BlockSpec((tm, w), lambda i: (done(i), 0))
    return pl.pallas_call(
        functools.partial(_inproj_kernel, nt=nt, ntiles=ntiles),
        grid=(ntiles + 1,),
        in_specs=[row_in, prev, nxt, _const_spec((1, D_MODEL)), _const_spec((D_MODEL, IN_W)),
                  _const_spec((3, 3 * HY_CH)), _const_spec((1, 3 * HY_CH)),
                  _const_spec((ATT_W, ATT_W)), _const_spec((KV_W, KV_W)),
                  _const_spec((1, ATT_W)), _const_spec((1, KV_W))],
        out_specs=[row_out(HY_CH), row_out(ATT_W), row_out(2 * KV_W),
                   pl.BlockSpec((1, KV_W, tm), lambda i: (done(i) // nt, 0, done(i) % nt))],
        out_shape=[jax.ShapeDtypeStruct((t, HY_CH), jnp.uint32),
                   jax.ShapeDtypeStruct((t, ATT_W), BF16), jax.ShapeDtypeStruct((t, 2 * KV_W), BF16),
                   jax.ShapeDtypeStruct((t // seq_len, KV_W, seq_len), BF16)],
        scratch_shapes=[pltpu.VMEM((tm + 2 * HALO, 3 * HY_CH), F32), pltpu.VMEM((tm, IN_W - 3 * HY_CH), F32)],
        compiler_params=_cparams("arbitrary"),
        name="inproj",
    )(x2d, x2d, x2d, g1, w_in_b, conv_w, conv_b, bq, bk, qg, kg)


def _stage_a_kernel(*refs, nz, a, mb, packed_in):
    x_refs, f_ref, o_ref = refs[:SLABS], refs[SLABS], refs[SLABS + 1]
    kblk = mb // 2
    xs = [r.reshape(nz * a * N2_TILE, LANES) for r in x_refs]
    o2 = o_ref.reshape(SLABS * kblk * N2_TILE, LANES)
    f = f_ref[...]

    def gather(s):
        x = jnp.concatenate(
            [jnp.concatenate([xs[c][pl.ds(z * a * N2_TILE + s, a, stride=N2_TILE), :] for z in range(nz)], axis=0)
             for c in range(SLABS)], axis=1)
        return (_unpack_pair(x)[0] if packed_in else x).astype(BF16)

    r = [jnp.dot(f, gather(s), preferred_element_type=F32) for s in range(N2_TILE)]
    for s in range(N2_TILE):
        packed = _pack_pair(r[s][:kblk], r[s][kblk:])
        for c in range(SLABS):
            o2[pl.ds(c * kblk * N2_TILE + s, kblk, stride=N2_TILE), :] = packed[:, c * LANES:(c + 1) * LANES]


def _pack_pair(re, im):
    rb = lax.bitcast_convert_type(re.astype(BF16).astype(F32), jnp.uint32)
    ib = lax.bitcast_convert_type(im.astype(BF16).astype(F32), jnp.uint32)
    return (rb >> 16) | ib


def _unpack_pair(p):
    re = lax.bitcast_convert_type(p << 16, F32)
    im = lax.bitcast_convert_type(p & jnp.uint32(0xFFFF0000), F32)
    return re, im


def _stage_a(x5, fmat, mb):
    nz, p, a, n2, _ = x5.shape
    m = fmat.shape[0]
    kblk = mb // 2
    xspec = lambda c: pl.BlockSpec((nz, 1, a, N2_TILE, LANES), lambda pi, j, mi: (0, pi, 0, j, c))
    return pl.pallas_call(
        functools.partial(_stage_a_kernel, nz=nz, a=a, mb=mb, packed_in=x5.dtype == jnp.uint32),
        grid=(p, n2 // N2_TILE, m // mb),
        in_specs=[xspec(c) for c in range(SLABS)] + [pl.BlockSpec((mb, nz * a), lambda pi, j, mi: (mi, 0))],
        out_specs=pl.BlockSpec((1, SLABS, kblk, N2_TILE, LANES), lambda pi, j, mi: (pi, 0, mi, j, 0)),
        out_shape=jax.ShapeDtypeStruct((p, SLABS, m // 2, n2, LANES), jnp.uint32),
        compiler_params=_cparams("parallel", "parallel", "parallel"),
        name="hy_stage_a",
    )(*([x5] * SLABS), fmat)


def _pair_a_kernel(ux_ref, f_ref, o_ref):
    rows = ux_ref.shape[0] * ux_ref.shape[2] * N2_TILE
    u = _unpack_pair(ux_ref.reshape(rows, HY_CH)[...])[0].astype(BF16)
    r = jnp.dot(f_ref[...], u, preferred_element_type=F32)
    half = r.shape[0] // 2
    o_ref.reshape(half, HY_CH)[...] = _pack_pair(r[:half], r[half:])


def _pair_a(ux5, fx):
    nz, p, a, n2, c = ux5.shape
    n1 = fx.shape[0] // (2 * N2_TILE)
    return pl.pallas_call(
        _pair_a_kernel,
        grid=(p, n2 // N2_TILE),
        in_specs=[pl.BlockSpec((nz, 1, a, N2_TILE, c), lambda pi, j: (0, pi, 0, j, 0)), _const_spec(fx.shape)],
        out_specs=pl.BlockSpec((1, n1, N2_TILE, c), lambda pi, j: (pi, 0, j, 0)),
        out_shape=jax.ShapeDtypeStruct((p, n1, n2, c), jnp.uint32),
        compiler_params=_cparams("parallel", "parallel"),
        name="hy_stage_a",
    )(ux5, fx)


def _pair_c_kernel(v_ref, f_ref, ux_ref, skip_ref, o_ref):
    vr, vi = _unpack_pair(v_ref.reshape(v_ref.shape[1] * N2_TILE, HY_CH)[...])
    y = jnp.dot(f_ref[...], jnp.concatenate([vr, vi], axis=0).astype(BF16), preferred_element_type=F32)
    half = y.shape[0] // 2
    uu, xx = _unpack_pair(ux_ref.reshape(2 * half, HY_CH)[...])
    val = xx * (y + skip_ref[...] * uu)
    o_ref.reshape(half, HY_CH)[...] = _pack_pair(val[:half], val[half:])


def _pair_c(v4, fx, ux5, skip):
    nz, p, a, n2, c = ux5.shape
    n1 = v4.shape[1]
    return pl.pallas_call(
        _pair_c_kernel,
        grid=(p, n2 // N2_TILE),
        in_specs=[pl.BlockSpec((1, n1, N2_TILE, c), lambda pi, j: (pi, 0, j, 0)), _const_spec(fx.shape),
                  pl.BlockSpec((nz, 1, a, N2_TILE, c), lambda pi, j: (0, pi, 0, j, 0)), _const_spec((1, c))],
        out_specs=pl.BlockSpec((1, a, N2_TILE, c), lambda pi, j: (pi, 0, j, 0)),
        out_shape=jax.ShapeDtypeStruct((p, a, n2, c), jnp.uint32),
        compiler_params=_cparams("parallel", "parallel"),
        name="hy_stage_c",
    )(v4, fx, ux5, skip)


MXU_COLS = 256
SLABS_PER_DOT = MXU_COLS // LANES


def _slab_rows(a_ref, j, h):
    if len(a_ref.shape) == 4:
        re, im = _unpack_pair(a_ref[0, j, :, h * MXU_COLS:(h + 1) * MXU_COLS])
        return jnp.concatenate([re, im], axis=0).astype(BF16)
    parts = [_unpack_pair(a_ref[0, h * SLABS_PER_DOT + c, j]) for c in range(SLABS_PER_DOT)]
    return jnp.concatenate([jnp.concatenate([re, im], axis=0) for re, im in parts], axis=1).astype(BF16)


def _spectral_mul(u, kr, ki, n2):
    ur, ui = u[:n2], u[n2:]
    return jnp.concatenate([ur * kr - ui * ki, ur * ki + ui * kr], axis=0).astype(BF16)


def _stage_b_kernel(a_ref, kf_ref, g_ref, gt_ref, *rest, herm):
    if herm:
        edge_ref, kfh_ref, o_ref = rest
    else:
        (o_ref,) = rest
    kb, n2 = a_ref.shape[-3], a_ref.shape[-2]

    def store(j, h, v):
        packed = _pack_pair(v[:n2], v[n2:])
        if len(o_ref.shape) == 4:
            o_ref[0, j, :, h * MXU_COLS:(h + 1) * MXU_COLS] = packed
        else:
            for c in range(SLABS_PER_DOT):
                o_ref[0, h * SLABS_PER_DOT + c, j] = packed[:, c * LANES:(c + 1) * LANES]

    dot = functools.partial(jnp.dot, preferred_element_type=F32)
    chains = [(j, h) for j in range(kb) for h in range(SLABS // SLABS_PER_DOT)]
    for c0 in range(0, len(chains), STAGE_B_CHAINS):
        batch = chains[c0:c0 + STAGE_B_CHAINS]
        fwd = []
        for j, h in batch:
            x = _slab_rows(a_ref, j, h)
            if herm and j == 0:
                fwd.append((dot(edge_ref[0, 0], x), dot(edge_ref[0, 1], x)))
            else:
                fwd.append((dot(g_ref[j], x),))
        prod = []
        for (j, h), us in zip(batch, fwd):
            lanes = slice(h * MXU_COLS, (h + 1) * MXU_COLS)
            ps = [_spectral_mul(us[0], kf_ref[0, j, :, lanes], kf_ref[1, j, :, lanes], n2)]
            if len(us) == 2:
                ps.append(_spectral_mul(us[1], kfh_ref[0, 0, :, lanes], kfh_ref[0, 1, :, lanes], n2))
            prod.append(ps)
        for (j, h), ps in zip(batch, prod):
            if len(ps) == 2:
                store(j, h, dot(edge_ref[0, 2], ps[0]) + dot(edge_ref[0, 3], ps[1]))
            else:
                store(j, h, dot(gt_ref[j], ps[0]))


def _stage_b(a5, kf, g, gt, kb, edge=None, kfh=None):
    p, k1n, n2 = a5.shape[0], a5.shape[-3], a5.shape[-2]
    herm = edge is not None
    if a5.ndim == 4:
        blk = pl.BlockSpec((1, kb, n2, HY_CH), lambda i, pi: (pi, i, 0, 0))
    else:
        blk = pl.BlockSpec((1, SLABS, kb, n2, LANES), lambda i, pi: (pi, 0, i, 0, 0))
    gspec = pl.BlockSpec((kb, 2 * n2, 2 * n2), lambda i, pi: (i, 0, 0))
    extra = [pl.BlockSpec((1,) + edge.shape[1:], lambda i, pi: (i, 0, 0, 0)),
             pl.BlockSpec((1,) + kfh.shape[1:], lambda i, pi: (i, 0, 0, 0))] if herm else []
    return pl.pallas_call(
        functools.partial(_stage_b_kernel, herm=herm),
        grid=(k1n // kb, p),
        in_specs=[blk, pl.BlockSpec((2, kb, n2, HY_CH), lambda i, pi: (0, i, 0, 0)), gspec, gspec] + extra,
        out_specs=blk,
        out_shape=jax.ShapeDtypeStruct(a5.shape, jnp.uint32),
        compiler_params=_cparams("parallel", "parallel"),
        name="hy_stage_b",
    )(a5, kf, g, gt, *([edge, kfh] if herm else []))


def _stage_bf_kernel(af_ref, ab_ref, g_ref, *rest, scale, herm):
    if herm:
        edge_ref, o_ref, oh_ref = rest
    else:
        (o_ref,) = rest
    kb, n2 = af_ref.shape[2], af_ref.shape[3]

    def spectrum(mat, j, h):
        conj_mat = jnp.concatenate([mat[:n2], -mat[n2:]], axis=0)
        x = jnp.concatenate([_slab_rows(af_ref, j, h), _slab_rows(ab_ref, j, h)], axis=0)
        return jnp.dot(jnp.concatenate([mat, conj_mat], axis=1), x, preferred_element_type=F32) * scale

    for j in range(kb):
        for h in range(SLABS // SLABS_PER_DOT):
            lanes = slice(h * MXU_COLS, (h + 1) * MXU_COLS)
            if herm and j == 0:
                u = spectrum(edge_ref[0, 0], 0, h)
                uh = spectrum(edge_ref[0, 1], 0, h)
                oh_ref[0, 0, :, lanes] = uh[:n2]
                oh_ref[0, 1, :, lanes] = uh[n2:]
            else:
                u = spectrum(g_ref[j], j, h)
            o_ref[0, j, :, lanes] = u[:n2]
            o_ref[1, j, :, lanes] = u[n2:]


def _stage_bf(a5f, a5b, g, kb, scale, edge=None):
    _, _, k1n, n2, _ = a5f.shape
    herm = edge is not None
    aspec = pl.BlockSpec((1, SLABS, kb, n2, LANES), lambda i: (0, 0, i, 0, 0))
    out_specs = [pl.BlockSpec((2, kb, n2, HY_CH), lambda i: (0, i, 0, 0))]
    out_shape = [jax.ShapeDtypeStruct((2, k1n, n2, HY_CH), F32)]
    if herm:
        out_specs.append(pl.BlockSpec((1, 2, n2, HY_CH), lambda i: (i, 0, 0, 0)))
        out_shape.append(jax.ShapeDtypeStruct((k1n // kb, 2, n2, HY_CH), F32))
    return pl.pallas_call(
        functools.partial(_stage_bf_kernel, scale=scale, herm=herm),
        grid=(k1n // kb,),
        in_specs=[aspec, aspec, pl.BlockSpec((kb, 2 * n2, 2 * n2), lambda i: (i, 0, 0))]
        + ([pl.BlockSpec((1,) + edge.shape[1:], lambda i: (i, 0, 0, 0))] if herm else []),
        out_specs=out_specs,
        out_shape=out_shape,
        compiler_params=_cparams("parallel"),
        name="hy_filter_spectrum",
    )(a5f, a5b, g, *([edge] if herm else []))


def _stage_c_kernel(*refs, nz, a, mb):
    v_ref, f_ref = refs[0], refs[1]
    ux_refs = refs[2:2 + SLABS]
    skip_ref, o_ref, acc_ref = refs[2 + SLABS:]
    mk = pl.program_id(2)
    kblk = mb // 2
    v2 = v_ref.reshape(SLABS * kblk * N2_TILE, LANES)

    @pl.when(mk == 0)
    def _():
        acc_ref[...] = jnp.zeros_like(acc_ref)

    for s in range(N2_TILE):
        packed = jnp.concatenate([v2[pl.ds(c * kblk * N2_TILE + s, kblk, stride=N2_TILE), :] for c in range(SLABS)],
                                 axis=1)
        vs = jnp.concatenate(_unpack_pair(packed), axis=0).astype(BF16)
        acc_ref[s] += jnp.dot(f_ref[...], vs, preferred_element_type=F32)

    @pl.when(mk == pl.num_programs(2) - 1)
    def _():
        o2 = o_ref.reshape(SLABS * a * N2_TILE, LANES)
        ux2 = [r.reshape(nz * a * N2_TILE, LANES) for r in ux_refs]
        for s in range(N2_TILE):
            y = acc_ref[s]
            for c in range(SLABS):
                vals = []
                for z in range(nz):
                    uu, xx = _unpack_pair(ux2[c][pl.ds(z * a * N2_TILE + s, a, stride=N2_TILE), :])
                    yc = y[z * a:(z + 1) * a, c * LANES:(c + 1) * LANES]
                    vals.append(xx * (yc + skip_ref[0:1, c * LANES:(c + 1) * LANES] * uu))
                o2[pl.ds(c * a * N2_TILE + s, a, stride=N2_TILE), :] = _pack_pair(*vals) if nz == 2 else vals[0]


def _stage_c(v5, fmat, ux5, skip, mb):
    nz, p, a, n2, _ = ux5.shape
    m = 2 * v5.shape[2]
    xspec = lambda c: pl.BlockSpec((nz, 1, a, N2_TILE, LANES), lambda pi, j, mk: (0, pi, 0, j, c))
    return pl.pallas_call(
        functools.partial(_stage_c_kernel, nz=nz, a=a, mb=mb),
        grid=(p, n2 // N2_TILE, m // mb),
        in_specs=[pl.BlockSpec((1, SLABS, mb // 2, N2_TILE, LANES), lambda pi, j, mk: (pi, 0, mk, j, 0)),
                  pl.BlockSpec((nz * a, mb), lambda pi, j, mk: (0, mk))]
        + [xspec(c) for c in range(SLABS)] + [_const_spec((1, HY_CH))],
        out_specs=pl.BlockSpec((1, SLABS, a, N2_TILE, LANES), lambda pi, j, mk: (pi, 0, 0, j, 0)),
        out_shape=jax.ShapeDtypeStruct((p, SLABS, a, n2, LANES), jnp.uint32 if nz == 2 else F32),
        scratch_shapes=[pltpu.VMEM((N2_TILE, nz * a, HY_CH), F32)],
        compiler_params=_cparams("parallel", "parallel", "arbitrary"),
        name="hy_stage_c",
    )(v5, fmat, *([ux5] * SLABS), skip)


def _filter_kernel(zt_ref, tn_ref, w1t_ref, b1_ref, fr_ref, w2t_ref, b2_ref, w3_ref, dl_ref, of_ref, ob_ref):
    hi = lax.Precision.HIGHEST
    fr = fr_ref[...]
    h = jnp.sin(fr * (jnp.dot(w1t_ref[...], zt_ref[...], precision=hi, preferred_element_type=F32) + b1_ref[...]))
    h = jnp.sin(fr * (jnp.dot(w2t_ref[...], h, precision=hi, preferred_element_type=F32) + b2_ref[...]))
    taps = jnp.dot(h.T.astype(BF16), w3_ref[...], preferred_element_type=F32)
    tr = taps.shape[0]
    decay = jnp.exp(-tn_ref[...] * dl_ref[...])
    of_ref[...] = taps[:, :HY_CH] * decay
    t = pl.program_id(0) * tr + lax.broadcasted_iota(jnp.int32, (tr, HY_CH), 0)
    ob_ref[...] = jnp.where(t == 0, 0.0, taps[:, HY_CH:] * decay)


def _filter_taps(zfeat, f_w1, f_b1, f_freq, f_w2, f_b2, f_w3, absdelta, tr):
    n = zfeat.shape[0]
    cs = lambda a: _const_spec(a.shape)
    args = (zfeat.T, zfeat[:, 0:1], f_w1.T, f_b1.T, f_freq.T, f_w2.T, f_b2.T, f_w3.astype(BF16), absdelta)
    out = pl.BlockSpec((tr, HY_CH), lambda i: (i, 0))
    return pl.pallas_call(
        _filter_kernel,
        grid=(n // tr,),
        in_specs=[pl.BlockSpec((POS_DIM, tr), lambda i: (0, i)), pl.BlockSpec((tr, 1), lambda i: (i, 0))]
        + [cs(a) for a in args[2:]],
        out_specs=[out, out],
        out_shape=[jax.ShapeDtypeStruct((n, HY_CH), F32)] * 2,
        compiler_params=_cparams("parallel"),
        name="hy_filter_taps",
    )(*args)


def _bias_kernel(rb_ref, oh_ref, o_ref):
    o_ref[...] = jnp.dot(rb_ref[...], oh_ref[...], precision=lax.Precision.HIGHEST,
                         preferred_element_type=F32)


def _bias_table(rel_bias):
    i = jnp.arange(BLOCK)[:, None]
    j = jnp.arange(3 * BLOCK)[None, :]
    rel = j - BLOCK - i
    nb2 = N_BUCKETS // 2
    max_exact = nb2 // 2
    n = jnp.abs(rel)
    large = max_exact + (jnp.log(jnp.maximum(n, 1).astype(F32) / max_exact)
                         / math.log(MAX_DIST / max_exact) * (nb2 - max_exact)).astype(jnp.int32)
    large = jnp.minimum(large, nb2 - 1)
    bucket = jnp.where(rel > 0, nb2, 0) + jnp.where(n < max_exact, n, large)
    onehot = (bucket.reshape(1, -1) == jnp.arange(N_BUCKETS)[:, None]).astype(F32)
    cols = onehot.shape[1]
    tc = cols // 4
    table = pl.pallas_call(
        _bias_kernel,
        grid=(4,),
        in_specs=[_const_spec((ATT_HEADS, N_BUCKETS)), pl.BlockSpec((N_BUCKETS, tc), lambda c: (0, c))],
        out_specs=pl.BlockSpec((ATT_HEADS, tc), lambda c: (0, c)),
        out_shape=jax.ShapeDtypeStruct((ATT_HEADS, cols), F32),
        compiler_params=_cparams("parallel"),
        name="att_bias_table",
    )(rel_bias.T, onehot)
    table = table.reshape(ATT_HEADS, BLOCK, 3 * BLOCK)
    table = jnp.where((n <= WINDOW)[None], table, -jnp.inf)
    table = table.reshape(KV_HEADS, GQA_GROUP, BLOCK, 3 * BLOCK)
    return table.transpose(0, 3, 1, 2).reshape(KV_HEADS, 3 * BLOCK, GQA_GROUP * BLOCK)


def _attn_kernel(q_ref, km_ref, kp_ref, kn_ref, vm_ref, vp_ref, vn_ref, bias_ref, sink_ref, o_ref, *, nsub):
    i = pl.program_id(1)
    neg = -jnp.inf
    pen_first = jnp.where(i == 0, neg, 0.0)
    pen_last = jnp.where(i == pl.num_programs(1) - 1, neg, 0.0)
    kwin = jnp.concatenate([kp_ref[0], km_ref[0], kn_ref[0]], axis=0)
    vwin = jnp.concatenate([vp_ref[0], vm_ref[0], vn_ref[0]], axis=1)
    lane = lax.broadcasted_iota(jnp.int32, (BLOCK, LANES), 1)
    lo_half = lane < HEAD_DIM
    ones = jnp.ones((SUM_ROWS, 3 * BLOCK), BF16)
    units = [(s, g) for s in range(nsub) for g in range(KV_HEADS)]

    def scores(s, g):
        qs = q_ref[0, s * BLOCK:(s + 1) * BLOCK, :]
        rows = []
        for pr in range(2):
            qp = qs[:, (2 * g + pr) * LANES:(2 * g + pr + 1) * LANES]
            rows.append(jnp.where(lo_half, qp, jnp.zeros_like(qp)))
            rows.append(jnp.where(lo_half, jnp.zeros_like(qp), qp))
        qg = jnp.concatenate(rows, axis=0)
        kg = kwin[s * BLOCK:(s + 3) * BLOCK, g * LANES:(g + 1) * LANES]
        t = lax.dot_general(kg, qg, _NT, preferred_element_type=F32) + bias_ref[g]
        if s == 0:
            t = jnp.concatenate([t[:BLOCK] + pen_first, t[BLOCK:]], axis=0)
        if s == nsub - 1:
            t = jnp.concatenate([t[:2 * BLOCK], t[2 * BLOCK:] + pen_last], axis=0)
        return t

    def pv(s, g, p):
        vg = jnp.concatenate([vwin[g * HEAD_DIM:(g + 1) * HEAD_DIM, s * BLOCK:(s + 3) * BLOCK], ones],
                             axis=0)
        return jnp.dot(vg, p, preferred_element_type=F32)

    for u0 in range(0, len(units), ATT_UNITS):
        batch = units[u0:u0 + ATT_UNITS]
        sc = [scores(s, g) for s, g in batch]
        m = [jnp.maximum(jnp.max(t, axis=0, keepdims=True), sink_ref[g]) for t, (_, g) in zip(sc, batch)]
        p = [jnp.exp2(t - mm).astype(BF16) for t, mm in zip(sc, m)]
        o = [pv(s, g, pp) for (s, g), pp in zip(batch, p)]
        for (s, g), oo, mm in zip(batch, o, m):
            on = oo[:HEAD_DIM] / (oo[HEAD_DIM:HEAD_DIM + 1] + jnp.exp2(sink_ref[g] - mm))
            for pr in range(2):
                pair = jnp.concatenate([on[:, (2 * pr) * BLOCK:(2 * pr + 1) * BLOCK],
                                        on[:, (2 * pr + 1) * BLOCK:(2 * pr + 2) * BLOCK]], axis=0)
                o_ref[0, s * BLOCK:(s + 1) * BLOCK, (2 * g + pr) * LANES:(2 * g + pr + 1) * LANES] = (
                    pair.T.astype(o_ref.dtype))


SUM_ROWS = 16
ATT_UNITS = 8
ATT_TQ = 1024


def _attention(q, kd, vt, bias_t, sink_t, tq):
    b, l, _ = q.shape
    nsub = tq // BLOCK
    nblk = l // BLOCK
    kw = 2 * KV_W
    main = lambda w: pl.BlockSpec((1, tq, w), lambda bi, i: (bi, i, 0))
    prev = pl.BlockSpec((1, BLOCK, kw), lambda bi, i: (bi, jnp.maximum(i * nsub - 1, 0), 0))
    nxt = pl.BlockSpec((1, BLOCK, kw), lambda bi, i: (bi, jnp.minimum((i + 1) * nsub, nblk - 1), 0))
    vmain = pl.BlockSpec((1, KV_W, tq), lambda bi, i: (bi, 0, i))
    vprev = pl.BlockSpec((1, KV_W, BLOCK), lambda bi, i: (bi, 0, jnp.maximum(i * nsub - 1, 0)))
    vnxt = pl.BlockSpec((1, KV_W, BLOCK), lambda bi, i: (bi, 0, jnp.minimum((i + 1) * nsub, nblk - 1)))
    return pl.pallas_call(
        functools.partial(_attn_kernel, nsub=nsub),
        grid=(b, l // tq),
        in_specs=[main(ATT_W), main(kw), prev, nxt, vmain, vprev, vnxt,
                  _const_spec(bias_t.shape), _const_spec(sink_t.shape)],
        out_specs=main(ATT_W),
        out_shape=jax.ShapeDtypeStruct((b, l, ATT_W), BF16),
        compiler_params=_cparams("parallel", "parallel"),
        name="window_attn",
    )(q, kd, kd, kd, vt, vt, vt, bias_t, sink_t)


FF_CHUNK = 256


def _ffn_kernel(x_ref, yh_ref, ya_ref, hg_ref, ag_ref, wo_ref, g2_ref, wgu_ref, wd_ref, o_ref, act_ref, *, pairs, nt):
    def rms(t, g):
        return (t * lax.rsqrt(jnp.mean(t * t, axis=-1, keepdims=True) + EPS) * g).astype(BF16)

    if len(yh_ref.shape) == 3:
        yh = yh_ref[0]
    else:
        yh = jnp.concatenate([yh_ref[0, c] for c in range(SLABS)], axis=1)
    if pairs:
        lo, hi = _unpack_pair(yh)
        yh = jnp.where(pl.program_id(0) // nt < pairs, lo, hi)
    mixed = jnp.concatenate([rms(yh, hg_ref[...]), rms(ya_ref[...].astype(F32), ag_ref[...])], axis=1)
    h = x_ref[...] + jnp.dot(mixed, wo_ref[...], preferred_element_type=F32)
    hn = rms(h, g2_ref[...])
    for c in range(D_FF // FF_CHUNK):
        lo = c * FF_CHUNK
        gate = jnp.dot(hn, wgu_ref[:, lo:lo + FF_CHUNK], preferred_element_type=F32)
        up = jnp.dot(hn, wgu_ref[:, D_FF + lo:D_FF + lo + FF_CHUNK], preferred_element_type=F32)
        act_ref[:, lo:lo + FF_CHUNK] = (gate / (1.0 + jnp.exp(-gate)) * up).astype(BF16)
    o_ref[...] = h + jnp.dot(act_ref[...], wd_ref[...], preferred_element_type=F32)


def _ffn(x2d, yh, ya, hg, ag, wo_b, g2, wgu_b, wd_b, tm):
    t = x2d.shape[0]
    nt = yh.shape[-2] // tm
    pairs = yh.shape[0] if yh.dtype == jnp.uint32 else 0
    row = lambda w: pl.BlockSpec((tm, w), lambda i: (i, 0))
    if yh.ndim == 3:
        yspec = pl.BlockSpec((1, tm, HY_CH), lambda i: ((i // nt) % yh.shape[0], i % nt, 0))
    else:
        yspec = pl.BlockSpec((1, SLABS, tm, LANES), lambda i: ((i // nt) % yh.shape[0], 0, i % nt, 0))
    return pl.pallas_call(
        functools.partial(_ffn_kernel, pairs=pairs, nt=nt),
        grid=(t // tm,),
        in_specs=[row(D_MODEL), yspec, row(ATT_W), _const_spec((1, HY_CH)), _const_spec((1, ATT_W)),
                  _const_spec((D_MODEL, D_MODEL)), _const_spec((1, D_MODEL)),
                  _const_spec((D_MODEL, 2 * D_FF)), _const_spec((D_FF, D_MODEL))],
        out_specs=row(D_MODEL),
        out_shape=jax.ShapeDtypeStruct((t, D_MODEL), F32),
        scratch_shapes=[pltpu.VMEM((tm, D_FF), BF16)],
        compiler_params=_cparams("parallel"),
        name="outproj_swiglu",
    )(x2d, yh, ya, hg, ag, wo_b, g2, wgu_b, wd_b)


def _dft_tables(seq_len, paired, kblk):
    n = 2 * seq_len
    n1 = n // FFT_N2
    a = n1 // 2
    kk = np.arange(n1)[:, None]
    th_half = 2.0 * np.pi * ((kk * np.arange(a)[None, :]) % n1) / n1
    c, s = np.cos(th_half), np.sin(th_half)

    def blocked(re, im):
        return np.concatenate([np.concatenate([re[i:i + kblk], im[i:i + kblk]], axis=0)
                               for i in range(0, re.shape[0], kblk)], axis=0)

    def real_input_rows(cc, ss):
        im = -ss[:a].copy()
        im[0] = cc[a]
        return cc[:a], im

    if paired:
        fa = blocked(np.concatenate([c, s], axis=1), np.concatenate([-s, c], axis=1))
        fc = fa.T
        faf = blocked(c, -s)
    else:
        fa = blocked(*real_input_rows(c, s))
        re_cols = np.concatenate([c[:1], 2.0 * c[1:a]], axis=0)
        im_cols = np.concatenate([c[a:a + 1], -2.0 * s[1:a]], axis=0)
        fc = blocked(re_cols, im_cols).T
        faf = fa
    to = lambda m: jnp.asarray(m.astype(np.float32)).astype(BF16)
    kron = None
    if paired and 2 * n1 * N2_TILE <= PAIR_KRON_ROWS:
        eye = np.eye(N2_TILE)
        kron = (to(np.kron(fa, eye)), to(np.kron(fc, eye)))
    return to(fa), to(fc), to(faf), kron


def _edge_matrices(g, half, kb):
    zero = jnp.zeros_like(g[0])
    z = zero[:, :FFT_N2]
    col = lambda m: m[:, :FFT_N2]
    sa = jnp.concatenate([col(g[0]), z], axis=1)
    sb = jnp.concatenate([z, col(g[half])], axis=1)
    blocks = [jnp.stack([sa, sb, sa.T, sb.T])]
    blocks += [jnp.stack([g[i], zero, g[i].T, zero]) for i in range(kb, half, kb)]
    return jnp.stack(blocks)


def _twiddle_tables(seq_len, k1_count):
    n = 2 * seq_len
    n1 = n // FFT_N2
    k1 = np.arange(k1_count, dtype=np.int64)[:, None, None]
    k2 = np.arange(FFT_N2, dtype=np.int64)[None, :, None]
    n2 = np.arange(FFT_N2, dtype=np.int64)[None, None, :]
    ang = ((n2 * (k2 * n1 + k1)) % n) * (-2.0 * np.pi / n)
    gr, gi = np.cos(ang), np.sin(ang)
    g = np.concatenate([np.concatenate([gr, -gi], axis=2), np.concatenate([gi, gr], axis=2)], axis=1)
    g = g.astype(np.float32)
    return jnp.asarray(g.astype(BF16)), jnp.asarray(np.swapaxes(g, 1, 2).astype(BF16))


def _filter_features(seq_len):
    t_idx = jnp.arange(seq_len, dtype=F32)[:, None]
    t_norm = t_idx / max(seq_len - 1, 1)
    bands = jnp.linspace(1e-4, N_BANDS - 1, N_BANDS, dtype=F32)
    w = (2.0 * math.pi) * t_idx * bands[None, :] / seq_len
    return jnp.concatenate([t_norm, jnp.cos(w), -jnp.sin(w)], axis=-1)


def _abs_deltas():
    min_decay = math.log(DECAY_TARGET) / FAST_DECAY_PCT
    max_decay = math.log(DECAY_TARGET) / SLOW_DECAY_PCT
    return jnp.abs(jnp.linspace(min_decay, max_decay, HY_CH, dtype=F32))[None, :]


def _pick(n, target):
    t = min(n, target)
    while n % t:
        t //= 2
    return t


def _layer(x, lw, shared, st):
    b, l, _ = x.shape
    t = b * l
    x2d = x.reshape(t, D_MODEL)
    tm = _pick(l, FFN_TM)
    ux, q, kd, vt = _inproj(x2d, lw["norm1"], lw["w_in"], lw["conv_w"], lw["conv_b"],
                            shared["bq"], shared["bk"], lw["qg"], lw["kg"], _pick(l, INPROJ_TM), l)
    r3 = lambda a: a.reshape(b, l, a.shape[-1])

    n1 = 2 * l // FFT_N2
    a = n1 // 2
    nz = st["nz"]
    p = b // nz
    ux5 = ux.reshape(nz, p, a, FFT_N2, HY_CH)
    mb = _pick(st["fa"].shape[0], STAGE_MB)
    kf, kfh = lw["kf"][st["key"]]
    if st["kron"] is not None:
        ah = _pair_a(ux5, st["kron"][0])
        vv = _stage_b(ah, kf, st["g"], st["gt"], _pick(n1, STAGE_KB))
        yh = _pair_c(vv, st["kron"][1], ux5, lw["skip"]).reshape(p, l, HY_CH)
    else:
        ah = _stage_a(ux5, st["fa"], mb)
        vv = _stage_b(ah, kf, st["g"], st["gt"], _pick(ah.shape[2], STAGE_KB), st["edge"], kfh)
        yh = _stage_c(vv, st["fc"], ux5, lw["skip"], mb).reshape(p, SLABS, l, LANES)

    ya = _attention(r3(q), r3(kd), vt, shared["bias"], lw["sink_t"], _pick(l, ATT_TQ))
    ya = ya.reshape(t, ATT_W)

    out = _ffn(x2d, yh, ya, lw["hy_gain"], lw["at_gain"], lw["w_out"], lw["norm2"],
               lw["w_gate_up"], lw["w_down"], tm)
    return out.reshape(b, l, D_MODEL)


def _filter_spectrum(lw_raw, st, seq_len):
    n = 2 * seq_len
    a = seq_len // FFT_N2
    taps_f, taps_b = _filter_taps(st["zfeat"], lw_raw["f_w1"], lw_raw["f_b1"], lw_raw["f_freq"], lw_raw["f_w2"],
                                  lw_raw["f_b2"], lw_raw["f_w3"], st["absdelta"], _pick(seq_len, 512))
    mb = _pick(st["faf"].shape[0], STAGE_MB)
    ahf = _stage_a(taps_f.reshape(1, 1, a, FFT_N2, HY_CH), st["faf"], mb)
    ahb = _stage_a(taps_b.reshape(1, 1, a, FFT_N2, HY_CH), st["faf"], mb)
    out = _stage_bf(ahf, ahb, st["g"], _pick(ahf.shape[2], STAGE_KB), 1.0 / n, st["edge"])
    return (out[0], out[1]) if st["edge"] is not None else (out[0], None)


def kernel(x_prompt, x_sample, norm1, w_in, conv_w, conv_b, f_w1, f_b1, f_freq, f_w2, f_b2, f_w3,
           hy_skip, q_gain, k_gain, sink, rel_bias, hy_gain, at_gain, w_out, norm2, w_gate_up, w_down):
    depth = norm1.shape[0]
    streams = {}
    for key, x in (("prompt", x_prompt), ("sample", x_sample)):
        b, l, _ = x.shape
        paired = b % 2 == 0
        n1 = 2 * l // FFT_N2
        fa, fc, faf, kron = _dft_tables(l, paired, _pick(2 * n1 if paired else n1, STAGE_MB) // 2)
        g, gt = _twiddle_tables(l, n1 if paired else n1 // 2 + 1)
        streams[key] = dict(key=key, nz=2 if paired else 1, fa=fa, fc=fc, faf=faf, kron=kron, g=g, gt=gt,
                            edge=None if paired else _edge_matrices(g, n1 // 2, _pick(n1 // 2, STAGE_KB)),
                            zfeat=_filter_features(l), absdelta=_abs_deltas())

    head_id = np.arange(ATT_W) // HEAD_DIM
    shared = dict(
        bq=jnp.asarray((head_id[:, None] == head_id[None, :]).astype(np.float32) / HEAD_DIM).astype(BF16),
        bk=jnp.asarray((head_id[:KV_W, None] == head_id[None, :KV_W]).astype(np.float32) / HEAD_DIM).astype(BF16),
        bias=_bias_table(rel_bias) * LOG2E,
    )

    y_prompt, y_sample = x_prompt, x_sample
    for li in range(depth):
        raw = dict(f_w1=f_w1[li], f_b1=f_b1[li][None, :], f_freq=f_freq[li][None, :], f_w2=f_w2[li],
                   f_b2=f_b2[li][None, :], f_w3=f_w3[li])
        lw = dict(
            norm1=norm1[li][None, :], w_in=w_in[li].astype(BF16),
            qg=jnp.tile(q_gain[li], ATT_HEADS)[None, :] * (HEAD_DIM ** -0.5 * LOG2E),
            kg=jnp.tile(k_gain[li], KV_HEADS)[None, :],
            conv_w=conv_w[li], conv_b=conv_b[li][None, :],
            skip=hy_skip[li][None, :],
            sink_t=jnp.repeat(sink[li] * LOG2E, BLOCK).reshape(KV_HEADS, 1, GQA_GROUP * BLOCK),
            hy_gain=hy_gain[li][None, :], at_gain=at_gain[li][None, :],
            w_out=w_out[li].astype(BF16), norm2=norm2[li][None, :],
            w_gate_up=w_gate_up[li].astype(BF16), w_down=w_down[li].astype(BF16),
        )
        lw["kf"] = {key: _filter_spectrum(raw, st, {"prompt": x_prompt, "sample": x_sample}[key].shape[1])
                    for key, st in streams.items()}
        y_prompt = _layer(y_prompt, lw, shared, streams["prompt"])
        y_sample = _layer(y_sample, lw, shared, streams["sample"])
    return (y_prompt, y_sample)
```

```python
import functools
import math

import numpy as np
import jax
import jax.numpy as jnp
from jax import lax
from jax.experimental import pallas as pl
from jax.experimental.pallas import tpu as pltpu

F32 = jnp.float32
BF16 = jnp.bfloat16

D_MODEL = 1024
ATT_HEADS = 8
KV_HEADS = 2
HEAD_DIM = 64
GQA_GROUP = ATT_HEADS // KV_HEADS
ATT_W = ATT_HEADS * HEAD_DIM
KV_W = KV_HEADS * HEAD_DIM
WINDOW = 128
BLOCK = 128
N_BUCKETS = 32
MAX_DIST = 128
HY_CH = D_MODEL - ATT_W
FILTER_HIDDEN = 64
N_BANDS = 16
POS_DIM = 1 + 2 * N_BANDS
FAST_DECAY_PCT = 0.3
SLOW_DECAY_PCT = 1.5
DECAY_TARGET = 1e-2
IN_W = 3 * HY_CH + (ATT_HEADS + 2 * KV_HEADS) * HEAD_DIM
D_FF = -(-8 * D_MODEL // (3 * 256)) * 256
EPS = 1e-6
LOG2E = math.log2(math.e)

FFT_N2 = 64
LANES = 128
N2_TILE = 8
SLABS = HY_CH // LANES
STAGE_MB = 512
STAGE_KB = 16
PAIR_KRON_ROWS = 1024
PAIR_STEP_ROWS = 32
STAGE_B_CHAINS = 8
INPROJ_TM = 512
FFN_TM = 512
VMEM_LIMIT = 56 * 1024 * 1024

_NT = (((1,), (1,)), ((), ()))


def _cparams(*sem):
    return pltpu.CompilerParams(dimension_semantics=sem, vmem_limit_bytes=VMEM_LIMIT)


def _const_spec(shape):
    nd = len(shape)
    return pl.BlockSpec(shape, lambda *_: (0,) * nd, pipeline_mode=pl.Buffered(1))


HALO = 16


def _inproj_kernel(x_ref, xp_ref, xn_ref, g1_ref, w_ref, cw_ref, cb_ref, bq_ref, bk_ref, qg_ref, kg_ref,
                   ux_ref, q_ref, k_ref, v_ref, he_ref, qkv_ref, *, nt, ntiles):
    i = pl.program_id(0)
    tm = x_ref.shape[0]

    @pl.when(i == 0)
    def _():
        he_ref[...] = jnp.zeros_like(he_ref)
        qkv_ref[...] = jnp.zeros_like(qkv_ref)

    v = qkv_ref[:, ATT_W + KV_W:]
    v_ref[0] = v.T.astype(BF16)

    k = qkv_ref[:, ATT_W:ATT_W + KV_W]
    ms = jnp.dot((k * k).astype(BF16), bk_ref[...], preferred_element_type=F32)
    kn = k * lax.rsqrt(ms + EPS) * kg_ref[...]
    lo_half = lax.broadcasted_iota(jnp.int32, kn.shape, 1) < HEAD_DIM
    kr = pltpu.roll(kn, HEAD_DIM, axis=1)
    k_ref[...] = jnp.concatenate([jnp.where(lo_half, kn, kr), jnp.where(lo_half, kr, kn)], axis=1).astype(BF16)

    q = qkv_ref[:, 0:ATT_W]
    ms = jnp.dot((q * q).astype(BF16), bq_ref[...], preferred_element_type=F32)
    q_ref[...] = (q * lax.rsqrt(ms + EPS) * qg_ref[...]).astype(BF16)

    rows = slice(HALO, HALO + tm)

    def conv(j):
        c = slice(j * HY_CH, (j + 1) * HY_CH)
        he = he_ref[:, c]
        return (pltpu.roll(he, 1, axis=0)[rows] * cw_ref[0:1, c] + he[rows] * cw_ref[1:2, c]
                + pltpu.roll(he, tm + 2 * HALO - 1, axis=0)[rows] * cw_ref[2:3, c] + cb_ref[0:1, c])

    ux_ref[...] = _pack_pair(conv(1) * conv(2), conv(0))

    ti = jnp.minimum(i, ntiles - 1)
    xp = jnp.where(ti % nt == 0, 0.0, xp_ref[...])
    xn = jnp.where(ti % nt == nt - 1, 0.0, xn_ref[...])
    xe = jnp.concatenate([xp, x_ref[...], xn], axis=0)
    xe = (xe * lax.rsqrt(jnp.mean(xe * xe, axis=-1, keepdims=True) + EPS) * g1_ref[...]).astype(BF16)
    he_ref[...] = jnp.dot(xe, w_ref[:, 0:3 * HY_CH], preferred_element_type=F32)
    qkv_ref[...] = jnp.dot(xe[HALO:HALO + tm], w_ref[:, 3 * HY_CH:], preferred_element_type=F32)


def _inproj(x2d, g1, w_in_b, conv_w, conv_b, bq, bk, qg, kg, tm, seq_len):
    t = x2d.shape[0]
    nt = seq_len // tm
    ntiles = t // tm
    nh = tm // HALO
    cur = lambda i: jnp.minimum(i, ntiles - 1)
    done = lambda i: jnp.maximum(i - 1, 0)
    row_in = pl.BlockSpec((tm, D_MODEL), lambda i: (cur(i), 0))
    prev = pl.BlockSpec((HALO, D_MODEL), lambda i: (jnp.maximum(cur(i) * nh - 1, 0), 0))
    nxt = pl.BlockSpec((HALO, D_MODEL), lambda i: (jnp.minimum((cur(i) + 1) * nh, t // HALO - 1), 0))
    row_out = lambda w: pl.BlockSpec((tm, w), lambda i: (done(i), 0))
    return pl.pallas_call(
        functools.partial(_inproj_kernel, nt=nt, ntiles=ntiles),
        grid=(ntiles + 1,),
        in_specs=[row_in, prev, nxt, _const_spec((1, D_MODEL)), _const_spec((D_MODEL, IN_W)),
                  _const_spec((3, 3 * HY_CH)), _const_spec((1, 3 * HY_CH)),
                  _const_spec((ATT_W, ATT_W)), _const_spec((KV_W, KV_W)),
                  _const_spec((1, ATT_W)), _const_spec((1, KV_W))],
        out_specs=[row_out(HY_CH), row_out(ATT_W), row_out(2 * KV_W),
                   pl.BlockSpec((1, KV_W, tm), lambda i: (done(i) // nt, 0, done(i) % nt))],
        out_shape=[jax.ShapeDtypeStruct((t, HY_CH), jnp.uint32),
                   jax.ShapeDtypeStruct((t, ATT_W), BF16), jax.ShapeDtypeStruct((t, 2 * KV_W), BF16),
                   jax.ShapeDtypeStruct((t // seq_len, KV_W, seq_len), BF16)],
        scratch_shapes=[pltpu.VMEM((tm + 2 * HALO, 3 * HY_CH), F32), pltpu.VMEM((tm, IN_W - 3 * HY_CH), F32)],
        compiler_params=_cparams("arbitrary"),
        name="inproj",
    )(x2d, x2d, x2d, g1, w_in_b, conv_w, conv_b, bq, bk, qg, kg)


def _stage_a_kernel(*refs, nz, a, mb, packed_in):
    x_refs, f_ref, o_ref = refs[:SLABS], refs[SLABS], refs[SLABS + 1]
    kblk = mb // 2
    xs = [r.reshape(nz * a * N2_TILE, LANES) for r in x_refs]
    o2 = o_ref.reshape(SLABS * kblk * N2_TILE, LANES)
    f = f_ref[...]

    def gather(s):
        x = jnp.concatenate(
            [jnp.concatenate([xs[c][pl.ds(z * a * N2_TILE + s, a, stride=N2_TILE), :] for z in range(nz)], axis=0)
             for c in range(SLABS)], axis=1)
        return (_unpack_pair(x)[0] if packed_in else x).astype(BF16)

    r = [jnp.dot(f, gather(s), preferred_element_type=F32) for s in range(N2_TILE)]
    for s in range(N2_TILE):
        packed = _pack_pair(r[s][:kblk], r[s][kblk:])
        for c in range(SLABS):
            o2[pl.ds(c * kblk * N2_TILE + s, kblk, stride=N2_TILE), :] = packed[:, c * LANES:(c + 1) * LANES]


def _pack_pair(re, im):
    rb = lax.bitcast_convert_type(re.astype(BF16).astype(F32), jnp.uint32)
    ib = lax.bitcast_convert_type(im.astype(BF16).astype(F32), jnp.uint32)
    return (rb >> 16) | ib


def _unpack_pair(p):
    re = lax.bitcast_convert_type(p << 16, F32)
    im = lax.bitcast_convert_type(p & jnp.uint32(0xFFFF0000), F32)
    return re, im


def _stage_a(x5, fmat, mb):
    nz, p, a, n2, _ = x5.shape
    m = fmat.shape[0]
    kblk = mb // 2
    xspec = lambda c: pl.BlockSpec((nz, 1, a, N2_TILE, LANES), lambda pi, j, mi: (0, pi, 0, j, c))
    return pl.pallas_call(
        functools.partial(_stage_a_kernel, nz=nz, a=a, mb=mb, packed_in=x5.dtype == jnp.uint32),
        grid=(p, n2 // N2_TILE, m // mb),
        in_specs=[xspec(c) for c in range(SLABS)] + [pl.BlockSpec((mb, nz * a), lambda pi, j, mi: (mi, 0))],
        out_specs=pl.BlockSpec((1, SLABS, kblk, N2_TILE, LANES), lambda pi, j, mi: (pi, 0, mi, j, 0)),
        out_shape=jax.ShapeDtypeStruct((p, SLABS, m // 2, n2, LANES), jnp.uint32),
        compiler_params=_cparams("parallel", "parallel", "parallel"),
        name="hy_stage_a",
    )(*([x5] * SLABS), fmat)


def _pair_a_kernel(ux_ref, f_ref, o_ref):
    nz, _, a, tiles, c = ux_ref.shape
    n1 = o_ref.shape[1]
    for t in range(tiles // N2_TILE):
        sl = slice(t * N2_TILE, (t + 1) * N2_TILE)
        u = _unpack_pair(ux_ref[:, 0, :, sl, :].reshape(nz * a * N2_TILE, c))[0].astype(BF16)
        r = jnp.dot(f_ref[...], u, preferred_element_type=F32)
        half = r.shape[0] // 2
        o_ref[0, :, sl, :] = _pack_pair(r[:half], r[half:]).reshape(n1, N2_TILE, c)


def _pair_a(ux5, fx):
    nz, p, a, n2, c = ux5.shape
    n1 = fx.shape[0] // (2 * N2_TILE)
    rows = _pick(n2, PAIR_STEP_ROWS)
    return pl.pallas_call(
        _pair_a_kernel,
        grid=(p, n2 // rows),
        in_specs=[pl.BlockSpec((nz, 1, a, rows, c), lambda pi, j: (0, pi, 0, j, 0)), _const_spec(fx.shape)],
        out_specs=pl.BlockSpec((1, n1, rows, c), lambda pi, j: (pi, 0, j, 0)),
        out_shape=jax.ShapeDtypeStruct((p, n1, n2, c), jnp.uint32),
        compiler_params=_cparams("parallel", "parallel"),
        name="hy_stage_a",
    )(ux5, fx)


def _pair_c_kernel(v_ref, f_ref, ux_ref, skip_ref, o_ref):
    nz, _, a, tiles, c = ux_ref.shape
    n1 = v_ref.shape[1]
    for t in range(tiles // N2_TILE):
        sl = slice(t * N2_TILE, (t + 1) * N2_TILE)
        vr, vi = _unpack_pair(v_ref[0, :, sl, :].reshape(n1 * N2_TILE, c))
        y = jnp.dot(f_ref[...], jnp.concatenate([vr, vi], axis=0).astype(BF16), preferred_element_type=F32)
        half = y.shape[0] // 2
        uu, xx = _unpack_pair(ux_ref[:, 0, :, sl, :].reshape(2 * half, c))
        val = xx * (y + skip_ref[...] * uu)
        o_ref[0, :, sl, :] = _pack_pair(val[:half], val[half:]).reshape(a, N2_TILE, c)


def _pair_c(v4, fx, ux5, skip):
    nz, p, a, n2, c = ux5.shape
    n1 = v4.shape[1]
    rows = _pick(n2, PAIR_STEP_ROWS)
    return pl.pallas_call(
        _pair_c_kernel,
        grid=(p, n2 // rows),
        in_specs=[pl.BlockSpec((1, n1, rows, c), lambda pi, j: (pi, 0, j, 0)), _const_spec(fx.shape),
                  pl.BlockSpec((nz, 1, a, rows, c), lambda pi, j: (0, pi, 0, j, 0)), _const_spec((1, c))],
        out_specs=pl.BlockSpec((1, a, rows, c), lambda pi, j: (pi, 0, j, 0)),
        out_shape=jax.ShapeDtypeStruct((p, a, n2, c), jnp.uint32),
        compiler_params=_cparams("parallel", "parallel"),
        name="hy_stage_c",
    )(v4, fx, ux5, skip)


MXU_COLS = 256
SLABS_PER_DOT = MXU_COLS // LANES


def _slab_rows(a_ref, j, h):
    if len(a_ref.shape) == 4:
        re, im = _unpack_pair(a_ref[0, j, :, h * MXU_COLS:(h + 1) * MXU_COLS])
        return jnp.concatenate([re, im], axis=0).astype(BF16)
    parts = [_unpack_pair(a_ref[0, h * SLABS_PER_DOT + c, j]) for c in range(SLABS_PER_DOT)]
    return jnp.concatenate([jnp.concatenate([re, im], axis=0) for re, im in parts], axis=1).astype(BF16)


def _spectral_mul(u, kr, ki, n2):
    ur, ui = u[:n2], u[n2:]
    return jnp.concatenate([ur * kr - ui * ki, ur * ki + ui * kr], axis=0).astype(BF16)


def _stage_b_kernel(a_ref, kf_ref, g_ref, gt_ref, *rest, herm):
    if herm:
        edge_ref, kfh_ref, o_ref = rest
    else:
        (o_ref,) = rest
    kb, n2 = a_ref.shape[-3], a_ref.shape[-2]

    def store(j, h, v):
        packed = _pack_pair(v[:n2], v[n2:])
        if len(o_ref.shape) == 4:
            o_ref[0, j, :, h * MXU_COLS:(h + 1) * MXU_COLS] = packed
        else:
            for c in range(SLABS_PER_DOT):
                o_ref[0, h * SLABS_PER_DOT + c, j] = packed[:, c * LANES:(c + 1) * LANES]

    dot = functools.partial(jnp.dot, preferred_element_type=F32)
    chains = [(j, h) for j in range(kb) for h in range(SLABS // SLABS_PER_DOT)]
    for c0 in range(0, len(chains), STAGE_B_CHAINS):
        batch = chains[c0:c0 + STAGE_B_CHAINS]
        fwd = []
        for j, h in batch:
            x = _slab_rows(a_ref, j, h)
            if herm and j == 0:
                fwd.append((dot(edge_ref[0, 0], x), dot(edge_ref[0, 1], x)))
            else:
                fwd.append((dot(g_ref[j], x),))
        prod = []
        for (j, h), us in zip(batch, fwd):
            lanes = slice(h * MXU_COLS, (h + 1) * MXU_COLS)
            ps = [_spectral_mul(us[0], kf_ref[0, j, :, lanes], kf_ref[1, j, :, lanes], n2)]
            if len(us) == 2:
                ps.append(_spectral_mul(us[1], kfh_ref[0, 0, :, lanes], kfh_ref[0, 1, :, lanes], n2))
            prod.append(ps)
        for (j, h), ps in zip(batch, prod):
            if len(ps) == 2:
                store(j, h, dot(edge_ref[0, 2], ps[0]) + dot(edge_ref[0, 3], ps[1]))
            else:
                store(j, h, dot(gt_ref[j], ps[0]))


def _stage_b(a5, kf, g, gt, kb, edge=None, kfh=None):
    p, k1n, n2 = a5.shape[0], a5.shape[-3], a5.shape[-2]
    herm = edge is not None
    if a5.ndim == 4:
        blk = pl.BlockSpec((1, kb, n2, HY_CH), lambda i, pi: (pi, i, 0, 0))
    else:
        blk = pl.BlockSpec((1, SLABS, kb, n2, LANES), lambda i, pi: (pi, 0, i, 0, 0))
    gspec = pl.BlockSpec((kb, 2 * n2, 2 * n2), lambda i, pi: (i, 0, 0))
    extra = [pl.BlockSpec((1,) + edge.shape[1:], lambda i, pi: (i, 0, 0, 0)),
             pl.BlockSpec((1,) + kfh.shape[1:], lambda i, pi: (i, 0, 0, 0))] if herm else []
    return pl.pallas_call(
        functools.partial(_stage_b_kernel, herm=herm),
        grid=(k1n // kb, p),
        in_specs=[blk, pl.BlockSpec((2, kb, n2, HY_CH), lambda i, pi: (0, i, 0, 0)), gspec, gspec] + extra,
        out_specs=blk,
        out_shape=jax.ShapeDtypeStruct(a5.shape, jnp.uint32),
        compiler_params=_cparams("parallel", "parallel"),
        name="hy_stage_b",
    )(a5, kf, g, gt, *([edge, kfh] if herm else []))


def _stage_bf_kernel(af_ref, ab_ref, g_ref, *rest, scale, herm):
    if herm:
        edge_ref, o_ref, oh_ref = rest
    else:
        (o_ref,) = rest
    kb, n2 = af_ref.shape[2], af_ref.shape[3]

    def spectrum(mat, j, h):
        conj_mat = jnp.concatenate([mat[:n2], -mat[n2:]], axis=0)
        x = jnp.concatenate([_slab_rows(af_ref, j, h), _slab_rows(ab_ref, j, h)], axis=0)
        return jnp.dot(jnp.concatenate([mat, conj_mat], axis=1), x, preferred_element_type=F32) * scale

    for j in range(kb):
        for h in range(SLABS // SLABS_PER_DOT):
            lanes = slice(h * MXU_COLS, (h + 1) * MXU_COLS)
            if herm and j == 0:
                u = spectrum(edge_ref[0, 0], 0, h)
                uh = spectrum(edge_ref[0, 1], 0, h)
                oh_ref[0, 0, :, lanes] = uh[:n2]
                oh_ref[0, 1, :, lanes] = uh[n2:]
            else:
                u = spectrum(g_ref[j], j, h)
            o_ref[0, j, :, lanes] = u[:n2]
            o_ref[1, j, :, lanes] = u[n2:]


def _stage_bf(a5f, a5b, g, kb, scale, edge=None):
    _, _, k1n, n2, _ = a5f.shape
    herm = edge is not None
    aspec = pl.BlockSpec((1, SLABS, kb, n2, LANES), lambda i: (0, 0, i, 0, 0))
    out_specs = [pl.BlockSpec((2, kb, n2, HY_CH), lambda i: (0, i, 0, 0))]
    out_shape = [jax.ShapeDtypeStruct((2, k1n, n2, HY_CH), F32)]
    if herm:
        out_specs.append(pl.BlockSpec((1, 2, n2, HY_CH), lambda i: (i, 0, 0, 0)))
        out_shape.append(jax.ShapeDtypeStruct((k1n // kb, 2, n2, HY_CH), F32))
    return pl.pallas_call(
        functools.partial(_stage_bf_kernel, scale=scale, herm=herm),
        grid=(k1n // kb,),
        in_specs=[aspec, aspec, pl.BlockSpec((kb, 2 * n2, 2 * n2), lambda i: (i, 0, 0))]
        + ([pl.BlockSpec((1,) + edge.shape[1:], lambda i: (i, 0, 0, 0))] if herm else []),
        out_specs=out_specs,
        out_shape=out_shape,
        compiler_params=_cparams("parallel"),
        name="hy_filter_spectrum",
    )(a5f, a5b, g, *([edge] if herm else []))


def _stage_c_kernel(*refs, nz, a, mb):
    v_ref, f_ref = refs[0], refs[1]
    ux_refs = refs[2:2 + SLABS]
    skip_ref, o_ref, acc_ref = refs[2 + SLABS:]
    mk = pl.program_id(2)
    kblk = mb // 2
    v2 = v_ref.reshape(SLABS * kblk * N2_TILE, LANES)

    @pl.when(mk == 0)
    def _():
        acc_ref[...] = jnp.zeros_like(acc_ref)

    for s in range(N2_TILE):
        packed = jnp.concatenate([v2[pl.ds(c * kblk * N2_TILE + s, kblk, stride=N2_TILE), :] for c in range(SLABS)],
                                 axis=1)
        vs = jnp.concatenate(_unpack_pair(packed), axis=0).astype(BF16)
        acc_ref[s] += jnp.dot(f_ref[...], vs, preferred_element_type=F32)

    @pl.when(mk == pl.num_programs(2) - 1)
    def _():
        o2 = o_ref.reshape(SLABS * a * N2_TILE, LANES)
        ux2 = [r.reshape(nz * a * N2_TILE, LANES) for r in ux_refs]
        for s in range(N2_TILE):
            y = acc_ref[s]
            for c in range(SLABS):
                vals = []
                for z in range(nz):
                    uu, xx = _unpack_pair(ux2[c][pl.ds(z * a * N2_TILE + s, a, stride=N2_TILE), :])
                    yc = y[z * a:(z + 1) * a, c * LANES:(c + 1) * LANES]
                    vals.append(xx * (yc + skip_ref[0:1, c * LANES:(c + 1) * LANES] * uu))
                o2[pl.ds(c * a * N2_TILE + s, a, stride=N2_TILE), :] = _pack_pair(*vals) if nz == 2 else vals[0]


def _stage_c(v5, fmat, ux5, skip, mb):
    nz, p, a, n2, _ = ux5.shape
    m = 2 * v5.shape[2]
    xspec = lambda c: pl.BlockSpec((nz, 1, a, N2_TILE, LANES), lambda pi, j, mk: (0, pi, 0, j, c))
    return pl.pallas_call(
        functools.partial(_stage_c_kernel, nz=nz, a=a, mb=mb),
        grid=(p, n2 // N2_TILE, m // mb),
        in_specs=[pl.BlockSpec((1, SLABS, mb // 2, N2_TILE, LANES), lambda pi, j, mk: (pi, 0, mk, j, 0)),
                  pl.BlockSpec((nz * a, mb), lambda pi, j, mk: (0, mk))]
        + [xspec(c) for c in range(SLABS)] + [_const_spec((1, HY_CH))],
        out_specs=pl.BlockSpec((1, SLABS, a, N2_TILE, LANES), lambda pi, j, mk: (pi, 0, 0, j, 0)),
        out_shape=jax.ShapeDtypeStruct((p, SLABS, a, n2, LANES), jnp.uint32 if nz == 2 else F32),
        scratch_shapes=[pltpu.VMEM((N2_TILE, nz * a, HY_CH), F32)],
        compiler_params=_cparams("parallel", "parallel", "arbitrary"),
        name="hy_stage_c",
    )(v5, fmat, *([ux5] * SLABS), skip)


def _filter_kernel(zt_ref, tn_ref, w1t_ref, b1_ref, fr_ref, w2t_ref, b2_ref, w3_ref, dl_ref, of_ref, ob_ref):
    hi = lax.Precision.HIGHEST
    fr = fr_ref[...]
    h = jnp.sin(fr * (jnp.dot(w1t_ref[...], zt_ref[...], precision=hi, preferred_element_type=F32) + b1_ref[...]))
    h = jnp.sin(fr * (jnp.dot(w2t_ref[...], h, precision=hi, preferred_element_type=F32) + b2_ref[...]))
    taps = jnp.dot(h.T.astype(BF16), w3_ref[...], preferred_element_type=F32)
    tr = taps.shape[0]
    decay = jnp.exp(-tn_ref[...] * dl_ref[...])
    of_ref[...] = taps[:, :HY_CH] * decay
    t = pl.program_id(0) * tr + lax.broadcasted_iota(jnp.int32, (tr, HY_CH), 0)
    ob_ref[...] = jnp.where(t == 0, 0.0, taps[:, HY_CH:] * decay)


def _filter_taps(zfeat, f_w1, f_b1, f_freq, f_w2, f_b2, f_w3, absdelta, tr):
    n = zfeat.shape[0]
    cs = lambda a: _const_spec(a.shape)
    args = (zfeat.T, zfeat[:, 0:1], f_w1.T, f_b1.T, f_freq.T, f_w2.T, f_b2.T, f_w3.astype(BF16), absdelta)
    out = pl.BlockSpec((tr, HY_CH), lambda i: (i, 0))
    return pl.pallas_call(
        _filter_kernel,
        grid=(n // tr,),
        in_specs=[pl.BlockSpec((POS_DIM, tr), lambda i: (0, i)), pl.BlockSpec((tr, 1), lambda i: (i, 0))]
        + [cs(a) for a in args[2:]],
        out_specs=[out, out],
        out_shape=[jax.ShapeDtypeStruct((n, HY_CH), F32)] * 2,
        compiler_params=_cparams("parallel"),
        name="hy_filter_taps",
    )(*args)


def _bias_kernel(rb_ref, oh_ref, o_ref):
    o_ref[...] = jnp.dot(rb_ref[...], oh_ref[...], precision=lax.Precision.HIGHEST,
                         preferred_element_type=F32)


def _bias_table(rel_bias):
    i = jnp.arange(BLOCK)[:, None]
    j = jnp.arange(3 * BLOCK)[None, :]
    rel = j - BLOCK - i
    nb2 = N_BUCKETS // 2
    max_exact = nb2 // 2
    n = jnp.abs(rel)
    large = max_exact + (jnp.log(jnp.maximum(n, 1).astype(F32) / max_exact)
                         / math.log(MAX_DIST / max_exact) * (nb2 - max_exact)).astype(jnp.int32)
    large = jnp.minimum(large, nb2 - 1)
    bucket = jnp.where(rel > 0, nb2, 0) + jnp.where(n < max_exact, n, large)
    onehot = (bucket.reshape(1, -1) == jnp.arange(N_BUCKETS)[:, None]).astype(F32)
    cols = onehot.shape[1]
    tc = cols // 4
    table = pl.pallas_call(
        _bias_kernel,
        grid=(4,),
        in_specs=[_const_spec((ATT_HEADS, N_BUCKETS)), pl.BlockSpec((N_BUCKETS, tc), lambda c: (0, c))],
        out_specs=pl.BlockSpec((ATT_HEADS, tc), lambda c: (0, c)),
        out_shape=jax.ShapeDtypeStruct((ATT_HEADS, cols), F32),
        compiler_params=_cparams("parallel"),
        name="att_bias_table",
    )(rel_bias.T, onehot)
    table = table.reshape(ATT_HEADS, BLOCK, 3 * BLOCK)
    table = jnp.where((n <= WINDOW)[None], table, -jnp.inf)
    table = table.reshape(KV_HEADS, GQA_GROUP, BLOCK, 3 * BLOCK)
    return table.transpose(0, 3, 1, 2).reshape(KV_HEADS, 3 * BLOCK, GQA_GROUP * BLOCK)


def _attn_kernel(q_ref, km_ref, kp_ref, kn_ref, vm_ref, vp_ref, vn_ref, bias_ref, sink_ref, o_ref, *, nsub):
    i = pl.program_id(1)
    neg = -jnp.inf
    pen_first = jnp.where(i == 0, neg, 0.0)
    pen_last = jnp.where(i == pl.num_programs(1) - 1, neg, 0.0)
    kwin = jnp.concatenate([kp_ref[0], km_ref[0], kn_ref[0]], axis=0)
    vwin = jnp.concatenate([vp_ref[0], vm_ref[0], vn_ref[0]], axis=1)
    lane = lax.broadcasted_iota(jnp.int32, (BLOCK, LANES), 1)
    lo_half = lane < HEAD_DIM
    ones = jnp.ones((SUM_ROWS, 3 * BLOCK), BF16)
    units = [(s, g) for s in range(nsub) for g in range(KV_HEADS)]

    def scores(s, g):
        qs = q_ref[0, s * BLOCK:(s + 1) * BLOCK, :]
        rows = []
        for pr in range(2):
            qp = qs[:, (2 * g + pr) * LANES:(2 * g + pr + 1) * LANES]
            rows.append(jnp.where(lo_half, qp, jnp.zeros_like(qp)))
            rows.append(jnp.where(lo_half, jnp.zeros_like(qp), qp))
        qg = jnp.concatenate(rows, axis=0)
        kg = kwin[s * BLOCK:(s + 3) * BLOCK, g * LANES:(g + 1) * LANES]
        t = lax.dot_general(kg, qg, _NT, preferred_element_type=F32) + bias_ref[g]
        if s == 0:
            t = jnp.concatenate([t[:BLOCK] + pen_first, t[BLOCK:]], axis=0)
        if s == nsub - 1:
            t = jnp.concatenate([t[:2 * BLOCK], t[2 * BLOCK:] + pen_last], axis=0)
        return t

    def pv(s, g, p):
        vg = jnp.concatenate([vwin[g * HEAD_DIM:(g + 1) * HEAD_DIM, s * BLOCK:(s + 3) * BLOCK], ones],
                             axis=0)
        return jnp.dot(vg, p, preferred_element_type=F32)

    for u0 in range(0, len(units), ATT_UNITS):
        batch = units[u0:u0 + ATT_UNITS]
        sc = [scores(s, g) for s, g in batch]
        m = [jnp.maximum(jnp.max(t, axis=0, keepdims=True), sink_ref[g]) for t, (_, g) in zip(sc, batch)]
        p = [jnp.exp2(t - mm).astype(BF16) for t, mm in zip(sc, m)]
        o = [pv(s, g, pp) for (s, g), pp in zip(batch, p)]
        for (s, g), oo, mm in zip(batch, o, m):
            on = oo[:HEAD_DIM] / (oo[HEAD_DIM:HEAD_DIM + 1] + jnp.exp2(sink_ref[g] - mm))
            for pr in range(2):
                pair = jnp.concatenate([on[:, (2 * pr) * BLOCK:(2 * pr + 1) * BLOCK],
                                        on[:, (2 * pr + 1) * BLOCK:(2 * pr + 2) * BLOCK]], axis=0)
                o_ref[0, s * BLOCK:(s + 1) * BLOCK, (2 * g + pr) * LANES:(2 * g + pr + 1) * LANES] = (
                    pair.T.astype(o_ref.dtype))


SUM_ROWS = 16
ATT_UNITS = 8
ATT_TQ = 1024


def _attention(q, kd, vt, bias_t, sink_t, tq):
    b, l, _ = q.shape
    nsub = tq // BLOCK
    nblk = l // BLOCK
    kw = 2 * KV_W
    main = lambda w: pl.BlockSpec((1, tq, w), lambda bi, i: (bi, i, 0))
    prev = pl.BlockSpec((1, BLOCK, kw), lambda bi, i: (bi, jnp.maximum(i * nsub - 1, 0), 0))
    nxt = pl.BlockSpec((1, BLOCK, kw), lambda bi, i: (bi, jnp.minimum((i + 1) * nsub, nblk - 1), 0))
    vmain = pl.BlockSpec((1, KV_W, tq), lambda bi, i: (bi, 0, i))
    vprev = pl.BlockSpec((1, KV_W, BLOCK), lambda bi, i: (bi, 0, jnp.maximum(i * nsub - 1, 0)))
    vnxt = pl.BlockSpec((1, KV_W, BLOCK), lambda bi, i: (bi, 0, jnp.minimum((i + 1) * nsub, nblk - 1)))
    return pl.pallas_call(
        functools.partial(_attn_kernel, nsub=nsub),
        grid=(b, l // tq),
        in_specs=[main(ATT_W), main(kw), prev, nxt, vmain, vprev, vnxt,
                  _const_spec(bias_t.shape), _const_spec(sink_t.shape)],
        out_specs=main(ATT_W),
        out_shape=jax.ShapeDtypeStruct((b, l, ATT_W), BF16),
        compiler_params=_cparams("parallel", "parallel"),
        name="window_attn",
    )(q, kd, kd, kd, vt, vt, vt, bias_t, sink_t)


FF_CHUNK = 256


def _ffn_kernel(x_ref, yh_ref, ya_ref, hg_ref, ag_ref, wo_ref, g2_ref, wgu_ref, wd_ref, o_ref, act_ref, *, pairs, nt):
    def rms(t, g):
        return (t * lax.rsqrt(jnp.mean(t * t, axis=-1, keepdims=True) + EPS) * g).astype(BF16)

    if len(yh_ref.shape) == 3:
        yh = yh_ref[0]
    else:
        yh = jnp.concatenate([yh_ref[0, c] for c in range(SLABS)], axis=1)
    if pairs:
        lo, hi = _unpack_pair(yh)
        yh = jnp.where(pl.program_id(0) // nt < pairs, lo, hi)
    mixed = jnp.concatenate([rms(yh, hg_ref[...]), rms(ya_ref[...].astype(F32), ag_ref[...])], axis=1)
    h = x_ref[...] + jnp.dot(mixed, wo_ref[...], preferred_element_type=F32)
    hn = rms(h, g2_ref[...])
    for c in range(D_FF // FF_CHUNK):
        lo = c * FF_CHUNK
        gate = jnp.dot(hn, wgu_ref[:, lo:lo + FF_CHUNK], preferred_element_type=F32)
        up = jnp.dot(hn, wgu_ref[:, D_FF + lo:D_FF + lo + FF_CHUNK], preferred_element_type=F32)
        act_ref[:, lo:lo + FF_CHUNK] = (gate / (1.0 + jnp.exp(-gate)) * up).astype(BF16)
    o_ref[...] = h + jnp.dot(act_ref[...], wd_ref[...], preferred_element_type=F32)


def _ffn(x2d, yh, ya, hg, ag, wo_b, g2, wgu_b, wd_b, tm):
    t = x2d.shape[0]
    nt = yh.shape[-2] // tm
    pairs = yh.shape[0] if yh.dtype == jnp.uint32 else 0
    row = lambda w: pl.BlockSpec((tm, w), lambda i: (i, 0))
    if yh.ndim == 3:
        yspec = pl.BlockSpec((1, tm, HY_CH), lambda i: ((i // nt) % yh.shape[0], i % nt, 0))
    else:
        yspec = pl.BlockSpec((1, SLABS, tm, LANES), lambda i: ((i // nt) % yh.shape[0], 0, i % nt, 0))
    return pl.pallas_call(
        functools.partial(_ffn_kernel, pairs=pairs, nt=nt),
        grid=(t // tm,),
        in_specs=[row(D_MODEL), yspec, row(ATT_W), _const_spec((1, HY_CH)), _const_spec((1, ATT_W)),
                  _const_spec((D_MODEL, D_MODEL)), _const_spec((1, D_MODEL)),
                  _const_spec((D_MODEL, 2 * D_FF)), _const_spec((D_FF, D_MODEL))],
        out_specs=row(D_MODEL),
        out_shape=jax.ShapeDtypeStruct((t, D_MODEL), F32),
        scratch_shapes=[pltpu.VMEM((tm, D_FF), BF16)],
        compiler_params=_cparams("parallel"),
        name="outproj_swiglu",
    )(x2d, yh, ya, hg, ag, wo_b, g2, wgu_b, wd_b)


def _dft_tables(seq_len, paired, kblk):
    n = 2 * seq_len
    n1 = n // FFT_N2
    a = n1 // 2
    kk = np.arange(n1)[:, None]
    th_half = 2.0 * np.pi * ((kk * np.arange(a)[None, :]) % n1) / n1
    c, s = np.cos(th_half), np.sin(th_half)

    def blocked(re, im):
        return np.concatenate([np.concatenate([re[i:i + kblk], im[i:i + kblk]], axis=0)
                               for i in range(0, re.shape[0], kblk)], axis=0)

    def real_input_rows(cc, ss):
        im = -ss[:a].copy()
        im[0] = cc[a]
        return cc[:a], im

    if paired:
        fa = blocked(np.concatenate([c, s], axis=1), np.concatenate([-s, c], axis=1))
        fc = fa.T
        faf = blocked(c, -s)
    else:
        fa = blocked(*real_input_rows(c, s))
        re_cols = np.concatenate([c[:1], 2.0 * c[1:a]], axis=0)
        im_cols = np.concatenate([c[a:a + 1], -2.0 * s[1:a]], axis=0)
        fc = blocked(re_cols, im_cols).T
        faf = fa
    to = lambda m: jnp.asarray(m.astype(np.float32)).astype(BF16)
    kron = None
    if paired and 2 * n1 * N2_TILE <= PAIR_KRON_ROWS:
        eye = np.eye(N2_TILE)
        kron = (to(np.kron(fa, eye)), to(np.kron(fc, eye)))
    return to(fa), to(fc), to(faf), kron


def _edge_matrices(g, half, kb):
    zero = jnp.zeros_like(g[0])
    z = zero[:, :FFT_N2]
    col = lambda m: m[:, :FFT_N2]
    sa = jnp.concatenate([col(g[0]), z], axis=1)
    sb = jnp.concatenate([z, col(g[half])], axis=1)
    blocks = [jnp.stack([sa, sb, sa.T, sb.T])]
    blocks += [jnp.stack([g[i], zero, g[i].T, zero]) for i in range(kb, half, kb)]
    return jnp.stack(blocks)


def _twiddle_tables(seq_len, k1_count):
    n = 2 * seq_len
    n1 = n // FFT_N2
    k1 = np.arange(k1_count, dtype=np.int64)[:, None, None]
    k2 = np.arange(FFT_N2, dtype=np.int64)[None, :, None]
    n2 = np.arange(FFT_N2, dtype=np.int64)[None, None, :]
    ang = ((n2 * (k2 * n1 + k1)) % n) * (-2.0 * np.pi / n)
    gr, gi = np.cos(ang), np.sin(ang)
    g = np.concatenate([np.concatenate([gr, -gi], axis=2), np.concatenate([gi, gr], axis=2)], axis=1)
    g = g.astype(np.float32)
    return jnp.asarray(g.astype(BF16)), jnp.asarray(np.swapaxes(g, 1, 2).astype(BF16))


def _filter_features(seq_len):
    t_idx = jnp.arange(seq_len, dtype=F32)[:, None]
    t_norm = t_idx / max(seq_len - 1, 1)
    bands = jnp.linspace(1e-4, N_BANDS - 1, N_BANDS, dtype=F32)
    w = (2.0 * math.pi) * t_idx * bands[None, :] / seq_len
    return jnp.concatenate([t_norm, jnp.cos(w), -jnp.sin(w)], axis=-1)


def _abs_deltas():
    min_decay = math.log(DECAY_TARGET) / FAST_DECAY_PCT
    max_decay = math.log(DECAY_TARGET) / SLOW_DECAY_PCT
    return jnp.abs(jnp.linspace(min_decay, max_decay, HY_CH, dtype=F32))[None, :]


def _pick(n, target):
    t = min(n, target)
    while n % t:
        t //= 2
    return t


def _layer(x, lw, shared, st):
    b, l, _ = x.shape
    t = b * l
    x2d = x.reshape(t, D_MODEL)
    tm = _pick(l, FFN_TM)
    ux, q, kd, vt = _inproj(x2d, lw["norm1"], lw["w_in"], lw["conv_w"], lw["conv_b"],
                            shared["bq"], shared["bk"], lw["qg"], lw["kg"], _pick(l, INPROJ_TM), l)
    r3 = lambda a: a.reshape(b, l, a.shape[-1])

    n1 = 2 * l // FFT_N2
    a = n1 // 2
    nz = st["nz"]
    p = b // nz
    ux5 = ux.reshape(nz, p, a, FFT_N2, HY_CH)
    mb = _pick(st["fa"].shape[0], STAGE_MB)
    kf, kfh = lw["kf"][st["key"]]
    if st["kron"] is not None:
        ah = _pair_a(ux5, st["kron"][0])
        vv = _stage_b(ah, kf, st["g"], st["gt"], _pick(n1, STAGE_KB))
        yh = _pair_c(vv, st["kron"][1], ux5, lw["skip"]).reshape(p, l, HY_CH)
    else:
        ah = _stage_a(ux5, st["fa"], mb)
        vv = _stage_b(ah, kf, st["g"], st["gt"], _pick(ah.shape[2], STAGE_KB), st["edge"], kfh)
        yh = _stage_c(vv, st["fc"], ux5, lw["skip"], mb).reshape(p, SLABS, l, LANES)

    ya = _attention(r3(q), r3(kd), vt, shared["bias"], lw["sink_t"], _pick(l, ATT_TQ))
    ya = ya.reshape(t, ATT_W)

    out = _ffn(x2d, yh, ya, lw["hy_gain"], lw["at_gain"], lw["w_out"], lw["norm2"],
               lw["w_gate_up"], lw["w_down"], tm)
    return out.reshape(b, l, D_MODEL)


def _filter_spectrum(lw_raw, st, seq_len):
    n = 2 * seq_len
    a = seq_len // FFT_N2
    taps_f, taps_b = _filter_taps(st["zfeat"], lw_raw["f_w1"], lw_raw["f_b1"], lw_raw["f_freq"], lw_raw["f_w2"],
                                  lw_raw["f_b2"], lw_raw["f_w3"], st["absdelta"], _pick(seq_len, 512))
    mb = _pick(st["faf"].shape[0], STAGE_MB)
    ahf = _stage_a(taps_f.reshape(1, 1, a, FFT_N2, HY_CH), st["faf"], mb)
    ahb = _stage_a(taps_b.reshape(1, 1, a, FFT_N2, HY_CH), st["faf"], mb)
    out = _stage_bf(ahf, ahb, st["g"], _pick(ahf.shape[2], STAGE_KB), 1.0 / n, st["edge"])
    return (out[0], out[1]) if st["edge"] is not None else (out[0], None)


def kernel(x_prompt, x_sample, norm1, w_in, conv_w, conv_b, f_w1, f_b1, f_freq, f_w2, f_b2, f_w3,
           hy_skip, q_gain, k_gain, sink, rel_bias, hy_gain, at_gain, w_out, norm2, w_gate_up, w_down):
    depth = norm1.shape[0]
    streams = {}
    for key, x in (("prompt", x_prompt), ("sample", x_sample)):
        b, l, _ = x.shape
        paired = b % 2 == 0
        n1 = 2 * l // FFT_N2
        fa, fc, faf, kron = _dft_tables(l, paired, _pick(2 * n1 if paired else n1, STAGE_MB) // 2)
        g, gt = _twiddle_tables(l, n1 if paired else n1 // 2 + 1)
        streams[key] = dict(key=key, nz=2 if paired else 1, fa=fa, fc=fc, faf=faf, kron=kron, g=g, gt=gt,
                            edge=None if paired else _edge_matrices(g, n1 // 2, _pick(n1 // 2, STAGE_KB)),
                            zfeat=_filter_features(l), absdelta=_abs_deltas())

    head_id = np.arange(ATT_W) // HEAD_DIM
    shared = dict(
        bq=jnp.asarray((head_id[:, None] == head_id[None, :]).astype(np.float32) / HEAD_DIM).astype(BF16),
        bk=jnp.asarray((head_id[:KV_W, None] == head_id[None, :KV_W]).astype(np.float32) / HEAD_DIM).astype(BF16),
        bias=_bias_table(rel_bias) * LOG2E,
    )

    y_prompt, y_sample = x_prompt, x_sample
    for li in range(depth):
        raw = dict(f_w1=f_w1[li], f_b1=f_b1[li][None, :], f_freq=f_freq[li][None, :], f_w2=f_w2[li],
                   f_b2=f_b2[li][None, :], f_w3=f_w3[li])
        lw = dict(
            norm1=norm1[li][None, :], w_in=w_in[li].astype(BF16),
            qg=jnp.tile(q_gain[li], ATT_HEADS)[None, :] * (HEAD_DIM ** -0.5 * LOG2E),
            kg=jnp.tile(k_gain[li], KV_HEADS)[None, :],
            conv_w=conv_w[li], conv_b=conv_b[li][None, :],
            skip=hy_skip[li][None, :],
            sink_t=jnp.repeat(sink[li] * LOG2E, BLOCK).reshape(KV_HEADS, 1, GQA_GROUP * BLOCK),
            hy_gain=hy_gain[li][None, :], at_gain=at_gain[li][None, :],
            w_out=w_out[li].astype(BF16), norm2=norm2[li][None, :],
            w_gate_up=w_gate_up[li].astype(BF16), w_down=w_down[li].astype(BF16),
        )
        lw["kf"] = {key: _filter_spectrum(raw, st, {"prompt": x_prompt, "sample": x_sample}[key].shape[1])
                    for key, st in streams.items()}
        y_prompt = _layer(y_prompt, lw, shared, streams["prompt"])
        y_sample = _layer(y_sample, lw, shared, streams["sample"])
    return (y_prompt, y_sample)
```

```python
import functools
import math

import numpy as np
import jax
import jax.numpy as jnp
from jax import lax
from jax.experimental import pallas as pl
from jax.experimental.pallas import tpu as pltpu

F32 = jnp.float32
BF16 = jnp.bfloat16

D_MODEL = 1024
ATT_HEADS = 8
KV_HEADS = 2
HEAD_DIM = 64
GQA_GROUP = ATT_HEADS // KV_HEADS
ATT_W = ATT_HEADS * HEAD_DIM
KV_W = KV_HEADS * HEAD_DIM
WINDOW = 128
BLOCK = 128
N_BUCKETS = 32
MAX_DIST = 128
HY_CH = D_MODEL - ATT_W
FILTER_HIDDEN = 64
N_BANDS = 16
POS_DIM = 1 + 2 * N_BANDS
FAST_DECAY_PCT = 0.3
SLOW_DECAY_PCT = 1.5
DECAY_TARGET = 1e-2
IN_W = 3 * HY_CH + (ATT_HEADS + 2 * KV_HEADS) * HEAD_DIM
D_FF = -(-8 * D_MODEL // (3 * 256)) * 256
EPS = 1e-6
LOG2E = math.log2(math.e)

FFT_N2 = 64
LANES = 128
N2_TILE = 8
SLABS = HY_CH // LANES
STAGE_MB = 512
STAGE_KB = 32
PAIR_KRON_ROWS = 1024
PAIR_STEP_ROWS = 64
STAGE_B_CHAINS = 8
INPROJ_TM = 512
FFN_TM = 512
VMEM_LIMIT = 56 * 1024 * 1024

_NT = (((1,), (1,)), ((), ()))


def _cparams(*sem):
    return pltpu.CompilerParams(dimension_semantics=sem, vmem_limit_bytes=VMEM_LIMIT)


def _const_spec(shape):
    nd = len(shape)
    return pl.BlockSpec(shape, lambda *_: (0,) * nd, pipeline_mode=pl.Buffered(1))


HALO = 16


def _inproj_kernel(x_ref, xp_ref, xn_ref, g1_ref, w_ref, cw_ref, cb_ref, bq_ref, bk_ref, qg_ref, kg_ref,
                   ux_ref, q_ref, k_ref, v_ref, he_ref, qkv_ref, *, nt, ntiles):
    i = pl.program_id(0)
    tm = x_ref.shape[0]

    @pl.when(i == 0)
    def _():
        he_ref[...] = jnp.zeros_like(he_ref)
        qkv_ref[...] = jnp.zeros_like(qkv_ref)

    v = qkv_ref[:, ATT_W + KV_W:]
    v_ref[0] = v.T.astype(BF16)

    k = qkv_ref[:, ATT_W:ATT_W + KV_W]
    ms = jnp.dot((k * k).astype(BF16), bk_ref[...], preferred_element_type=F32)
    kn = k * lax.rsqrt(ms + EPS) * kg_ref[...]
    lo_half = lax.broadcasted_iota(jnp.int32, kn.shape, 1) < HEAD_DIM
    kr = pltpu.roll(kn, HEAD_DIM, axis=1)
    k_ref[...] = jnp.concatenate([jnp.where(lo_half, kn, kr), jnp.where(lo_half, kr, kn)], axis=1).astype(BF16)

    q = qkv_ref[:, 0:ATT_W]
    ms = jnp.dot((q * q).astype(BF16), bq_ref[...], preferred_element_type=F32)
    q_ref[...] = (q * lax.rsqrt(ms + EPS) * qg_ref[...]).astype(BF16)

    rows = slice(HALO, HALO + tm)

    def conv(j):
        c = slice(j * HY_CH, (j + 1) * HY_CH)
        he = he_ref[:, c]
        return (pltpu.roll(he, 1, axis=0)[rows] * cw_ref[0:1, c] + he[rows] * cw_ref[1:2, c]
                + pltpu.roll(he, tm + 2 * HALO - 1, axis=0)[rows] * cw_ref[2:3, c] + cb_ref[0:1, c])

    ux_ref[...] = _pack_pair(conv(1) * conv(2), conv(0))

    ti = jnp.minimum(i, ntiles - 1)
    xp = jnp.where(ti % nt == 0, 0.0, xp_ref[...])
    xn = jnp.where(ti % nt == nt - 1, 0.0, xn_ref[...])
    xe = jnp.concatenate([xp, x_ref[...], xn], axis=0)
    xe = (xe * lax.rsqrt(jnp.mean(xe * xe, axis=-1, keepdims=True) + EPS) * g1_ref[...]).astype(BF16)
    he_ref[...] = jnp.dot(xe, w_ref[:, 0:3 * HY_CH], preferred_element_type=F32)
    qkv_ref[...] = jnp.dot(xe[HALO:HALO + tm], w_ref[:, 3 * HY_CH:], preferred_element_type=F32)


def _inproj(x2d, g1, w_in_b, conv_w, conv_b, bq, bk, qg, kg, tm, seq_len):
    t = x2d.shape[0]
    nt = seq_len // tm
    ntiles = t // tm
    nh = tm // HALO
    cur = lambda i: jnp.minimum(i, ntiles - 1)
    done = lambda i: jnp.maximum(i - 1, 0)
    row_in = pl.BlockSpec((tm, D_MODEL), lambda i: (cur(i), 0))
    prev = pl.BlockSpec((HALO, D_MODEL), lambda i: (jnp.maximum(cur(i) * nh - 1, 0), 0))
    nxt = pl.BlockSpec((HALO, D_MODEL), lambda i: (jnp.minimum((cur(i) + 1) * nh, t // HALO - 1), 0))
    row_out = lambda w: pl.BlockSpec((tm, w), lambda i: (done(i), 0))
    return pl.pallas_call(
        functools.partial(_inproj_kernel, nt=nt, ntiles=ntiles),
        grid=(ntiles + 1,),
        in_specs=[row_in, prev, nxt, _const_spec((1, D_MODEL)), _const_spec((D_MODEL, IN_W)),
                  _const_spec((3, 3 * HY_CH)), _const_spec((1, 3 * HY_CH)),
                  _const_spec((ATT_W, ATT_W)), _const_spec((KV_W, KV_W)),
                  _const_spec((1, ATT_W)), _const_spec((1, KV_W))],
        out_specs=[row_out(HY_CH), row_out(ATT_W), row_out(2 * KV_W),
                   pl.BlockSpec((1, KV_W, tm), lambda i: (done(i) // nt, 0, done(i) % nt))],
        out_shape=[jax.ShapeDtypeStruct((t, HY_CH), jnp.uint32),
                   jax.ShapeDtypeStruct((t, ATT_W), BF16), jax.ShapeDtypeStruct((t, 2 * KV_W), BF16),
                   jax.ShapeDtypeStruct((t // seq_len, KV_W, seq_len), BF16)],
        scratch_shapes=[pltpu.VMEM((tm + 2 * HALO, 3 * HY_CH), F32), pltpu.VMEM((tm, IN_W - 3 * HY_CH), F32)],
        compiler_params=_cparams("arbitrary"),
        name="inproj",
    )(x2d, x2d, x2d, g1, w_in_b, conv_w, conv_b, bq, bk, qg, kg)


def _stage_a_kernel(*refs, nz, a, mb, packed_in):
    x_refs, f_ref, o_ref = refs[:SLABS], refs[SLABS], refs[SLABS + 1]
    kblk = mb // 2
    xs = [r.reshape(nz * a * N2_TILE, LANES) for r in x_refs]
    o2 = o_ref.reshape(SLABS * kblk * N2_TILE, LANES)
    f = f_ref[...]

    def gather(s):
        x = jnp.concatenate(
            [jnp.concatenate([xs[c][pl.ds(z * a * N2_TILE + s, a, stride=N2_TILE), :] for z in range(nz)], axis=0)
             for c in range(SLABS)], axis=1)
        return (_unpack_pair(x)[0] if packed_in else x).astype(BF16)

    r = [jnp.dot(f, gather(s), preferred_element_type=F32) for s in range(N2_TILE)]
    for s in range(N2_TILE):
        packed = _pack_pair(r[s][:kblk], r[s][kblk:])
        for c in range(SLABS):
            o2[pl.ds(c * kblk * N2_TILE + s, kblk, stride=N2_TILE), :] = packed[:, c * LANES:(c + 1) * LANES]


def _pack_pair(re, im):
    rb = lax.bitcast_convert_type(re.astype(BF16).astype(F32), jnp.uint32)
    ib = lax.bitcast_convert_type(im.astype(BF16).astype(F32), jnp.uint32)
    return (rb >> 16) | ib


def _unpack_pair(p):
    re = lax.bitcast_convert_type(p << 16, F32)
    im = lax.bitcast_convert_type(p & jnp.uint32(0xFFFF0000), F32)
    return re, im


def _stage_a(x5, fmat, mb):
    nz, p, a, n2, _ = x5.shape
    m = fmat.shape[0]
    kblk = mb // 2
    xspec = lambda c: pl.BlockSpec((nz, 1, a, N2_TILE, LANES), lambda pi, j, mi: (0, pi, 0, j, c))
    return pl.pallas_call(
        functools.partial(_stage_a_kernel, nz=nz, a=a, mb=mb, packed_in=x5.dtype == jnp.uint32),
        grid=(p, n2 // N2_TILE, m // mb),
        in_specs=[xspec(c) for c in range(SLABS)] + [pl.BlockSpec((mb, nz * a), lambda pi, j, mi: (mi, 0))],
        out_specs=pl.BlockSpec((1, SLABS, kblk, N2_TILE, LANES), lambda pi, j, mi: (pi, 0, mi, j, 0)),
        out_shape=jax.ShapeDtypeStruct((p, SLABS, m // 2, n2, LANES), jnp.uint32),
        compiler_params=_cparams("parallel", "parallel", "parallel"),
        name="hy_stage_a",
    )(*([x5] * SLABS), fmat)


def _pair_a_kernel(ux_ref, f_ref, o_ref):
    nz, _, a, tiles, c = ux_ref.shape
    n1 = o_ref.shape[1]
    for t in range(tiles // N2_TILE):
        sl = slice(t * N2_TILE, (t + 1) * N2_TILE)
        u = _unpack_pair(ux_ref[:, 0, :, sl, :].reshape(nz * a * N2_TILE, c))[0].astype(BF16)
        r = jnp.dot(f_ref[...], u, preferred_element_type=F32)
        half = r.shape[0] // 2
        o_ref[0, :, sl, :] = _pack_pair(r[:half], r[half:]).reshape(n1, N2_TILE, c)


def _pair_a(ux5, fx):
    nz, p, a, n2, c = ux5.shape
    n1 = fx.shape[0] // (2 * N2_TILE)
    rows = _pick(n2, PAIR_STEP_ROWS)
    return pl.pallas_call(
        _pair_a_kernel,
        grid=(p, n2 // rows),
        in_specs=[pl.BlockSpec((nz, 1, a, rows, c), lambda pi, j: (0, pi, 0, j, 0)), _const_spec(fx.shape)],
        out_specs=pl.BlockSpec((1, n1, rows, c), lambda pi, j: (pi, 0, j, 0)),
        out_shape=jax.ShapeDtypeStruct((p, n1, n2, c), jnp.uint32),
        compiler_params=_cparams("parallel", "parallel"),
        name="hy_stage_a",
    )(ux5, fx)


def _pair_c_kernel(v_ref, f_ref, ux_ref, skip_ref, o_ref):
    nz, _, a, tiles, c = ux_ref.shape
    n1 = v_ref.shape[1]
    for t in range(tiles // N2_TILE):
        sl = slice(t * N2_TILE, (t + 1) * N2_TILE)
        vr, vi = _unpack_pair(v_ref[0, :, sl, :].reshape(n1 * N2_TILE, c))
        y = jnp.dot(f_ref[...], jnp.concatenate([vr, vi], axis=0).astype(BF16), preferred_element_type=F32)
        half = y.shape[0] // 2
        uu, xx = _unpack_pair(ux_ref[:, 0, :, sl, :].reshape(2 * half, c))
        val = xx * (y + skip_ref[...] * uu)
        o_ref[0, :, sl, :] = _pack_pair(val[:half], val[half:]).reshape(a, N2_TILE, c)


def _pair_c(v4, fx, ux5, skip):
    nz, p, a, n2, c = ux5.shape
    n1 = v4.shape[1]
    rows = _pick(n2, PAIR_STEP_ROWS)
    return pl.pallas_call(
        _pair_c_kernel,
        grid=(p, n2 // rows),
        in_specs=[pl.BlockSpec((1, n1, rows, c), lambda pi, j: (pi, 0, j, 0)), _const_spec(fx.shape),
                  pl.BlockSpec((nz, 1, a, rows, c), lambda pi, j: (0, pi, 0, j, 0)), _const_spec((1, c))],
        out_specs=pl.BlockSpec((1, a, rows, c), lambda pi, j: (pi, 0, j, 0)),
        out_shape=jax.ShapeDtypeStruct((p, a, n2, c), jnp.uint32),
        compiler_params=_cparams("parallel", "parallel"),
        name="hy_stage_c",
    )(v4, fx, ux5, skip)


MXU_COLS = 256
SLABS_PER_DOT = MXU_COLS // LANES


def _slab_rows(a_ref, j, h):
    if len(a_ref.shape) == 4:
        re, im = _unpack_pair(a_ref[0, j, :, h * MXU_COLS:(h + 1) * MXU_COLS])
        return jnp.concatenate([re, im], axis=0).astype(BF16)
    parts = [_unpack_pair(a_ref[0, h * SLABS_PER_DOT + c, j]) for c in range(SLABS_PER_DOT)]
    return jnp.concatenate([jnp.concatenate([re, im], axis=0) for re, im in parts], axis=1).astype(BF16)


def _spectral_mul(u, kr, ki, n2):
    ur, ui = u[:n2], u[n2:]
    return jnp.concatenate([ur * kr - ui * ki, ur * ki + ui * kr], axis=0).astype(BF16)


def _stage_b_kernel(a_ref, kf_ref, g_ref, gt_ref, *rest, herm):
    if herm:
        edge_ref, kfh_ref, o_ref = rest
    else:
        (o_ref,) = rest
    kb, n2 = a_ref.shape[-3], a_ref.shape[-2]

    def store(j, h, v):
        packed = _pack_pair(v[:n2], v[n2:])
        if len(o_ref.shape) == 4:
            o_ref[0, j, :, h * MXU_COLS:(h + 1) * MXU_COLS] = packed
        else:
            for c in range(SLABS_PER_DOT):
                o_ref[0, h * SLABS_PER_DOT + c, j] = packed[:, c * LANES:(c + 1) * LANES]

    dot = functools.partial(jnp.dot, preferred_element_type=F32)
    chains = [(j, h) for j in range(kb) for h in range(SLABS // SLABS_PER_DOT)]
    for c0 in range(0, len(chains), STAGE_B_CHAINS):
        batch = chains[c0:c0 + STAGE_B_CHAINS]
        fwd = []
        for j, h in batch:
            x = _slab_rows(a_ref, j, h)
            if herm and j == 0:
                fwd.append((dot(edge_ref[0, 0], x), dot(edge_ref[0, 1], x)))
            else:
                fwd.append((dot(g_ref[j], x),))
        prod = []
        for (j, h), us in zip(batch, fwd):
            lanes = slice(h * MXU_COLS, (h + 1) * MXU_COLS)
            ps = [_spectral_mul(us[0], kf_ref[0, j, :, lanes], kf_ref[1, j, :, lanes], n2)]
            if len(us) == 2:
                ps.append(_spectral_mul(us[1], kfh_ref[0, 0, :, lanes], kfh_ref[0, 1, :, lanes], n2))
            prod.append(ps)
        for (j, h), ps in zip(batch, prod):
            if len(ps) == 2:
                store(j, h, dot(edge_ref[0, 2], ps[0]) + dot(edge_ref[0, 3], ps[1]))
            else:
                store(j, h, dot(gt_ref[j], ps[0]))


def _stage_b(a5, kf, g, gt, kb, edge=None, kfh=None):
    p, k1n, n2 = a5.shape[0], a5.shape[-3], a5.shape[-2]
    herm = edge is not None
    if a5.ndim == 4:
        blk = pl.BlockSpec((1, kb, n2, HY_CH), lambda i, pi: (pi, i, 0, 0))
    else:
        blk = pl.BlockSpec((1, SLABS, kb, n2, LANES), lambda i, pi: (pi, 0, i, 0, 0))
    gspec = pl.BlockSpec((kb, 2 * n2, 2 * n2), lambda i, pi: (i, 0, 0))
    extra = [pl.BlockSpec((1,) + edge.shape[1:], lambda i, pi: (i, 0, 0, 0)),
             pl.BlockSpec((1,) + kfh.shape[1:], lambda i, pi: (i, 0, 0, 0))] if herm else []
    return pl.pallas_call(
        functools.partial(_stage_b_kernel, herm=herm),
        grid=(k1n // kb, p),
        in_specs=[blk, pl.BlockSpec((2, kb, n2, HY_CH), lambda i, pi: (0, i, 0, 0)), gspec, gspec] + extra,
        out_specs=blk,
        out_shape=jax.ShapeDtypeStruct(a5.shape, jnp.uint32),
        compiler_params=_cparams("parallel", "parallel"),
        name="hy_stage_b",
    )(a5, kf, g, gt, *([edge, kfh] if herm else []))


def _stage_bf_kernel(af_ref, ab_ref, g_ref, *rest, scale, herm):
    if herm:
        edge_ref, o_ref, oh_ref = rest
    else:
        (o_ref,) = rest
    kb, n2 = af_ref.shape[2], af_ref.shape[3]

    def spectrum(mat, j, h):
        conj_mat = jnp.concatenate([mat[:n2], -mat[n2:]], axis=0)
        x = jnp.concatenate([_slab_rows(af_ref, j, h), _slab_rows(ab_ref, j, h)], axis=0)
        return jnp.dot(jnp.concatenate([mat, conj_mat], axis=1), x, preferred_element_type=F32) * scale

    for j in range(kb):
        for h in range(SLABS // SLABS_PER_DOT):
            lanes = slice(h * MXU_COLS, (h + 1) * MXU_COLS)
            if herm and j == 0:
                u = spectrum(edge_ref[0, 0], 0, h)
                uh = spectrum(edge_ref[0, 1], 0, h)
                oh_ref[0, 0, :, lanes] = uh[:n2]
                oh_ref[0, 1, :, lanes] = uh[n2:]
            else:
                u = spectrum(g_ref[j], j, h)
            o_ref[0, j, :, lanes] = u[:n2]
            o_ref[1, j, :, lanes] = u[n2:]


def _stage_bf(a5f, a5b, g, kb, scale, edge=None):
    _, _, k1n, n2, _ = a5f.shape
    herm = edge is not None
    aspec = pl.BlockSpec((1, SLABS, kb, n2, LANES), lambda i: (0, 0, i, 0, 0))
    out_specs = [pl.BlockSpec((2, kb, n2, HY_CH), lambda i: (0, i, 0, 0))]
    out_shape = [jax.ShapeDtypeStruct((2, k1n, n2, HY_CH), F32)]
    if herm:
        out_specs.append(pl.BlockSpec((1, 2, n2, HY_CH), lambda i: (i, 0, 0, 0)))
        out_shape.append(jax.ShapeDtypeStruct((k1n // kb, 2, n2, HY_CH), F32))
    return pl.pallas_call(
        functools.partial(_stage_bf_kernel, scale=scale, herm=herm),
        grid=(k1n // kb,),
        in_specs=[aspec, aspec, pl.BlockSpec((kb, 2 * n2, 2 * n2), lambda i: (i, 0, 0))]
        + ([pl.BlockSpec((1,) + edge.shape[1:], lambda i: (i, 0, 0, 0))] if herm else []),
        out_specs=out_specs,
        out_shape=out_shape,
        compiler_params=_cparams("parallel"),
        name="hy_filter_spectrum",
    )(a5f, a5b, g, *([edge] if herm else []))


def _stage_c_kernel(*refs, nz, a, mb):
    v_ref, f_ref = refs[0], refs[1]
    ux_refs = refs[2:2 + SLABS]
    skip_ref, o_ref, acc_ref = refs[2 + SLABS:]
    mk = pl.program_id(2)
    kblk = mb // 2
    v2 = v_ref.reshape(SLABS * kblk * N2_TILE, LANES)

    @pl.when(mk == 0)
    def _():
        acc_ref[...] = jnp.zeros_like(acc_ref)

    for s in range(N2_TILE):
        packed = jnp.concatenate([v2[pl.ds(c * kblk * N2_TILE + s, kblk, stride=N2_TILE), :] for c in range(SLABS)],
                                 axis=1)
        vs = jnp.concatenate(_unpack_pair(packed), axis=0).astype(BF16)
        acc_ref[s] += jnp.dot(f_ref[...], vs, preferred_element_type=F32)

    @pl.when(mk == pl.num_programs(2) - 1)
    def _():
        o2 = o_ref.reshape(SLABS * a * N2_TILE, LANES)
        ux2 = [r.reshape(nz * a * N2_TILE, LANES) for r in ux_refs]
        for s in range(N2_TILE):
            y = acc_ref[s]
            for c in range(SLABS):
                vals = []
                for z in range(nz):
                    uu, xx = _unpack_pair(ux2[c][pl.ds(z * a * N2_TILE + s, a, stride=N2_TILE), :])
                    yc = y[z * a:(z + 1) * a, c * LANES:(c + 1) * LANES]
                    vals.append(xx * (yc + skip_ref[0:1, c * LANES:(c + 1) * LANES] * uu))
                o2[pl.ds(c * a * N2_TILE + s, a, stride=N2_TILE), :] = _pack_pair(*vals) if nz == 2 else vals[0]


def _stage_c(v5, fmat, ux5, skip, mb):
    nz, p, a, n2, _ = ux5.shape
    m = 2 * v5.shape[2]
    xspec = lambda c: pl.BlockSpec((nz, 1, a, N2_TILE, LANES), lambda pi, j, mk: (0, pi, 0, j, c))
    return pl.pallas_call(
        functools.partial(_stage_c_kernel, nz=nz, a=a, mb=mb),
        grid=(p, n2 // N2_TILE, m // mb),
        in_specs=[pl.BlockSpec((1, SLABS, mb // 2, N2_TILE, LANES), lambda pi, j, mk: (pi, 0, mk, j, 0)),
                  pl.BlockSpec((nz * a, mb), lambda pi, j, mk: (0, mk))]
        + [xspec(c) for c in range(SLABS)] + [_const_spec((1, HY_CH))],
        out_specs=pl.BlockSpec((1, SLABS, a, N2_TILE, LANES), lambda pi, j, mk: (pi, 0, 0, j, 0)),
        out_shape=jax.ShapeDtypeStruct((p, SLABS, a, n2, LANES), jnp.uint32 if nz == 2 else F32),
        scratch_shapes=[pltpu.VMEM((N2_TILE, nz * a, HY_CH), F32)],
        compiler_params=_cparams("parallel", "parallel", "arbitrary"),
        name="hy_stage_c",
    )(v5, fmat, *([ux5] * SLABS), skip)


def _filter_kernel(zt_ref, tn_ref, w1t_ref, b1_ref, fr_ref, w2t_ref, b2_ref, w3_ref, dl_ref, of_ref, ob_ref):
    hi = lax.Precision.HIGHEST
    fr = fr_ref[...]
    h = jnp.sin(fr * (jnp.dot(w1t_ref[...], zt_ref[...], precision=hi, preferred_element_type=F32) + b1_ref[...]))
    h = jnp.sin(fr * (jnp.dot(w2t_ref[...], h, precision=hi, preferred_element_type=F32) + b2_ref[...]))
    taps = jnp.dot(h.T.astype(BF16), w3_ref[...], preferred_element_type=F32)
    tr = taps.shape[0]
    decay = jnp.exp(-tn_ref[...] * dl_ref[...])
    of_ref[...] = taps[:, :HY_CH] * decay
    t = pl.program_id(0) * tr + lax.broadcasted_iota(jnp.int32, (tr, HY_CH), 0)
    ob_ref[...] = jnp.where(t == 0, 0.0, taps[:, HY_CH:] * decay)


def _filter_taps(zfeat, f_w1, f_b1, f_freq, f_w2, f_b2, f_w3, absdelta, tr):
    n = zfeat.shape[0]
    cs = lambda a: _const_spec(a.shape)
    args = (zfeat.T, zfeat[:, 0:1], f_w1.T, f_b1.T, f_freq.T, f_w2.T, f_b2.T, f_w3.astype(BF16), absdelta)
    out = pl.BlockSpec((tr, HY_CH), lambda i: (i, 0))
    return pl.pallas_call(
        _filter_kernel,
        grid=(n // tr,),
        in_specs=[pl.BlockSpec((POS_DIM, tr), lambda i: (0, i)), pl.BlockSpec((tr, 1), lambda i: (i, 0))]
        + [cs(a) for a in args[2:]],
        out_specs=[out, out],
        out_shape=[jax.ShapeDtypeStruct((n, HY_CH), F32)] * 2,
        compiler_params=_cparams("parallel"),
        name="hy_filter_taps",
    )(*args)


def _bias_kernel(rb_ref, oh_ref, o_ref):
    o_ref[...] = jnp.dot(rb_ref[...], oh_ref[...], precision=lax.Precision.HIGHEST,
                         preferred_element_type=F32)


def _bias_table(rel_bias):
    i = jnp.arange(BLOCK)[:, None]
    j = jnp.arange(3 * BLOCK)[None, :]
    rel = j - BLOCK - i
    nb2 = N_BUCKETS // 2
    max_exact = nb2 // 2
    n = jnp.abs(rel)
    large = max_exact + (jnp.log(jnp.maximum(n, 1).astype(F32) / max_exact)
                         / math.log(MAX_DIST / max_exact) * (nb2 - max_exact)).astype(jnp.int32)
    large = jnp.minimum(large, nb2 - 1)
    bucket = jnp.where(rel > 0, nb2, 0) + jnp.where(n < max_exact, n, large)
    onehot = (bucket.reshape(1, -1) == jnp.arange(N_BUCKETS)[:, None]).astype(F32)
    cols = onehot.shape[1]
    tc = cols // 4
    table = pl.pallas_call(
        _bias_kernel,
        grid=(4,),
        in_specs=[_const_spec((ATT_HEADS, N_BUCKETS)), pl.BlockSpec((N_BUCKETS, tc), lambda c: (0, c))],
        out_specs=pl.BlockSpec((ATT_HEADS, tc), lambda c: (0, c)),
        out_shape=jax.ShapeDtypeStruct((ATT_HEADS, cols), F32),
        compiler_params=_cparams("parallel"),
        name="att_bias_table",
    )(rel_bias.T, onehot)
    table = table.reshape(ATT_HEADS, BLOCK, 3 * BLOCK)
    table = jnp.where((n <= WINDOW)[None], table, -jnp.inf)
    table = table.reshape(KV_HEADS, GQA_GROUP, BLOCK, 3 * BLOCK)
    return table.transpose(0, 3, 1, 2).reshape(KV_HEADS, 3 * BLOCK, GQA_GROUP * BLOCK)


def _attn_kernel(q_ref, km_ref, kp_ref, kn_ref, vm_ref, vp_ref, vn_ref, bias_ref, sink_ref, o_ref, *, nsub):
    i = pl.program_id(1)
    neg = -jnp.inf
    pen_first = jnp.where(i == 0, neg, 0.0)
    pen_last = jnp.where(i == pl.num_programs(1) - 1, neg, 0.0)
    kwin = jnp.concatenate([kp_ref[0], km_ref[0], kn_ref[0]], axis=0)
    vwin = jnp.concatenate([vp_ref[0], vm_ref[0], vn_ref[0]], axis=1)
    lane = lax.broadcasted_iota(jnp.int32, (BLOCK, LANES), 1)
    lo_half = lane < HEAD_DIM
    ones = jnp.ones((SUM_ROWS, 3 * BLOCK), BF16)
    units = [(s, g) for s in range(nsub) for g in range(KV_HEADS)]

    def scores(s, g):
        qs = q_ref[0, s * BLOCK:(s + 1) * BLOCK, :]
        rows = []
        for pr in range(2):
            qp = qs[:, (2 * g + pr) * LANES:(2 * g + pr + 1) * LANES]
            rows.append(jnp.where(lo_half, qp, jnp.zeros_like(qp)))
            rows.append(jnp.where(lo_half, jnp.zeros_like(qp), qp))
        qg = jnp.concatenate(rows, axis=0)
        kg = kwin[s * BLOCK:(s + 3) * BLOCK, g * LANES:(g + 1) * LANES]
        t = lax.dot_general(kg, qg, _NT, preferred_element_type=F32) + bias_ref[g]
        if s == 0:
            t = jnp.concatenate([t[:BLOCK] + pen_first, t[BLOCK:]], axis=0)
        if s == nsub - 1:
            t = jnp.concatenate([t[:2 * BLOCK], t[2 * BLOCK:] + pen_last], axis=0)
        return t

    def pv(s, g, p):
        vg = jnp.concatenate([vwin[g * HEAD_DIM:(g + 1) * HEAD_DIM, s * BLOCK:(s + 3) * BLOCK], ones],
                             axis=0)
        return jnp.dot(vg, p, preferred_element_type=F32)

    for u0 in range(0, len(units), ATT_UNITS):
        batch = units[u0:u0 + ATT_UNITS]
        sc = [scores(s, g) for s, g in batch]
        m = [jnp.maximum(jnp.max(t, axis=0, keepdims=True), sink_ref[g]) for t, (_, g) in zip(sc, batch)]
        p = [jnp.exp2(t - mm).astype(BF16) for t, mm in zip(sc, m)]
        o = [pv(s, g, pp) for (s, g), pp in zip(batch, p)]
        for (s, g), oo, mm in zip(batch, o, m):
            on = oo[:HEAD_DIM] / (oo[HEAD_DIM:HEAD_DIM + 1] + jnp.exp2(sink_ref[g] - mm))
            for pr in range(2):
                pair = jnp.concatenate([on[:, (2 * pr) * BLOCK:(2 * pr + 1) * BLOCK],
                                        on[:, (2 * pr + 1) * BLOCK:(2 * pr + 2) * BLOCK]], axis=0)
                o_ref[0, s * BLOCK:(s + 1) * BLOCK, (2 * g + pr) * LANES:(2 * g + pr + 1) * LANES] = (
                    pair.T.astype(o_ref.dtype))


SUM_ROWS = 16
ATT_UNITS = 8
ATT_TQ = 1024


def _attention(q, kd, vt, bias_t, sink_t, tq):
    b, l, _ = q.shape
    nsub = tq // BLOCK
    nblk = l // BLOCK
    kw = 2 * KV_W
    main = lambda w: pl.BlockSpec((1, tq, w), lambda bi, i: (bi, i, 0))
    prev = pl.BlockSpec((1, BLOCK, kw), lambda bi, i: (bi, jnp.maximum(i * nsub - 1, 0), 0))
    nxt = pl.BlockSpec((1, BLOCK, kw), lambda bi, i: (bi, jnp.minimum((i + 1) * nsub, nblk - 1), 0))
    vmain = pl.BlockSpec((1, KV_W, tq), lambda bi, i: (bi, 0, i))
    vprev = pl.BlockSpec((1, KV_W, BLOCK), lambda bi, i: (bi, 0, jnp.maximum(i * nsub - 1, 0)))
    vnxt = pl.BlockSpec((1, KV_W, BLOCK), lambda bi, i: (bi, 0, jnp.minimum((i + 1) * nsub, nblk - 1)))
    return pl.pallas_call(
        functools.partial(_attn_kernel, nsub=nsub),
        grid=(b, l // tq),
        in_specs=[main(ATT_W), main(kw), prev, nxt, vmain, vprev, vnxt,
                  _const_spec(bias_t.shape), _const_spec(sink_t.shape)],
        out_specs=main(ATT_W),
        out_shape=jax.ShapeDtypeStruct((b, l, ATT_W), BF16),
        compiler_params=_cparams("parallel", "parallel"),
        name="window_attn",
    )(q, kd, kd, kd, vt, vt, vt, bias_t, sink_t)


FF_CHUNK = 256


def _ffn_kernel(x_ref, yh_ref, ya_ref, hg_ref, ag_ref, wo_ref, g2_ref, wgu_ref, wd_ref, o_ref, act_ref, *, pairs, nt):
    def rms(t, g):
        return (t * lax.rsqrt(jnp.mean(t * t, axis=-1, keepdims=True) + EPS) * g).astype(BF16)

    if len(yh_ref.shape) == 3:
        yh = yh_ref[0]
    else:
        yh = jnp.concatenate([yh_ref[0, c] for c in range(SLABS)], axis=1)
    if pairs:
        lo, hi = _unpack_pair(yh)
        yh = jnp.where(pl.program_id(0) // nt < pairs, lo, hi)
    mixed = jnp.concatenate([rms(yh, hg_ref[...]), rms(ya_ref[...].astype(F32), ag_ref[...])], axis=1)
    h = x_ref[...] + jnp.dot(mixed, wo_ref[...], preferred_element_type=F32)
    hn = rms(h, g2_ref[...])
    for c in range(D_FF // FF_CHUNK):
        lo = c * FF_CHUNK
        gate = jnp.dot(hn, wgu_ref[:, lo:lo + FF_CHUNK], preferred_element_type=F32)
        up = jnp.dot(hn, wgu_ref[:, D_FF + lo:D_FF + lo + FF_CHUNK], preferred_element_type=F32)
        act_ref[:, lo:lo + FF_CHUNK] = (gate / (1.0 + jnp.exp(-gate)) * up).astype(BF16)
    o_ref[...] = h + jnp.dot(act_ref[...], wd_ref[...], preferred_element_type=F32)


def _ffn(x2d, yh, ya, hg, ag, wo_b, g2, wgu_b, wd_b, tm):
    t = x2d.shape[0]
    nt = yh.shape[-2] // tm
    pairs = yh.shape[0] if yh.dtype == jnp.uint32 else 0
    row = lambda w: pl.BlockSpec((tm, w), lambda i: (i, 0))
    if yh.ndim == 3:
        yspec = pl.BlockSpec((1, tm, HY_CH), lambda i: ((i // nt) % yh.shape[0], i % nt, 0))
    else:
        yspec = pl.BlockSpec((1, SLABS, tm, LANES), lambda i: ((i // nt) % yh.shape[0], 0, i % nt, 0))
    return pl.pallas_call(
        functools.partial(_ffn_kernel, pairs=pairs, nt=nt),
        grid=(t // tm,),
        in_specs=[row(D_MODEL), yspec, row(ATT_W), _const_spec((1, HY_CH)), _const_spec((1, ATT_W)),
                  _const_spec((D_MODEL, D_MODEL)), _const_spec((1, D_MODEL)),
                  _const_spec((D_MODEL, 2 * D_FF)), _const_spec((D_FF, D_MODEL))],
        out_specs=row(D_MODEL),
        out_shape=jax.ShapeDtypeStruct((t, D_MODEL), F32),
        scratch_shapes=[pltpu.VMEM((tm, D_FF), BF16)],
        compiler_params=_cparams("parallel"),
        name="outproj_swiglu",
    )(x2d, yh, ya, hg, ag, wo_b, g2, wgu_b, wd_b)


def _dft_tables(seq_len, paired, kblk):
    n = 2 * seq_len
    n1 = n // FFT_N2
    a = n1 // 2
    kk = np.arange(n1)[:, None]
    th_half = 2.0 * np.pi * ((kk * np.arange(a)[None, :]) % n1) / n1
    c, s = np.cos(th_half), np.sin(th_half)

    def blocked(re, im):
        return np.concatenate([np.concatenate([re[i:i + kblk], im[i:i + kblk]], axis=0)
                               for i in range(0, re.shape[0], kblk)], axis=0)

    def real_input_rows(cc, ss):
        im = -ss[:a].copy()
        im[0] = cc[a]
        return cc[:a], im

    if paired:
        fa = blocked(np.concatenate([c, s], axis=1), np.concatenate([-s, c], axis=1))
        fc = fa.T
        faf = blocked(c, -s)
    else:
        fa = blocked(*real_input_rows(c, s))
        re_cols = np.concatenate([c[:1], 2.0 * c[1:a]], axis=0)
        im_cols = np.concatenate([c[a:a + 1], -2.0 * s[1:a]], axis=0)
        fc = blocked(re_cols, im_cols).T
        faf = fa
    to = lambda m: jnp.asarray(m.astype(np.float32)).astype(BF16)
    kron = None
    if paired and 2 * n1 * N2_TILE <= PAIR_KRON_ROWS:
        eye = np.eye(N2_TILE)
        kron = (to(np.kron(fa, eye)), to(np.kron(fc, eye)))
    return to(fa), to(fc), to(faf), kron


def _edge_matrices(g, half, kb):
    zero = jnp.zeros_like(g[0])
    z = zero[:, :FFT_N2]
    col = lambda m: m[:, :FFT_N2]
    sa = jnp.concatenate([col(g[0]), z], axis=1)
    sb = jnp.concatenate([z, col(g[half])], axis=1)
    blocks = [jnp.stack([sa, sb, sa.T, sb.T])]
    blocks += [jnp.stack([g[i], zero, g[i].T, zero]) for i in range(kb, half, kb)]
    return jnp.stack(blocks)


def _twiddle_tables(seq_len, k1_count):
    n = 2 * seq_len
    n1 = n // FFT_N2
    k1 = np.arange(k1_count, dtype=np.int64)[:, None, None]
    k2 = np.arange(FFT_N2, dtype=np.int64)[None, :, None]
    n2 = np.arange(FFT_N2, dtype=np.int64)[None, None, :]
    ang = ((n2 * (k2 * n1 + k1)) % n) * (-2.0 * np.pi / n)
    gr, gi = np.cos(ang), np.sin(ang)
    g = np.concatenate([np.concatenate([gr, -gi], axis=2), np.concatenate([gi, gr], axis=2)], axis=1)
    g = g.astype(np.float32)
    return jnp.asarray(g.astype(BF16)), jnp.asarray(np.swapaxes(g, 1, 2).astype(BF16))


def _filter_features(seq_len):
    t_idx = jnp.arange(seq_len, dtype=F32)[:, None]
    t_norm = t_idx / max(seq_len - 1, 1)
    bands = jnp.linspace(1e-4, N_BANDS - 1, N_BANDS, dtype=F32)
    w = (2.0 * math.pi) * t_idx * bands[None, :] / seq_len
    return jnp.concatenate([t_norm, jnp.cos(w), -jnp.sin(w)], axis=-1)


def _abs_deltas():
    min_decay = math.log(DECAY_TARGET) / FAST_DECAY_PCT
    max_decay = math.log(DECAY_TARGET) / SLOW_DECAY_PCT
    return jnp.abs(jnp.linspace(min_decay, max_decay, HY_CH, dtype=F32))[None, :]


def _pick(n, target):
    t = min(n, target)
    while n % t:
        t //= 2
    return t


def _layer(x, lw, shared, st):
    b, l, _ = x.shape
    t = b * l
    x2d = x.reshape(t, D_MODEL)
    tm = _pick(l, FFN_TM)
    ux, q, kd, vt = _inproj(x2d, lw["norm1"], lw["w_in"], lw["conv_w"], lw["conv_b"],
                            shared["bq"], shared["bk"], lw["qg"], lw["kg"], _pick(l, INPROJ_TM), l)
    r3 = lambda a: a.reshape(b, l, a.shape[-1])

    n1 = 2 * l // FFT_N2
    a = n1 // 2
    nz = st["nz"]
    p = b // nz
    ux5 = ux.reshape(nz, p, a, FFT_N2, HY_CH)
    mb = _pick(st["fa"].shape[0], STAGE_MB)
    kf, kfh = lw["kf"][st["key"]]
    if st["kron"] is not None:
        ah = _pair_a(ux5, st["kron"][0])
        vv = _stage_b(ah, kf, st["g"], st["gt"], _pick(n1, STAGE_KB))
        yh = _pair_c(vv, st["kron"][1], ux5, lw["skip"]).reshape(p, l, HY_CH)
    else:
        ah = _stage_a(ux5, st["fa"], mb)
        vv = _stage_b(ah, kf, st["g"], st["gt"], _pick(ah.shape[2], STAGE_KB), st["edge"], kfh)
        yh = _stage_c(vv, st["fc"], ux5, lw["skip"], mb).reshape(p, SLABS, l, LANES)

    ya = _attention(r3(q), r3(kd), vt, shared["bias"], lw["sink_t"], _pick(l, ATT_TQ))
    ya = ya.reshape(t, ATT_W)

    out = _ffn(x2d, yh, ya, lw["hy_gain"], lw["at_gain"], lw["w_out"], lw["norm2"],
               lw["w_gate_up"], lw["w_down"], tm)
    return out.reshape(b, l, D_MODEL)


def _filter_spectrum(lw_raw, st, seq_len):
    n = 2 * seq_len
    a = seq_len // FFT_N2
    taps_f, taps_b = _filter_taps(st["zfeat"], lw_raw["f_w1"], lw_raw["f_b1"], lw_raw["f_freq"], lw_raw["f_w2"],
                                  lw_raw["f_b2"], lw_raw["f_w3"], st["absdelta"], _pick(seq_len, 512))
    mb = _pick(st["faf"].shape[0], STAGE_MB)
    ahf = _stage_a(taps_f.reshape(1, 1, a, FFT_N2, HY_CH), st["faf"], mb)
    ahb = _stage_a(taps_b.reshape(1, 1, a, FFT_N2, HY_CH), st["faf"], mb)
    out = _stage_bf(ahf, ahb, st["g"], _pick(ahf.shape[2], STAGE_KB), 1.0 / n, st["edge"])
    return (out[0], out[1]) if st["edge"] is not None else (out[0], None)


def kernel(x_prompt, x_sample, norm1, w_in, conv_w, conv_b, f_w1, f_b1, f_freq, f_w2, f_b2, f_w3,
           hy_skip, q_gain, k_gain, sink, rel_bias, hy_gain, at_gain, w_out, norm2, w_gate_up, w_down):
    depth = norm1.shape[0]
    streams = {}
    for key, x in (("prompt", x_prompt), ("sample", x_sample)):
        b, l, _ = x.shape
        paired = b % 2 == 0
        n1 = 2 * l // FFT_N2
        fa, fc, faf, kron = _dft_tables(l, paired, _pick(2 * n1 if paired else n1, STAGE_MB) // 2)
        g, gt = _twiddle_tables(l, n1 if paired else n1 // 2 + 1)
        streams[key] = dict(key=key, nz=2 if paired else 1, fa=fa, fc=fc, faf=faf, kron=kron, g=g, gt=gt,
                            edge=None if paired else _edge_matrices(g, n1 // 2, _pick(n1 // 2, STAGE_KB)),
                            zfeat=_filter_features(l), absdelta=_abs_deltas())

    head_id = np.arange(ATT_W) // HEAD_DIM
    shared = dict(
        bq=jnp.asarray((head_id[:, None] == head_id[None, :]).astype(np.float32) / HEAD_DIM).astype(BF16),
        bk=jnp.asarray((head_id[:KV_W, None] == head_id[None, :KV_W]).astype(np.float32) / HEAD_DIM).astype(BF16),
        bias=_bias_table(rel_bias) * LOG2E,
    )

    y_prompt, y_sample = x_prompt, x_sample
    for li in range(depth):
        raw = dict(f_w1=f_w1[li], f_b1=f_b1[li][None, :], f_freq=f_freq[li][None, :], f_w2=f_w2[li],
                   f_b2=f_b2[li][None, :], f_w3=f_w3[li])
        lw = dict(
            norm1=norm1[li][None, :], w_in=w_in[li].astype(BF16),
            qg=jnp.tile(q_gain[li], ATT_HEADS)[None, :] * (HEAD_DIM ** -0.5 * LOG2E),
            kg=jnp.tile(k_gain[li], KV_HEADS)[None, :],
            conv_w=conv_w[li], conv_b=conv_b[li][None, :],
            skip=hy_skip[li][None, :],
            sink_t=jnp.repeat(sink[li] * LOG2E, BLOCK).reshape(KV_HEADS, 1, GQA_GROUP * BLOCK),
            hy_gain=hy_gain[li][None, :], at_gain=at_gain[li][None, :],
            w_out=w_out[li].astype(BF16), norm2=norm2[li][None, :],
            w_gate_up=w_gate_up[li].astype(BF16), w_down=w_down[li].astype(BF16),
        )
        lw["kf"] = {key: _filter_spectrum(raw, st, {"prompt": x_prompt, "sample": x_sample}[key].shape[1])
                    for key, st in streams.items()}
        y_prompt = _layer(y_prompt, lw, shared, streams["prompt"])
        y_sample = _layer(y_sample, lw, shared, streams["sample"])
    return (y_prompt, y_sample)
```

```python
import functools
import math

import numpy as np
import jax
import jax.numpy as jnp
from jax import lax
from jax.experimental import pallas as pl
from jax.experimental.pallas import tpu as pltpu

F32 = jnp.float32
BF16 = jnp.bfloat16

D_MODEL = 1024
ATT_HEADS = 8
KV_HEADS = 2
HEAD_DIM = 64
GQA_GROUP = ATT_HEADS // KV_HEADS
ATT_W = ATT_HEADS * HEAD_DIM
KV_W = KV_HEADS * HEAD_DIM
WINDOW = 128
BLOCK = 128
N_BUCKETS = 32
MAX_DIST = 128
HY_CH = D_MODEL - ATT_W
FILTER_HIDDEN = 64
N_BANDS = 16
POS_DIM = 1 + 2 * N_BANDS
FAST_DECAY_PCT = 0.3
SLOW_DECAY_PCT = 1.5
DECAY_TARGET = 1e-2
IN_W = 3 * HY_CH + (ATT_HEADS + 2 * KV_HEADS) * HEAD_DIM
D_FF = -(-8 * D_MODEL // (3 * 256)) * 256
EPS = 1e-6
LOG2E = math.log2(math.e)

FFT_N2 = 64
LANES = 128
N2_TILE = 8
SLABS = HY_CH // LANES
STAGE_MB = 512
STAGE_KB = 32
PAIR_KRON_ROWS = 1024
PAIR_STEP_ROWS = 64
STAGE_B_CHAINS = 8
INPROJ_TM = 512
FFN_TM = 1024
FFN_GROUPS = 2
VMEM_LIMIT = 56 * 1024 * 1024

_NT = (((1,), (1,)), ((), ()))


def _cparams(*sem):
    return pltpu.CompilerParams(dimension_semantics=sem, vmem_limit_bytes=VMEM_LIMIT)


def _const_spec(shape):
    nd = len(shape)
    return pl.BlockSpec(shape, lambda *_: (0,) * nd, pipeline_mode=pl.Buffered(1))


HALO = 16


def _inproj_kernel(x_ref, xp_ref, xn_ref, g1_ref, w_ref, cw_ref, cb_ref, bq_ref, bk_ref, qg_ref, kg_ref,
                   ux_ref, q_ref, k_ref, v_ref, he_ref, qkv_ref, *, nt, ntiles):
    i = pl.program_id(0)
    tm = x_ref.shape[0]

    @pl.when(i == 0)
    def _():
        he_ref[...] = jnp.zeros_like(he_ref)
        qkv_ref[...] = jnp.zeros_like(qkv_ref)

    v = qkv_ref[:, ATT_W + KV_W:]
    v_ref[0] = v.T.astype(BF16)

    k = qkv_ref[:, ATT_W:ATT_W + KV_W]
    ms = jnp.dot((k * k).astype(BF16), bk_ref[...], preferred_element_type=F32)
    kn = k * lax.rsqrt(ms + EPS) * kg_ref[...]
    lo_half = lax.broadcasted_iota(jnp.int32, kn.shape, 1) < HEAD_DIM
    kr = pltpu.roll(kn, HEAD_DIM, axis=1)
    k_ref[...] = jnp.concatenate([jnp.where(lo_half, kn, kr), jnp.where(lo_half, kr, kn)], axis=1).astype(BF16)

    q = qkv_ref[:, 0:ATT_W]
    ms = jnp.dot((q * q).astype(BF16), bq_ref[...], preferred_element_type=F32)
    q_ref[...] = (q * lax.rsqrt(ms + EPS) * qg_ref[...]).astype(BF16)

    rows = slice(HALO, HALO + tm)

    def conv(j):
        c = slice(j * HY_CH, (j + 1) * HY_CH)
        he = he_ref[:, c]
        return (pltpu.roll(he, 1, axis=0)[rows] * cw_ref[0:1, c] + he[rows] * cw_ref[1:2, c]
                + pltpu.roll(he, tm + 2 * HALO - 1, axis=0)[rows] * cw_ref[2:3, c] + cb_ref[0:1, c])

    ux_ref[...] = _pack_pair(conv(1) * conv(2), conv(0))

    ti = jnp.minimum(i, ntiles - 1)
    xp = jnp.where(ti % nt == 0, 0.0, xp_ref[...])
    xn = jnp.where(ti % nt == nt - 1, 0.0, xn_ref[...])
    xe = jnp.concatenate([xp, x_ref[...], xn], axis=0)
    xe = (xe * lax.rsqrt(jnp.mean(xe * xe, axis=-1, keepdims=True) + EPS) * g1_ref[...]).astype(BF16)
    he_ref[...] = jnp.dot(xe, w_ref[:, 0:3 * HY_CH], preferred_element_type=F32)
    qkv_ref[...] = jnp.dot(xe[HALO:HALO + tm], w_ref[:, 3 * HY_CH:], preferred_element_type=F32)


def _inproj(x2d, g1, w_in_b, conv_w, conv_b, bq, bk, qg, kg, tm, seq_len):
    t = x2d.shape[0]
    nt = seq_len // tm
    ntiles = t // tm
    nh = tm // HALO
    cur = lambda i: jnp.minimum(i, ntiles - 1)
    done = lambda i: jnp.maximum(i - 1, 0)
    row_in = pl.BlockSpec((tm, D_MODEL), lambda i: (cur(i), 0))
    prev = pl.BlockSpec((HALO, D_MODEL), lambda i: (jnp.maximum(cur(i) * nh - 1, 0), 0))
    nxt = pl.BlockSpec((HALO, D_MODEL), lambda i: (jnp.minimum((cur(i) + 1) * nh, t // HALO - 1), 0))
    row_out = lambda w: pl.BlockSpec((tm, w), lambda i: (done(i), 0))
    return pl.pallas_call(
        functools.partial(_inproj_kernel, nt=nt, ntiles=ntiles),
        grid=(ntiles + 1,),
        in_specs=[row_in, prev, nxt, _const_spec((1, D_MODEL)), _const_spec((D_MODEL, IN_W)),
                  _const_spec((3, 3 * HY_CH)), _const_spec((1, 3 * HY_CH)),
                  _const_spec((ATT_W, ATT_W)), _const_spec((KV_W, KV_W)),
                  _const_spec((1, ATT_W)), _const_spec((1, KV_W))],
        out_specs=[row_out(HY_CH), row_out(ATT_W), row_out(2 * KV_W),
                   pl.BlockSpec((1, KV_W, tm), lambda i: (done(i) // nt, 0, done(i) % nt))],
        out_shape=[jax.ShapeDtypeStruct((t, HY_CH), jnp.uint32),
                   jax.ShapeDtypeStruct((t, ATT_W), BF16), jax.ShapeDtypeStruct((t, 2 * KV_W), BF16),
                   jax.ShapeDtypeStruct((t // seq_len, KV_W, seq_len), BF16)],
        scratch_shapes=[pltpu.VMEM((tm + 2 * HALO, 3 * HY_CH), F32), pltpu.VMEM((tm, IN_W - 3 * HY_CH), F32)],
        compiler_params=_cparams("arbitrary"),
        name="inproj",
    )(x2d, x2d, x2d, g1, w_in_b, conv_w, conv_b, bq, bk, qg, kg)


def _stage_a_kernel(*refs, nz, a, mb, packed_in):
    x_refs, f_ref, o_ref = refs[:SLABS], refs[SLABS], refs[SLABS + 1]
    kblk = mb // 2
    xs = [r.reshape(nz * a * N2_TILE, LANES) for r in x_refs]
    o2 = o_ref.reshape(SLABS * kblk * N2_TILE, LANES)
    f = f_ref[...]

    def gather(s):
        x = jnp.concatenate(
            [jnp.concatenate([xs[c][pl.ds(z * a * N2_TILE + s, a, stride=N2_TILE), :] for z in range(nz)], axis=0)
             for c in range(SLABS)], axis=1)
        return (_unpack_pair(x)[0] if packed_in else x).astype(BF16)

    r = [jnp.dot(f, gather(s), preferred_element_type=F32) for s in range(N2_TILE)]
    for s in range(N2_TILE):
        packed = _pack_pair(r[s][:kblk], r[s][kblk:])
        for c in range(SLABS):
            o2[pl.ds(c * kblk * N2_TILE + s, kblk, stride=N2_TILE), :] = packed[:, c * LANES:(c + 1) * LANES]


def _pack_pair(re, im):
    rb = lax.bitcast_convert_type(re.astype(BF16).astype(F32), jnp.uint32)
    ib = lax.bitcast_convert_type(im.astype(BF16).astype(F32), jnp.uint32)
    return (rb >> 16) | ib


def _unpack_pair(p):
    re = lax.bitcast_convert_type(p << 16, F32)
    im = lax.bitcast_convert_type(p & jnp.uint32(0xFFFF0000), F32)
    return re, im


def _stage_a(x5, fmat, mb):
    nz, p, a, n2, _ = x5.shape
    m = fmat.shape[0]
    kblk = mb // 2
    xspec = lambda c: pl.BlockSpec((nz, 1, a, N2_TILE, LANES), lambda pi, j, mi: (0, pi, 0, j, c))
    return pl.pallas_call(
        functools.partial(_stage_a_kernel, nz=nz, a=a, mb=mb, packed_in=x5.dtype == jnp.uint32),
        grid=(p, n2 // N2_TILE, m // mb),
        in_specs=[xspec(c) for c in range(SLABS)] + [pl.BlockSpec((mb, nz * a), lambda pi, j, mi: (mi, 0))],
        out_specs=pl.BlockSpec((1, SLABS, kblk, N2_TILE, LANES), lambda pi, j, mi: (pi, 0, mi, j, 0)),
        out_shape=jax.ShapeDtypeStruct((p, SLABS, m // 2, n2, LANES), jnp.uint32),
        compiler_params=_cparams("parallel", "parallel", "parallel"),
        name="hy_stage_a",
    )(*([x5] * SLABS), fmat)


def _pair_a_kernel(ux_ref, f_ref, o_ref):
    nz, _, a, tiles, c = ux_ref.shape
    n1 = o_ref.shape[1]
    for t in range(tiles // N2_TILE):
        sl = slice(t * N2_TILE, (t + 1) * N2_TILE)
        u = _unpack_pair(ux_ref[:, 0, :, sl, :].reshape(nz * a * N2_TILE, c))[0].astype(BF16)
        r = jnp.dot(f_ref[...], u, preferred_element_type=F32)
        half = r.shape[0] // 2
        o_ref[0, :, sl, :] = _pack_pair(r[:half], r[half:]).reshape(n1, N2_TILE, c)


def _pair_a(ux5, fx):
    nz, p, a, n2, c = ux5.shape
    n1 = fx.shape[0] // (2 * N2_TILE)
    rows = _pick(n2, PAIR_STEP_ROWS)
    return pl.pallas_call(
        _pair_a_kernel,
        grid=(p, n2 // rows),
        in_specs=[pl.BlockSpec((nz, 1, a, rows, c), lambda pi, j: (0, pi, 0, j, 0)), _const_spec(fx.shape)],
        out_specs=pl.BlockSpec((1, n1, rows, c), lambda pi, j: (pi, 0, j, 0)),
        out_shape=jax.ShapeDtypeStruct((p, n1, n2, c), jnp.uint32),
        compiler_params=_cparams("parallel", "parallel"),
        name="hy_stage_a",
    )(ux5, fx)


def _pair_c_kernel(v_ref, f_ref, ux_ref, skip_ref, o_ref):
    nz, _, a, tiles, c = ux_ref.shape
    n1 = v_ref.shape[1]
    for t in range(tiles // N2_TILE):
        sl = slice(t * N2_TILE, (t + 1) * N2_TILE)
        vr, vi = _unpack_pair(v_ref[0, :, sl, :].reshape(n1 * N2_TILE, c))
        y = jnp.dot(f_ref[...], jnp.concatenate([vr, vi], axis=0).astype(BF16), preferred_element_type=F32)
        half = y.shape[0] // 2
        uu, xx = _unpack_pair(ux_ref[:, 0, :, sl, :].reshape(2 * half, c))
        val = xx * (y + skip_ref[...] * uu)
        o_ref[0, :, sl, :] = _pack_pair(val[:half], val[half:]).reshape(a, N2_TILE, c)


def _pair_c(v4, fx, ux5, skip):
    nz, p, a, n2, c = ux5.shape
    n1 = v4.shape[1]
    rows = _pick(n2, PAIR_STEP_ROWS)
    return pl.pallas_call(
        _pair_c_kernel,
        grid=(p, n2 // rows),
        in_specs=[pl.BlockSpec((1, n1, rows, c), lambda pi, j: (pi, 0, j, 0)), _const_spec(fx.shape),
                  pl.BlockSpec((nz, 1, a, rows, c), lambda pi, j: (0, pi, 0, j, 0)), _const_spec((1, c))],
        out_specs=pl.BlockSpec((1, a, rows, c), lambda pi, j: (pi, 0, j, 0)),
        out_shape=jax.ShapeDtypeStruct((p, a, n2, c), jnp.uint32),
        compiler_params=_cparams("parallel", "parallel"),
        name="hy_stage_c",
    )(v4, fx, ux5, skip)


MXU_COLS = 256
SLABS_PER_DOT = MXU_COLS // LANES


def _slab_rows(a_ref, j, h):
    if len(a_ref.shape) == 4:
        re, im = _unpack_pair(a_ref[0, j, :, h * MXU_COLS:(h + 1) * MXU_COLS])
        return jnp.concatenate([re, im], axis=0).astype(BF16)
    parts = [_unpack_pair(a_ref[0, h * SLABS_PER_DOT + c, j]) for c in range(SLABS_PER_DOT)]
    return jnp.concatenate([jnp.concatenate([re, im], axis=0) for re, im in parts], axis=1).astype(BF16)


def _spectral_mul(u, kr, ki, n2):
    ur, ui = u[:n2], u[n2:]
    return jnp.concatenate([ur * kr - ui * ki, ur * ki + ui * kr], axis=0).astype(BF16)


def _stage_b_kernel(a_ref, kf_ref, g_ref, gt_ref, *rest, herm):
    if herm:
        edge_ref, kfh_ref, o_ref = rest
    else:
        (o_ref,) = rest
    kb, n2 = a_ref.shape[-3], a_ref.shape[-2]

    def store(j, h, v):
        packed = _pack_pair(v[:n2], v[n2:])
        if len(o_ref.shape) == 4:
            o_ref[0, j, :, h * MXU_COLS:(h + 1) * MXU_COLS] = packed
        else:
            for c in range(SLABS_PER_DOT):
                o_ref[0, h * SLABS_PER_DOT + c, j] = packed[:, c * LANES:(c + 1) * LANES]

    dot = functools.partial(jnp.dot, preferred_element_type=F32)
    chains = [(j, h) for j in range(kb) for h in range(SLABS // SLABS_PER_DOT)]
    for c0 in range(0, len(chains), STAGE_B_CHAINS):
        batch = chains[c0:c0 + STAGE_B_CHAINS]
        fwd = []
        for j, h in batch:
            x = _slab_rows(a_ref, j, h)
            if herm and j == 0:
                fwd.append((dot(edge_ref[0, 0], x), dot(edge_ref[0, 1], x)))
            else:
                fwd.append((dot(g_ref[j], x),))
        prod = []
        for (j, h), us in zip(batch, fwd):
            lanes = slice(h * MXU_COLS, (h + 1) * MXU_COLS)
            ps = [_spectral_mul(us[0], kf_ref[0, j, :, lanes], kf_ref[1, j, :, lanes], n2)]
            if len(us) == 2:
                ps.append(_spectral_mul(us[1], kfh_ref[0, 0, :, lanes], kfh_ref[0, 1, :, lanes], n2))
            prod.append(ps)
        for (j, h), ps in zip(batch, prod):
            if len(ps) == 2:
                store(j, h, dot(edge_ref[0, 2], ps[0]) + dot(edge_ref[0, 3], ps[1]))
            else:
                store(j, h, dot(gt_ref[j], ps[0]))


def _stage_b(a5, kf, g, gt, kb, edge=None, kfh=None):
    p, k1n, n2 = a5.shape[0], a5.shape[-3], a5.shape[-2]
    herm = edge is not None
    if a5.ndim == 4:
        blk = pl.BlockSpec((1, kb, n2, HY_CH), lambda i, pi: (pi, i, 0, 0))
    else:
        blk = pl.BlockSpec((1, SLABS, kb, n2, LANES), lambda i, pi: (pi, 0, i, 0, 0))
    gspec = pl.BlockSpec((kb, 2 * n2, 2 * n2), lambda i, pi: (i, 0, 0))
    extra = [pl.BlockSpec((1,) + edge.shape[1:], lambda i, pi: (i, 0, 0, 0)),
             pl.BlockSpec((1,) + kfh.shape[1:], lambda i, pi: (i, 0, 0, 0))] if herm else []
    return pl.pallas_call(
        functools.partial(_stage_b_kernel, herm=herm),
        grid=(k1n // kb, p),
        in_specs=[blk, pl.BlockSpec((2, kb, n2, HY_CH), lambda i, pi: (0, i, 0, 0)), gspec, gspec] + extra,
        out_specs=blk,
        out_shape=jax.ShapeDtypeStruct(a5.shape, jnp.uint32),
        compiler_params=_cparams("parallel", "parallel"),
        name="hy_stage_b",
    )(a5, kf, g, gt, *([edge, kfh] if herm else []))


def _stage_bf_kernel(af_ref, ab_ref, g_ref, *rest, scale, herm):
    if herm:
        edge_ref, o_ref, oh_ref = rest
    else:
        (o_ref,) = rest
    kb, n2 = af_ref.shape[2], af_ref.shape[3]

    def spectrum(mat, j, h):
        conj_mat = jnp.concatenate([mat[:n2], -mat[n2:]], axis=0)
        x = jnp.concatenate([_slab_rows(af_ref, j, h), _slab_rows(ab_ref, j, h)], axis=0)
        return jnp.dot(jnp.concatenate([mat, conj_mat], axis=1), x, preferred_element_type=F32) * scale

    for j in range(kb):
        for h in range(SLABS // SLABS_PER_DOT):
            lanes = slice(h * MXU_COLS, (h + 1) * MXU_COLS)
            if herm and j == 0:
                u = spectrum(edge_ref[0, 0], 0, h)
                uh = spectrum(edge_ref[0, 1], 0, h)
                oh_ref[0, 0, :, lanes] = uh[:n2]
                oh_ref[0, 1, :, lanes] = uh[n2:]
            else:
                u = spectrum(g_ref[j], j, h)
            o_ref[0, j, :, lanes] = u[:n2]
            o_ref[1, j, :, lanes] = u[n2:]


def _stage_bf(a5f, a5b, g, kb, scale, edge=None):
    _, _, k1n, n2, _ = a5f.shape
    herm = edge is not None
    aspec = pl.BlockSpec((1, SLABS, kb, n2, LANES), lambda i: (0, 0, i, 0, 0))
    out_specs = [pl.BlockSpec((2, kb, n2, HY_CH), lambda i: (0, i, 0, 0))]
    out_shape = [jax.ShapeDtypeStruct((2, k1n, n2, HY_CH), F32)]
    if herm:
        out_specs.append(pl.BlockSpec((1, 2, n2, HY_CH), lambda i: (i, 0, 0, 0)))
        out_shape.append(jax.ShapeDtypeStruct((k1n // kb, 2, n2, HY_CH), F32))
    return pl.pallas_call(
        functools.partial(_stage_bf_kernel, scale=scale, herm=herm),
        grid=(k1n // kb,),
        in_specs=[aspec, aspec, pl.BlockSpec((kb, 2 * n2, 2 * n2), lambda i: (i, 0, 0))]
        + ([pl.BlockSpec((1,) + edge.shape[1:], lambda i: (i, 0, 0, 0))] if herm else []),
        out_specs=out_specs,
        out_shape=out_shape,
        compiler_params=_cparams("parallel"),
        name="hy_filter_spectrum",
    )(a5f, a5b, g, *([edge] if herm else []))


def _stage_c_kernel(*refs, nz, a, mb):
    v_ref, f_ref = refs[0], refs[1]
    ux_refs = refs[2:2 + SLABS]
    skip_ref, o_ref, acc_ref = refs[2 + SLABS:]
    mk = pl.program_id(2)
    kblk = mb // 2
    v2 = v_ref.reshape(SLABS * kblk * N2_TILE, LANES)

    @pl.when(mk == 0)
    def _():
        acc_ref[...] = jnp.zeros_like(acc_ref)

    for s in range(N2_TILE):
        packed = jnp.concatenate([v2[pl.ds(c * kblk * N2_TILE + s, kblk, stride=N2_TILE), :] for c in range(SLABS)],
                                 axis=1)
        vs = jnp.concatenate(_unpack_pair(packed), axis=0).astype(BF16)
        acc_ref[s] += jnp.dot(f_ref[...], vs, preferred_element_type=F32)

    @pl.when(mk == pl.num_programs(2) - 1)
    def _():
        o2 = o_ref.reshape(SLABS * a * N2_TILE, LANES)
        ux2 = [r.reshape(nz * a * N2_TILE, LANES) for r in ux_refs]
        for s in range(N2_TILE):
            y = acc_ref[s]
            for c in range(SLABS):
                vals = []
                for z in range(nz):
                    uu, xx = _unpack_pair(ux2[c][pl.ds(z * a * N2_TILE + s, a, stride=N2_TILE), :])
                    yc = y[z * a:(z + 1) * a, c * LANES:(c + 1) * LANES]
                    vals.append(xx * (yc + skip_ref[0:1, c * LANES:(c + 1) * LANES] * uu))
                o2[pl.ds(c * a * N2_TILE + s, a, stride=N2_TILE), :] = _pack_pair(*vals) if nz == 2 else vals[0]


def _stage_c(v5, fmat, ux5, skip, mb):
    nz, p, a, n2, _ = ux5.shape
    m = 2 * v5.shape[2]
    xspec = lambda c: pl.BlockSpec((nz, 1, a, N2_TILE, LANES), lambda pi, j, mk: (0, pi, 0, j, c))
    return pl.pallas_call(
        functools.partial(_stage_c_kernel, nz=nz, a=a, mb=mb),
        grid=(p, n2 // N2_TILE, m // mb),
        in_specs=[pl.BlockSpec((1, SLABS, mb // 2, N2_TILE, LANES), lambda pi, j, mk: (pi, 0, mk, j, 0)),
                  pl.BlockSpec((nz * a, mb), lambda pi, j, mk: (0, mk))]
        + [xspec(c) for c in range(SLABS)] + [_const_spec((1, HY_CH))],
        out_specs=pl.BlockSpec((1, SLABS, a, N2_TILE, LANES), lambda pi, j, mk: (pi, 0, 0, j, 0)),
        out_shape=jax.ShapeDtypeStruct((p, SLABS, a, n2, LANES), jnp.uint32 if nz == 2 else F32),
        scratch_shapes=[pltpu.VMEM((N2_TILE, nz * a, HY_CH), F32)],
        compiler_params=_cparams("parallel", "parallel", "arbitrary"),
        name="hy_stage_c",
    )(v5, fmat, *([ux5] * SLABS), skip)


def _filter_kernel(zt_ref, tn_ref, w1t_ref, b1_ref, fr_ref, w2t_ref, b2_ref, w3_ref, dl_ref, of_ref, ob_ref):
    hi = lax.Precision.HIGHEST
    fr = fr_ref[...]
    h = jnp.sin(fr * (jnp.dot(w1t_ref[...], zt_ref[...], precision=hi, preferred_element_type=F32) + b1_ref[...]))
    h = jnp.sin(fr * (jnp.dot(w2t_ref[...], h, precision=hi, preferred_element_type=F32) + b2_ref[...]))
    taps = jnp.dot(h.T.astype(BF16), w3_ref[...], preferred_element_type=F32)
    tr = taps.shape[0]
    decay = jnp.exp(-tn_ref[...] * dl_ref[...])
    of_ref[...] = taps[:, :HY_CH] * decay
    t = pl.program_id(0) * tr + lax.broadcasted_iota(jnp.int32, (tr, HY_CH), 0)
    ob_ref[...] = jnp.where(t == 0, 0.0, taps[:, HY_CH:] * decay)


def _filter_taps(zfeat, f_w1, f_b1, f_freq, f_w2, f_b2, f_w3, absdelta, tr):
    n = zfeat.shape[0]
    cs = lambda a: _const_spec(a.shape)
    args = (zfeat.T, zfeat[:, 0:1], f_w1.T, f_b1.T, f_freq.T, f_w2.T, f_b2.T, f_w3.astype(BF16), absdelta)
    out = pl.BlockSpec((tr, HY_CH), lambda i: (i, 0))
    return pl.pallas_call(
        _filter_kernel,
        grid=(n // tr,),
        in_specs=[pl.BlockSpec((POS_DIM, tr), lambda i: (0, i)), pl.BlockSpec((tr, 1), lambda i: (i, 0))]
        + [cs(a) for a in args[2:]],
        out_specs=[out, out],
        out_shape=[jax.ShapeDtypeStruct((n, HY_CH), F32)] * 2,
        compiler_params=_cparams("parallel"),
        name="hy_filter_taps",
    )(*args)


def _bias_kernel(rb_ref, oh_ref, o_ref):
    o_ref[...] = jnp.dot(rb_ref[...], oh_ref[...], precision=lax.Precision.HIGHEST,
                         preferred_element_type=F32)


def _bias_table(rel_bias):
    i = jnp.arange(BLOCK)[:, None]
    j = jnp.arange(3 * BLOCK)[None, :]
    rel = j - BLOCK - i
    nb2 = N_BUCKETS // 2
    max_exact = nb2 // 2
    n = jnp.abs(rel)
    large = max_exact + (jnp.log(jnp.maximum(n, 1).astype(F32) / max_exact)
                         / math.log(MAX_DIST / max_exact) * (nb2 - max_exact)).astype(jnp.int32)
    large = jnp.minimum(large, nb2 - 1)
    bucket = jnp.where(rel > 0, nb2, 0) + jnp.where(n < max_exact, n, large)
    onehot = (bucket.reshape(1, -1) == jnp.arange(N_BUCKETS)[:, None]).astype(F32)
    cols = onehot.shape[1]
    tc = cols // 4
    table = pl.pallas_call(
        _bias_kernel,
        grid=(4,),
        in_specs=[_const_spec((ATT_HEADS, N_BUCKETS)), pl.BlockSpec((N_BUCKETS, tc), lambda c: (0, c))],
        out_specs=pl.BlockSpec((ATT_HEADS, tc), lambda c: (0, c)),
        out_shape=jax.ShapeDtypeStruct((ATT_HEADS, cols), F32),
        compiler_params=_cparams("parallel"),
        name="att_bias_table",
    )(rel_bias.T, onehot)
    table = table.reshape(ATT_HEADS, BLOCK, 3 * BLOCK)
    table = jnp.where((n <= WINDOW)[None], table, -jnp.inf)
    table = table.reshape(KV_HEADS, GQA_GROUP, BLOCK, 3 * BLOCK)
    return table.transpose(0, 3, 1, 2).reshape(KV_HEADS, 3 * BLOCK, GQA_GROUP * BLOCK)


def _attn_kernel(q_ref, km_ref, kp_ref, kn_ref, vm_ref, vp_ref, vn_ref, bias_ref, sink_ref, o_ref, *, nsub):
    i = pl.program_id(1)
    neg = -jnp.inf
    pen_first = jnp.where(i == 0, neg, 0.0)
    pen_last = jnp.where(i == pl.num_programs(1) - 1, neg, 0.0)
    kwin = jnp.concatenate([kp_ref[0], km_ref[0], kn_ref[0]], axis=0)
    vwin = jnp.concatenate([vp_ref[0], vm_ref[0], vn_ref[0]], axis=1)
    lane = lax.broadcasted_iota(jnp.int32, (BLOCK, LANES), 1)
    lo_half = lane < HEAD_DIM
    ones = jnp.ones((SUM_ROWS, 3 * BLOCK), BF16)
    units = [(s, g) for s in range(nsub) for g in range(KV_HEADS)]

    def scores(s, g):
        qs = q_ref[0, s * BLOCK:(s + 1) * BLOCK, :]
        rows = []
        for pr in range(2):
            qp = qs[:, (2 * g + pr) * LANES:(2 * g + pr + 1) * LANES]
            rows.append(jnp.where(lo_half, qp, jnp.zeros_like(qp)))
            rows.append(jnp.where(lo_half, jnp.zeros_like(qp), qp))
        qg = jnp.concatenate(rows, axis=0)
        kg = kwin[s * BLOCK:(s + 3) * BLOCK, g * LANES:(g + 1) * LANES]
        t = lax.dot_general(kg, qg, _NT, preferred_element_type=F32) + bias_ref[g]
        if s == 0:
            t = jnp.concatenate([t[:BLOCK] + pen_first, t[BLOCK:]], axis=0)
        if s == nsub - 1:
            t = jnp.concatenate([t[:2 * BLOCK], t[2 * BLOCK:] + pen_last], axis=0)
        return t

    def pv(s, g, p):
        vg = jnp.concatenate([vwin[g * HEAD_DIM:(g + 1) * HEAD_DIM, s * BLOCK:(s + 3) * BLOCK], ones],
                             axis=0)
        return jnp.dot(vg, p, preferred_element_type=F32)

    for u0 in range(0, len(units), ATT_UNITS):
        batch = units[u0:u0 + ATT_UNITS]
        sc = [scores(s, g) for s, g in batch]
        m = [jnp.maximum(jnp.max(t, axis=0, keepdims=True), sink_ref[g]) for t, (_, g) in zip(sc, batch)]
        p = [jnp.exp2(t - mm).astype(BF16) for t, mm in zip(sc, m)]
        o = [pv(s, g, pp) for (s, g), pp in zip(batch, p)]
        for (s, g), oo, mm in zip(batch, o, m):
            on = oo[:HEAD_DIM] / (oo[HEAD_DIM:HEAD_DIM + 1] + jnp.exp2(sink_ref[g] - mm))
            for pr in range(2):
                pair = jnp.concatenate([on[:, (2 * pr) * BLOCK:(2 * pr + 1) * BLOCK],
                                        on[:, (2 * pr + 1) * BLOCK:(2 * pr + 2) * BLOCK]], axis=0)
                o_ref[0, s * BLOCK:(s + 1) * BLOCK, (2 * g + pr) * LANES:(2 * g + pr + 1) * LANES] = (
                    pair.T.astype(o_ref.dtype))


SUM_ROWS = 16
ATT_UNITS = 8
ATT_TQ = 1024


def _attention(q, kd, vt, bias_t, sink_t, tq):
    b, l, _ = q.shape
    nsub = tq // BLOCK
    nblk = l // BLOCK
    kw = 2 * KV_W
    main = lambda w: pl.BlockSpec((1, tq, w), lambda bi, i: (bi, i, 0))
    prev = pl.BlockSpec((1, BLOCK, kw), lambda bi, i: (bi, jnp.maximum(i * nsub - 1, 0), 0))
    nxt = pl.BlockSpec((1, BLOCK, kw), lambda bi, i: (bi, jnp.minimum((i + 1) * nsub, nblk - 1), 0))
    vmain = pl.BlockSpec((1, KV_W, tq), lambda bi, i: (bi, 0, i))
    vprev = pl.BlockSpec((1, KV_W, BLOCK), lambda bi, i: (bi, 0, jnp.maximum(i * nsub - 1, 0)))
    vnxt = pl.BlockSpec((1, KV_W, BLOCK), lambda bi, i: (bi, 0, jnp.minimum((i + 1) * nsub, nblk - 1)))
    return pl.pallas_call(
        functools.partial(_attn_kernel, nsub=nsub),
        grid=(b, l // tq),
        in_specs=[main(ATT_W), main(kw), prev, nxt, vmain, vprev, vnxt,
                  _const_spec(bias_t.shape), _const_spec(sink_t.shape)],
        out_specs=main(ATT_W),
        out_shape=jax.ShapeDtypeStruct((b, l, ATT_W), BF16),
        compiler_params=_cparams("parallel", "parallel"),
        name="window_attn",
    )(q, kd, kd, kd, vt, vt, vt, bias_t, sink_t)


FF_CHUNK = 256


def _ffn_kernel(x_ref, yh_ref, ya_ref, hg_ref, ag_ref, wo_ref, g2_ref, wgu_ref, wd_ref, o_ref, act_ref, *, pairs, nt):
    def rms(t, g):
        return (t * lax.rsqrt(jnp.mean(t * t, axis=-1, keepdims=True) + EPS) * g).astype(BF16)

    tm = x_ref.shape[0]
    groups = [slice(r, r + tm // FFN_GROUPS) for r in range(0, tm, tm // FFN_GROUPS)]

    def hyena_rows(rs):
        if len(yh_ref.shape) == 3:
            yh = yh_ref[0, rs, :]
        else:
            yh = jnp.concatenate([yh_ref[0, c, rs, :] for c in range(SLABS)], axis=1)
        if pairs:
            lo, hi = _unpack_pair(yh)
            yh = jnp.where(pl.program_id(0) // nt < pairs, lo, hi)
        return yh

    mixed = [jnp.concatenate([rms(hyena_rows(rs), hg_ref[...]), rms(ya_ref[rs, :].astype(F32), ag_ref[...])], axis=1)
             for rs in groups]
    h = [x_ref[rs, :] + jnp.dot(m, wo_ref[...], preferred_element_type=F32) for rs, m in zip(groups, mixed)]
    hn = [rms(t, g2_ref[...]) for t in h]
    for c in range(D_FF // FF_CHUNK):
        lo = c * FF_CHUNK
        for rs, t in zip(groups, hn):
            gate = jnp.dot(t, wgu_ref[:, lo:lo + FF_CHUNK], preferred_element_type=F32)
            up = jnp.dot(t, wgu_ref[:, D_FF + lo:D_FF + lo + FF_CHUNK], preferred_element_type=F32)
            act_ref[rs, lo:lo + FF_CHUNK] = (gate / (1.0 + jnp.exp(-gate)) * up).astype(BF16)
    for rs, t in zip(groups, h):
        o_ref[rs, :] = t + jnp.dot(act_ref[rs, :], wd_ref[...], preferred_element_type=F32)


def _ffn(x2d, yh, ya, hg, ag, wo_b, g2, wgu_b, wd_b, tm):
    t = x2d.shape[0]
    nt = yh.shape[-2] // tm
    pairs = yh.shape[0] if yh.dtype == jnp.uint32 else 0
    row = lambda w: pl.BlockSpec((tm, w), lambda i: (i, 0))
    if yh.ndim == 3:
        yspec = pl.BlockSpec((1, tm, HY_CH), lambda i: ((i // nt) % yh.shape[0], i % nt, 0))
    else:
        yspec = pl.BlockSpec((1, SLABS, tm, LANES), lambda i: ((i // nt) % yh.shape[0], 0, i % nt, 0))
    return pl.pallas_call(
        functools.partial(_ffn_kernel, pairs=pairs, nt=nt),
        grid=(t // tm,),
        in_specs=[row(D_MODEL), yspec, row(ATT_W), _const_spec((1, HY_CH)), _const_spec((1, ATT_W)),
                  _const_spec((D_MODEL, D_MODEL)), _const_spec((1, D_MODEL)),
                  _const_spec((D_MODEL, 2 * D_FF)), _const_spec((D_FF, D_MODEL))],
        out_specs=row(D_MODEL),
        out_shape=jax.ShapeDtypeStruct((t, D_MODEL), F32),
        scratch_shapes=[pltpu.VMEM((tm, D_FF), BF16)],
        compiler_params=_cparams("parallel"),
        name="outproj_swiglu",
    )(x2d, yh, ya, hg, ag, wo_b, g2, wgu_b, wd_b)


def _dft_tables(seq_len, paired, kblk):
    n = 2 * seq_len
    n1 = n // FFT_N2
    a = n1 // 2
    kk = np.arange(n1)[:, None]
    th_half = 2.0 * np.pi * ((kk * np.arange(a)[None, :]) % n1) / n1
    c, s = np.cos(th_half), np.sin(th_half)

    def blocked(re, im):
        return np.concatenate([np.concatenate([re[i:i + kblk], im[i:i + kblk]], axis=0)
                               for i in range(0, re.shape[0], kblk)], axis=0)

    def real_input_rows(cc, ss):
        im = -ss[:a].copy()
        im[0] = cc[a]
        return cc[:a], im

    if paired:
        fa = blocked(np.concatenate([c, s], axis=1), np.concatenate([-s, c], axis=1))
        fc = fa.T
        faf = blocked(c, -s)
    else:
        fa = blocked(*real_input_rows(c, s))
        re_cols = np.concatenate([c[:1], 2.0 * c[1:a]], axis=0)
        im_cols = np.concatenate([c[a:a + 1], -2.0 * s[1:a]], axis=0)
        fc = blocked(re_cols, im_cols).T
        faf = fa
    to = lambda m: jnp.asarray(m.astype(np.float32)).astype(BF16)
    kron = None
    if paired and 2 * n1 * N2_TILE <= PAIR_KRON_ROWS:
        eye = np.eye(N2_TILE)
        kron = (to(np.kron(fa, eye)), to(np.kron(fc, eye)))
    return to(fa), to(fc), to(faf), kron


def _edge_matrices(g, half, kb):
    zero = jnp.zeros_like(g[0])
    z = zero[:, :FFT_N2]
    col = lambda m: m[:, :FFT_N2]
    sa = jnp.concatenate([col(g[0]), z], axis=1)
    sb = jnp.concatenate([z, col(g[half])], axis=1)
    blocks = [jnp.stack([sa, sb, sa.T, sb.T])]
    blocks += [jnp.stack([g[i], zero, g[i].T, zero]) for i in range(kb, half, kb)]
    return jnp.stack(blocks)


def _twiddle_tables(seq_len, k1_count):
    n = 2 * seq_len
    n1 = n // FFT_N2
    k1 = np.arange(k1_count, dtype=np.int64)[:, None, None]
    k2 = np.arange(FFT_N2, dtype=np.int64)[None, :, None]
    n2 = np.arange(FFT_N2, dtype=np.int64)[None, None, :]
    ang = ((n2 * (k2 * n1 + k1)) % n) * (-2.0 * np.pi / n)
    gr, gi = np.cos(ang), np.sin(ang)
    g = np.concatenate([np.concatenate([gr, -gi], axis=2), np.concatenate([gi, gr], axis=2)], axis=1)
    g = g.astype(np.float32)
    return jnp.asarray(g.astype(BF16)), jnp.asarray(np.swapaxes(g, 1, 2).astype(BF16))


def _filter_features(seq_len):
    t_idx = jnp.arange(seq_len, dtype=F32)[:, None]
    t_norm = t_idx / max(seq_len - 1, 1)
    bands = jnp.linspace(1e-4, N_BANDS - 1, N_BANDS, dtype=F32)
    w = (2.0 * math.pi) * t_idx * bands[None, :] / seq_len
    return jnp.concatenate([t_norm, jnp.cos(w), -jnp.sin(w)], axis=-1)


def _abs_deltas():
    min_decay = math.log(DECAY_TARGET) / FAST_DECAY_PCT
    max_decay = math.log(DECAY_TARGET) / SLOW_DECAY_PCT
    return jnp.abs(jnp.linspace(min_decay, max_decay, HY_CH, dtype=F32))[None, :]


def _pick(n, target):
    t = min(n, target)
    while n % t:
        t //= 2
    return t


def _layer(x, lw, shared, st):
    b, l, _ = x.shape
    t = b * l
    x2d = x.reshape(t, D_MODEL)
    tm = _pick(l, FFN_TM)
    ux, q, kd, vt = _inproj(x2d, lw["norm1"], lw["w_in"], lw["conv_w"], lw["conv_b"],
                            shared["bq"], shared["bk"], lw["qg"], lw["kg"], _pick(l, INPROJ_TM), l)
    r3 = lambda a: a.reshape(b, l, a.shape[-1])

    n1 = 2 * l // FFT_N2
    a = n1 // 2
    nz = st["nz"]
    p = b // nz
    ux5 = ux.reshape(nz, p, a, FFT_N2, HY_CH)
    mb = _pick(st["fa"].shape[0], STAGE_MB)
    kf, kfh = lw["kf"][st["key"]]
    if st["kron"] is not None:
        ah = _pair_a(ux5, st["kron"][0])
        vv = _stage_b(ah, kf, st["g"], st["gt"], _pick(n1, STAGE_KB))
        yh = _pair_c(vv, st["kron"][1], ux5, lw["skip"]).reshape(p, l, HY_CH)
    else:
        ah = _stage_a(ux5, st["fa"], mb)
        vv = _stage_b(ah, kf, st["g"], st["gt"], _pick(ah.shape[2], STAGE_KB), st["edge"], kfh)
        yh = _stage_c(vv, st["fc"], ux5, lw["skip"], mb).reshape(p, SLABS, l, LANES)

    ya = _attention(r3(q), r3(kd), vt, shared["bias"], lw["sink_t"], _pick(l, ATT_TQ))
    ya = ya.reshape(t, ATT_W)

    out = _ffn(x2d, yh, ya, lw["hy_gain"], lw["at_gain"], lw["w_out"], lw["norm2"],
               lw["w_gate_up"], lw["w_down"], tm)
    return out.reshape(b, l, D_MODEL)


def _filter_spectrum(lw_raw, st, seq_len):
    n = 2 * seq_len
    a = seq_len // FFT_N2
    taps_f, taps_b = _filter_taps(st["zfeat"], lw_raw["f_w1"], lw_raw["f_b1"], lw_raw["f_freq"], lw_raw["f_w2"],
                                  lw_raw["f_b2"], lw_raw["f_w3"], st["absdelta"], _pick(seq_len, 512))
    mb = _pick(st["faf"].shape[0], STAGE_MB)
    ahf = _stage_a(taps_f.reshape(1, 1, a, FFT_N2, HY_CH), st["faf"], mb)
    ahb = _stage_a(taps_b.reshape(1, 1, a, FFT_N2, HY_CH), st["faf"], mb)
    out = _stage_bf(ahf, ahb, st["g"], _pick(ahf.shape[2], STAGE_KB), 1.0 / n, st["edge"])
    return (out[0], out[1]) if st["edge"] is not None else (out[0], None)


def kernel(x_prompt, x_sample, norm1, w_in, conv_w, conv_b, f_w1, f_b1, f_freq, f_w2, f_b2, f_w3,
           hy_skip, q_gain, k_gain, sink, rel_bias, hy_gain, at_gain, w_out, norm2, w_gate_up, w_down):
    depth = norm1.shape[0]
    streams = {}
    for key, x in (("prompt", x_prompt), ("sample", x_sample)):
        b, l, _ = x.shape
        paired = b % 2 == 0
        n1 = 2 * l // FFT_N2
        fa, fc, faf, kron = _dft_tables(l, paired, _pick(2 * n1 if paired else n1, STAGE_MB) // 2)
        g, gt = _twiddle_tables(l, n1 if paired else n1 // 2 + 1)
        streams[key] = dict(key=key, nz=2 if paired else 1, fa=fa, fc=fc, faf=faf, kron=kron, g=g, gt=gt,
                            edge=None if paired else _edge_matrices(g, n1 // 2, _pick(n1 // 2, STAGE_KB)),
                            zfeat=_filter_features(l), absdelta=_abs_deltas())

    head_id = np.arange(ATT_W) // HEAD_DIM
    shared = dict(
        bq=jnp.asarray((head_id[:, None] == head_id[None, :]).astype(np.float32) / HEAD_DIM).astype(BF16),
        bk=jnp.asarray((head_id[:KV_W, None] == head_id[None, :KV_W]).astype(np.float32) / HEAD_DIM).astype(BF16),
        bias=_bias_table(rel_bias) * LOG2E,
    )

    y_prompt, y_sample = x_prompt, x_sample
    for li in range(depth):
        raw = dict(f_w1=f_w1[li], f_b1=f_b1[li][None, :], f_freq=f_freq[li][None, :], f_w2=f_w2[li],
                   f_b2=f_b2[li][None, :], f_w3=f_w3[li])
        lw = dict(
            norm1=norm1[li][None, :], w_in=w_in[li].astype(BF16),
            qg=jnp.tile(q_gain[li], ATT_HEADS)[None, :] * (HEAD_DIM ** -0.5 * LOG2E),
            kg=jnp.tile(k_gain[li], KV_HEADS)[None, :],
            conv_w=conv_w[li], conv_b=conv_b[li][None, :],
            skip=hy_skip[li][None, :],
            sink_t=jnp.repeat(sink[li] * LOG2E, BLOCK).reshape(KV_HEADS, 1, GQA_GROUP * BLOCK),
            hy_gain=hy_gain[li][None, :], at_gain=at_gain[li][None, :],
            w_out=w_out[li].astype(BF16), norm2=norm2[li][None, :],
            w_gate_up=w_gate_up[li].astype(BF16), w_down=w_down[li].astype(BF16),
        )
        lw["kf"] = {key: _filter_spectrum(raw, st, {"prompt": x_prompt, "sample": x_sample}[key].shape[1])
                    for key, st in streams.items()}
        y_prompt = _layer(y_prompt, lw, shared, streams["prompt"])
        y_sample = _layer(y_sample, lw, shared, streams["sample"])
    return (y_prompt, y_sample)
```

```python
import functools
import math

import numpy as np
import jax
import jax.numpy as jnp
from jax import lax
from jax.experimental import pallas as pl
from jax.experimental.pallas import tpu as pltpu

F32 = jnp.float32
BF16 = jnp.bfloat16

D_MODEL = 1024
ATT_HEADS = 8
KV_HEADS = 2
HEAD_DIM = 64
GQA_GROUP = ATT_HEADS // KV_HEADS
ATT_W = ATT_HEADS * HEAD_DIM
KV_W = KV_HEADS * HEAD_DIM
WINDOW = 128
BLOCK = 128
N_BUCKETS = 32
MAX_DIST = 128
HY_CH = D_MODEL - ATT_W
FILTER_HIDDEN = 64
N_BANDS = 16
POS_DIM = 1 + 2 * N_BANDS
FAST_DECAY_PCT = 0.3
SLOW_DECAY_PCT = 1.5
DECAY_TARGET = 1e-2
IN_W = 3 * HY_CH + (ATT_HEADS + 2 * KV_HEADS) * HEAD_DIM
D_FF = -(-8 * D_MODEL // (3 * 256)) * 256
EPS = 1e-6
LOG2E = math.log2(math.e)

FFT_N2 = 64
LANES = 128
N2_TILE = 8
SLABS = HY_CH // LANES
STAGE_MB = 512
STAGE_KB = 32
PAIR_KRON_ROWS = 1024
PAIR_STEP_ROWS = 64
STAGE_B_CHAINS = 8
INPROJ_TM = 512
FFN_TM = 1024
FFN_GROUPS = 2
VMEM_LIMIT = 56 * 1024 * 1024

_NT = (((1,), (1,)), ((), ()))


def _cparams(*sem):
    return pltpu.CompilerParams(dimension_semantics=sem, vmem_limit_bytes=VMEM_LIMIT)


def _const_spec(shape):
    nd = len(shape)
    return pl.BlockSpec(shape, lambda *_: (0,) * nd, pipeline_mode=pl.Buffered(1))


HALO = 16


def _inproj_kernel(x_ref, xp_ref, xn_ref, g1_ref, w_ref, cw_ref, cb_ref, bq_ref, bk_ref, qg_ref, kg_ref,
                   ux_ref, q_ref, k_ref, v_ref, he_ref, qkv_ref, *, nt, ntiles):
    i = pl.program_id(0)
    tm = x_ref.shape[0]

    @pl.when(i == 0)
    def _():
        he_ref[...] = jnp.zeros_like(he_ref)
        qkv_ref[...] = jnp.zeros_like(qkv_ref)

    v = qkv_ref[:, ATT_W + KV_W:]
    v_ref[0] = v.T.astype(BF16)

    k = qkv_ref[:, ATT_W:ATT_W + KV_W]
    ms = jnp.dot((k * k).astype(BF16), bk_ref[...], preferred_element_type=F32)
    kn = k * lax.rsqrt(ms + EPS) * kg_ref[...]
    lo_half = lax.broadcasted_iota(jnp.int32, kn.shape, 1) < HEAD_DIM
    kr = pltpu.roll(kn, HEAD_DIM, axis=1)
    k_ref[...] = jnp.concatenate([jnp.where(lo_half, kn, kr), jnp.where(lo_half, kr, kn)], axis=1).astype(BF16)

    q = qkv_ref[:, 0:ATT_W]
    ms = jnp.dot((q * q).astype(BF16), bq_ref[...], preferred_element_type=F32)
    q_ref[...] = (q * lax.rsqrt(ms + EPS) * qg_ref[...]).astype(BF16)

    rows = slice(HALO, HALO + tm)

    def conv(j):
        c = slice(j * HY_CH, (j + 1) * HY_CH)
        he = he_ref[:, c]
        return (pltpu.roll(he, 1, axis=0)[rows] * cw_ref[0:1, c] + he[rows] * cw_ref[1:2, c]
                + pltpu.roll(he, tm + 2 * HALO - 1, axis=0)[rows] * cw_ref[2:3, c] + cb_ref[0:1, c])

    ux_ref[...] = _pack_pair(conv(1) * conv(2), conv(0))

    ti = jnp.minimum(i, ntiles - 1)
    xp = jnp.where(ti % nt == 0, 0.0, xp_ref[...])
    xn = jnp.where(ti % nt == nt - 1, 0.0, xn_ref[...])
    xe = jnp.concatenate([xp, x_ref[...], xn], axis=0)
    xe = (xe * lax.rsqrt(jnp.mean(xe * xe, axis=-1, keepdims=True) + EPS) * g1_ref[...]).astype(BF16)
    he_ref[...] = jnp.dot(xe, w_ref[:, 0:3 * HY_CH], preferred_element_type=F32)
    qkv_ref[...] = jnp.dot(xe[HALO:HALO + tm], w_ref[:, 3 * HY_CH:], preferred_element_type=F32)


def _inproj(x2d, g1, w_in_b, conv_w, conv_b, bq, bk, qg, kg, tm, seq_len):
    t = x2d.shape[0]
    nt = seq_len // tm
    ntiles = t // tm
    nh = tm // HALO
    cur = lambda i: jnp.minimum(i, ntiles - 1)
    done = lambda i: jnp.maximum(i - 1, 0)
    row_in = pl.BlockSpec((tm, D_MODEL), lambda i: (cur(i), 0))
    prev = pl.BlockSpec((HALO, D_MODEL), lambda i: (jnp.maximum(cur(i) * nh - 1, 0), 0))
    nxt = pl.BlockSpec((HALO, D_MODEL), lambda i: (jnp.minimum((cur(i) + 1) * nh, t // HALO - 1), 0))
    row_out = lambda w: pl.BlockSpec((tm, w), lambda i: (done(i), 0))
    return pl.pallas_call(
        functools.partial(_inproj_kernel, nt=nt, ntiles=ntiles),
        grid=(ntiles + 1,),
        in_specs=[row_in, prev, nxt, _const_spec((1, D_MODEL)), _const_spec((D_MODEL, IN_W)),
                  _const_spec((3, 3 * HY_CH)), _const_spec((1, 3 * HY_CH)),
                  _const_spec((ATT_W, ATT_W)), _const_spec((KV_W, KV_W)),
                  _const_spec((1, ATT_W)), _const_spec((1, KV_W))],
        out_specs=[row_out(HY_CH), row_out(ATT_W), row_out(2 * KV_W),
                   pl.BlockSpec((1, KV_W, tm), lambda i: (done(i) // nt, 0, done(i) % nt))],
        out_shape=[jax.ShapeDtypeStruct((t, HY_CH), jnp.uint32),
                   jax.ShapeDtypeStruct((t, ATT_W), BF16), jax.ShapeDtypeStruct((t, 2 * KV_W), BF16),
                   jax.ShapeDtypeStruct((t // seq_len, KV_W, seq_len), BF16)],
        scratch_shapes=[pltpu.VMEM((tm + 2 * HALO, 3 * HY_CH), F32), pltpu.VMEM((tm, IN_W - 3 * HY_CH), F32)],
        compiler_params=_cparams("arbitrary"),
        name="inproj",
    )(x2d, x2d, x2d, g1, w_in_b, conv_w, conv_b, bq, bk, qg, kg)


def _stage_a_kernel(*refs, nz, a, mb, packed_in):
    x_refs, f_ref, o_ref = refs[:SLABS], refs[SLABS], refs[SLABS + 1]
    kblk = mb // 2
    xs = [r.reshape(nz * a * N2_TILE, LANES) for r in x_refs]
    o2 = o_ref.reshape(SLABS * kblk * N2_TILE, LANES)
    f = f_ref[...]

    def gather(s):
        x = jnp.concatenate(
            [jnp.concatenate([xs[c][pl.ds(z * a * N2_TILE + s, a, stride=N2_TILE), :] for z in range(nz)], axis=0)
             for c in range(SLABS)], axis=1)
        return (_unpack_pair(x)[0] if packed_in else x).astype(BF16)

    r = [jnp.dot(f, gather(s), preferred_element_type=F32) for s in range(N2_TILE)]
    for s in range(N2_TILE):
        packed = _pack_pair(r[s][:kblk], r[s][kblk:])
        for c in range(SLABS):
            o2[pl.ds(c * kblk * N2_TILE + s, kblk, stride=N2_TILE), :] = packed[:, c * LANES:(c + 1) * LANES]


def _pack_pair(re, im):
    rb = lax.bitcast_convert_type(re.astype(BF16).astype(F32), jnp.uint32)
    ib = lax.bitcast_convert_type(im.astype(BF16).astype(F32), jnp.uint32)
    return (rb >> 16) | ib


def _unpack_pair(p):
    re = lax.bitcast_convert_type(p << 16, F32)
    im = lax.bitcast_convert_type(p & jnp.uint32(0xFFFF0000), F32)
    return re, im


def _stage_a(x5, fmat, mb):
    nz, p, a, n2, _ = x5.shape
    m = fmat.shape[0]
    kblk = mb // 2
    xspec = lambda c: pl.BlockSpec((nz, 1, a, N2_TILE, LANES), lambda pi, j, mi: (0, pi, 0, j, c))
    return pl.pallas_call(
        functools.partial(_stage_a_kernel, nz=nz, a=a, mb=mb, packed_in=x5.dtype == jnp.uint32),
        grid=(p, n2 // N2_TILE, m // mb),
        in_specs=[xspec(c) for c in range(SLABS)] + [pl.BlockSpec((mb, nz * a), lambda pi, j, mi: (mi, 0))],
        out_specs=pl.BlockSpec((1, SLABS, kblk, N2_TILE, LANES), lambda pi, j, mi: (pi, 0, mi, j, 0)),
        out_shape=jax.ShapeDtypeStruct((p, SLABS, m // 2, n2, LANES), jnp.uint32),
        compiler_params=_cparams("parallel", "parallel", "parallel"),
        name="hy_stage_a",
    )(*([x5] * SLABS), fmat)


def _pair_a_kernel(ux_ref, f_ref, o_ref):
    nz, _, a, tiles, c = ux_ref.shape
    n1 = o_ref.shape[1]
    for t in range(tiles // N2_TILE):
        sl = slice(t * N2_TILE, (t + 1) * N2_TILE)
        u = _unpack_pair(ux_ref[:, 0, :, sl, :].reshape(nz * a * N2_TILE, c))[0].astype(BF16)
        r = jnp.dot(f_ref[...], u, preferred_element_type=F32)
        half = r.shape[0] // 2
        o_ref[0, :, sl, :] = _pack_pair(r[:half], r[half:]).reshape(n1, N2_TILE, c)


def _pair_a(ux5, fx):
    nz, p, a, n2, c = ux5.shape
    n1 = fx.shape[0] // (2 * N2_TILE)
    rows = _pick(n2, PAIR_STEP_ROWS)
    return pl.pallas_call(
        _pair_a_kernel,
        grid=(p, n2 // rows),
        in_specs=[pl.BlockSpec((nz, 1, a, rows, c), lambda pi, j: (0, pi, 0, j, 0)), _const_spec(fx.shape)],
        out_specs=pl.BlockSpec((1, n1, rows, c), lambda pi, j: (pi, 0, j, 0)),
        out_shape=jax.ShapeDtypeStruct((p, n1, n2, c), jnp.uint32),
        compiler_params=_cparams("parallel", "parallel"),
        name="hy_stage_a",
    )(ux5, fx)


def _pair_c_kernel(v_ref, f_ref, ux_ref, skip_ref, o_ref):
    nz, _, a, tiles, c = ux_ref.shape
    n1 = v_ref.shape[1]
    for t in range(tiles // N2_TILE):
        sl = slice(t * N2_TILE, (t + 1) * N2_TILE)
        vr, vi = _unpack_pair(v_ref[0, :, sl, :].reshape(n1 * N2_TILE, c))
        y = jnp.dot(f_ref[...], jnp.concatenate([vr, vi], axis=0).astype(BF16), preferred_element_type=F32)
        half = y.shape[0] // 2
        uu, xx = _unpack_pair(ux_ref[:, 0, :, sl, :].reshape(2 * half, c))
        val = xx * (y + skip_ref[...] * uu)
        o_ref[0, :, sl, :] = _pack_pair(val[:half], val[half:]).reshape(a, N2_TILE, c)


def _pair_c(v4, fx, ux5, skip):
    nz, p, a, n2, c = ux5.shape
    n1 = v4.shape[1]
    rows = _pick(n2, PAIR_STEP_ROWS)
    return pl.pallas_call(
        _pair_c_kernel,
        grid=(p, n2 // rows),
        in_specs=[pl.BlockSpec((1, n1, rows, c), lambda pi, j: (pi, 0, j, 0)), _const_spec(fx.shape),
                  pl.BlockSpec((nz, 1, a, rows, c), lambda pi, j: (0, pi, 0, j, 0)), _const_spec((1, c))],
        out_specs=pl.BlockSpec((1, a, rows, c), lambda pi, j: (pi, 0, j, 0)),
        out_shape=jax.ShapeDtypeStruct((p, a, n2, c), jnp.uint32),
        compiler_params=_cparams("parallel", "parallel"),
        name="hy_stage_c",
    )(v4, fx, ux5, skip)


MXU_COLS = 256
SLABS_PER_DOT = MXU_COLS // LANES


def _slab_rows(a_ref, j, h):
    if len(a_ref.shape) == 4:
        re, im = _unpack_pair(a_ref[0, j, :, h * MXU_COLS:(h + 1) * MXU_COLS])
        return jnp.concatenate([re, im], axis=0).astype(BF16)
    parts = [_unpack_pair(a_ref[0, h * SLABS_PER_DOT + c, j]) for c in range(SLABS_PER_DOT)]
    return jnp.concatenate([jnp.concatenate([re, im], axis=0) for re, im in parts], axis=1).astype(BF16)


def _spectral_mul(u, kr, ki, n2):
    ur, ui = u[:n2], u[n2:]
    return jnp.concatenate([ur * kr - ui * ki, ur * ki + ui * kr], axis=0).astype(BF16)


def _stage_b_kernel(a_ref, kf_ref, g_ref, gt_ref, *rest, herm):
    if herm:
        edge_ref, kfh_ref, o_ref = rest
    else:
        (o_ref,) = rest
    kb, n2 = a_ref.shape[-3], a_ref.shape[-2]

    def store(j, h, v):
        packed = _pack_pair(v[:n2], v[n2:])
        if len(o_ref.shape) == 4:
            o_ref[0, j, :, h * MXU_COLS:(h + 1) * MXU_COLS] = packed
        else:
            for c in range(SLABS_PER_DOT):
                o_ref[0, h * SLABS_PER_DOT + c, j] = packed[:, c * LANES:(c + 1) * LANES]

    dot = functools.partial(jnp.dot, preferred_element_type=F32)
    chains = [(j, h) for j in range(kb) for h in range(SLABS // SLABS_PER_DOT)]
    for c0 in range(0, len(chains), STAGE_B_CHAINS):
        batch = chains[c0:c0 + STAGE_B_CHAINS]
        fwd = []
        for j, h in batch:
            x = _slab_rows(a_ref, j, h)
            if herm and j == 0:
                fwd.append((dot(edge_ref[0, 0], x), dot(edge_ref[0, 1], x)))
            else:
                fwd.append((dot(g_ref[j], x),))
        prod = []
        for (j, h), us in zip(batch, fwd):
            lanes = slice(h * MXU_COLS, (h + 1) * MXU_COLS)
            ps = [_spectral_mul(us[0], kf_ref[0, j, :, lanes], kf_ref[1, j, :, lanes], n2)]
            if len(us) == 2:
                ps.append(_spectral_mul(us[1], kfh_ref[0, 0, :, lanes], kfh_ref[0, 1, :, lanes], n2))
            prod.append(ps)
        for (j, h), ps in zip(batch, prod):
            if len(ps) == 2:
                store(j, h, dot(edge_ref[0, 2], ps[0]) + dot(edge_ref[0, 3], ps[1]))
            else:
                store(j, h, dot(gt_ref[j], ps[0]))


def _stage_b(a5, kf, g, gt, kb, edge=None, kfh=None):
    p, k1n, n2 = a5.shape[0], a5.shape[-3], a5.shape[-2]
    herm = edge is not None
    if a5.ndim == 4:
        blk = pl.BlockSpec((1, kb, n2, HY_CH), lambda i, pi: (pi, i, 0, 0))
    else:
        blk = pl.BlockSpec((1, SLABS, kb, n2, LANES), lambda i, pi: (pi, 0, i, 0, 0))
    gspec = pl.BlockSpec((kb, 2 * n2, 2 * n2), lambda i, pi: (i, 0, 0))
    extra = [pl.BlockSpec((1,) + edge.shape[1:], lambda i, pi: (i, 0, 0, 0)),
             pl.BlockSpec((1,) + kfh.shape[1:], lambda i, pi: (i, 0, 0, 0))] if herm else []
    return pl.pallas_call(
        functools.partial(_stage_b_kernel, herm=herm),
        grid=(k1n // kb, p),
        in_specs=[blk, pl.BlockSpec((2, kb, n2, HY_CH), lambda i, pi: (0, i, 0, 0)), gspec, gspec] + extra,
        out_specs=blk,
        out_shape=jax.ShapeDtypeStruct(a5.shape, jnp.uint32),
        compiler_params=_cparams("parallel", "parallel"),
        name="hy_stage_b",
    )(a5, kf, g, gt, *([edge, kfh] if herm else []))


def _stage_bf_kernel(af_ref, ab_ref, g_ref, *rest, scale, herm):
    if herm:
        edge_ref, o_ref, oh_ref = rest
    else:
        (o_ref,) = rest
    kb, n2 = af_ref.shape[2], af_ref.shape[3]

    def spectrum(mat, j, h):
        conj_mat = jnp.concatenate([mat[:n2], -mat[n2:]], axis=0)
        x = jnp.concatenate([_slab_rows(af_ref, j, h), _slab_rows(ab_ref, j, h)], axis=0)
        return jnp.dot(jnp.concatenate([mat, conj_mat], axis=1), x, preferred_element_type=F32) * scale

    for j in range(kb):
        for h in range(SLABS // SLABS_PER_DOT):
            lanes = slice(h * MXU_COLS, (h + 1) * MXU_COLS)
            if herm and j == 0:
                u = spectrum(edge_ref[0, 0], 0, h)
                uh = spectrum(edge_ref[0, 1], 0, h)
                oh_ref[0, 0, :, lanes] = uh[:n2]
                oh_ref[0, 1, :, lanes] = uh[n2:]
            else:
                u = spectrum(g_ref[j], j, h)
            o_ref[0, j, :, lanes] = u[:n2]
            o_ref[1, j, :, lanes] = u[n2:]


def _stage_bf(a5f, a5b, g, kb, scale, edge=None):
    _, _, k1n, n2, _ = a5f.shape
    herm = edge is not None
    aspec = pl.BlockSpec((1, SLABS, kb, n2, LANES), lambda i: (0, 0, i, 0, 0))
    out_specs = [pl.BlockSpec((2, kb, n2, HY_CH), lambda i: (0, i, 0, 0))]
    out_shape = [jax.ShapeDtypeStruct((2, k1n, n2, HY_CH), F32)]
    if herm:
        out_specs.append(pl.BlockSpec((1, 2, n2, HY_CH), lambda i: (i, 0, 0, 0)))
        out_shape.append(jax.ShapeDtypeStruct((k1n // kb, 2, n2, HY_CH), F32))
    return pl.pallas_call(
        functools.partial(_stage_bf_kernel, scale=scale, herm=herm),
        grid=(k1n // kb,),
        in_specs=[aspec, aspec, pl.BlockSpec((kb, 2 * n2, 2 * n2), lambda i: (i, 0, 0))]
        + ([pl.BlockSpec((1,) + edge.shape[1:], lambda i: (i, 0, 0, 0))] if herm else []),
        out_specs=out_specs,
        out_shape=out_shape,
        compiler_params=_cparams("parallel"),
        name="hy_filter_spectrum",
    )(a5f, a5b, g, *([edge] if herm else []))


def _stage_c_kernel(*refs, nz, a, mb):
    v_ref, f_ref = refs[0], refs[1]
    ux_refs = refs[2:2 + SLABS]
    skip_ref, o_ref, acc_ref = refs[2 + SLABS:]
    mk = pl.program_id(2)
    kblk = mb // 2
    v2 = v_ref.reshape(SLABS * kblk * N2_TILE, LANES)

    @pl.when(mk == 0)
    def _():
        acc_ref[...] = jnp.zeros_like(acc_ref)

    for s in range(N2_TILE):
        packed = jnp.concatenate([v2[pl.ds(c * kblk * N2_TILE + s, kblk, stride=N2_TILE), :] for c in range(SLABS)],
                                 axis=1)
        vs = jnp.concatenate(_unpack_pair(packed), axis=0).astype(BF16)
        acc_ref[s] += jnp.dot(f_ref[...], vs, preferred_element_type=F32)

    @pl.when(mk == pl.num_programs(2) - 1)
    def _():
        o2 = o_ref.reshape(SLABS * a * N2_TILE, LANES)
        ux2 = [r.reshape(nz * a * N2_TILE, LANES) for r in ux_refs]
        for s in range(N2_TILE):
            y = acc_ref[s]
            for c in range(SLABS):
                vals = []
                for z in range(nz):
                    uu, xx = _unpack_pair(ux2[c][pl.ds(z * a * N2_TILE + s, a, stride=N2_TILE), :])
                    yc = y[z * a:(z + 1) * a, c * LANES:(c + 1) * LANES]
                    vals.append(xx * (yc + skip_ref[0:1, c * LANES:(c + 1) * LANES] * uu))
                o2[pl.ds(c * a * N2_TILE + s, a, stride=N2_TILE), :] = _pack_pair(*vals) if nz == 2 else vals[0]


def _stage_c(v5, fmat, ux5, skip, mb):
    nz, p, a, n2, _ = ux5.shape
    m = 2 * v5.shape[2]
    xspec = lambda c: pl.BlockSpec((nz, 1, a, N2_TILE, LANES), lambda pi, j, mk: (0, pi, 0, j, c))
    return pl.pallas_call(
        functools.partial(_stage_c_kernel, nz=nz, a=a, mb=mb),
        grid=(p, n2 // N2_TILE, m // mb),
        in_specs=[pl.BlockSpec((1, SLABS, mb // 2, N2_TILE, LANES), lambda pi, j, mk: (pi, 0, mk, j, 0)),
                  pl.BlockSpec((nz * a, mb), lambda pi, j, mk: (0, mk))]
        + [xspec(c) for c in range(SLABS)] + [_const_spec((1, HY_CH))],
        out_specs=pl.BlockSpec((1, SLABS, a, N2_TILE, LANES), lambda pi, j, mk: (pi, 0, 0, j, 0)),
        out_shape=jax.ShapeDtypeStruct((p, SLABS, a, n2, LANES), jnp.uint32 if nz == 2 else F32),
        scratch_shapes=[pltpu.VMEM((N2_TILE, nz * a, HY_CH), F32)],
        compiler_params=_cparams("parallel", "parallel", "arbitrary"),
        name="hy_stage_c",
    )(v5, fmat, *([ux5] * SLABS), skip)


def _filter_kernel(zt_ref, tn_ref, w1t_ref, b1_ref, fr_ref, w2t_ref, b2_ref, w3_ref, dl_ref, of_ref, ob_ref):
    hi = lax.Precision.HIGHEST
    fr = fr_ref[...]
    h = jnp.sin(fr * (jnp.dot(w1t_ref[...], zt_ref[...], precision=hi, preferred_element_type=F32) + b1_ref[...]))
    h = jnp.sin(fr * (jnp.dot(w2t_ref[...], h, precision=hi, preferred_element_type=F32) + b2_ref[...]))
    taps = jnp.dot(h.T.astype(BF16), w3_ref[...], preferred_element_type=F32)
    tr = taps.shape[0]
    decay = jnp.exp(-tn_ref[...] * dl_ref[...])
    of_ref[...] = taps[:, :HY_CH] * decay
    t = pl.program_id(0) * tr + lax.broadcasted_iota(jnp.int32, (tr, HY_CH), 0)
    ob_ref[...] = jnp.where(t == 0, 0.0, taps[:, HY_CH:] * decay)


def _filter_taps(zfeat, f_w1, f_b1, f_freq, f_w2, f_b2, f_w3, absdelta, tr):
    n = zfeat.shape[0]
    cs = lambda a: _const_spec(a.shape)
    args = (zfeat.T, zfeat[:, 0:1], f_w1.T, f_b1.T, f_freq.T, f_w2.T, f_b2.T, f_w3.astype(BF16), absdelta)
    out = pl.BlockSpec((tr, HY_CH), lambda i: (i, 0))
    return pl.pallas_call(
        _filter_kernel,
        grid=(n // tr,),
        in_specs=[pl.BlockSpec((POS_DIM, tr), lambda i: (0, i)), pl.BlockSpec((tr, 1), lambda i: (i, 0))]
        + [cs(a) for a in args[2:]],
        out_specs=[out, out],
        out_shape=[jax.ShapeDtypeStruct((n, HY_CH), F32)] * 2,
        compiler_params=_cparams("parallel"),
        name="hy_filter_taps",
    )(*args)


def _bias_kernel(rb_ref, oh_ref, o_ref):
    o_ref[...] = jnp.dot(rb_ref[...], oh_ref[...], precision=lax.Precision.HIGHEST,
                         preferred_element_type=F32)


def _bias_table(rel_bias):
    i = jnp.arange(BLOCK)[:, None]
    j = jnp.arange(3 * BLOCK)[None, :]
    rel = j - BLOCK - i
    nb2 = N_BUCKETS // 2
    max_exact = nb2 // 2
    n = jnp.abs(rel)
    large = max_exact + (jnp.log(jnp.maximum(n, 1).astype(F32) / max_exact)
                         / math.log(MAX_DIST / max_exact) * (nb2 - max_exact)).astype(jnp.int32)
    large = jnp.minimum(large, nb2 - 1)
    bucket = jnp.where(rel > 0, nb2, 0) + jnp.where(n < max_exact, n, large)
    onehot = (bucket.reshape(1, -1) == jnp.arange(N_BUCKETS)[:, None]).astype(F32)
    cols = onehot.shape[1]
    tc = cols // 4
    table = pl.pallas_call(
        _bias_kernel,
        grid=(4,),
        in_specs=[_const_spec((ATT_HEADS, N_BUCKETS)), pl.BlockSpec((N_BUCKETS, tc), lambda c: (0, c))],
        out_specs=pl.BlockSpec((ATT_HEADS, tc), lambda c: (0, c)),
        out_shape=jax.ShapeDtypeStruct((ATT_HEADS, cols), F32),
        compiler_params=_cparams("parallel"),
        name="att_bias_table",
    )(rel_bias.T, onehot)
    table = table.reshape(ATT_HEADS, BLOCK, 3 * BLOCK)
    table = jnp.where((n <= WINDOW)[None], table, -jnp.inf)
    table = table.reshape(KV_HEADS, GQA_GROUP, BLOCK, 3 * BLOCK)
    return table.transpose(0, 3, 1, 2).reshape(KV_HEADS, 3 * BLOCK, GQA_GROUP * BLOCK)


def _attn_kernel(q_ref, km_ref, kp_ref, kn_ref, vm_ref, vp_ref, vn_ref, bias_ref, sink_ref, o_ref, *, nsub):
    i = pl.program_id(1)
    neg = -jnp.inf
    pen_first = jnp.where(i == 0, neg, 0.0)
    pen_last = jnp.where(i == pl.num_programs(1) - 1, neg, 0.0)
    kwin = jnp.concatenate([kp_ref[0], km_ref[0], kn_ref[0]], axis=0)
    vwin = jnp.concatenate([vp_ref[0], vm_ref[0], vn_ref[0]], axis=1)
    lane = lax.broadcasted_iota(jnp.int32, (BLOCK, LANES), 1)
    lo_half = lane < HEAD_DIM
    ones = jnp.ones((SUM_ROWS, 3 * BLOCK), BF16)
    units = [(s, g) for s in range(nsub) for g in range(KV_HEADS)]

    def scores(s, g):
        qs = q_ref[0, s * BLOCK:(s + 1) * BLOCK, :]
        rows = []
        for pr in range(2):
            qp = qs[:, (2 * g + pr) * LANES:(2 * g + pr + 1) * LANES]
            rows.append(jnp.where(lo_half, qp, jnp.zeros_like(qp)))
            rows.append(jnp.where(lo_half, jnp.zeros_like(qp), qp))
        qg = jnp.concatenate(rows, axis=0)
        kg = kwin[s * BLOCK:(s + 3) * BLOCK, g * LANES:(g + 1) * LANES]
        t = lax.dot_general(kg, qg, _NT, preferred_element_type=F32) + bias_ref[g]
        if s == 0:
            t = jnp.concatenate([t[:BLOCK] + pen_first, t[BLOCK:]], axis=0)
        if s == nsub - 1:
            t = jnp.concatenate([t[:2 * BLOCK], t[2 * BLOCK:] + pen_last], axis=0)
        return t

    def pv(s, g, p):
        vg = jnp.concatenate([vwin[g * HEAD_DIM:(g + 1) * HEAD_DIM, s * BLOCK:(s + 3) * BLOCK], ones],
                             axis=0)
        return jnp.dot(vg, p, preferred_element_type=F32)

    for u0 in range(0, len(units), ATT_UNITS):
        batch = units[u0:u0 + ATT_UNITS]
        sc = [scores(s, g) for s, g in batch]
        m = [jnp.maximum(jnp.max(t, axis=0, keepdims=True), sink_ref[g]) for t, (_, g) in zip(sc, batch)]
        p = [jnp.exp2(t - mm).astype(BF16) for t, mm in zip(sc, m)]
        o = [pv(s, g, pp) for (s, g), pp in zip(batch, p)]
        for (s, g), oo, mm in zip(batch, o, m):
            on = oo[:HEAD_DIM] / (oo[HEAD_DIM:HEAD_DIM + 1] + jnp.exp2(sink_ref[g] - mm))
            for pr in range(2):
                pair = jnp.concatenate([on[:, (2 * pr) * BLOCK:(2 * pr + 1) * BLOCK],
                                        on[:, (2 * pr + 1) * BLOCK:(2 * pr + 2) * BLOCK]], axis=0)
                o_ref[0, s * BLOCK:(s + 1) * BLOCK, (2 * g + pr) * LANES:(2 * g + pr + 1) * LANES] = (
                    pair.T.astype(o_ref.dtype))


SUM_ROWS = 16
ATT_UNITS = 16
ATT_TQ = 1024


def _attention(q, kd, vt, bias_t, sink_t, tq):
    b, l, _ = q.shape
    nsub = tq // BLOCK
    nblk = l // BLOCK
    kw = 2 * KV_W
    main = lambda w: pl.BlockSpec((1, tq, w), lambda bi, i: (bi, i, 0))
    prev = pl.BlockSpec((1, BLOCK, kw), lambda bi, i: (bi, jnp.maximum(i * nsub - 1, 0), 0))
    nxt = pl.BlockSpec((1, BLOCK, kw), lambda bi, i: (bi, jnp.minimum((i + 1) * nsub, nblk - 1), 0))
    vmain = pl.BlockSpec((1, KV_W, tq), lambda bi, i: (bi, 0, i))
    vprev = pl.BlockSpec((1, KV_W, BLOCK), lambda bi, i: (bi, 0, jnp.maximum(i * nsub - 1, 0)))
    vnxt = pl.BlockSpec((1, KV_W, BLOCK), lambda bi, i: (bi, 0, jnp.minimum((i + 1) * nsub, nblk - 1)))
    return pl.pallas_call(
        functools.partial(_attn_kernel, nsub=nsub),
        grid=(b, l // tq),
        in_specs=[main(ATT_W), main(kw), prev, nxt, vmain, vprev, vnxt,
                  _const_spec(bias_t.shape), _const_spec(sink_t.shape)],
        out_specs=main(ATT_W),
        out_shape=jax.ShapeDtypeStruct((b, l, ATT_W), BF16),
        compiler_params=_cparams("parallel", "parallel"),
        name="window_attn",
    )(q, kd, kd, kd, vt, vt, vt, bias_t, sink_t)


FF_CHUNK = 256


def _ffn_kernel(x_ref, yh_ref, ya_ref, hg_ref, ag_ref, wo_ref, g2_ref, wgu_ref, wd_ref, o_ref, act_ref, *, pairs, nt):
    def rms(t, g):
        return (t * lax.rsqrt(jnp.mean(t * t, axis=-1, keepdims=True) + EPS) * g).astype(BF16)

    tm = x_ref.shape[0]
    groups = [slice(r, r + tm // FFN_GROUPS) for r in range(0, tm, tm // FFN_GROUPS)]

    def hyena_rows(rs):
        if len(yh_ref.shape) == 3:
            yh = yh_ref[0, rs, :]
        else:
            yh = jnp.concatenate([yh_ref[0, c, rs, :] for c in range(SLABS)], axis=1)
        if pairs:
            lo, hi = _unpack_pair(yh)
            yh = jnp.where(pl.program_id(0) // nt < pairs, lo, hi)
        return yh

    mixed = [jnp.concatenate([rms(hyena_rows(rs), hg_ref[...]), rms(ya_ref[rs, :].astype(F32), ag_ref[...])], axis=1)
             for rs in groups]
    h = [x_ref[rs, :] + jnp.dot(m, wo_ref[...], preferred_element_type=F32) for rs, m in zip(groups, mixed)]
    hn = [rms(t, g2_ref[...]) for t in h]
    for c in range(D_FF // FF_CHUNK):
        lo = c * FF_CHUNK
        for rs, t in zip(groups, hn):
            gate = jnp.dot(t, wgu_ref[:, lo:lo + FF_CHUNK], preferred_element_type=F32)
            up = jnp.dot(t, wgu_ref[:, D_FF + lo:D_FF + lo + FF_CHUNK], preferred_element_type=F32)
            act_ref[rs, lo:lo + FF_CHUNK] = (gate / (1.0 + jnp.exp(-gate)) * up).astype(BF16)
    for rs, t in zip(groups, h):
        o_ref[rs, :] = t + jnp.dot(act_ref[rs, :], wd_ref[...], preferred_element_type=F32)


def _ffn(x2d, yh, ya, hg, ag, wo_b, g2, wgu_b, wd_b, tm):
    t = x2d.shape[0]
    nt = yh.shape[-2] // tm
    pairs = yh.shape[0] if yh.dtype == jnp.uint32 else 0
    row = lambda w: pl.BlockSpec((tm, w), lambda i: (i, 0))
    if yh.ndim == 3:
        yspec = pl.BlockSpec((1, tm, HY_CH), lambda i: ((i // nt) % yh.shape[0], i % nt, 0))
    else:
        yspec = pl.BlockSpec((1, SLABS, tm, LANES), lambda i: ((i // nt) % yh.shape[0], 0, i % nt, 0))
    return pl.pallas_call(
        functools.partial(_ffn_kernel, pairs=pairs, nt=nt),
        grid=(t // tm,),
        in_specs=[row(D_MODEL), yspec, row(ATT_W), _const_spec((1, HY_CH)), _const_spec((1, ATT_W)),
                  _const_spec((D_MODEL, D_MODEL)), _const_spec((1, D_MODEL)),
                  _const_spec((D_MODEL, 2 * D_FF)), _const_spec((D_FF, D_MODEL))],
        out_specs=row(D_MODEL),
        out_shape=jax.ShapeDtypeStruct((t, D_MODEL), F32),
        scratch_shapes=[pltpu.VMEM((tm, D_FF), BF16)],
        compiler_params=_cparams("parallel"),
        name="outproj_swiglu",
    )(x2d, yh, ya, hg, ag, wo_b, g2, wgu_b, wd_b)


def _dft_tables(seq_len, paired, kblk):
    n = 2 * seq_len
    n1 = n // FFT_N2
    a = n1 // 2
    kk = np.arange(n1)[:, None]
    th_half = 2.0 * np.pi * ((kk * np.arange(a)[None, :]) % n1) / n1
    c, s = np.cos(th_half), np.sin(th_half)

    def blocked(re, im):
        return np.concatenate([np.concatenate([re[i:i + kblk], im[i:i + kblk]], axis=0)
                               for i in range(0, re.shape[0], kblk)], axis=0)

    def real_input_rows(cc, ss):
        im = -ss[:a].copy()
        im[0] = cc[a]
        return cc[:a], im

    if paired:
        fa = blocked(np.concatenate([c, s], axis=1), np.concatenate([-s, c], axis=1))
        fc = fa.T
        faf = blocked(c, -s)
    else:
        fa = blocked(*real_input_rows(c, s))
        re_cols = np.concatenate([c[:1], 2.0 * c[1:a]], axis=0)
        im_cols = np.concatenate([c[a:a + 1], -2.0 * s[1:a]], axis=0)
        fc = blocked(re_cols, im_cols).T
        faf = fa
    to = lambda m: jnp.asarray(m.astype(np.float32)).astype(BF16)
    kron = None
    if paired and 2 * n1 * N2_TILE <= PAIR_KRON_ROWS:
        eye = np.eye(N2_TILE)
        kron = (to(np.kron(fa, eye)), to(np.kron(fc, eye)))
    return to(fa), to(fc), to(faf), kron


def _edge_matrices(g, half, kb):
    zero = jnp.zeros_like(g[0])
    z = zero[:, :FFT_N2]
    col = lambda m: m[:, :FFT_N2]
    sa = jnp.concatenate([col(g[0]), z], axis=1)
    sb = jnp.concatenate([z, col(g[half])], axis=1)
    blocks = [jnp.stack([sa, sb, sa.T, sb.T])]
    blocks += [jnp.stack([g[i], zero, g[i].T, zero]) for i in range(kb, half, kb)]
    return jnp.stack(blocks)


def _twiddle_tables(seq_len, k1_count):
    n = 2 * seq_len
    n1 = n // FFT_N2
    k1 = np.arange(k1_count, dtype=np.int64)[:, None, None]
    k2 = np.arange(FFT_N2, dtype=np.int64)[None, :, None]
    n2 = np.arange(FFT_N2, dtype=np.int64)[None, None, :]
    ang = ((n2 * (k2 * n1 + k1)) % n) * (-2.0 * np.pi / n)
    gr, gi = np.cos(ang), np.sin(ang)
    g = np.concatenate([np.concatenate([gr, -gi], axis=2), np.concatenate([gi, gr], axis=2)], axis=1)
    g = g.astype(np.float32)
    return jnp.asarray(g.astype(BF16)), jnp.asarray(np.swapaxes(g, 1, 2).astype(BF16))


def _filter_features(seq_len):
    t_idx = jnp.arange(seq_len, dtype=F32)[:, None]
    t_norm = t_idx / max(seq_len - 1, 1)
    bands = jnp.linspace(1e-4, N_BANDS - 1, N_BANDS, dtype=F32)
    w = (2.0 * math.pi) * t_idx * bands[None, :] / seq_len
    return jnp.concatenate([t_norm, jnp.cos(w), -jnp.sin(w)], axis=-1)


def _abs_deltas():
    min_decay = math.log(DECAY_TARGET) / FAST_DECAY_PCT
    max_decay = math.log(DECAY_TARGET) / SLOW_DECAY_PCT
    return jnp.abs(jnp.linspace(min_decay, max_decay, HY_CH, dtype=F32))[None, :]


def _pick(n, target):
    t = min(n, target)
    while n % t:
        t //= 2
    return t


def _layer(x, lw, shared, st):
    b, l, _ = x.shape
    t = b * l
    x2d = x.reshape(t, D_MODEL)
    tm = _pick(l, FFN_TM)
    ux, q, kd, vt = _inproj(x2d, lw["norm1"], lw["w_in"], lw["conv_w"], lw["conv_b"],
                            shared["bq"], shared["bk"], lw["qg"], lw["kg"], _pick(l, INPROJ_TM), l)
    r3 = lambda a: a.reshape(b, l, a.shape[-1])

    n1 = 2 * l // FFT_N2
    a = n1 // 2
    nz = st["nz"]
    p = b // nz
    ux5 = ux.reshape(nz, p, a, FFT_N2, HY_CH)
    mb = _pick(st["fa"].shape[0], STAGE_MB)
    kf, kfh = lw["kf"][st["key"]]
    if st["kron"] is not None:
        ah = _pair_a(ux5, st["kron"][0])
        vv = _stage_b(ah, kf, st["g"], st["gt"], _pick(n1, STAGE_KB))
        yh = _pair_c(vv, st["kron"][1], ux5, lw["skip"]).reshape(p, l, HY_CH)
    else:
        ah = _stage_a(ux5, st["fa"], mb)
        vv = _stage_b(ah, kf, st["g"], st["gt"], _pick(ah.shape[2], STAGE_KB), st["edge"], kfh)
        yh = _stage_c(vv, st["fc"], ux5, lw["skip"], mb).reshape(p, SLABS, l, LANES)

    ya = _attention(r3(q), r3(kd), vt, shared["bias"], lw["sink_t"], _pick(l, ATT_TQ))
    ya = ya.reshape(t, ATT_W)

    out = _ffn(x2d, yh, ya, lw["hy_gain"], lw["at_gain"], lw["w_out"], lw["norm2"],
               lw["w_gate_up"], lw["w_down"], tm)
    return out.reshape(b, l, D_MODEL)


def _filter_spectrum(lw_raw, st, seq_len):
    n = 2 * seq_len
    a = seq_len // FFT_N2
    taps_f, taps_b = _filter_taps(st["zfeat"], lw_raw["f_w1"], lw_raw["f_b1"], lw_raw["f_freq"], lw_raw["f_w2"],
                                  lw_raw["f_b2"], lw_raw["f_w3"], st["absdelta"], _pick(seq_len, 512))
    mb = _pick(st["faf"].shape[0], STAGE_MB)
    ahf = _stage_a(taps_f.reshape(1, 1, a, FFT_N2, HY_CH), st["faf"], mb)
    ahb = _stage_a(taps_b.reshape(1, 1, a, FFT_N2, HY_CH), st["faf"], mb)
    out = _stage_bf(ahf, ahb, st["g"], _pick(ahf.shape[2], STAGE_KB), 1.0 / n, st["edge"])
    return (out[0], out[1]) if st["edge"] is not None else (out[0], None)


def kernel(x_prompt, x_sample, norm1, w_in, conv_w, conv_b, f_w1, f_b1, f_freq, f_w2, f_b2, f_w3,
           hy_skip, q_gain, k_gain, sink, rel_bias, hy_gain, at_gain, w_out, norm2, w_gate_up, w_down):
    depth = norm1.shape[0]
    streams = {}
    for key, x in (("prompt", x_prompt), ("sample", x_sample)):
        b, l, _ = x.shape
        paired = b % 2 == 0
        n1 = 2 * l // FFT_N2
        fa, fc, faf, kron = _dft_tables(l, paired, _pick(2 * n1 if paired else n1, STAGE_MB) // 2)
        g, gt = _twiddle_tables(l, n1 if paired else n1 // 2 + 1)
        streams[key] = dict(key=key, nz=2 if paired else 1, fa=fa, fc=fc, faf=faf, kron=kron, g=g, gt=gt,
                            edge=None if paired else _edge_matrices(g, n1 // 2, _pick(n1 // 2, STAGE_KB)),
                            zfeat=_filter_features(l), absdelta=_abs_deltas())

    head_id = np.arange(ATT_W) // HEAD_DIM
    shared = dict(
        bq=jnp.asarray((head_id[:, None] == head_id[None, :]).astype(np.float32) / HEAD_DIM).astype(BF16),
        bk=jnp.asarray((head_id[:KV_W, None] == head_id[None, :KV_W]).astype(np.float32) / HEAD_DIM).astype(BF16),
        bias=_bias_table(rel_bias) * LOG2E,
    )

    y_prompt, y_sample = x_prompt, x_sample
    for li in range(depth):
        raw = dict(f_w1=f_w1[li], f_b1=f_b1[li][None, :], f_freq=f_freq[li][None, :], f_w2=f_w2[li],
                   f_b2=f_b2[li][None, :], f_w3=f_w3[li])
        lw = dict(
            norm1=norm1[li][None, :], w_in=w_in[li].astype(BF16),
            qg=jnp.tile(q_gain[li], ATT_HEADS)[None, :] * (HEAD_DIM ** -0.5 * LOG2E),
            kg=jnp.tile(k_gain[li], KV_HEADS)[None, :],
            conv_w=conv_w[li], conv_b=conv_b[li][None, :],
            skip=hy_skip[li][None, :],
            sink_t=jnp.repeat(sink[li] * LOG2E, BLOCK).reshape(KV_HEADS, 1, GQA_GROUP * BLOCK),
            hy_gain=hy_gain[li][None, :], at_gain=at_gain[li][None, :],
            w_out=w_out[li].astype(BF16), norm2=norm2[li][None, :],
            w_gate_up=w_gate_up[li].astype(BF16), w_down=w_down[li].astype(BF16),
        )
        lw["kf"] = {key: _filter_spectrum(raw, st, {"prompt": x_prompt, "sample": x_sample}[key].shape[1])
                    for key, st in streams.items()}
        y_prompt = _layer(y_prompt, lw, shared, streams["prompt"])
        y_sample = _layer(y_sample, lw, shared, streams["sample"])
    return (y_prompt, y_sample)
```

```python
import functools
import math

import numpy as np
import jax
import jax.numpy as jnp
from jax import lax
from jax.experimental import pallas as pl
from jax.experimental.pallas import tpu as pltpu

F32 = jnp.float32
BF16 = jnp.bfloat16

D_MODEL = 1024
ATT_HEADS = 8
KV_HEADS = 2
HEAD_DIM = 64
GQA_GROUP = ATT_HEADS // KV_HEADS
ATT_W = ATT_HEADS * HEAD_DIM
KV_W = KV_HEADS * HEAD_DIM
WINDOW = 128
BLOCK = 128
N_BUCKETS = 32
MAX_DIST = 128
HY_CH = D_MODEL - ATT_W
FILTER_HIDDEN = 64
N_BANDS = 16
POS_DIM = 1 + 2 * N_BANDS
FAST_DECAY_PCT = 0.3
SLOW_DECAY_PCT = 1.5
DECAY_TARGET = 1e-2
IN_W = 3 * HY_CH + (ATT_HEADS + 2 * KV_HEADS) * HEAD_DIM
D_FF = -(-8 * D_MODEL // (3 * 256)) * 256
EPS = 1e-6
LOG2E = math.log2(math.e)

FFT_N2 = 64
LANES = 128
N2_TILE = 8
SLABS = HY_CH // LANES
STAGE_MB = 512
STAGE_KB = 32
PAIR_KRON_ROWS = 1024
PAIR_STEP_ROWS = 64
STAGE_B_CHAINS = 8
INPROJ_TM = 512
FFN_TM = 1024
FFN_GROUPS = 2
VMEM_LIMIT = 56 * 1024 * 1024

_NT = (((1,), (1,)), ((), ()))


def _cparams(*sem):
    return pltpu.CompilerParams(dimension_semantics=sem, vmem_limit_bytes=VMEM_LIMIT)


def _const_spec(shape, layer=None):
    nd = len(shape)
    if layer is None:
        return pl.BlockSpec(shape, lambda *_: (0,) * nd, pipeline_mode=pl.Buffered(1))
    return pl.BlockSpec((None,) + tuple(shape), lambda *_: (layer,) + (0,) * nd, pipeline_mode=pl.Buffered(1))


HALO = 16


def _inproj_kernel(x_ref, xp_ref, xn_ref, g1_ref, w_ref, cw_ref, cb_ref, bq_ref, bk_ref, qg_ref, kg_ref,
                   ux_ref, q_ref, k_ref, v_ref, he_ref, qkv_ref, *, nt, ntiles):
    i = pl.program_id(0)
    tm = x_ref.shape[0]

    @pl.when(i == 0)
    def _():
        he_ref[...] = jnp.zeros_like(he_ref)
        qkv_ref[...] = jnp.zeros_like(qkv_ref)

    v = qkv_ref[:, ATT_W + KV_W:]
    v_ref[0] = v.T.astype(BF16)

    k = qkv_ref[:, ATT_W:ATT_W + KV_W]
    ms = jnp.dot((k * k).astype(BF16), bk_ref[...], preferred_element_type=F32)
    kn = k * lax.rsqrt(ms + EPS) * kg_ref[...]
    lo_half = lax.broadcasted_iota(jnp.int32, kn.shape, 1) < HEAD_DIM
    kr = pltpu.roll(kn, HEAD_DIM, axis=1)
    k_ref[...] = jnp.concatenate([jnp.where(lo_half, kn, kr), jnp.where(lo_half, kr, kn)], axis=1).astype(BF16)

    q = qkv_ref[:, 0:ATT_W]
    ms = jnp.dot((q * q).astype(BF16), bq_ref[...], preferred_element_type=F32)
    q_ref[...] = (q * lax.rsqrt(ms + EPS) * qg_ref[...]).astype(BF16)

    rows = slice(HALO, HALO + tm)

    def conv(j):
        c = slice(j * HY_CH, (j + 1) * HY_CH)
        he = he_ref[:, c]
        return (pltpu.roll(he, 1, axis=0)[rows] * cw_ref[0:1, c] + he[rows] * cw_ref[1:2, c]
                + pltpu.roll(he, tm + 2 * HALO - 1, axis=0)[rows] * cw_ref[2:3, c] + cb_ref[0:1, c])

    ux_ref[...] = _pack_pair(conv(1) * conv(2), conv(0))

    ti = jnp.minimum(i, ntiles - 1)
    xp = jnp.where(ti % nt == 0, 0.0, xp_ref[...])
    xn = jnp.where(ti % nt == nt - 1, 0.0, xn_ref[...])
    xe = jnp.concatenate([xp, x_ref[...], xn], axis=0)
    xe = (xe * lax.rsqrt(jnp.mean(xe * xe, axis=-1, keepdims=True) + EPS) * g1_ref[...]).astype(BF16)
    he_ref[...] = jnp.dot(xe, w_ref[:, 0:3 * HY_CH], preferred_element_type=F32)
    qkv_ref[...] = jnp.dot(xe[HALO:HALO + tm], w_ref[:, 3 * HY_CH:], preferred_element_type=F32)


def _inproj(x2d, g1, w_in_b, conv_w, conv_b, bq, bk, qg, kg, tm, seq_len, layer):
    t = x2d.shape[0]
    nt = seq_len // tm
    ntiles = t // tm
    nh = tm // HALO
    cur = lambda i: jnp.minimum(i, ntiles - 1)
    done = lambda i: jnp.maximum(i - 1, 0)
    row_in = pl.BlockSpec((tm, D_MODEL), lambda i: (cur(i), 0))
    prev = pl.BlockSpec((HALO, D_MODEL), lambda i: (jnp.maximum(cur(i) * nh - 1, 0), 0))
    nxt = pl.BlockSpec((HALO, D_MODEL), lambda i: (jnp.minimum((cur(i) + 1) * nh, t // HALO - 1), 0))
    row_out = lambda w: pl.BlockSpec((tm, w), lambda i: (done(i), 0))
    return pl.pallas_call(
        functools.partial(_inproj_kernel, nt=nt, ntiles=ntiles),
        grid=(ntiles + 1,),
        in_specs=[row_in, prev, nxt, _const_spec((1, D_MODEL)), _const_spec((D_MODEL, IN_W), layer),
                  _const_spec((3, 3 * HY_CH)), _const_spec((1, 3 * HY_CH)),
                  _const_spec((ATT_W, ATT_W)), _const_spec((KV_W, KV_W)),
                  _const_spec((1, ATT_W)), _const_spec((1, KV_W))],
        out_specs=[row_out(HY_CH), row_out(ATT_W), row_out(2 * KV_W),
                   pl.BlockSpec((1, KV_W, tm), lambda i: (done(i) // nt, 0, done(i) % nt))],
        out_shape=[jax.ShapeDtypeStruct((t, HY_CH), jnp.uint32),
                   jax.ShapeDtypeStruct((t, ATT_W), BF16), jax.ShapeDtypeStruct((t, 2 * KV_W), BF16),
                   jax.ShapeDtypeStruct((t // seq_len, KV_W, seq_len), BF16)],
        scratch_shapes=[pltpu.VMEM((tm + 2 * HALO, 3 * HY_CH), F32), pltpu.VMEM((tm, IN_W - 3 * HY_CH), F32)],
        compiler_params=_cparams("arbitrary"),
        name="inproj",
    )(x2d, x2d, x2d, g1, w_in_b, conv_w, conv_b, bq, bk, qg, kg)


def _stage_a_kernel(*refs, nz, a, mb, packed_in):
    x_refs, f_ref, o_ref = refs[:SLABS], refs[SLABS], refs[SLABS + 1]
    kblk = mb // 2
    xs = [r.reshape(nz * a * N2_TILE, LANES) for r in x_refs]
    o2 = o_ref.reshape(SLABS * kblk * N2_TILE, LANES)
    f = f_ref[...]

    def gather(s):
        x = jnp.concatenate(
            [jnp.concatenate([xs[c][pl.ds(z * a * N2_TILE + s, a, stride=N2_TILE), :] for z in range(nz)], axis=0)
             for c in range(SLABS)], axis=1)
        return (_unpack_pair(x)[0] if packed_in else x).astype(BF16)

    r = [jnp.dot(f, gather(s), preferred_element_type=F32) for s in range(N2_TILE)]
    for s in range(N2_TILE):
        packed = _pack_pair(r[s][:kblk], r[s][kblk:])
        for c in range(SLABS):
            o2[pl.ds(c * kblk * N2_TILE + s, kblk, stride=N2_TILE), :] = packed[:, c * LANES:(c + 1) * LANES]


def _pack_pair(re, im):
    rb = lax.bitcast_convert_type(re.astype(BF16).astype(F32), jnp.uint32)
    ib = lax.bitcast_convert_type(im.astype(BF16).astype(F32), jnp.uint32)
    return (rb >> 16) | ib


def _unpack_pair(p):
    re = lax.bitcast_convert_type(p << 16, F32)
    im = lax.bitcast_convert_type(p & jnp.uint32(0xFFFF0000), F32)
    return re, im


def _stage_a(x5, fmat, mb):
    nz, p, a, n2, _ = x5.shape
    m = fmat.shape[0]
    kblk = mb // 2
    xspec = lambda c: pl.BlockSpec((nz, 1, a, N2_TILE, LANES), lambda pi, j, mi: (0, pi, 0, j, c))
    return pl.pallas_call(
        functools.partial(_stage_a_kernel, nz=nz, a=a, mb=mb, packed_in=x5.dtype == jnp.uint32),
        grid=(p, n2 // N2_TILE, m // mb),
        in_specs=[xspec(c) for c in range(SLABS)] + [pl.BlockSpec((mb, nz * a), lambda pi, j, mi: (mi, 0))],
        out_specs=pl.BlockSpec((1, SLABS, kblk, N2_TILE, LANES), lambda pi, j, mi: (pi, 0, mi, j, 0)),
        out_shape=jax.ShapeDtypeStruct((p, SLABS, m // 2, n2, LANES), jnp.uint32),
        compiler_params=_cparams("parallel", "parallel", "parallel"),
        name="hy_stage_a",
    )(*([x5] * SLABS), fmat)


def _pair_a_kernel(ux_ref, f_ref, o_ref):
    nz, _, a, tiles, c = ux_ref.shape
    n1 = o_ref.shape[1]
    for t in range(tiles // N2_TILE):
        sl = slice(t * N2_TILE, (t + 1) * N2_TILE)
        u = _unpack_pair(ux_ref[:, 0, :, sl, :].reshape(nz * a * N2_TILE, c))[0].astype(BF16)
        r = jnp.dot(f_ref[...], u, preferred_element_type=F32)
        half = r.shape[0] // 2
        o_ref[0, :, sl, :] = _pack_pair(r[:half], r[half:]).reshape(n1, N2_TILE, c)


def _pair_a(ux5, fx):
    nz, p, a, n2, c = ux5.shape
    n1 = fx.shape[0] // (2 * N2_TILE)
    rows = _pick(n2, PAIR_STEP_ROWS)
    return pl.pallas_call(
        _pair_a_kernel,
        grid=(p, n2 // rows),
        in_specs=[pl.BlockSpec((nz, 1, a, rows, c), lambda pi, j: (0, pi, 0, j, 0)), _const_spec(fx.shape)],
        out_specs=pl.BlockSpec((1, n1, rows, c), lambda pi, j: (pi, 0, j, 0)),
        out_shape=jax.ShapeDtypeStruct((p, n1, n2, c), jnp.uint32),
        compiler_params=_cparams("parallel", "parallel"),
        name="hy_stage_a",
    )(ux5, fx)


def _pair_c_kernel(v_ref, f_ref, ux_ref, skip_ref, o_ref):
    nz, _, a, tiles, c = ux_ref.shape
    n1 = v_ref.shape[1]
    for t in range(tiles // N2_TILE):
        sl = slice(t * N2_TILE, (t + 1) * N2_TILE)
        vr, vi = _unpack_pair(v_ref[0, :, sl, :].reshape(n1 * N2_TILE, c))
        y = jnp.dot(f_ref[...], jnp.concatenate([vr, vi], axis=0).astype(BF16), preferred_element_type=F32)
        half = y.shape[0] // 2
        uu, xx = _unpack_pair(ux_ref[:, 0, :, sl, :].reshape(2 * half, c))
        val = xx * (y + skip_ref[...] * uu)
        o_ref[0, :, sl, :] = _pack_pair(val[:half], val[half:]).reshape(a, N2_TILE, c)


def _pair_c(v4, fx, ux5, skip):
    nz, p, a, n2, c = ux5.shape
    n1 = v4.shape[1]
    rows = _pick(n2, PAIR_STEP_ROWS)
    return pl.pallas_call(
        _pair_c_kernel,
        grid=(p, n2 // rows),
        in_specs=[pl.BlockSpec((1, n1, rows, c), lambda pi, j: (pi, 0, j, 0)), _const_spec(fx.shape),
                  pl.BlockSpec((nz, 1, a, rows, c), lambda pi, j: (0, pi, 0, j, 0)), _const_spec((1, c))],
        out_specs=pl.BlockSpec((1, a, rows, c), lambda pi, j: (pi, 0, j, 0)),
        out_shape=jax.ShapeDtypeStruct((p, a, n2, c), jnp.uint32),
        compiler_params=_cparams("parallel", "parallel"),
        name="hy_stage_c",
    )(v4, fx, ux5, skip)


MXU_COLS = 256
SLABS_PER_DOT = MXU_COLS // LANES


def _slab_rows(a_ref, j, h):
    if len(a_ref.shape) == 4:
        re, im = _unpack_pair(a_ref[0, j, :, h * MXU_COLS:(h + 1) * MXU_COLS])
        return jnp.concatenate([re, im], axis=0).astype(BF16)
    parts = [_unpack_pair(a_ref[0, h * SLABS_PER_DOT + c, j]) for c in range(SLABS_PER_DOT)]
    return jnp.concatenate([jnp.concatenate([re, im], axis=0) for re, im in parts], axis=1).astype(BF16)


def _spectral_mul(u, kr, ki, n2):
    ur, ui = u[:n2], u[n2:]
    return jnp.concatenate([ur * kr - ui * ki, ur * ki + ui * kr], axis=0).astype(BF16)


def _stage_b_kernel(a_ref, kf_ref, g_ref, gt_ref, *rest, herm):
    if herm:
        edge_ref, kfh_ref, o_ref = rest
    else:
        (o_ref,) = rest
    kb, n2 = a_ref.shape[-3], a_ref.shape[-2]

    def store(j, h, v):
        packed = _pack_pair(v[:n2], v[n2:])
        if len(o_ref.shape) == 4:
            o_ref[0, j, :, h * MXU_COLS:(h + 1) * MXU_COLS] = packed
        else:
            for c in range(SLABS_PER_DOT):
                o_ref[0, h * SLABS_PER_DOT + c, j] = packed[:, c * LANES:(c + 1) * LANES]

    dot = functools.partial(jnp.dot, preferred_element_type=F32)
    chains = [(j, h) for j in range(kb) for h in range(SLABS // SLABS_PER_DOT)]
    for c0 in range(0, len(chains), STAGE_B_CHAINS):
        batch = chains[c0:c0 + STAGE_B_CHAINS]
        fwd = []
        for j, h in batch:
            x = _slab_rows(a_ref, j, h)
            if herm and j == 0:
                fwd.append((dot(edge_ref[0, 0], x), dot(edge_ref[0, 1], x)))
            else:
                fwd.append((dot(g_ref[j], x),))
        prod = []
        for (j, h), us in zip(batch, fwd):
            lanes = slice(h * MXU_COLS, (h + 1) * MXU_COLS)
            ps = [_spectral_mul(us[0], kf_ref[0, j, :, lanes], kf_ref[1, j, :, lanes], n2)]
            if len(us) == 2:
                ps.append(_spectral_mul(us[1], kfh_ref[0, 0, :, lanes], kfh_ref[0, 1, :, lanes], n2))
            prod.append(ps)
        for (j, h), ps in zip(batch, prod):
            if len(ps) == 2:
                store(j, h, dot(edge_ref[0, 2], ps[0]) + dot(edge_ref[0, 3], ps[1]))
            else:
                store(j, h, dot(gt_ref[j], ps[0]))


def _stage_b(a5, kf, g, gt, kb, edge=None, kfh=None):
    p, k1n, n2 = a5.shape[0], a5.shape[-3], a5.shape[-2]
    herm = edge is not None
    if a5.ndim == 4:
        blk = pl.BlockSpec((1, kb, n2, HY_CH), lambda i, pi: (pi, i, 0, 0))
    else:
        blk = pl.BlockSpec((1, SLABS, kb, n2, LANES), lambda i, pi: (pi, 0, i, 0, 0))
    gspec = pl.BlockSpec((kb, 2 * n2, 2 * n2), lambda i, pi: (i, 0, 0))
    extra = [pl.BlockSpec((1,) + edge.shape[1:], lambda i, pi: (i, 0, 0, 0)),
             pl.BlockSpec((1,) + kfh.shape[1:], lambda i, pi: (i, 0, 0, 0))] if herm else []
    return pl.pallas_call(
        functools.partial(_stage_b_kernel, herm=herm),
        grid=(k1n // kb, p),
        in_specs=[blk, pl.BlockSpec((2, kb, n2, HY_CH), lambda i, pi: (0, i, 0, 0)), gspec, gspec] + extra,
        out_specs=blk,
        out_shape=jax.ShapeDtypeStruct(a5.shape, jnp.uint32),
        compiler_params=_cparams("parallel", "parallel"),
        name="hy_stage_b",
    )(a5, kf, g, gt, *([edge, kfh] if herm else []))


def _stage_bf_kernel(af_ref, ab_ref, g_ref, *rest, scale, herm):
    if herm:
        edge_ref, o_ref, oh_ref = rest
    else:
        (o_ref,) = rest
    kb, n2 = af_ref.shape[2], af_ref.shape[3]

    def spectrum(mat, j, h):
        conj_mat = jnp.concatenate([mat[:n2], -mat[n2:]], axis=0)
        x = jnp.concatenate([_slab_rows(af_ref, j, h), _slab_rows(ab_ref, j, h)], axis=0)
        return jnp.dot(jnp.concatenate([mat, conj_mat], axis=1), x, preferred_element_type=F32) * scale

    for j in range(kb):
        for h in range(SLABS // SLABS_PER_DOT):
            lanes = slice(h * MXU_COLS, (h + 1) * MXU_COLS)
            if herm and j == 0:
                u = spectrum(edge_ref[0, 0], 0, h)
                uh = spectrum(edge_ref[0, 1], 0, h)
                oh_ref[0, 0, :, lanes] = uh[:n2]
                oh_ref[0, 1, :, lanes] = uh[n2:]
            else:
                u = spectrum(g_ref[j], j, h)
            o_ref[0, j, :, lanes] = u[:n2]
            o_ref[1, j, :, lanes] = u[n2:]


def _stage_bf(a5f, a5b, g, kb, scale, edge=None):
    _, _, k1n, n2, _ = a5f.shape
    herm = edge is not None
    aspec = pl.BlockSpec((1, SLABS, kb, n2, LANES), lambda i: (0, 0, i, 0, 0))
    out_specs = [pl.BlockSpec((2, kb, n2, HY_CH), lambda i: (0, i, 0, 0))]
    out_shape = [jax.ShapeDtypeStruct((2, k1n, n2, HY_CH), F32)]
    if herm:
        out_specs.append(pl.BlockSpec((1, 2, n2, HY_CH), lambda i: (i, 0, 0, 0)))
        out_shape.append(jax.ShapeDtypeStruct((k1n // kb, 2, n2, HY_CH), F32))
    return pl.pallas_call(
        functools.partial(_stage_bf_kernel, scale=scale, herm=herm),
        grid=(k1n // kb,),
        in_specs=[aspec, aspec, pl.BlockSpec((kb, 2 * n2, 2 * n2), lambda i: (i, 0, 0))]
        + ([pl.BlockSpec((1,) + edge.shape[1:], lambda i: (i, 0, 0, 0))] if herm else []),
        out_specs=out_specs,
        out_shape=out_shape,
        compiler_params=_cparams("parallel"),
        name="hy_filter_spectrum",
    )(a5f, a5b, g, *([edge] if herm else []))


def _stage_c_kernel(*refs, nz, a, mb):
    v_ref, f_ref = refs[0], refs[1]
    ux_refs = refs[2:2 + SLABS]
    skip_ref, o_ref, acc_ref = refs[2 + SLABS:]
    mk = pl.program_id(2)
    kblk = mb // 2
    v2 = v_ref.reshape(SLABS * kblk * N2_TILE, LANES)

    @pl.when(mk == 0)
    def _():
        acc_ref[...] = jnp.zeros_like(acc_ref)

    for s in range(N2_TILE):
        packed = jnp.concatenate([v2[pl.ds(c * kblk * N2_TILE + s, kblk, stride=N2_TILE), :] for c in range(SLABS)],
                                 axis=1)
        vs = jnp.concatenate(_unpack_pair(packed), axis=0).astype(BF16)
        acc_ref[s] += jnp.dot(f_ref[...], vs, preferred_element_type=F32)

    @pl.when(mk == pl.num_programs(2) - 1)
    def _():
        o2 = o_ref.reshape(SLABS * a * N2_TILE, LANES)
        ux2 = [r.reshape(nz * a * N2_TILE, LANES) for r in ux_refs]
        for s in range(N2_TILE):
            y = acc_ref[s]
            for c in range(SLABS):
                vals = []
                for z in range(nz):
                    uu, xx = _unpack_pair(ux2[c][pl.ds(z * a * N2_TILE + s, a, stride=N2_TILE), :])
                    yc = y[z * a:(z + 1) * a, c * LANES:(c + 1) * LANES]
                    vals.append(xx * (yc + skip_ref[0:1, c * LANES:(c + 1) * LANES] * uu))
                o2[pl.ds(c * a * N2_TILE + s, a, stride=N2_TILE), :] = _pack_pair(*vals) if nz == 2 else vals[0]


def _stage_c(v5, fmat, ux5, skip, mb):
    nz, p, a, n2, _ = ux5.shape
    m = 2 * v5.shape[2]
    xspec = lambda c: pl.BlockSpec((nz, 1, a, N2_TILE, LANES), lambda pi, j, mk: (0, pi, 0, j, c))
    return pl.pallas_call(
        functools.partial(_stage_c_kernel, nz=nz, a=a, mb=mb),
        grid=(p, n2 // N2_TILE, m // mb),
        in_specs=[pl.BlockSpec((1, SLABS, mb // 2, N2_TILE, LANES), lambda pi, j, mk: (pi, 0, mk, j, 0)),
                  pl.BlockSpec((nz * a, mb), lambda pi, j, mk: (0, mk))]
        + [xspec(c) for c in range(SLABS)] + [_const_spec((1, HY_CH))],
        out_specs=pl.BlockSpec((1, SLABS, a, N2_TILE, LANES), lambda pi, j, mk: (pi, 0, 0, j, 0)),
        out_shape=jax.ShapeDtypeStruct((p, SLABS, a, n2, LANES), jnp.uint32 if nz == 2 else F32),
        scratch_shapes=[pltpu.VMEM((N2_TILE, nz * a, HY_CH), F32)],
        compiler_params=_cparams("parallel", "parallel", "arbitrary"),
        name="hy_stage_c",
    )(v5, fmat, *([ux5] * SLABS), skip)


def _filter_kernel(zt_ref, tn_ref, w1t_ref, b1_ref, fr_ref, w2t_ref, b2_ref, w3_ref, dl_ref, of_ref, ob_ref):
    hi = lax.Precision.HIGHEST
    fr = fr_ref[...]
    h = jnp.sin(fr * (jnp.dot(w1t_ref[...], zt_ref[...], precision=hi, preferred_element_type=F32) + b1_ref[...]))
    h = jnp.sin(fr * (jnp.dot(w2t_ref[...], h, precision=hi, preferred_element_type=F32) + b2_ref[...]))
    taps = jnp.dot(h.T.astype(BF16), w3_ref[...], preferred_element_type=F32)
    tr = taps.shape[0]
    decay = jnp.exp(-tn_ref[...] * dl_ref[...])
    of_ref[...] = taps[:, :HY_CH] * decay
    t = pl.program_id(0) * tr + lax.broadcasted_iota(jnp.int32, (tr, HY_CH), 0)
    ob_ref[...] = jnp.where(t == 0, 0.0, taps[:, HY_CH:] * decay)


def _filter_taps(zfeat, f_w1, f_b1, f_freq, f_w2, f_b2, f_w3, absdelta, tr):
    n = zfeat.shape[0]
    cs = lambda a: _const_spec(a.shape)
    args = (zfeat.T, zfeat[:, 0:1], f_w1.T, f_b1.T, f_freq.T, f_w2.T, f_b2.T, f_w3.astype(BF16), absdelta)
    out = pl.BlockSpec((tr, HY_CH), lambda i: (i, 0))
    return pl.pallas_call(
        _filter_kernel,
        grid=(n // tr,),
        in_specs=[pl.BlockSpec((POS_DIM, tr), lambda i: (0, i)), pl.BlockSpec((tr, 1), lambda i: (i, 0))]
        + [cs(a) for a in args[2:]],
        out_specs=[out, out],
        out_shape=[jax.ShapeDtypeStruct((n, HY_CH), F32)] * 2,
        compiler_params=_cparams("parallel"),
        name="hy_filter_taps",
    )(*args)


def _bias_kernel(rb_ref, oh_ref, o_ref):
    o_ref[...] = jnp.dot(rb_ref[...], oh_ref[...], precision=lax.Precision.HIGHEST,
                         preferred_element_type=F32)


def _bias_table(rel_bias):
    i = jnp.arange(BLOCK)[:, None]
    j = jnp.arange(3 * BLOCK)[None, :]
    rel = j - BLOCK - i
    nb2 = N_BUCKETS // 2
    max_exact = nb2 // 2
    n = jnp.abs(rel)
    large = max_exact + (jnp.log(jnp.maximum(n, 1).astype(F32) / max_exact)
                         / math.log(MAX_DIST / max_exact) * (nb2 - max_exact)).astype(jnp.int32)
    large = jnp.minimum(large, nb2 - 1)
    bucket = jnp.where(rel > 0, nb2, 0) + jnp.where(n < max_exact, n, large)
    onehot = (bucket.reshape(1, -1) == jnp.arange(N_BUCKETS)[:, None]).astype(F32)
    cols = onehot.shape[1]
    tc = cols // 4
    table = pl.pallas_call(
        _bias_kernel,
        grid=(4,),
        in_specs=[_const_spec((ATT_HEADS, N_BUCKETS)), pl.BlockSpec((N_BUCKETS, tc), lambda c: (0, c))],
        out_specs=pl.BlockSpec((ATT_HEADS, tc), lambda c: (0, c)),
        out_shape=jax.ShapeDtypeStruct((ATT_HEADS, cols), F32),
        compiler_params=_cparams("parallel"),
        name="att_bias_table",
    )(rel_bias.T, onehot)
    table = table.reshape(ATT_HEADS, BLOCK, 3 * BLOCK)
    table = jnp.where((n <= WINDOW)[None], table, -jnp.inf)
    table = table.reshape(KV_HEADS, GQA_GROUP, BLOCK, 3 * BLOCK)
    return table.transpose(0, 3, 1, 2).reshape(KV_HEADS, 3 * BLOCK, GQA_GROUP * BLOCK)


def _attn_kernel(q_ref, km_ref, kp_ref, kn_ref, vm_ref, vp_ref, vn_ref, bias_ref, sink_ref, o_ref, *, nsub):
    i = pl.program_id(1)
    neg = -jnp.inf
    pen_first = jnp.where(i == 0, neg, 0.0)
    pen_last = jnp.where(i == pl.num_programs(1) - 1, neg, 0.0)
    kwin = jnp.concatenate([kp_ref[0], km_ref[0], kn_ref[0]], axis=0)
    vwin = jnp.concatenate([vp_ref[0], vm_ref[0], vn_ref[0]], axis=1)
    lane = lax.broadcasted_iota(jnp.int32, (BLOCK, LANES), 1)
    lo_half = lane < HEAD_DIM
    ones = jnp.ones((SUM_ROWS, 3 * BLOCK), BF16)
    units = [(s, g) for s in range(nsub) for g in range(KV_HEADS)]

    def scores(s, g):
        qs = q_ref[0, s * BLOCK:(s + 1) * BLOCK, :]
        rows = []
        for pr in range(2):
            qp = qs[:, (2 * g + pr) * LANES:(2 * g + pr + 1) * LANES]
            rows.append(jnp.where(lo_half, qp, jnp.zeros_like(qp)))
            rows.append(jnp.where(lo_half, jnp.zeros_like(qp), qp))
        qg = jnp.concatenate(rows, axis=0)
        kg = kwin[s * BLOCK:(s + 3) * BLOCK, g * LANES:(g + 1) * LANES]
        t = lax.dot_general(kg, qg, _NT, preferred_element_type=F32) + bias_ref[g]
        if s == 0:
            t = jnp.concatenate([t[:BLOCK] + pen_first, t[BLOCK:]], axis=0)
        if s == nsub - 1:
            t = jnp.concatenate([t[:2 * BLOCK], t[2 * BLOCK:] + pen_last], axis=0)
        return t

    def pv(s, g, p):
        vg = jnp.concatenate([vwin[g * HEAD_DIM:(g + 1) * HEAD_DIM, s * BLOCK:(s + 3) * BLOCK], ones],
                             axis=0)
        return jnp.dot(vg, p, preferred_element_type=F32)

    for u0 in range(0, len(units), ATT_UNITS):
        batch = units[u0:u0 + ATT_UNITS]
        sc = [scores(s, g) for s, g in batch]
        m = [jnp.maximum(jnp.max(t, axis=0, keepdims=True), sink_ref[g]) for t, (_, g) in zip(sc, batch)]
        p = [jnp.exp2(t - mm).astype(BF16) for t, mm in zip(sc, m)]
        o = [pv(s, g, pp) for (s, g), pp in zip(batch, p)]
        for (s, g), oo, mm in zip(batch, o, m):
            on = oo[:HEAD_DIM] / (oo[HEAD_DIM:HEAD_DIM + 1] + jnp.exp2(sink_ref[g] - mm))
            for pr in range(2):
                pair = jnp.concatenate([on[:, (2 * pr) * BLOCK:(2 * pr + 1) * BLOCK],
                                        on[:, (2 * pr + 1) * BLOCK:(2 * pr + 2) * BLOCK]], axis=0)
                o_ref[0, s * BLOCK:(s + 1) * BLOCK, (2 * g + pr) * LANES:(2 * g + pr + 1) * LANES] = (
                    pair.T.astype(o_ref.dtype))


SUM_ROWS = 16
ATT_UNITS = 16
ATT_TQ = 1024


def _attention(q, kd, vt, bias_t, sink_t, tq):
    b, l, _ = q.shape
    nsub = tq // BLOCK
    nblk = l // BLOCK
    kw = 2 * KV_W
    main = lambda w: pl.BlockSpec((1, tq, w), lambda bi, i: (bi, i, 0))
    prev = pl.BlockSpec((1, BLOCK, kw), lambda bi, i: (bi, jnp.maximum(i * nsub - 1, 0), 0))
    nxt = pl.BlockSpec((1, BLOCK, kw), lambda bi, i: (bi, jnp.minimum((i + 1) * nsub, nblk - 1), 0))
    vmain = pl.BlockSpec((1, KV_W, tq), lambda bi, i: (bi, 0, i))
    vprev = pl.BlockSpec((1, KV_W, BLOCK), lambda bi, i: (bi, 0, jnp.maximum(i * nsub - 1, 0)))
    vnxt = pl.BlockSpec((1, KV_W, BLOCK), lambda bi, i: (bi, 0, jnp.minimum((i + 1) * nsub, nblk - 1)))
    return pl.pallas_call(
        functools.partial(_attn_kernel, nsub=nsub),
        grid=(b, l // tq),
        in_specs=[main(ATT_W), main(kw), prev, nxt, vmain, vprev, vnxt,
                  _const_spec(bias_t.shape), _const_spec(sink_t.shape)],
        out_specs=main(ATT_W),
        out_shape=jax.ShapeDtypeStruct((b, l, ATT_W), BF16),
        compiler_params=_cparams("parallel", "parallel"),
        name="window_attn",
    )(q, kd, kd, kd, vt, vt, vt, bias_t, sink_t)


FF_CHUNK = 256


def _ffn_kernel(x_ref, yh_ref, ya_ref, hg_ref, ag_ref, wo_ref, g2_ref, wgu_ref, wd_ref, o_ref, act_ref, *, pairs, nt):
    def rms(t, g):
        return (t * lax.rsqrt(jnp.mean(t * t, axis=-1, keepdims=True) + EPS) * g).astype(BF16)

    tm = x_ref.shape[0]
    groups = [slice(r, r + tm // FFN_GROUPS) for r in range(0, tm, tm // FFN_GROUPS)]

    def hyena_rows(rs):
        if len(yh_ref.shape) == 3:
            yh = yh_ref[0, rs, :]
        else:
            yh = jnp.concatenate([yh_ref[0, c, rs, :] for c in range(SLABS)], axis=1)
        if pairs:
            lo, hi = _unpack_pair(yh)
            yh = jnp.where(pl.program_id(0) // nt < pairs, lo, hi)
        return yh

    mixed = [jnp.concatenate([rms(hyena_rows(rs), hg_ref[...]), rms(ya_ref[rs, :].astype(F32), ag_ref[...])], axis=1)
             for rs in groups]
    h = [x_ref[rs, :] + jnp.dot(m, wo_ref[...], preferred_element_type=F32) for rs, m in zip(groups, mixed)]
    hn = [rms(t, g2_ref[...]) for t in h]
    for c in range(D_FF // FF_CHUNK):
        lo = c * FF_CHUNK
        for rs, t in zip(groups, hn):
            gate = jnp.dot(t, wgu_ref[:, lo:lo + FF_CHUNK], preferred_element_type=F32)
            up = jnp.dot(t, wgu_ref[:, D_FF + lo:D_FF + lo + FF_CHUNK], preferred_element_type=F32)
            act_ref[rs, lo:lo + FF_CHUNK] = (gate / (1.0 + jnp.exp(-gate)) * up).astype(BF16)
    for rs, t in zip(groups, h):
        o_ref[rs, :] = t + jnp.dot(act_ref[rs, :], wd_ref[...], preferred_element_type=F32)


def _ffn(x2d, yh, ya, hg, ag, wo_b, g2, wgu_b, wd_b, tm, layer):
    t = x2d.shape[0]
    nt = yh.shape[-2] // tm
    pairs = yh.shape[0] if yh.dtype == jnp.uint32 else 0
    row = lambda w: pl.BlockSpec((tm, w), lambda i: (i, 0))
    if yh.ndim == 3:
        yspec = pl.BlockSpec((1, tm, HY_CH), lambda i: ((i // nt) % yh.shape[0], i % nt, 0))
    else:
        yspec = pl.BlockSpec((1, SLABS, tm, LANES), lambda i: ((i // nt) % yh.shape[0], 0, i % nt, 0))
    return pl.pallas_call(
        functools.partial(_ffn_kernel, pairs=pairs, nt=nt),
        grid=(t // tm,),
        in_specs=[row(D_MODEL), yspec, row(ATT_W), _const_spec((1, HY_CH)), _const_spec((1, ATT_W)),
                  _const_spec((D_MODEL, D_MODEL), layer), _const_spec((1, D_MODEL)),
                  _const_spec((D_MODEL, 2 * D_FF), layer), _const_spec((D_FF, D_MODEL), layer)],
        out_specs=row(D_MODEL),
        out_shape=jax.ShapeDtypeStruct((t, D_MODEL), F32),
        scratch_shapes=[pltpu.VMEM((tm, D_FF), BF16)],
        compiler_params=_cparams("parallel"),
        name="outproj_swiglu",
    )(x2d, yh, ya, hg, ag, wo_b, g2, wgu_b, wd_b)


def _dft_tables(seq_len, paired, kblk):
    n = 2 * seq_len
    n1 = n // FFT_N2
    a = n1 // 2
    kk = np.arange(n1)[:, None]
    th_half = 2.0 * np.pi * ((kk * np.arange(a)[None, :]) % n1) / n1
    c, s = np.cos(th_half), np.sin(th_half)

    def blocked(re, im):
        return np.concatenate([np.concatenate([re[i:i + kblk], im[i:i + kblk]], axis=0)
                               for i in range(0, re.shape[0], kblk)], axis=0)

    def real_input_rows(cc, ss):
        im = -ss[:a].copy()
        im[0] = cc[a]
        return cc[:a], im

    if paired:
        fa = blocked(np.concatenate([c, s], axis=1), np.concatenate([-s, c], axis=1))
        fc = fa.T
        faf = blocked(c, -s)
    else:
        fa = blocked(*real_input_rows(c, s))
        re_cols = np.concatenate([c[:1], 2.0 * c[1:a]], axis=0)
        im_cols = np.concatenate([c[a:a + 1], -2.0 * s[1:a]], axis=0)
        fc = blocked(re_cols, im_cols).T
        faf = fa
    to = lambda m: jnp.asarray(m.astype(np.float32)).astype(BF16)
    kron = None
    if paired and 2 * n1 * N2_TILE <= PAIR_KRON_ROWS:
        eye = np.eye(N2_TILE)
        kron = (to(np.kron(fa, eye)), to(np.kron(fc, eye)))
    return to(fa), to(fc), to(faf), kron


def _edge_matrices(g, half, kb):
    zero = jnp.zeros_like(g[0])
    z = zero[:, :FFT_N2]
    col = lambda m: m[:, :FFT_N2]
    sa = jnp.concatenate([col(g[0]), z], axis=1)
    sb = jnp.concatenate([z, col(g[half])], axis=1)
    blocks = [jnp.stack([sa, sb, sa.T, sb.T])]
    blocks += [jnp.stack([g[i], zero, g[i].T, zero]) for i in range(kb, half, kb)]
    return jnp.stack(blocks)


def _twiddle_tables(seq_len, k1_count):
    n = 2 * seq_len
    n1 = n // FFT_N2
    k1 = np.arange(k1_count, dtype=np.int64)[:, None, None]
    k2 = np.arange(FFT_N2, dtype=np.int64)[None, :, None]
    n2 = np.arange(FFT_N2, dtype=np.int64)[None, None, :]
    ang = ((n2 * (k2 * n1 + k1)) % n) * (-2.0 * np.pi / n)
    gr, gi = np.cos(ang), np.sin(ang)
    g = np.concatenate([np.concatenate([gr, -gi], axis=2), np.concatenate([gi, gr], axis=2)], axis=1)
    g = g.astype(np.float32)
    return jnp.asarray(g.astype(BF16)), jnp.asarray(np.swapaxes(g, 1, 2).astype(BF16))


def _filter_features(seq_len):
    t_idx = jnp.arange(seq_len, dtype=F32)[:, None]
    t_norm = t_idx / max(seq_len - 1, 1)
    bands = jnp.linspace(1e-4, N_BANDS - 1, N_BANDS, dtype=F32)
    w = (2.0 * math.pi) * t_idx * bands[None, :] / seq_len
    return jnp.concatenate([t_norm, jnp.cos(w), -jnp.sin(w)], axis=-1)


def _abs_deltas():
    min_decay = math.log(DECAY_TARGET) / FAST_DECAY_PCT
    max_decay = math.log(DECAY_TARGET) / SLOW_DECAY_PCT
    return jnp.abs(jnp.linspace(min_decay, max_decay, HY_CH, dtype=F32))[None, :]


def _pick(n, target):
    t = min(n, target)
    while n % t:
        t //= 2
    return t


def _layer(x, lw, shared, st):
    b, l, _ = x.shape
    t = b * l
    x2d = x.reshape(t, D_MODEL)
    tm = _pick(l, FFN_TM)
    ux, q, kd, vt = _inproj(x2d, lw["norm1"], lw["w_in"], lw["conv_w"], lw["conv_b"],
                            shared["bq"], shared["bk"], lw["qg"], lw["kg"], _pick(l, INPROJ_TM), l, lw["layer"])
    r3 = lambda a: a.reshape(b, l, a.shape[-1])

    n1 = 2 * l // FFT_N2
    a = n1 // 2
    nz = st["nz"]
    p = b // nz
    ux5 = ux.reshape(nz, p, a, FFT_N2, HY_CH)
    mb = _pick(st["fa"].shape[0], STAGE_MB)
    kf, kfh = lw["kf"][st["key"]]
    if st["kron"] is not None:
        ah = _pair_a(ux5, st["kron"][0])
        vv = _stage_b(ah, kf, st["g"], st["gt"], _pick(n1, STAGE_KB))
        yh = _pair_c(vv, st["kron"][1], ux5, lw["skip"]).reshape(p, l, HY_CH)
    else:
        ah = _stage_a(ux5, st["fa"], mb)
        vv = _stage_b(ah, kf, st["g"], st["gt"], _pick(ah.shape[2], STAGE_KB), st["edge"], kfh)
        yh = _stage_c(vv, st["fc"], ux5, lw["skip"], mb).reshape(p, SLABS, l, LANES)

    ya = _attention(r3(q), r3(kd), vt, shared["bias"], lw["sink_t"], _pick(l, ATT_TQ))
    ya = ya.reshape(t, ATT_W)

    out = _ffn(x2d, yh, ya, lw["hy_gain"], lw["at_gain"], lw["w_out"], lw["norm2"],
               lw["w_gate_up"], lw["w_down"], tm, lw["layer"])
    return out.reshape(b, l, D_MODEL)


def _filter_spectrum(lw_raw, st, seq_len):
    n = 2 * seq_len
    a = seq_len // FFT_N2
    taps_f, taps_b = _filter_taps(st["zfeat"], lw_raw["f_w1"], lw_raw["f_b1"], lw_raw["f_freq"], lw_raw["f_w2"],
                                  lw_raw["f_b2"], lw_raw["f_w3"], st["absdelta"], _pick(seq_len, 512))
    mb = _pick(st["faf"].shape[0], STAGE_MB)
    ahf = _stage_a(taps_f.reshape(1, 1, a, FFT_N2, HY_CH), st["faf"], mb)
    ahb = _stage_a(taps_b.reshape(1, 1, a, FFT_N2, HY_CH), st["faf"], mb)
    out = _stage_bf(ahf, ahb, st["g"], _pick(ahf.shape[2], STAGE_KB), 1.0 / n, st["edge"])
    return (out[0], out[1]) if st["edge"] is not None else (out[0], None)


def kernel(x_prompt, x_sample, norm1, w_in, conv_w, conv_b, f_w1, f_b1, f_freq, f_w2, f_b2, f_w3,
           hy_skip, q_gain, k_gain, sink, rel_bias, hy_gain, at_gain, w_out, norm2, w_gate_up, w_down):
    depth = norm1.shape[0]
    streams = {}
    for key, x in (("prompt", x_prompt), ("sample", x_sample)):
        b, l, _ = x.shape
        paired = b % 2 == 0
        n1 = 2 * l // FFT_N2
        fa, fc, faf, kron = _dft_tables(l, paired, _pick(2 * n1 if paired else n1, STAGE_MB) // 2)
        g, gt = _twiddle_tables(l, n1 if paired else n1 // 2 + 1)
        streams[key] = dict(key=key, nz=2 if paired else 1, fa=fa, fc=fc, faf=faf, kron=kron, g=g, gt=gt,
                            edge=None if paired else _edge_matrices(g, n1 // 2, _pick(n1 // 2, STAGE_KB)),
                            zfeat=_filter_features(l), absdelta=_abs_deltas())

    head_id = np.arange(ATT_W) // HEAD_DIM
    shared = dict(
        bq=jnp.asarray((head_id[:, None] == head_id[None, :]).astype(np.float32) / HEAD_DIM).astype(BF16),
        bk=jnp.asarray((head_id[:KV_W, None] == head_id[None, :KV_W]).astype(np.float32) / HEAD_DIM).astype(BF16),
        bias=_bias_table(rel_bias) * LOG2E,
    )

    w_in_b, w_out_b = w_in.astype(BF16), w_out.astype(BF16)
    w_gate_up_b, w_down_b = w_gate_up.astype(BF16), w_down.astype(BF16)

    y_prompt, y_sample = x_prompt, x_sample
    for li in range(depth):
        raw = dict(f_w1=f_w1[li], f_b1=f_b1[li][None, :], f_freq=f_freq[li][None, :], f_w2=f_w2[li],
                   f_b2=f_b2[li][None, :], f_w3=f_w3[li])
        lw = dict(
            layer=li, norm1=norm1[li][None, :], w_in=w_in_b,
            qg=jnp.tile(q_gain[li], ATT_HEADS)[None, :] * (HEAD_DIM ** -0.5 * LOG2E),
            kg=jnp.tile(k_gain[li], KV_HEADS)[None, :],
            conv_w=conv_w[li], conv_b=conv_b[li][None, :],
            skip=hy_skip[li][None, :],
            sink_t=jnp.repeat(sink[li] * LOG2E, BLOCK).reshape(KV_HEADS, 1, GQA_GROUP * BLOCK),
            hy_gain=hy_gain[li][None, :], at_gain=at_gain[li][None, :],
            w_out=w_out_b, norm2=norm2[li][None, :],
            w_gate_up=w_gate_up_b, w_down=w_down_b,
        )
        lw["kf"] = {key: _filter_spectrum(raw, st, {"prompt": x_prompt, "sample": x_sample}[key].shape[1])
                    for key, st in streams.items()}
        y_prompt = _layer(y_prompt, lw, shared, streams["prompt"])
        y_sample = _layer(y_sample, lw, shared, streams["sample"])
    return (y_prompt, y_sample)
```
